```python
import jax, jax.numpy as jnp
from jax import lax
import numpy as np

D_MODEL = 1024
BATCH = 2
SEQ = 16384
DEPTH = 2

HEAD_DIM = 64
N_HEADS = D_MODEL // HEAD_DIM
H_FOX = N_HEADS // 2
H_NSA = N_HEADS - H_FOX
NSA_GROUPS = 2
H_MOBA = N_HEADS
D_FF = -(-(8 * D_MODEL) // (3 * 256)) * 256
ROPE_THETA = 10000.0
RMS_EPS = 1e-6
Q_BLOCK = 128
CMP_STRIDE = 16
CMP_LEN = 2 * CMP_STRIDE
CMP_HIDDEN = 4 * HEAD_DIM
SLC_BLOCK = 64
SLC_TOPN = 16
WINDOW = 512
MOBA_BLOCK = 256
MOBA_TOPK = 3
MOBA_Q_CHUNK = 32
NEG_INF = -1e30

EVEN_SPLITS = (3 * [H_FOX * HEAD_DIM] + [H_FOX] + [H_NSA * HEAD_DIM]
               + 6 * [NSA_GROUPS * HEAD_DIM] + [3 * H_NSA])
EVEN_IN = sum(EVEN_SPLITS)
EVEN_SPLIT_IDX = [int(c) for c in np.cumsum(EVEN_SPLITS)[:-1]]

kernel_name = 'hybrid_fox_nsa_moba_trunk'


def rmsnorm(x, g):
    xf = x.astype(jnp.float32)
    y = xf * lax.rsqrt(jnp.mean(xf * xf, axis=-1, keepdims=True) + RMS_EPS)
    return (y * g.astype(jnp.float32)).astype(x.dtype)


def to_heads(t, n):
    b, s, _ = t.shape
    return t.reshape(b, s, n, HEAD_DIM).transpose(0, 2, 1, 3)


def from_heads(t):
    b, h, s, d = t.shape
    return t.transpose(0, 2, 1, 3).reshape(b, s, h * d)


def rope_tables(s):
    inv = ROPE_THETA ** (-jnp.arange(0, HEAD_DIM, 2, dtype=jnp.float32) / HEAD_DIM)
    ang = jnp.arange(s, dtype=jnp.float32)[:, None] * inv[None, :]
    return jnp.cos(ang), jnp.sin(ang)


def apply_rope(t, cos, sin):
    tf = t.astype(jnp.float32)
    t1, t2 = tf[..., :HEAD_DIM // 2], tf[..., HEAD_DIM // 2:]
    return jnp.concatenate([t1 * cos - t2 * sin, t2 * cos + t1 * sin], axis=-1).astype(t.dtype)


def masked_softmax(s, mask):
    s = jnp.where(mask, s, NEG_INF)
    e = jnp.where(mask, jnp.exp(s - jnp.max(s, axis=-1, keepdims=True)), 0.0)
    return e / jnp.maximum(jnp.sum(e, axis=-1, keepdims=True), 1e-30)


def sweep(fn, s, block):
    out = lax.map(fn, jnp.arange(s // block))
    nb, b, h, q, d = out.shape
    return out.transpose(1, 2, 0, 3, 4).reshape(b, h, nb * q, d)


def fox_attention(q, k, v, log_f):
    b, h, s, d = q.shape
    c = jnp.cumsum(log_f, axis=-1)
    kpos = jnp.arange(s)
    scale = d ** -0.5

    def block(i):
        qs = i * Q_BLOCK
        qb = lax.dynamic_slice_in_dim(q, qs, Q_BLOCK, axis=2)
        cb = lax.dynamic_slice_in_dim(c, qs, Q_BLOCK, axis=2)
        qpos = qs + jnp.arange(Q_BLOCK)
        sc = (jnp.einsum('bhqd,bhkd->bhqk', qb, k, preferred_element_type=jnp.float32) * scale
              + cb[..., :, None] - c[..., None, :])
        p = masked_softmax(sc, kpos[None, :] <= qpos[:, None])
        return jnp.einsum('bhqk,bhkd->bhqd', p.astype(v.dtype), v)

    return sweep(block, s, Q_BLOCK)


def nsa_compress(t, pe, w1, w2):
    b, g, s, d = t.shape
    ch = t.reshape(b, g, s // CMP_STRIDE, CMP_STRIDE, d)
    blk = jnp.concatenate([ch[:, :, :-1], ch[:, :, 1:]], axis=3) + pe
    blk = blk.reshape(b, g, blk.shape[2], CMP_LEN * d)
    return jax.nn.gelu(blk @ w1) @ w2


def nsa_attention(q, kc, vc, ks, vs, kw, vw, gates):
    b, h, s, d = q.shape
    g = ks.shape[1]
    r = h // g
    nc = kc.shape[2]
    ns = s // SLC_BLOCK
    n_sel = min(SLC_TOPN, ns)
    ratio = SLC_BLOCK // CMP_STRIDE
    scale = d ** -0.5
    cmp_end = jnp.arange(nc) * CMP_STRIDE + CMP_LEN - 1
    ks_blk = ks.reshape(b, g, ns, SLC_BLOCK, d)
    vs_blk = vs.reshape(b, g, ns, SLC_BLOCK, d)
    wpad = ((0, 0), (0, 0), (WINDOW, 0), (0, 0))
    kw_pad, vw_pad = jnp.pad(kw, wpad), jnp.pad(vw, wpad)
    bi = jnp.arange(b)[:, None, None, None]
    gi = jnp.arange(g)[None, :, None, None]
    sblk = jnp.arange(ns)

    def block(i):
        qs = i * Q_BLOCK
        qpos = qs + jnp.arange(Q_BLOCK)
        qb = lax.dynamic_slice_in_dim(q, qs, Q_BLOCK, axis=2).reshape(b, g, r, Q_BLOCK, d)
        gb = lax.dynamic_slice_in_dim(gates, qs, Q_BLOCK, axis=2).reshape(b, g, r, Q_BLOCK, 3)
        sc = jnp.einsum('bgrqd,bgcd->bgrqc', qb, kc, preferred_element_type=jnp.float32) * scale
        pc = masked_softmax(sc, cmp_end[None, :] <= qpos[:, None])
        oc = jnp.einsum('bgrqc,bgcd->bgrqd', pc.astype(vc.dtype), vc)
        imp = jnp.pad(pc.sum(axis=2), ((0, 0), (0, 0), (0, 0), (1, 1)))
        imp = imp[..., :ratio * ns].reshape(b, g, Q_BLOCK, ns, ratio).sum(-1) + imp[..., ratio::ratio]
        qblk = qpos // SLC_BLOCK
        forced = ((sblk[None, :] == 0) | (sblk[None, :] == qblk[:, None])
                  | (sblk[None, :] == qblk[:, None] - 1))
        imp = jnp.where(forced, 1e9, jnp.where(sblk[None, :] > qblk[:, None], -1e9, imp))
        _, idx = lax.top_k(imp, n_sel)
        ksel = ks_blk[bi, gi, idx].reshape(b, g, Q_BLOCK, n_sel * SLC_BLOCK, d)
        vsel = vs_blk[bi, gi, idx].reshape(b, g, Q_BLOCK, n_sel * SLC_BLOCK, d)
        spos = (idx[..., None] * SLC_BLOCK + jnp.arange(SLC_BLOCK)).reshape(b, g, 1, Q_BLOCK, n_sel * SLC_BLOCK)
        ss = jnp.einsum('bgrqd,bgqkd->bgrqk', qb, ksel, preferred_element_type=jnp.float32) * scale
        psl = masked_softmax(ss, spos <= qpos[:, None])
        osl = jnp.einsum('bgrqk,bgqkd->bgrqd', psl.astype(vsel.dtype), vsel)
        kwin = lax.dynamic_slice_in_dim(kw_pad, qs, WINDOW + Q_BLOCK, axis=2)
        vwin = lax.dynamic_slice_in_dim(vw_pad, qs, WINDOW + Q_BLOCK, axis=2)
        wpos = qs - WINDOW + jnp.arange(WINDOW + Q_BLOCK)
        dlt = qpos[:, None] - wpos[None, :]
        sw = jnp.einsum('bgrqd,bgkd->bgrqk', qb, kwin, preferred_element_type=jnp.float32) * scale
        pw = masked_softmax(sw, (dlt >= 0) & (dlt < WINDOW) & (wpos[None, :] >= 0))
        ow = jnp.einsum('bgrqk,bgkd->bgrqd', pw.astype(vwin.dtype), vwin)
        o = gb[..., 0:1] * oc + gb[..., 1:2] * osl + gb[..., 2:3] * ow
        return o.reshape(b, h, Q_BLOCK, d).astype(q.dtype)

    return sweep(block, s, Q_BLOCK)


def moba_attention(q, k, v):
    b, h, s, d = q.shape
    s_pad = -(-s // MOBA_BLOCK) * MOBA_BLOCK
    pad = ((0, 0), (0, 0), (0, s_pad - s), (0, 0))
    kp, vp = jnp.pad(k, pad), jnp.pad(v, pad)
    nb = s_pad // MOBA_BLOCK
    n_top = min(MOBA_TOPK, nb)
    n_s = n_top * MOBA_BLOCK
    kb = kp.reshape(b, h, nb, MOBA_BLOCK, d)
    vb = vp.reshape(b, h, nb, MOBA_BLOCK, d)
    kbar = jnp.mean(kb.astype(jnp.float32), axis=3)
    bi = jnp.arange(b)[:, None, None, None]
    hi = jnp.arange(h)[None, :, None, None]
    blk_ids = jnp.arange(nb)
    scale = d ** -0.5

    def chunk(i):
        qs = i * MOBA_Q_CHUNK
        cur = qs // MOBA_BLOCK
        qpos = qs + jnp.arange(MOBA_Q_CHUNK)
        qb = lax.dynamic_slice_in_dim(q, qs, MOBA_Q_CHUNK, axis=2)
        gate = jnp.einsum('bhqd,bhnd->bhqn', qb.astype(jnp.float32), kbar)
        gate = jnp.where(blk_ids < cur, gate, NEG_INF)
        _, idx = lax.top_k(gate, n_top)
        valid = jnp.broadcast_to((idx < cur)[..., None], idx.shape + (MOBA_BLOCK,))
        valid = valid.reshape(b, h, MOBA_Q_CHUNK, n_s)
        ksel = kb[bi, hi, idx].reshape(b, h, MOBA_Q_CHUNK, n_s, d)
        vsel = vb[bi, hi, idx].reshape(b, h, MOBA_Q_CHUNK, n_s, d)
        kown = lax.dynamic_slice_in_dim(kp, cur * MOBA_BLOCK, MOBA_BLOCK, axis=2)
        vown = lax.dynamic_slice_in_dim(vp, cur * MOBA_BLOCK, MOBA_BLOCK, axis=2)
        own_pos = cur * MOBA_BLOCK + jnp.arange(MOBA_BLOCK)
        own_mask = jnp.broadcast_to(own_pos[None, :] <= qpos[:, None], (b, h, MOBA_Q_CHUNK, MOBA_BLOCK))
        s_sel = jnp.einsum('bhqd,bhqkd->bhqk', qb, ksel, preferred_element_type=jnp.float32)
        s_own = jnp.einsum('bhqd,bhkd->bhqk', qb, kown, preferred_element_type=jnp.float32)
        p = masked_softmax(jnp.concatenate([s_sel, s_own], axis=-1) * scale,
                           jnp.concatenate([valid, own_mask], axis=-1)).astype(v.dtype)
        return (jnp.einsum('bhqk,bhqkd->bhqd', p[..., :n_s], vsel)
                + jnp.einsum('bhqk,bhkd->bhqd', p[..., n_s:], vown))

    return sweep(chunk, s, MOBA_Q_CHUNK)


def even_mixer(h, w_in, b_f, pe_k, w1_k, w2_k, pe_v, w1_v, w2_v, w_out, cos, sin):
    b, s, _ = h.shape
    (fq, fk, fv, f_logit, nq, kc, vc, ks, vs, kw, vw, g_logit) = jnp.split(h @ w_in, EVEN_SPLIT_IDX, axis=-1)
    log_f = jax.nn.log_sigmoid(f_logit.astype(jnp.float32) + b_f.astype(jnp.float32)).transpose(0, 2, 1)
    o_fox = fox_attention(to_heads(fq, H_FOX), to_heads(fk, H_FOX), to_heads(fv, H_FOX), log_f)
    rot = lambda t, n: apply_rope(to_heads(t, n), cos, sin)
    gates = jax.nn.sigmoid(g_logit.astype(jnp.float32)).reshape(b, s, H_NSA, 3).transpose(0, 2, 1, 3)
    o_nsa = nsa_attention(rot(nq, H_NSA),
                          nsa_compress(rot(kc, NSA_GROUPS), pe_k, w1_k, w2_k),
                          nsa_compress(to_heads(vc, NSA_GROUPS), pe_v, w1_v, w2_v),
                          rot(ks, NSA_GROUPS), to_heads(vs, NSA_GROUPS),
                          rot(kw, NSA_GROUPS), to_heads(vw, NSA_GROUPS), gates)
    return jnp.concatenate([from_heads(o_fox), from_heads(o_nsa)], axis=-1) @ w_out


def odd_mixer(h, w_in, w_out, cos, sin):
    q, k, v = jnp.split(h @ w_in, 3, axis=-1)
    o = moba_attention(apply_rope(to_heads(q, H_MOBA), cos, sin),
                       apply_rope(to_heads(k, H_MOBA), cos, sin),
                       to_heads(v, H_MOBA))
    return from_heads(o) @ w_out


def swiglu(h, w_gate, w_up, w_down):
    return (jax.nn.silu(h @ w_gate) * (h @ w_up)) @ w_down


def setup_inputs(seed: int = 0) -> dict:
    key = jax.random.key(seed)
    ks = jax.random.split(key, 20)
    ne, no = (DEPTH + 1) // 2, DEPTH // 2
    f32 = jnp.float32

    def dense(k, shape, fan_in):
        return jax.random.normal(k, shape, f32) * fan_in ** -0.5

    def gain(k):
        return 1.0 + 0.05 * jax.random.normal(k, (DEPTH, D_MODEL), f32)

    cmp_in = CMP_LEN * HEAD_DIM
    return {
        'x': jax.random.normal(ks[0], (BATCH, SEQ, D_MODEL), f32),
        'ev_w_in': dense(ks[1], (ne, D_MODEL, EVEN_IN), D_MODEL),
        'ev_b_f': jax.random.uniform(ks[2], (ne, H_FOX), f32, 1.0, 6.0),
        'ev_cmp_pe_k': 0.5 * jax.random.normal(ks[3], (ne, CMP_LEN, HEAD_DIM), f32),
        'ev_cmp_w1_k': dense(ks[4], (ne, cmp_in, CMP_HIDDEN), cmp_in),
        'ev_cmp_w2_k': dense(ks[5], (ne, CMP_HIDDEN, HEAD_DIM), CMP_HIDDEN),
        'ev_cmp_pe_v': 0.5 * jax.random.normal(ks[6], (ne, CMP_LEN, HEAD_DIM), f32),
        'ev_cmp_w1_v': dense(ks[7], (ne, cmp_in, CMP_HIDDEN), cmp_in),
        'ev_cmp_w2_v': dense(ks[8], (ne, CMP_HIDDEN, HEAD_DIM), CMP_HIDDEN),
        'ev_w_out': dense(ks[9], (ne, (H_FOX + H_NSA) * HEAD_DIM, D_MODEL), (H_FOX + H_NSA) * HEAD_DIM),
        'od_w_in': dense(ks[10], (no, D_MODEL, 3 * H_MOBA * HEAD_DIM), D_MODEL),
        'od_w_out': dense(ks[11], (no, H_MOBA * HEAD_DIM, D_MODEL), H_MOBA * HEAD_DIM),
        'g_mix_pre': gain(ks[12]),
        'g_mix_post': gain(ks[13]),
        'g_ffn_pre': gain(ks[14]),
        'g_ffn_post': gain(ks[15]),
        'ffn_w_gate': dense(ks[16], (DEPTH, D_MODEL, D_FF), D_MODEL),
        'ffn_w_up': dense(ks[17], (DEPTH, D_MODEL, D_FF), D_MODEL),
        'ffn_w_down': dense(ks[18], (DEPTH, D_FF, D_MODEL), D_FF),
    }


def reference(x, ev_w_in, ev_b_f, ev_cmp_pe_k, ev_cmp_w1_k, ev_cmp_w2_k, ev_cmp_pe_v,
              ev_cmp_w1_v, ev_cmp_w2_v, ev_w_out, od_w_in, od_w_out, g_mix_pre, g_mix_post,
              g_ffn_pre, g_ffn_post, ffn_w_gate, ffn_w_up, ffn_w_down):
    cos, sin = rope_tables(x.shape[1])
    for layer in range(DEPTH):
        h = rmsnorm(x, g_mix_pre[layer])
        if layer % 2 == 0:
            e = layer // 2
            y = even_mixer(h, ev_w_in[e], ev_b_f[e], ev_cmp_pe_k[e], ev_cmp_w1_k[e], ev_cmp_w2_k[e],
                           ev_cmp_pe_v[e], ev_cmp_w1_v[e], ev_cmp_w2_v[e], ev_w_out[e], cos, sin)
        else:
            o = layer // 2
            y = odd_mixer(h, od_w_in[o], od_w_out[o], cos, sin)
        x = x + rmsnorm(y, g_mix_post[layer])
        h = rmsnorm(x, g_ffn_pre[layer])
        x = x + rmsnorm(swiglu(h, ffn_w_gate[layer], ffn_w_up[layer], ffn_w_down[layer]), g_ffn_post[layer])
    return x
```

```python
import functools

import jax
import jax.numpy as jnp
import numpy as np
from jax import lax
from jax.experimental import pallas as pl
from jax.experimental.pallas import tpu as pltpu

F32 = jnp.float32
BF16 = jnp.bfloat16

HEAD_DIM = 64
LANES = 128
AUG = 2 * HEAD_DIM
ROPE_THETA = 10000.0
RMS_EPS = 1e-6
CMP_STRIDE = 16
CMP_LEN = 32
SLC_BLOCK = 64
SLC_TOPN = 16
WINDOW = 512
MOBA_BLOCK = 256
MOBA_TOPK = 3
MASKED = -1e30
M_INIT = -1e29
FLASH_T = 512
VMEM_LIMIT = 48 * 1024 * 1024


def _cparams(*sem):
    return pltpu.CompilerParams(dimension_semantics=sem, vmem_limit_bytes=VMEM_LIMIT)


def _rms(x, g):
    return x * lax.rsqrt(jnp.mean(x * x, axis=-1, keepdims=True) + RMS_EPS) * g


def _split3(x):
    hi = x.astype(BF16).astype(F32)
    r = x - hi
    mid = r.astype(BF16).astype(F32)
    lo = (r - mid).astype(BF16).astype(F32)
    return hi, mid, lo


def _norm_proj_kernel(x_ref, g_ref, w_ref, sc_ref, *rest, rope):
    if rope:
        cos_ref, sin_ref, o_ref, h_sc = rest
    else:
        o_ref, h_sc = rest

    @pl.when(pl.program_id(1) == 0)
    def _():
        h_sc[...] = _rms(x_ref[...], g_ref[...]).astype(BF16)

    t = jnp.dot(h_sc[...], w_ref[...], preferred_element_type=F32)
    if rope:
        tn = t.shape[1]
        reps = tn // LANES
        cos = jnp.tile(cos_ref[...], (1, reps))
        sin = jnp.tile(sin_ref[...], (1, reps))
        lane = lax.broadcasted_iota(jnp.int32, t.shape, 1)
        first = (lane & (HEAD_DIM - 1)) < HEAD_DIM // 2
        rot = jnp.where(first, pltpu.roll(t, tn - HEAD_DIM // 2, 1), pltpu.roll(t, HEAD_DIM // 2, 1))
        t = t * cos + rot * sin
    o_ref[...] = (t * sc_ref[...]).astype(o_ref.dtype)


def norm_proj(x, g, w, colscale, rope_tabs, seq, out_dtype, tm=512):
    n, d = x.shape
    nc = w.shape[1]
    tn = nc if nc <= 2048 else 1024
    assert n % tm == 0 and nc % tn == 0 and tn % LANES == 0 and seq % tm == 0
    rope = rope_tabs is not None
    in_specs = [
        pl.BlockSpec((tm, d), lambda i, j: (i, 0)),
        pl.BlockSpec((1, d), lambda i, j: (0, 0)),
        pl.BlockSpec((d, tn), lambda i, j: (0, j)),
        pl.BlockSpec((1, tn), lambda i, j: (0, j)),
    ]
    args = [x, g.reshape(1, d), w, colscale.reshape(1, nc)]
    if rope:
        spt = seq // tm
        in_specs += [pl.BlockSpec((tm, LANES), lambda i, j: (i % spt, 0))] * 2
        args += list(rope_tabs)
    return pl.pallas_call(
        functools.partial(_norm_proj_kernel, rope=rope),
        grid=(n // tm, nc // tn),
        in_specs=in_specs,
        out_specs=pl.BlockSpec((tm, tn), lambda i, j: (i, j)),
        out_shape=jax.ShapeDtypeStruct((n, nc), out_dtype),
        scratch_shapes=[pltpu.VMEM((tm, d), BF16)],
        compiler_params=_cparams("parallel", "arbitrary"),
        name="norm_proj_rope" if rope else "norm_proj",
    )(*args)


def _fox_decay_kernel(fl_ref, b_ref, o_ref):
    logf = jax.nn.log_sigmoid(fl_ref[0] + b_ref[...])
    h, s = logf.shape
    lane = lax.broadcasted_iota(jnp.int32, (h, s), 1)
    c = logf
    sh = 1
    while sh < s:
        c = c + jnp.where(lane >= sh, pltpu.roll(c, sh, 1), 0.0)
        sh *= 2
    hi, mid, lo = _split3(c)
    one = jnp.ones_like(c)
    rows = (hi, mid, lo, -hi, -mid, -lo, one, jnp.zeros_like(c))
    for r, val in enumerate(rows):
        for hh in range(h):
            o_ref[0, hh, r:r + 1, :] = val[hh:hh + 1, :].astype(o_ref.dtype)


def fox_decay(fl_t, b_f):
    b, h, s = fl_t.shape
    return pl.pallas_call(
        _fox_decay_kernel,
        grid=(b,),
        in_specs=[pl.BlockSpec((1, h, s), lambda i: (i, 0, 0)),
                  pl.BlockSpec((h, 1), lambda i: (0, 0))],
        out_specs=pl.BlockSpec((1, h, 8, s), lambda i: (i, 0, 0, 0)),
        out_shape=jax.ShapeDtypeStruct((b, h, 8, s), BF16),
        compiler_params=_cparams("parallel"),
        name="fox_decay",
    )(fl_t, b_f.reshape(h, 1))


def _nsa_compress_kernel(t_ref, pe_ref, w1_ref, w2_ref, o_ref):
    t = t_ref[0, 0].astype(F32)
    ncp = t.shape[0]
    a = jnp.dot((t + pe_ref[0:1, :]).astype(BF16), w1_ref[0], preferred_element_type=F32)
    bm = jnp.dot((t + pe_ref[1:2, :]).astype(BF16), w1_ref[1], preferred_element_type=F32)
    pre = a + pltpu.roll(bm, ncp - 1, 0)
    hid = jax.nn.gelu(pre)
    o_ref[0, 0] = jnp.dot(hid.astype(BF16), w2_ref[...], preferred_element_type=F32).astype(o_ref.dtype)


def nsa_compress(t, pe, w1, w2):
    b, g, ncp, cw = t.shape
    hid = w1.shape[-1]
    return pl.pallas_call(
        _nsa_compress_kernel,
        grid=(b, g),
        in_specs=[pl.BlockSpec((1, 1, ncp, cw), lambda i, j: (i, j, 0, 0)),
                  pl.BlockSpec((2, cw), lambda i, j: (0, 0)),
                  pl.BlockSpec((2, cw, hid), lambda i, j: (0, 0, 0)),
                  pl.BlockSpec((hid, HEAD_DIM), lambda i, j: (0, 0))],
        out_specs=pl.BlockSpec((1, 1, ncp, HEAD_DIM), lambda i, j: (i, j, 0, 0)),
        out_shape=jax.ShapeDtypeStruct((b, g, ncp, HEAD_DIM), BF16),
        compiler_params=_cparams("parallel", "parallel"),
        name="nsa_compress",
    )(t, pe, w1, w2)


def _topk_mask(work, col, k):
    sel = jnp.zeros(work.shape, jnp.bool_)
    col = col.astype(F32)
    for _ in range(k):
        mx = jnp.max(work, axis=-1, keepdims=True)
        first = jnp.min(jnp.where(work == mx, col, 1e9), axis=-1, keepdims=True)
        hit = col == first
        sel = jnp.logical_or(sel, hit)
        work = jnp.where(hit, -jnp.inf, work)
    return sel


def _nsa_select_kernel(q_ref, kct_ref, vc_ref, m_ref, oc_ref, sb_ref, *, tq, rep, nsup):
    i = pl.program_id(2)
    ncp = kct_ref.shape[-1]
    ns = m_ref.shape[-1]
    qpos = i * tq + lax.broadcasted_iota(jnp.int32, (tq, 1), 0)
    cend = lax.broadcasted_iota(jnp.int32, (1, ncp), 1) * CMP_STRIDE + (CMP_LEN - 1)
    cmask = cend <= qpos
    kct = kct_ref[0, 0]
    vc = vc_ref[0, 0]
    pcs = jnp.zeros((tq, ncp), F32)
    for r in range(rep):
        s = jnp.dot(q_ref[0, r], kct, preferred_element_type=F32)
        s = jnp.where(cmask, s, MASKED)
        e = jnp.where(cmask, jnp.exp(s - jnp.max(s, axis=-1, keepdims=True)), 0.0)
        p = e / jnp.maximum(jnp.sum(e, axis=-1, keepdims=True), 1e-30)
        oc_ref[0, r] = jnp.dot(p.astype(BF16), vc, preferred_element_type=F32)
        pcs = pcs + p
    mm = m_ref[...]
    imp = sum(jnp.dot(part.astype(BF16), mm, preferred_element_type=F32) for part in _split3(pcs))
    sblk = lax.broadcasted_iota(jnp.int32, (1, ns), 1)
    qblk = qpos >> 6
    forced = (sblk == 0) | (sblk == qblk) | (sblk == qblk - 1)
    imp = jnp.where(forced, 1e9, jnp.where(sblk > qblk, -1e9, imp))
    sel = _topk_mask(imp, sblk, min(SLC_TOPN, ns))
    bias = jnp.where(sel & (sblk <= qblk), 0.0, MASKED).astype(sb_ref.dtype)
    zeros = jnp.zeros((tq, HEAD_DIM), sb_ref.dtype)
    for j in range(nsup):
        sb_ref[0, 0, j] = jnp.concatenate([zeros, bias[:, j * HEAD_DIM:(j + 1) * HEAD_DIM]], axis=-1)


def nsa_select(q, kct, vc, m, tq=128):
    b, h, s, _ = q.shape
    g = kct.shape[1]
    rep = h // g
    ncp = kct.shape[-1]
    ns = m.shape[-1]
    nsup = ns // HEAD_DIM
    assert SLC_BLOCK == 64 and ns % HEAD_DIM == 0
    return pl.pallas_call(
        functools.partial(_nsa_select_kernel, tq=tq, rep=rep, nsup=nsup),
        grid=(b, g, s // tq),
        in_specs=[pl.BlockSpec((1, rep, tq, HEAD_DIM), lambda bi, gi, i: (bi, gi, i, 0)),
                  pl.BlockSpec((1, 1, HEAD_DIM, ncp), lambda bi, gi, i: (bi, gi, 0, 0)),
                  pl.BlockSpec((1, 1, ncp, HEAD_DIM), lambda bi, gi, i: (bi, gi, 0, 0)),
                  pl.BlockSpec((ncp, ns), lambda bi, gi, i: (0, 0))],
        out_specs=[pl.BlockSpec((1, rep, tq, HEAD_DIM), lambda bi, gi, i: (bi, gi, i, 0)),
                   pl.BlockSpec((1, 1, nsup, tq, LANES), lambda bi, gi, i: (bi, gi, 0, i, 0))],
        out_shape=[jax.ShapeDtypeStruct((b, h, s, HEAD_DIM), F32),
                   jax.ShapeDtypeStruct((b, g, nsup, s, LANES), BF16)],
        compiler_params=_cparams("parallel", "parallel", "parallel"),
        name="nsa_select",
    )(q, kct, vc, m)


def _moba_select_kernel(q_ref, k_ref, sb_ref, kbar_sc, *, tq):
    i = pl.program_id(2)
    s = k_ref.shape[2]
    nb = s // MOBA_BLOCK

    @pl.when(i == 0)
    def _():
        kb = k_ref[0, 0].astype(F32).reshape(nb, MOBA_BLOCK, HEAD_DIM)
        kbar_sc[...] = jnp.sum(kb, axis=1) * (1.0 / MOBA_BLOCK)

    q = q_ref[0, 0]
    dn = (((1,), (1,)), ((), ()))
    gate = sum(lax.dot_general(q, part.astype(BF16), dn, preferred_element_type=F32)
               for part in _split3(kbar_sc[...]))
    qpos = i * tq + lax.broadcasted_iota(jnp.int32, (tq, 1), 0)
    cur = qpos >> 8
    blk = lax.broadcasted_iota(jnp.int32, (1, nb), 1)
    past = blk < cur
    sel = _topk_mask(jnp.where(past, gate, -jnp.inf), blk, min(MOBA_TOPK, nb))
    sb_ref[0, 0] = jnp.where((sel & past) | (blk == cur), 0.0, MASKED).astype(sb_ref.dtype)


def moba_select(q, k, tq=512):
    b, h, s, _ = q.shape
    nb = s // MOBA_BLOCK
    assert MOBA_BLOCK == 256 and nb <= HEAD_DIM
    return pl.pallas_call(
        functools.partial(_moba_select_kernel, tq=tq),
        grid=(b, h, s // tq),
        in_specs=[pl.BlockSpec((1, 1, tq, HEAD_DIM), lambda bi, hi, i: (bi, hi, i, 0)),
                  pl.BlockSpec((1, 1, s, HEAD_DIM), lambda bi, hi, i: (bi, hi, 0, 0))],
        out_specs=pl.BlockSpec((1, 1, tq, nb), lambda bi, hi, i: (bi, hi, i, 0)),
        out_shape=jax.ShapeDtypeStruct((b, h, s, nb), BF16),
        scratch_shapes=[pltpu.VMEM((nb, HEAD_DIM), F32)],
        compiler_params=_cparams("parallel", "parallel", "arbitrary"),
        name="moba_select",
    )(q, k)


def _flash_kernel(*refs, t, tps, band, has_extra):
    if has_extra:
        qa_ref, ex_ref, kt_ref, v_ref, o_ref, m_sc, l_sc, acc_sc = refs
    else:
        qa_ref, kt_ref, v_ref, o_ref, m_sc, l_sc, acc_sc = refs
        ex_ref = None
    i = pl.program_id(2)
    m_sc[...] = jnp.full(m_sc.shape, M_INIT, F32)
    l_sc[...] = jnp.zeros(l_sc.shape, F32)
    acc_sc[...] = jnp.zeros(acc_sc.shape, F32)
    qa0 = qa_ref[0, 0]
    row = lax.broadcasted_iota(jnp.int32, (t, t), 0)
    col = lax.broadcasted_iota(jnp.int32, (t, t), 1)

    def step(j, mask):
        qa = qa0 if ex_ref is None else qa0 + ex_ref[0, 0, j // tps]
        start = pl.multiple_of(j * t, t)
        s = jnp.dot(qa, kt_ref[0, 0, :, pl.ds(start, t)], preferred_element_type=F32)
        if mask is not None:
            s = jnp.where(mask, s, MASKED)
        m_prev = m_sc[...]
        m_new = jnp.maximum(m_prev, jnp.max(s, axis=-1, keepdims=True))
        p = jnp.exp(s - m_new)
        alpha = jnp.exp(m_prev - m_new)
        l_sc[...] = alpha * l_sc[...] + jnp.sum(p, axis=-1, keepdims=True)
        acc_sc[...] = alpha * acc_sc[...] + jnp.dot(
            p.astype(BF16), v_ref[0, 0, pl.ds(start, t), :], preferred_element_type=F32)
        m_sc[...] = m_new

    if band:
        @pl.when(i > 0)
        def _():
            step(i - 1, col > row)
    else:
        def body(j, carry):
            step(j, None)
            return carry
        lax.fori_loop(0, i, body, 0)
    step(i, col <= row)
    o_ref[0, 0] = acc_sc[...] / l_sc[...]


def flash(qa, kta, v, extra=None, band=False):
    b, h, s, _ = qa.shape
    hk = kta.shape[1]
    rep = h // hk
    t = min(FLASH_T, s)
    assert s % t == 0 and (not band or t == WINDOW)
    in_specs = [pl.BlockSpec((1, 1, t, AUG), lambda bi, hi, i: (bi, hi, i, 0))]
    args = [qa]
    tps = 1
    if extra is not None:
        he, nsup = extra.shape[1], extra.shape[2]
        rep_e = h // he
        assert (s // nsup) % t == 0
        tps = (s // nsup) // t
        in_specs.append(pl.BlockSpec((1, 1, nsup, t, AUG), lambda bi, hi, i: (bi, hi // rep_e, 0, i, 0)))
        args.append(extra)
    in_specs += [pl.BlockSpec((1, 1, AUG, s), lambda bi, hi, i: (bi, hi // rep, 0, 0)),
                 pl.BlockSpec((1, 1, s, HEAD_DIM), lambda bi, hi, i: (bi, hi // rep, 0, 0))]
    args += [kta, v]
    return pl.pallas_call(
        functools.partial(_flash_kernel, t=t, tps=tps, band=band, has_extra=extra is not None),
        grid=(b, h, s // t),
        in_specs=in_specs,
        out_specs=pl.BlockSpec((1, 1, t, HEAD_DIM), lambda bi, hi, i: (bi, hi, i, 0)),
        out_shape=jax.ShapeDtypeStruct((b, h, s, HEAD_DIM), F32),
        scratch_shapes=[pltpu.VMEM((t, 1), F32), pltpu.VMEM((t, 1), F32), pltpu.VMEM((t, HEAD_DIM), F32)],
        compiler_params=_cparams("parallel", "parallel", "parallel"),
        name="flash_band" if band else ("flash_sel" if extra is not None else "flash"),
    )(*args)


def _nsa_combine_kernel(g_ref, oc_ref, os_ref, ow_ref, o_ref):
    g = jax.nn.sigmoid(g_ref[0, 0])
    o = g[:, 0:1] * oc_ref[0, 0] + g[:, 1:2] * os_ref[0, 0] + g[:, 2:3] * ow_ref[0, 0]
    o_ref[0, 0] = o.astype(o_ref.dtype)


def nsa_combine(gl, oc, osl, ow, ts=2048):
    b, h, s, d = oc.shape
    ts = min(ts, s)
    spec = pl.BlockSpec((1, 1, ts, d), lambda bi, hi, i: (bi, hi, i, 0))
    return pl.pallas_call(
        _nsa_combine_kernel,
        grid=(b, h, s // ts),
        in_specs=[pl.BlockSpec((1, 1, ts, 3), lambda bi, hi, i: (bi, hi, i, 0)), spec, spec, spec],
        out_specs=spec,
        out_shape=jax.ShapeDtypeStruct((b, h, s, d), BF16),
        compiler_params=_cparams("parallel", "parallel", "parallel"),
        name="nsa_combine",
    )(gl, oc, osl, ow)


def _proj_norm_res_kernel(a_ref, w_ref, g_ref, x_ref, o_ref):
    y = jnp.dot(a_ref[...], w_ref[...], preferred_element_type=F32)
    o_ref[...] = x_ref[...] + _rms(y, g_ref[...])


def proj_norm_res(a, w, g, x, tm=512):
    n, k = a.shape
    d = w.shape[1]
    return pl.pallas_call(
        _proj_norm_res_kernel,
        grid=(n // tm,),
        in_specs=[pl.BlockSpec((tm, k), lambda i: (i, 0)),
                  pl.BlockSpec((k, d), lambda i: (0, 0)),
                  pl.BlockSpec((1, d), lambda i: (0, 0)),
                  pl.BlockSpec((tm, d), lambda i: (i, 0))],
        out_specs=pl.BlockSpec((tm, d), lambda i: (i, 0)),
        out_shape=jax.ShapeDtypeStruct((n, d), F32),
        compiler_params=_cparams("parallel"),
        name="proj_norm_res",
    )(a, w, g.reshape(1, d), x)


def _ffn_kernel(x_ref, gpre_ref, wg_ref, wu_ref, wd_ref, gpost_ref, o_ref, h_sc, acc_sc):
    f = pl.program_id(1)

    @pl.when(f == 0)
    def _():
        h_sc[...] = _rms(x_ref[...], gpre_ref[...]).astype(BF16)
        acc_sc[...] = jnp.zeros(acc_sc.shape, F32)

    h = h_sc[...]
    a = jnp.dot(h, wg_ref[...], preferred_element_type=F32)
    u = jnp.dot(h, wu_ref[...], preferred_element_type=F32)
    act = (jax.nn.silu(a) * u).astype(BF16)
    acc_sc[...] += jnp.dot(act, wd_ref[...], preferred_element_type=F32)

    @pl.when(f == pl.num_programs(1) - 1)
    def _():
        o_ref[...] = x_ref[...] + _rms(acc_sc[...], gpost_ref[...])


def ffn(x, gpre, wg, wu, wd, gpost, tm=512, tf=256):
    n, d = x.shape
    dff = wg.shape[1]
    assert n % tm == 0 and dff % tf == 0
    return pl.pallas_call(
        _ffn_kernel,
        grid=(n // tm, dff // tf),
        in_specs=[pl.BlockSpec((tm, d), lambda i, f: (i, 0)),
                  pl.BlockSpec((1, d), lambda i, f: (0, 0)),
                  pl.BlockSpec((d, tf), lambda i, f: (0, f)),
                  pl.BlockSpec((d, tf), lambda i, f: (0, f)),
                  pl.BlockSpec((tf, d), lambda i, f: (f, 0)),
                  pl.BlockSpec((1, d), lambda i, f: (0, 0))],
        out_specs=pl.BlockSpec((tm, d), lambda i, f: (i, 0)),
        out_shape=jax.ShapeDtypeStruct((n, d), F32),
        scratch_shapes=[pltpu.VMEM((tm, d), BF16), pltpu.VMEM((tm, d), F32)],
        compiler_params=_cparams("parallel", "arbitrary"),
        name="ffn",
    )(x, gpre.reshape(1, d), wg, wu, wd, gpost.reshape(1, d))


def _rope_tables(s):
    inv = ROPE_THETA ** (-jnp.arange(0, HEAD_DIM, 2, dtype=F32) / HEAD_DIM)
    ang = jnp.arange(s, dtype=F32)[:, None] * inv[None, :]
    cos, sin = jnp.cos(ang), jnp.sin(ang)
    reps = LANES // HEAD_DIM
    return (jnp.tile(jnp.concatenate([cos, cos], -1), (1, reps)),
            jnp.tile(jnp.concatenate([-sin, sin], -1), (1, reps)))


def _heads(t, b, s):
    return t.reshape(b, s, -1, HEAD_DIM).transpose(0, 2, 1, 3)


def _heads_t(t, b, s):
    return t.reshape(b, s, -1, HEAD_DIM).transpose(0, 2, 3, 1)


def _unheads(t):
    b, h, s, d = t.shape
    return t.transpose(0, 2, 1, 3).reshape(b * s, h * d)


def _pad_q(q):
    return jnp.concatenate([q, jnp.zeros_like(q)], axis=-1)


def _block_indicator(n_rows, s, block):
    blk = (np.arange(s) // block) % n_rows
    return jnp.asarray(blk[None, :] == np.arange(n_rows)[:, None], BF16)


def _overlap_matrix(ncp, ns):
    ratio = SLC_BLOCK // CMP_STRIDE
    m = np.arange(ncp)[:, None]
    j = np.arange(ns)[None, :]
    ok = (m >= ratio * j - 1) & (m <= ratio * j + ratio - 1) & (m < ncp - 1)
    return jnp.asarray(ok, BF16)


def _even_mixer(x2, b, s, g_pre, w_in, b_f, pe_k, w1_k, w2_k, pe_v, w1_v, w2_v, rope_tabs):
    hf, hn, g = 8, 8, 2
    hd = HEAD_DIM
    (w_fq, w_fk, w_fv, w_fl, w_nq, w_kc, w_vc, w_ks, w_vs, w_kw, w_vw, w_gl) = jnp.split(
        w_in, list(np.cumsum([hf * hd] * 3 + [hf] + [hn * hd] + [g * hd] * 6)), axis=1)
    scale = hd ** -0.5

    def colscale(widths_scales):
        return jnp.concatenate([jnp.full((w,), v, F32) for w, v in widths_scales])

    w_rope = jnp.concatenate([w_nq, w_kc, w_ks, w_kw], axis=1).astype(BF16)
    p_rope = norm_proj(x2, g_pre, w_rope, colscale([(hn * hd, scale), (3 * g * hd, 1.0)]),
                       rope_tabs, s, BF16)
    w_plain = jnp.concatenate([w_fq, w_fk, w_fv, w_vc, w_vs, w_vw], axis=1).astype(BF16)
    p_plain = norm_proj(x2, g_pre, w_plain, colscale([(hf * hd, scale), (2 * hf * hd + 3 * g * hd, 1.0)]),
                        None, s, BF16)
    n_small = hf + 3 * hn
    w_small = jnp.pad(jnp.concatenate([w_fl, w_gl], axis=1), ((0, 0), (0, LANES - n_small))).astype(BF16)
    p_small = norm_proj(x2, g_pre, w_small, jnp.ones((LANES,), F32), None, s, F32)

    nq, kc, ks, kw = jnp.split(p_rope, [hn * hd, (hn + g) * hd, (hn + 2 * g) * hd], axis=1)
    fq, fk, fv, vc, vs, vw = jnp.split(
        p_plain, list(np.cumsum([hf * hd] * 3 + [g * hd] * 2)), axis=1)

    fl_t = p_small[:, :hf].reshape(b, s, hf).transpose(0, 2, 1)
    dec = fox_decay(fl_t, b_f)
    qx = jnp.concatenate([dec[:, :, 6:7], dec[:, :, 6:7], dec[:, :, 6:7], dec[:, :, 0:3]], axis=2)
    kx = jnp.concatenate([dec[:, :, 3:6], dec[:, :, 6:7], dec[:, :, 6:7], dec[:, :, 6:7]], axis=2)
    pad_q = jnp.zeros((b, hf, s, hd - 6), BF16)
    pad_k = jnp.zeros((b, hf, hd - 6, s), BF16)
    qa = jnp.concatenate([_heads(fq, b, s), qx.transpose(0, 1, 3, 2), pad_q], axis=-1)
    kta = jnp.concatenate([_heads_t(fk, b, s), kx, pad_k], axis=2)
    o_fox = flash(qa, kta, _heads(fv, b, s))

    q = _heads(nq, b, s)
    ncp = s // CMP_STRIDE
    cw = CMP_STRIDE * hd

    def compress(t, pe, w1, w2):
        tt = _heads(t, b, s).reshape(b, g, ncp, cw)
        return nsa_compress(tt, pe.reshape(2, cw), w1.reshape(2, cw, -1).astype(BF16), w2.astype(BF16))

    kcmp = compress(kc, pe_k, w1_k, w2_k)
    vcmp = compress(vc, pe_v, w1_v, w2_v)
    ns = s // SLC_BLOCK
    o_cmp, selb = nsa_select(q, kcmp.transpose(0, 1, 3, 2), vcmp, _overlap_matrix(ncp, ns))
    qa = _pad_q(q)
    zer = jnp.zeros((b, g, hd, s), BF16)
    ind = jnp.broadcast_to(_block_indicator(hd, s, SLC_BLOCK), (b, g, hd, s))
    o_slc = flash(qa, jnp.concatenate([_heads_t(ks, b, s), ind], axis=2), _heads(vs, b, s), extra=selb)
    o_win = flash(qa, jnp.concatenate([_heads_t(kw, b, s), zer], axis=2), _heads(vw, b, s), band=True)
    gl = p_small[:, hf:n_small].reshape(b, s, hn, 3).transpose(0, 2, 1, 3)
    o_nsa = nsa_combine(gl, o_cmp, o_slc, o_win)
    return jnp.concatenate([_unheads(o_fox.astype(BF16)), _unheads(o_nsa)], axis=-1)


def _odd_mixer(x2, b, s, g_pre, w_in, rope_tabs):
    h = 16
    hd = HEAD_DIM
    d = h * hd
    scale = hd ** -0.5
    cs = jnp.concatenate([jnp.full((d,), scale, F32), jnp.ones((d,), F32)])
    p_qk = norm_proj(x2, g_pre, w_in[:, :2 * d].astype(BF16), cs, rope_tabs, s, BF16)
    p_v = norm_proj(x2, g_pre, w_in[:, 2 * d:].astype(BF16), jnp.ones((d,), F32), None, s, BF16)
    q = _heads(p_qk[:, :d], b, s)
    selb = moba_select(q, _heads(p_qk[:, d:], b, s))
    nb = selb.shape[-1]
    if nb < hd:
        selb = jnp.pad(selb, ((0, 0), (0, 0), (0, 0), (0, hd - nb)))
    qa = jnp.concatenate([q, selb], axis=-1)
    ind = jnp.broadcast_to(_block_indicator(hd, s, MOBA_BLOCK), (b, h, hd, s))
    if nb < hd:
        ind = ind * jnp.asarray(np.arange(hd)[:, None] < nb, BF16)
    kta = jnp.concatenate([_heads_t(p_qk[:, d:], b, s), ind], axis=2)
    o = flash(qa, kta, _heads(p_v, b, s))
    return _unheads(o.astype(BF16))


def kernel(x, ev_w_in, ev_b_f, ev_cmp_pe_k, ev_cmp_w1_k, ev_cmp_w2_k, ev_cmp_pe_v, ev_cmp_w1_v,
           ev_cmp_w2_v, ev_w_out, od_w_in, od_w_out, g_mix_pre, g_mix_post, g_ffn_pre, g_ffn_post,
           ffn_w_gate, ffn_w_up, ffn_w_down):
    b, s, d = x.shape
    depth = g_mix_pre.shape[0]
    rope_tabs = _rope_tables(s)
    x2 = x.reshape(b * s, d)
    for layer in range(depth):
        if layer % 2 == 0:
            e = layer // 2
            o = _even_mixer(x2, b, s, g_mix_pre[layer], ev_w_in[e], ev_b_f[e], ev_cmp_pe_k[e],
                            ev_cmp_w1_k[e], ev_cmp_w2_k[e], ev_cmp_pe_v[e], ev_cmp_w1_v[e],
                            ev_cmp_w2_v[e], rope_tabs)
            w_out = ev_w_out[e]
        else:
            o = _odd_mixer(x2, b, s, g_mix_pre[layer], od_w_in[layer // 2], rope_tabs)
            w_out = od_w_out[layer // 2]
        x2 = proj_norm_res(o, w_out.astype(BF16), g_mix_post[layer], x2)
        x2 = ffn(x2, g_ffn_pre[layer], ffn_w_gate[layer].astype(BF16), ffn_w_up[layer].astype(BF16),
                 ffn_w_down[layer].astype(BF16), g_ffn_post[layer])
    return x2.reshape(b, s, d)
```

```python
import functools

import jax
import jax.numpy as jnp
import numpy as np
from jax import lax
from jax.experimental import pallas as pl
from jax.experimental.pallas import tpu as pltpu

F32 = jnp.float32
BF16 = jnp.bfloat16

HEAD_DIM = 64
LANES = 128
AUG = 2 * HEAD_DIM
ROPE_THETA = 10000.0
RMS_EPS = 1e-6
CMP_STRIDE = 16
CMP_LEN = 32
SLC_BLOCK = 64
SLC_TOPN = 16
WINDOW = 512
MOBA_BLOCK = 256
MOBA_TOPK = 3
MASKED = -1e30
M_INIT = -1e29
FLASH_TQ = 1024
FLASH_TK = 512
FLASH_STRIP = 256
VMEM_LIMIT = 48 * 1024 * 1024


def _cparams(*sem):
    return pltpu.CompilerParams(dimension_semantics=sem, vmem_limit_bytes=VMEM_LIMIT)


def _rms(x, g):
    return x * lax.rsqrt(jnp.mean(x * x, axis=-1, keepdims=True) + RMS_EPS) * g


def _split3(x):
    hi = x.astype(BF16).astype(F32)
    r = x - hi
    mid = r.astype(BF16).astype(F32)
    lo = (r - mid).astype(BF16).astype(F32)
    return hi, mid, lo


def _norm_proj_kernel(x_ref, g_ref, w_ref, sc_ref, *rest, rope):
    if rope:
        cos_ref, sin_ref, o_ref, h_sc = rest
    else:
        o_ref, h_sc = rest

    @pl.when(pl.program_id(1) == 0)
    def _():
        h_sc[...] = _rms(x_ref[...], g_ref[...]).astype(BF16)

    t = jnp.dot(h_sc[...], w_ref[...], preferred_element_type=F32)
    if rope:
        tn = t.shape[1]
        reps = tn // LANES
        cos = jnp.tile(cos_ref[...], (1, reps))
        sin = jnp.tile(sin_ref[...], (1, reps))
        lane = lax.broadcasted_iota(jnp.int32, t.shape, 1)
        first = (lane & (HEAD_DIM - 1)) < HEAD_DIM // 2
        rot = jnp.where(first, pltpu.roll(t, tn - HEAD_DIM // 2, 1), pltpu.roll(t, HEAD_DIM // 2, 1))
        t = t * cos + rot * sin
    o_ref[...] = (t * sc_ref[...]).astype(o_ref.dtype)


def norm_proj(x, g, w, colscale, rope_tabs, seq, out_dtype, tm=512):
    n, d = x.shape
    nc = w.shape[1]
    tn = nc if nc <= 2048 else 1024
    assert n % tm == 0 and nc % tn == 0 and tn % LANES == 0 and seq % tm == 0
    rope = rope_tabs is not None
    in_specs = [
        pl.BlockSpec((tm, d), lambda i, j: (i, 0)),
        pl.BlockSpec((1, d), lambda i, j: (0, 0)),
        pl.BlockSpec((d, tn), lambda i, j: (0, j)),
        pl.BlockSpec((1, tn), lambda i, j: (0, j)),
    ]
    args = [x, g.reshape(1, d), w, colscale.reshape(1, nc)]
    if rope:
        spt = seq // tm
        in_specs += [pl.BlockSpec((tm, LANES), lambda i, j: (i % spt, 0))] * 2
        args += list(rope_tabs)
    return pl.pallas_call(
        functools.partial(_norm_proj_kernel, rope=rope),
        grid=(n // tm, nc // tn),
        in_specs=in_specs,
        out_specs=pl.BlockSpec((tm, tn), lambda i, j: (i, j)),
        out_shape=jax.ShapeDtypeStruct((n, nc), out_dtype),
        scratch_shapes=[pltpu.VMEM((tm, d), BF16)],
        compiler_params=_cparams("parallel", "arbitrary"),
        name="norm_proj_rope" if rope else "norm_proj",
    )(*args)


def _fox_decay_kernel(fl_ref, b_ref, o_ref):
    logf = jax.nn.log_sigmoid(fl_ref[0] + b_ref[...])
    h, s = logf.shape
    lane = lax.broadcasted_iota(jnp.int32, (h, s), 1)
    c = logf
    sh = 1
    while sh < s:
        c = c + jnp.where(lane >= sh, pltpu.roll(c, sh, 1), 0.0)
        sh *= 2
    hi, mid, lo = _split3(c)
    one = jnp.ones_like(c)
    rows = (hi, mid, lo, -hi, -mid, -lo, one, jnp.zeros_like(c))
    for r, val in enumerate(rows):
        for hh in range(h):
            o_ref[0, hh, r:r + 1, :] = val[hh:hh + 1, :].astype(o_ref.dtype)


def fox_decay(fl_t, b_f):
    b, h, s = fl_t.shape
    return pl.pallas_call(
        _fox_decay_kernel,
        grid=(b,),
        in_specs=[pl.BlockSpec((1, h, s), lambda i: (i, 0, 0)),
                  pl.BlockSpec((h, 1), lambda i: (0, 0))],
        out_specs=pl.BlockSpec((1, h, 8, s), lambda i: (i, 0, 0, 0)),
        out_shape=jax.ShapeDtypeStruct((b, h, 8, s), BF16),
        compiler_params=_cparams("parallel"),
        name="fox_decay",
    )(fl_t, b_f.reshape(h, 1))


def _nsa_compress_kernel(t_ref, pe_ref, w1_ref, w2_ref, o_ref):
    t = t_ref[0, 0].astype(F32)
    ncp = t.shape[0]
    a = jnp.dot((t + pe_ref[0:1, :]).astype(BF16), w1_ref[0], preferred_element_type=F32)
    bm = jnp.dot((t + pe_ref[1:2, :]).astype(BF16), w1_ref[1], preferred_element_type=F32)
    pre = a + pltpu.roll(bm, ncp - 1, 0)
    hid = jax.nn.gelu(pre)
    o_ref[0, 0] = jnp.dot(hid.astype(BF16), w2_ref[...], preferred_element_type=F32).astype(o_ref.dtype)


def nsa_compress(t, pe, w1, w2):
    b, g, ncp, cw = t.shape
    hid = w1.shape[-1]
    return pl.pallas_call(
        _nsa_compress_kernel,
        grid=(b, g),
        in_specs=[pl.BlockSpec((1, 1, ncp, cw), lambda i, j: (i, j, 0, 0)),
                  pl.BlockSpec((2, cw), lambda i, j: (0, 0)),
                  pl.BlockSpec((2, cw, hid), lambda i, j: (0, 0, 0)),
                  pl.BlockSpec((hid, HEAD_DIM), lambda i, j: (0, 0))],
        out_specs=pl.BlockSpec((1, 1, ncp, HEAD_DIM), lambda i, j: (i, j, 0, 0)),
        out_shape=jax.ShapeDtypeStruct((b, g, ncp, HEAD_DIM), BF16),
        compiler_params=_cparams("parallel", "parallel"),
        name="nsa_compress",
    )(t, pe, w1, w2)


def _topk_mask(work, col, k):
    sel = jnp.zeros(work.shape, jnp.bool_)
    col = col.astype(F32)
    for _ in range(k):
        mx = jnp.max(work, axis=-1, keepdims=True)
        first = jnp.min(jnp.where(work == mx, col, 1e9), axis=-1, keepdims=True)
        hit = col == first
        sel = jnp.logical_or(sel, hit)
        work = jnp.where(hit, -jnp.inf, work)
    return sel


def _nsa_select_kernel(q_ref, kct_ref, vc_ref, m_ref, oc_ref, sb_ref, *, tq, rep, nsup):
    i = pl.program_id(2)
    ncp = kct_ref.shape[-1]
    ns = m_ref.shape[-1]
    qpos = i * tq + lax.broadcasted_iota(jnp.int32, (tq, 1), 0)
    cend = lax.broadcasted_iota(jnp.int32, (1, ncp), 1) * CMP_STRIDE + (CMP_LEN - 1)
    cmask = cend <= qpos
    kct = kct_ref[0, 0]
    vc = vc_ref[0, 0]
    pcs = jnp.zeros((tq, ncp), F32)
    for r in range(rep):
        s = jnp.dot(q_ref[0, r], kct, preferred_element_type=F32)
        s = jnp.where(cmask, s, MASKED)
        e = jnp.where(cmask, jnp.exp(s - jnp.max(s, axis=-1, keepdims=True)), 0.0)
        p = e / jnp.maximum(jnp.sum(e, axis=-1, keepdims=True), 1e-30)
        oc_ref[0, r] = jnp.dot(p.astype(BF16), vc, preferred_element_type=F32)
        pcs = pcs + p
    mm = m_ref[...]
    imp = sum(jnp.dot(part.astype(BF16), mm, preferred_element_type=F32) for part in _split3(pcs))
    sblk = lax.broadcasted_iota(jnp.int32, (1, ns), 1)
    qblk = qpos >> 6
    forced = (sblk == 0) | (sblk == qblk) | (sblk == qblk - 1)
    imp = jnp.where(forced, 1e9, jnp.where(sblk > qblk, -1e9, imp))
    sel = _topk_mask(imp, sblk, min(SLC_TOPN, ns))
    bias = jnp.where(sel & (sblk <= qblk), 0.0, MASKED).astype(sb_ref.dtype)
    zeros = jnp.zeros((tq, HEAD_DIM), sb_ref.dtype)
    for j in range(nsup):
        sb_ref[0, 0, j] = jnp.concatenate([zeros, bias[:, j * HEAD_DIM:(j + 1) * HEAD_DIM]], axis=-1)


def nsa_select(q, kct, vc, m, tq=128):
    b, h, s, _ = q.shape
    g = kct.shape[1]
    rep = h // g
    ncp = kct.shape[-1]
    ns = m.shape[-1]
    nsup = ns // HEAD_DIM
    assert SLC_BLOCK == 64 and ns % HEAD_DIM == 0
    return pl.pallas_call(
        functools.partial(_nsa_select_kernel, tq=tq, rep=rep, nsup=nsup),
        grid=(b, g, s // tq),
        in_specs=[pl.BlockSpec((1, rep, tq, HEAD_DIM), lambda bi, gi, i: (bi, gi, i, 0)),
                  pl.BlockSpec((1, 1, HEAD_DIM, ncp), lambda bi, gi, i: (bi, gi, 0, 0)),
                  pl.BlockSpec((1, 1, ncp, HEAD_DIM), lambda bi, gi, i: (bi, gi, 0, 0)),
                  pl.BlockSpec((ncp, ns), lambda bi, gi, i: (0, 0))],
        out_specs=[pl.BlockSpec((1, rep, tq, HEAD_DIM), lambda bi, gi, i: (bi, gi, i, 0)),
                   pl.BlockSpec((1, 1, nsup, tq, LANES), lambda bi, gi, i: (bi, gi, 0, i, 0))],
        out_shape=[jax.ShapeDtypeStruct((b, h, s, HEAD_DIM), F32),
                   jax.ShapeDtypeStruct((b, g, nsup, s, LANES), BF16)],
        compiler_params=_cparams("parallel", "parallel", "parallel"),
        name="nsa_select",
    )(q, kct, vc, m)


def _moba_select_kernel(q_ref, k_ref, sb_ref, kbar_sc, *, tq):
    i = pl.program_id(2)
    s = k_ref.shape[2]
    nb = s // MOBA_BLOCK

    @pl.when(i == 0)
    def _():
        kb = k_ref[0, 0].astype(F32).reshape(nb, MOBA_BLOCK, HEAD_DIM)
        kbar_sc[...] = jnp.sum(kb, axis=1) * (1.0 / MOBA_BLOCK)

    q = q_ref[0, 0]
    dn = (((1,), (1,)), ((), ()))
    gate = sum(lax.dot_general(q, part.astype(BF16), dn, preferred_element_type=F32)
               for part in _split3(kbar_sc[...]))
    qpos = i * tq + lax.broadcasted_iota(jnp.int32, (tq, 1), 0)
    cur = qpos >> 8
    blk = lax.broadcasted_iota(jnp.int32, (1, nb), 1)
    past = blk < cur
    sel = _topk_mask(jnp.where(past, gate, -jnp.inf), blk, min(MOBA_TOPK, nb))
    sb_ref[0, 0] = jnp.where((sel & past) | (blk == cur), 0.0, MASKED).astype(sb_ref.dtype)


def moba_select(q, k, tq=512):
    b, h, s, _ = q.shape
    nb = s // MOBA_BLOCK
    assert MOBA_BLOCK == 256 and nb <= HEAD_DIM
    return pl.pallas_call(
        functools.partial(_moba_select_kernel, tq=tq),
        grid=(b, h, s // tq),
        in_specs=[pl.BlockSpec((1, 1, tq, HEAD_DIM), lambda bi, hi, i: (bi, hi, i, 0)),
                  pl.BlockSpec((1, 1, s, HEAD_DIM), lambda bi, hi, i: (bi, hi, 0, 0))],
        out_specs=pl.BlockSpec((1, 1, tq, nb), lambda bi, hi, i: (bi, hi, i, 0)),
        out_shape=jax.ShapeDtypeStruct((b, h, s, nb), BF16),
        scratch_shapes=[pltpu.VMEM((nb, HEAD_DIM), F32)],
        compiler_params=_cparams("parallel", "parallel", "arbitrary"),
        name="moba_select",
    )(q, k)


def _flash_kernel(*refs, tq, tk, rs, tps, band, has_extra):
    if has_extra:
        qa_ref, ex_ref, kt_ref, v_ref, o_ref, m_sc, acc_sc = refs
    else:
        qa_ref, kt_ref, v_ref, o_ref, m_sc, acc_sc = refs
        ex_ref = None
    i = pl.program_id(2)
    m_sc[...] = jnp.full(m_sc.shape, M_INIT, F32)
    acc_sc[...] = jnp.zeros(acc_sc.shape, F32)
    row = lax.broadcasted_iota(jnp.int32, (rs, tk), 0)
    col = lax.broadcasted_iota(jnp.int32, (rs, tk), 1)
    nst = tq // rs
    kpq = tq // tk

    def run(items):
        tiles = {}

        def operands(j):
            if id(j) not in tiles:
                start = pl.multiple_of(j * tk, tk)
                tiles[id(j)] = (kt_ref[0, 0, :, pl.ds(start, tk)], v_ref[0, 0, pl.ds(start, tk), :])
            return tiles[id(j)]

        def logits(item):
            j, r, _ = item
            rows = pl.ds(r * rs, rs)
            qa = qa_ref[0, 0, rows, :]
            if ex_ref is not None:
                qa = qa + ex_ref[0, 0, j // tps, rows, :]
            return jnp.dot(qa, operands(j)[0], preferred_element_type=F32)

        s_next = logits(items[0])
        for n, (j, r, mask) in enumerate(items):
            rows = pl.ds(r * rs, rs)
            s = s_next
            if n + 1 < len(items):
                s_next = logits(items[n + 1])
            if mask is not None:
                s = jnp.where(mask, s, MASKED)
            m_prev = m_sc[rows, :]
            m_new = jnp.maximum(m_prev, jnp.max(s, axis=-1, keepdims=True))
            p = jnp.exp(s - jnp.tile(m_new, (1, tk // LANES)))
            alpha = jnp.exp(m_prev - m_new)
            acc_sc[rows, :] = alpha * acc_sc[rows, :] + jnp.dot(
                p.astype(BF16), operands(j)[1], preferred_element_type=F32)
            m_sc[rows, :] = m_new

    def edge_items(dj_list, j_of):
        items = []
        for dj in dj_list:
            j = j_of(dj)
            for r in range(nst):
                off, ko = r * rs, dj * tk
                lo = off - WINDOW + 1 if band else None
                if ko > off + rs - 1 or (band and ko + tk - 1 < lo):
                    continue
                full = ko + tk - 1 <= off and (not band or ko > off + rs - 1 - WINDOW)
                mask = None
                if not full:
                    mask = col + ko <= row + off
                    if band:
                        mask = mask & (col + ko > row + (off - WINDOW))
                items.append((j, r, mask))
        return items

    if band:
        for dj in range(-(WINDOW // tk), 0):
            @pl.when(i * kpq + dj >= 0)
            def _():
                jj = i * kpq + dj
                run(edge_items([dj], lambda _: jj))
    else:
        def body(j, carry):
            run([(j, r, None) for r in range(nst)])
            return carry
        lax.fori_loop(0, i * kpq, body, 0)
    diag = [i * kpq + dj for dj in range(kpq)]
    run(edge_items(list(range(kpq)), lambda dj: diag[dj]))
    acc = acc_sc[...]
    o_ref[0, 0] = acc[:, :HEAD_DIM] / acc[:, HEAD_DIM:HEAD_DIM + 1]


def flash(qa, kta, v, extra=None, band=False):
    b, h, s, _ = qa.shape
    hk = kta.shape[1]
    rep = h // hk
    tk = min(FLASH_TK, s)
    tq = tk if band else min(FLASH_TQ, s)
    rs = min(FLASH_STRIP, tq)
    assert s % tq == 0 and tq % tk == 0 and tq % rs == 0 and WINDOW % tk == 0
    in_specs = [pl.BlockSpec((1, 1, tq, AUG), lambda bi, hi, i: (bi, hi, i, 0))]
    args = [qa]
    tps = 1
    if extra is not None:
        he, nsup = extra.shape[1], extra.shape[2]
        rep_e = h // he
        assert (s // nsup) % tk == 0
        tps = (s // nsup) // tk
        in_specs.append(pl.BlockSpec((1, 1, nsup, tq, AUG), lambda bi, hi, i: (bi, hi // rep_e, 0, i, 0)))
        args.append(extra)
    in_specs += [pl.BlockSpec((1, 1, AUG, s), lambda bi, hi, i: (bi, hi // rep, 0, 0)),
                 pl.BlockSpec((1, 1, s, AUG), lambda bi, hi, i: (bi, hi // rep, 0, 0))]
    args += [kta, v]
    return pl.pallas_call(
        functools.partial(_flash_kernel, tq=tq, tk=tk, rs=rs, tps=tps, band=band,
                          has_extra=extra is not None),
        grid=(b, h, s // tq),
        in_specs=in_specs,
        out_specs=pl.BlockSpec((1, 1, tq, HEAD_DIM), lambda bi, hi, i: (bi, hi, i, 0)),
        out_shape=jax.ShapeDtypeStruct((b, h, s, HEAD_DIM), F32),
        scratch_shapes=[pltpu.VMEM((tq, LANES), F32), pltpu.VMEM((tq, AUG), F32)],
        compiler_params=_cparams("parallel", "parallel", "parallel"),
        name="flash_band" if band else ("flash_sel" if extra is not None else "flash"),
    )(*args)


def _nsa_combine_kernel(g_ref, oc_ref, os_ref, ow_ref, o_ref):
    g = jax.nn.sigmoid(g_ref[0, 0])
    o = g[:, 0:1] * oc_ref[0, 0] + g[:, 1:2] * os_ref[0, 0] + g[:, 2:3] * ow_ref[0, 0]
    o_ref[0, 0] = o.astype(o_ref.dtype)


def nsa_combine(gl, oc, osl, ow, ts=2048):
    b, h, s, d = oc.shape
    ts = min(ts, s)
    spec = pl.BlockSpec((1, 1, ts, d), lambda bi, hi, i: (bi, hi, i, 0))
    return pl.pallas_call(
        _nsa_combine_kernel,
        grid=(b, h, s // ts),
        in_specs=[pl.BlockSpec((1, 1, ts, 3), lambda bi, hi, i: (bi, hi, i, 0)), spec, spec, spec],
        out_specs=spec,
        out_shape=jax.ShapeDtypeStruct((b, h, s, d), BF16),
        compiler_params=_cparams("parallel", "parallel", "parallel"),
        name="nsa_combine",
    )(gl, oc, osl, ow)


def _proj_norm_res_kernel(a_ref, w_ref, g_ref, x_ref, o_ref):
    y = jnp.dot(a_ref[...], w_ref[...], preferred_element_type=F32)
    o_ref[...] = x_ref[...] + _rms(y, g_ref[...])


def proj_norm_res(a, w, g, x, tm=512):
    n, k = a.shape
    d = w.shape[1]
    return pl.pallas_call(
        _proj_norm_res_kernel,
        grid=(n // tm,),
        in_specs=[pl.BlockSpec((tm, k), lambda i: (i, 0)),
                  pl.BlockSpec((k, d), lambda i: (0, 0)),
                  pl.BlockSpec((1, d), lambda i: (0, 0)),
                  pl.BlockSpec((tm, d), lambda i: (i, 0))],
        out_specs=pl.BlockSpec((tm, d), lambda i: (i, 0)),
        out_shape=jax.ShapeDtypeStruct((n, d), F32),
        compiler_params=_cparams("parallel"),
        name="proj_norm_res",
    )(a, w, g.reshape(1, d), x)


def _ffn_kernel(x_ref, gpre_ref, wg_ref, wu_ref, wd_ref, gpost_ref, o_ref, h_sc, acc_sc):
    f = pl.program_id(1)

    @pl.when(f == 0)
    def _():
        h_sc[...] = _rms(x_ref[...], gpre_ref[...]).astype(BF16)
        acc_sc[...] = jnp.zeros(acc_sc.shape, F32)

    h = h_sc[...]
    a = jnp.dot(h, wg_ref[...], preferred_element_type=F32)
    u = jnp.dot(h, wu_ref[...], preferred_element_type=F32)
    act = (jax.nn.silu(a) * u).astype(BF16)
    acc_sc[...] += jnp.dot(act, wd_ref[...], preferred_element_type=F32)

    @pl.when(f == pl.num_programs(1) - 1)
    def _():
        o_ref[...] = x_ref[...] + _rms(acc_sc[...], gpost_ref[...])


def ffn(x, gpre, wg, wu, wd, gpost, tm=512, tf=256):
    n, d = x.shape
    dff = wg.shape[1]
    assert n % tm == 0 and dff % tf == 0
    return pl.pallas_call(
        _ffn_kernel,
        grid=(n // tm, dff // tf),
        in_specs=[pl.BlockSpec((tm, d), lambda i, f: (i, 0)),
                  pl.BlockSpec((1, d), lambda i, f: (0, 0)),
                  pl.BlockSpec((d, tf), lambda i, f: (0, f)),
                  pl.BlockSpec((d, tf), lambda i, f: (0, f)),
                  pl.BlockSpec((tf, d), lambda i, f: (f, 0)),
                  pl.BlockSpec((1, d), lambda i, f: (0, 0))],
        out_specs=pl.BlockSpec((tm, d), lambda i, f: (i, 0)),
        out_shape=jax.ShapeDtypeStruct((n, d), F32),
        scratch_shapes=[pltpu.VMEM((tm, d), BF16), pltpu.VMEM((tm, d), F32)],
        compiler_params=_cparams("parallel", "arbitrary"),
        name="ffn",
    )(x, gpre.reshape(1, d), wg, wu, wd, gpost.reshape(1, d))


def _rope_tables(s):
    inv = ROPE_THETA ** (-jnp.arange(0, HEAD_DIM, 2, dtype=F32) / HEAD_DIM)
    ang = jnp.arange(s, dtype=F32)[:, None] * inv[None, :]
    cos, sin = jnp.cos(ang), jnp.sin(ang)
    reps = LANES // HEAD_DIM
    return (jnp.tile(jnp.concatenate([cos, cos], -1), (1, reps)),
            jnp.tile(jnp.concatenate([-sin, sin], -1), (1, reps)))


def _heads(t, b, s):
    return t.reshape(b, s, -1, HEAD_DIM).transpose(0, 2, 1, 3)


def _heads_t(t, b, s):
    return t.reshape(b, s, -1, HEAD_DIM).transpose(0, 2, 3, 1)


def _unheads(t):
    b, h, s, d = t.shape
    return t.transpose(0, 2, 1, 3).reshape(b * s, h * d)


def _ones_col(v):
    one = jnp.ones(v.shape[:-1] + (1,), v.dtype)
    return jnp.concatenate([v, one, jnp.zeros(v.shape[:-1] + (HEAD_DIM - 1,), v.dtype)], axis=-1)


def _pad_q(q):
    return jnp.concatenate([q, jnp.zeros_like(q)], axis=-1)


def _block_indicator(n_rows, s, block):
    blk = (np.arange(s) // block) % n_rows
    return jnp.asarray(blk[None, :] == np.arange(n_rows)[:, None], BF16)


def _overlap_matrix(ncp, ns):
    ratio = SLC_BLOCK // CMP_STRIDE
    m = np.arange(ncp)[:, None]
    j = np.arange(ns)[None, :]
    ok = (m >= ratio * j - 1) & (m <= ratio * j + ratio - 1) & (m < ncp - 1)
    return jnp.asarray(ok, BF16)


def _even_mixer(x2, b, s, g_pre, w_in, b_f, pe_k, w1_k, w2_k, pe_v, w1_v, w2_v, rope_tabs):
    hf, hn, g = 8, 8, 2
    hd = HEAD_DIM
    (w_fq, w_fk, w_fv, w_fl, w_nq, w_kc, w_vc, w_ks, w_vs, w_kw, w_vw, w_gl) = jnp.split(
        w_in, list(np.cumsum([hf * hd] * 3 + [hf] + [hn * hd] + [g * hd] * 6)), axis=1)
    scale = hd ** -0.5

    def colscale(widths_scales):
        return jnp.concatenate([jnp.full((w,), v, F32) for w, v in widths_scales])

    w_rope = jnp.concatenate([w_nq, w_kc, w_ks, w_kw], axis=1).astype(BF16)
    p_rope = norm_proj(x2, g_pre, w_rope, colscale([(hn * hd, scale), (3 * g * hd, 1.0)]),
                       rope_tabs, s, BF16)
    w_plain = jnp.concatenate([w_fq, w_fk, w_fv, w_vc, w_vs, w_vw], axis=1).astype(BF16)
    p_plain = norm_proj(x2, g_pre, w_plain, colscale([(hf * hd, scale), (2 * hf * hd + 3 * g * hd, 1.0)]),
                        None, s, BF16)
    n_small = hf + 3 * hn
    w_small = jnp.pad(jnp.concatenate([w_fl, w_gl], axis=1), ((0, 0), (0, LANES - n_small))).astype(BF16)
    p_small = norm_proj(x2, g_pre, w_small, jnp.ones((LANES,), F32), None, s, F32)

    nq, kc, ks, kw = jnp.split(p_rope, [hn * hd, (hn + g) * hd, (hn + 2 * g) * hd], axis=1)
    fq, fk, fv, vc, vs, vw = jnp.split(
        p_plain, list(np.cumsum([hf * hd] * 3 + [g * hd] * 2)), axis=1)

    fl_t = p_small[:, :hf].reshape(b, s, hf).transpose(0, 2, 1)
    dec = fox_decay(fl_t, b_f)
    qx = jnp.concatenate([dec[:, :, 6:7], dec[:, :, 6:7], dec[:, :, 6:7], dec[:, :, 0:3]], axis=2)
    kx = jnp.concatenate([dec[:, :, 3:6], dec[:, :, 6:7], dec[:, :, 6:7], dec[:, :, 6:7]], axis=2)
    pad_q = jnp.zeros((b, hf, s, hd - 6), BF16)
    pad_k = jnp.zeros((b, hf, hd - 6, s), BF16)
    qa = jnp.concatenate([_heads(fq, b, s), qx.transpose(0, 1, 3, 2), pad_q], axis=-1)
    kta = jnp.concatenate([_heads_t(fk, b, s), kx, pad_k], axis=2)
    o_fox = flash(qa, kta, _ones_col(_heads(fv, b, s)))

    q = _heads(nq, b, s)
    ncp = s // CMP_STRIDE
    cw = CMP_STRIDE * hd

    def compress(t, pe, w1, w2):
        tt = _heads(t, b, s).reshape(b, g, ncp, cw)
        return nsa_compress(tt, pe.reshape(2, cw), w1.reshape(2, cw, -1).astype(BF16), w2.astype(BF16))

    kcmp = compress(kc, pe_k, w1_k, w2_k)
    vcmp = compress(vc, pe_v, w1_v, w2_v)
    ns = s // SLC_BLOCK
    o_cmp, selb = nsa_select(q, kcmp.transpose(0, 1, 3, 2), vcmp, _overlap_matrix(ncp, ns))
    qa = _pad_q(q)
    zer = jnp.zeros((b, g, hd, s), BF16)
    ind = jnp.broadcast_to(_block_indicator(hd, s, SLC_BLOCK), (b, g, hd, s))
    o_slc = flash(qa, jnp.concatenate([_heads_t(ks, b, s), ind], axis=2), _ones_col(_heads(vs, b, s)),
                  extra=selb)
    o_win = flash(qa, jnp.concatenate([_heads_t(kw, b, s), zer], axis=2), _ones_col(_heads(vw, b, s)),
                  band=True)
    gl = p_small[:, hf:n_small].reshape(b, s, hn, 3).transpose(0, 2, 1, 3)
    o_nsa = nsa_combine(gl, o_cmp, o_slc, o_win)
    return jnp.concatenate([_unheads(o_fox.astype(BF16)), _unheads(o_nsa)], axis=-1)


def _odd_mixer(x2, b, s, g_pre, w_in, rope_tabs):
    h = 16
    hd = HEAD_DIM
    d = h * hd
    scale = hd ** -0.5
    cs = jnp.concatenate([jnp.full((d,), scale, F32), jnp.ones((d,), F32)])
    p_qk = norm_proj(x2, g_pre, w_in[:, :2 * d].astype(BF16), cs, rope_tabs, s, BF16)
    p_v = norm_proj(x2, g_pre, w_in[:, 2 * d:].astype(BF16), jnp.ones((d,), F32), None, s, BF16)
    q = _heads(p_qk[:, :d], b, s)
    selb = moba_select(q, _heads(p_qk[:, d:], b, s))
    nb = selb.shape[-1]
    if nb < hd:
        selb = jnp.pad(selb, ((0, 0), (0, 0), (0, 0), (0, hd - nb)))
    qa = jnp.concatenate([q, selb], axis=-1)
    ind = jnp.broadcast_to(_block_indicator(hd, s, MOBA_BLOCK), (b, h, hd, s))
    if nb < hd:
        ind = ind * jnp.asarray(np.arange(hd)[:, None] < nb, BF16)
    kta = jnp.concatenate([_heads_t(p_qk[:, d:], b, s), ind], axis=2)
    o = flash(qa, kta, _ones_col(_heads(p_v, b, s)))
    return _unheads(o.astype(BF16))


def kernel(x, ev_w_in, ev_b_f, ev_cmp_pe_k, ev_cmp_w1_k, ev_cmp_w2_k, ev_cmp_pe_v, ev_cmp_w1_v,
           ev_cmp_w2_v, ev_w_out, od_w_in, od_w_out, g_mix_pre, g_mix_post, g_ffn_pre, g_ffn_post,
           ffn_w_gate, ffn_w_up, ffn_w_down):
    b, s, d = x.shape
    depth = g_mix_pre.shape[0]
    rope_tabs = _rope_tables(s)
    x2 = x.reshape(b * s, d)
    for layer in range(depth):
        if layer % 2 == 0:
            e = layer // 2
            o = _even_mixer(x2, b, s, g_mix_pre[layer], ev_w_in[e], ev_b_f[e], ev_cmp_pe_k[e],
                            ev_cmp_w1_k[e], ev_cmp_w2_k[e], ev_cmp_pe_v[e], ev_cmp_w1_v[e],
                            ev_cmp_w2_v[e], rope_tabs)
            w_out = ev_w_out[e]
        else:
            o = _odd_mixer(x2, b, s, g_mix_pre[layer], od_w_in[layer // 2], rope_tabs)
            w_out = od_w_out[layer // 2]
        x2 = proj_norm_res(o, w_out.astype(BF16), g_mix_post[layer], x2)
        x2 = ffn(x2, g_ffn_pre[layer], ffn_w_gate[layer].astype(BF16), ffn_w_up[layer].astype(BF16),
                 ffn_w_down[layer].astype(BF16), g_ffn_post[layer])
    return x2.reshape(b, s, d)
```

```python
import functools

import jax
import jax.numpy as jnp
import numpy as np
from jax import lax
from jax.experimental import pallas as pl
from jax.experimental.pallas import tpu as pltpu

F32 = jnp.float32
BF16 = jnp.bfloat16

HEAD_DIM = 64
LANES = 128
AUG = 2 * HEAD_DIM
ROPE_THETA = 10000.0
RMS_EPS = 1e-6
CMP_STRIDE = 16
CMP_LEN = 32
SLC_BLOCK = 64
SLC_TOPN = 16
WINDOW = 512
MOBA_BLOCK = 256
MOBA_TOPK = 3
MASKED = -1e30
M_INIT = -1e29
FLASH_TQ = 1024
FLASH_TK = 512
FLASH_STRIP = 512
VMEM_LIMIT = 48 * 1024 * 1024


def _cparams(*sem):
    return pltpu.CompilerParams(dimension_semantics=sem, vmem_limit_bytes=VMEM_LIMIT)


def _rms(x, g):
    return x * lax.rsqrt(jnp.mean(x * x, axis=-1, keepdims=True) + RMS_EPS) * g


def _split3(x):
    hi = x.astype(BF16).astype(F32)
    r = x - hi
    mid = r.astype(BF16).astype(F32)
    lo = (r - mid).astype(BF16).astype(F32)
    return hi, mid, lo


def _norm_proj_kernel(x_ref, g_ref, w_ref, sc_ref, *rest, rope):
    if rope:
        cos_ref, sin_ref, o_ref, h_sc = rest
    else:
        o_ref, h_sc = rest

    @pl.when(pl.program_id(1) == 0)
    def _():
        h_sc[...] = _rms(x_ref[...], g_ref[...]).astype(BF16)

    t = jnp.dot(h_sc[...], w_ref[...], preferred_element_type=F32)
    if rope:
        tn = t.shape[1]
        reps = tn // LANES
        cos = jnp.tile(cos_ref[...], (1, reps))
        sin = jnp.tile(sin_ref[...], (1, reps))
        lane = lax.broadcasted_iota(jnp.int32, t.shape, 1)
        first = (lane & (HEAD_DIM - 1)) < HEAD_DIM // 2
        rot = jnp.where(first, pltpu.roll(t, tn - HEAD_DIM // 2, 1), pltpu.roll(t, HEAD_DIM // 2, 1))
        t = t * cos + rot * sin
    o_ref[...] = (t * sc_ref[...]).astype(o_ref.dtype)


def norm_proj(x, g, w, colscale, rope_tabs, seq, out_dtype, tm=512):
    n, d = x.shape
    nc = w.shape[1]
    tn = nc if nc <= 2048 else 1024
    assert n % tm == 0 and nc % tn == 0 and tn % LANES == 0 and seq % tm == 0
    rope = rope_tabs is not None
    in_specs = [
        pl.BlockSpec((tm, d), lambda i, j: (i, 0)),
        pl.BlockSpec((1, d), lambda i, j: (0, 0)),
        pl.BlockSpec((d, tn), lambda i, j: (0, j)),
        pl.BlockSpec((1, tn), lambda i, j: (0, j)),
    ]
    args = [x, g.reshape(1, d), w, colscale.reshape(1, nc)]
    if rope:
        spt = seq // tm
        in_specs += [pl.BlockSpec((tm, LANES), lambda i, j: (i % spt, 0))] * 2
        args += list(rope_tabs)
    return pl.pallas_call(
        functools.partial(_norm_proj_kernel, rope=rope),
        grid=(n // tm, nc // tn),
        in_specs=in_specs,
        out_specs=pl.BlockSpec((tm, tn), lambda i, j: (i, j)),
        out_shape=jax.ShapeDtypeStruct((n, nc), out_dtype),
        scratch_shapes=[pltpu.VMEM((tm, d), BF16)],
        compiler_params=_cparams("parallel", "arbitrary"),
        name="norm_proj_rope" if rope else "norm_proj",
    )(*args)


def _fox_decay_kernel(fl_ref, b_ref, o_ref):
    logf = jax.nn.log_sigmoid(fl_ref[0] + b_ref[...])
    h, s = logf.shape
    lane = lax.broadcasted_iota(jnp.int32, (h, s), 1)
    c = logf
    sh = 1
    while sh < s:
        c = c + jnp.where(lane >= sh, pltpu.roll(c, sh, 1), 0.0)
        sh *= 2
    hi, mid, lo = _split3(c)
    one = jnp.ones_like(c)
    rows = (hi, mid, lo, -hi, -mid, -lo, one, jnp.zeros_like(c))
    for r, val in enumerate(rows):
        for hh in range(h):
            o_ref[0, hh, r:r + 1, :] = val[hh:hh + 1, :].astype(o_ref.dtype)


def fox_decay(fl_t, b_f):
    b, h, s = fl_t.shape
    return pl.pallas_call(
        _fox_decay_kernel,
        grid=(b,),
        in_specs=[pl.BlockSpec((1, h, s), lambda i: (i, 0, 0)),
                  pl.BlockSpec((h, 1), lambda i: (0, 0))],
        out_specs=pl.BlockSpec((1, h, 8, s), lambda i: (i, 0, 0, 0)),
        out_shape=jax.ShapeDtypeStruct((b, h, 8, s), BF16),
        compiler_params=_cparams("parallel"),
        name="fox_decay",
    )(fl_t, b_f.reshape(h, 1))


def _nsa_compress_kernel(t_ref, pe_ref, w1_ref, w2_ref, o_ref):
    t = t_ref[0, 0].astype(F32)
    ncp = t.shape[0]
    a = jnp.dot((t + pe_ref[0:1, :]).astype(BF16), w1_ref[0], preferred_element_type=F32)
    bm = jnp.dot((t + pe_ref[1:2, :]).astype(BF16), w1_ref[1], preferred_element_type=F32)
    pre = a + pltpu.roll(bm, ncp - 1, 0)
    hid = jax.nn.gelu(pre)
    o_ref[0, 0] = jnp.dot(hid.astype(BF16), w2_ref[...], preferred_element_type=F32).astype(o_ref.dtype)


def nsa_compress(t, pe, w1, w2):
    b, g, ncp, cw = t.shape
    hid = w1.shape[-1]
    return pl.pallas_call(
        _nsa_compress_kernel,
        grid=(b, g),
        in_specs=[pl.BlockSpec((1, 1, ncp, cw), lambda i, j: (i, j, 0, 0)),
                  pl.BlockSpec((2, cw), lambda i, j: (0, 0)),
                  pl.BlockSpec((2, cw, hid), lambda i, j: (0, 0, 0)),
                  pl.BlockSpec((hid, HEAD_DIM), lambda i, j: (0, 0))],
        out_specs=pl.BlockSpec((1, 1, ncp, HEAD_DIM), lambda i, j: (i, j, 0, 0)),
        out_shape=jax.ShapeDtypeStruct((b, g, ncp, HEAD_DIM), BF16),
        compiler_params=_cparams("parallel", "parallel"),
        name="nsa_compress",
    )(t, pe, w1, w2)


def _topk_mask(work, col, k):
    sel = jnp.zeros(work.shape, jnp.bool_)
    col = col.astype(F32)
    for _ in range(k):
        mx = jnp.max(work, axis=-1, keepdims=True)
        first = jnp.min(jnp.where(work == mx, col, 1e9), axis=-1, keepdims=True)
        hit = col == first
        sel = jnp.logical_or(sel, hit)
        work = jnp.where(hit, -jnp.inf, work)
    return sel


def _nsa_select_kernel(q_ref, kct_ref, vc_ref, m_ref, oc_ref, sb_ref, *, tq, rep, nsup):
    i = pl.program_id(2)
    ncp = kct_ref.shape[-1]
    ns = m_ref.shape[-1]
    qpos = i * tq + lax.broadcasted_iota(jnp.int32, (tq, 1), 0)
    cend = lax.broadcasted_iota(jnp.int32, (1, ncp), 1) * CMP_STRIDE + (CMP_LEN - 1)
    cmask = cend <= qpos
    kct = kct_ref[0, 0]
    vc = vc_ref[0, 0]
    pcs = jnp.zeros((tq, ncp), F32)
    for r in range(rep):
        s = jnp.dot(q_ref[0, r], kct, preferred_element_type=F32)
        s = jnp.where(cmask, s, MASKED)
        e = jnp.where(cmask, jnp.exp(s - jnp.max(s, axis=-1, keepdims=True)), 0.0)
        p = e / jnp.maximum(jnp.sum(e, axis=-1, keepdims=True), 1e-30)
        oc_ref[0, r] = jnp.dot(p.astype(BF16), vc, preferred_element_type=F32)
        pcs = pcs + p
    mm = m_ref[...]
    imp = sum(jnp.dot(part.astype(BF16), mm, preferred_element_type=F32) for part in _split3(pcs))
    sblk = lax.broadcasted_iota(jnp.int32, (1, ns), 1)
    qblk = qpos >> 6
    forced = (sblk == 0) | (sblk == qblk) | (sblk == qblk - 1)
    imp = jnp.where(forced, 1e9, jnp.where(sblk > qblk, -1e9, imp))
    sel = _topk_mask(imp, sblk, min(SLC_TOPN, ns))
    bias = jnp.where(sel & (sblk <= qblk), 0.0, MASKED).astype(sb_ref.dtype)
    zeros = jnp.zeros((tq, HEAD_DIM), sb_ref.dtype)
    for j in range(nsup):
        sb_ref[0, 0, j] = jnp.concatenate([zeros, bias[:, j * HEAD_DIM:(j + 1) * HEAD_DIM]], axis=-1)


def nsa_select(q, kct, vc, m, tq=512):
    b, h, s, _ = q.shape
    g = kct.shape[1]
    rep = h // g
    ncp = kct.shape[-1]
    ns = m.shape[-1]
    nsup = ns // HEAD_DIM
    assert SLC_BLOCK == 64 and ns % HEAD_DIM == 0
    return pl.pallas_call(
        functools.partial(_nsa_select_kernel, tq=tq, rep=rep, nsup=nsup),
        grid=(b, g, s // tq),
        in_specs=[pl.BlockSpec((1, rep, tq, HEAD_DIM), lambda bi, gi, i: (bi, gi, i, 0)),
                  pl.BlockSpec((1, 1, HEAD_DIM, ncp), lambda bi, gi, i: (bi, gi, 0, 0)),
                  pl.BlockSpec((1, 1, ncp, HEAD_DIM), lambda bi, gi, i: (bi, gi, 0, 0)),
                  pl.BlockSpec((ncp, ns), lambda bi, gi, i: (0, 0))],
        out_specs=[pl.BlockSpec((1, rep, tq, HEAD_DIM), lambda bi, gi, i: (bi, gi, i, 0)),
                   pl.BlockSpec((1, 1, nsup, tq, LANES), lambda bi, gi, i: (bi, gi, 0, i, 0))],
        out_shape=[jax.ShapeDtypeStruct((b, h, s, HEAD_DIM), F32),
                   jax.ShapeDtypeStruct((b, g, nsup, s, LANES), BF16)],
        compiler_params=_cparams("parallel", "parallel", "parallel"),
        name="nsa_select",
    )(q, kct, vc, m)


def _moba_select_kernel(q_ref, k_ref, sb_ref, kbar_sc, *, tq):
    i = pl.program_id(2)
    s = k_ref.shape[2]
    nb = s // MOBA_BLOCK

    @pl.when(i == 0)
    def _():
        kb = k_ref[0, 0].astype(F32).reshape(nb, MOBA_BLOCK, HEAD_DIM)
        kbar_sc[...] = jnp.sum(kb, axis=1) * (1.0 / MOBA_BLOCK)

    q = q_ref[0, 0]
    dn = (((1,), (1,)), ((), ()))
    gate = sum(lax.dot_general(q, part.astype(BF16), dn, preferred_element_type=F32)
               for part in _split3(kbar_sc[...]))
    qpos = i * tq + lax.broadcasted_iota(jnp.int32, (tq, 1), 0)
    cur = qpos >> 8
    blk = lax.broadcasted_iota(jnp.int32, (1, nb), 1)
    past = blk < cur
    sel = _topk_mask(jnp.where(past, gate, -jnp.inf), blk, min(MOBA_TOPK, nb))
    sb_ref[0, 0] = jnp.where((sel & past) | (blk == cur), 0.0, MASKED).astype(sb_ref.dtype)


def moba_select(q, k, tq=512):
    b, h, s, _ = q.shape
    nb = s // MOBA_BLOCK
    assert MOBA_BLOCK == 256 and nb <= HEAD_DIM
    return pl.pallas_call(
        functools.partial(_moba_select_kernel, tq=tq),
        grid=(b, h, s // tq),
        in_specs=[pl.BlockSpec((1, 1, tq, HEAD_DIM), lambda bi, hi, i: (bi, hi, i, 0)),
                  pl.BlockSpec((1, 1, s, HEAD_DIM), lambda bi, hi, i: (bi, hi, 0, 0))],
        out_specs=pl.BlockSpec((1, 1, tq, nb), lambda bi, hi, i: (bi, hi, i, 0)),
        out_shape=jax.ShapeDtypeStruct((b, h, s, nb), BF16),
        scratch_shapes=[pltpu.VMEM((nb, HEAD_DIM), F32)],
        compiler_params=_cparams("parallel", "parallel", "arbitrary"),
        name="moba_select",
    )(q, k)


def _flash_kernel(*refs, tq, tk, rs, tps, band, has_extra):
    if has_extra:
        qa_ref, ex_ref, kt_ref, v_ref, o_ref, m_sc, acc_sc = refs
    else:
        qa_ref, kt_ref, v_ref, o_ref, m_sc, acc_sc = refs
        ex_ref = None
    i = pl.program_id(2)
    m_sc[...] = jnp.full(m_sc.shape, M_INIT, F32)
    acc_sc[...] = jnp.zeros(acc_sc.shape, F32)
    row = lax.broadcasted_iota(jnp.int32, (rs, tk), 0)
    col = lax.broadcasted_iota(jnp.int32, (rs, tk), 1)
    nst = tq // rs
    kpq = tq // tk

    def run(items):
        tiles = {}

        def operands(j):
            if id(j) not in tiles:
                start = pl.multiple_of(j * tk, tk)
                tiles[id(j)] = (kt_ref[0, 0, :, pl.ds(start, tk)], v_ref[0, 0, pl.ds(start, tk), :])
            return tiles[id(j)]

        def logits(item):
            j, r, _ = item
            rows = pl.ds(r * rs, rs)
            qa = qa_ref[0, 0, rows, :]
            if ex_ref is not None:
                qa = qa + ex_ref[0, 0, j // tps, rows, :]
            return jnp.dot(qa, operands(j)[0], preferred_element_type=F32)

        s_next = logits(items[0])
        for n, (j, r, mask) in enumerate(items):
            rows = pl.ds(r * rs, rs)
            s = s_next
            if n + 1 < len(items):
                s_next = logits(items[n + 1])
            if mask is not None:
                s = jnp.where(mask, s, MASKED)
            m_prev = m_sc[rows, :]
            m_new = jnp.maximum(m_prev, jnp.max(s, axis=-1, keepdims=True))
            p = jnp.exp(s - jnp.tile(m_new, (1, tk // LANES)))
            alpha = jnp.exp(m_prev - m_new)
            acc_sc[rows, :] = alpha * acc_sc[rows, :] + jnp.dot(
                p.astype(BF16), operands(j)[1], preferred_element_type=F32)
            m_sc[rows, :] = m_new

    def edge_items(dj_list, j_of):
        items = []
        for dj in dj_list:
            j = j_of(dj)
            for r in range(nst):
                off, ko = r * rs, dj * tk
                lo = off - WINDOW + 1 if band else None
                if ko > off + rs - 1 or (band and ko + tk - 1 < lo):
                    continue
                full = ko + tk - 1 <= off and (not band or ko > off + rs - 1 - WINDOW)
                mask = None
                if not full:
                    mask = col + ko <= row + off
                    if band:
                        mask = mask & (col + ko > row + (off - WINDOW))
                items.append((j, r, mask))
        return items

    if band:
        for dj in range(-(WINDOW // tk), 0):
            @pl.when(i * kpq + dj >= 0)
            def _():
                jj = i * kpq + dj
                run(edge_items([dj], lambda _: jj))
    else:
        def body(jj, carry):
            tiles = [jj * kpq + dj for dj in range(kpq)]
            run([(j, r, None) for j in tiles for r in range(nst)])
            return carry
        lax.fori_loop(0, i, body, 0)
    diag = [i * kpq + dj for dj in range(kpq)]
    run(edge_items(list(range(kpq)), lambda dj: diag[dj]))
    acc = acc_sc[...]
    o_ref[0, 0] = acc[:, :HEAD_DIM] / acc[:, HEAD_DIM:HEAD_DIM + 1]


def flash(qa, kta, v, extra=None, band=False):
    b, h, s, _ = qa.shape
    hk = kta.shape[1]
    rep = h // hk
    tk = min(WINDOW if band else FLASH_TK, s)
    tq = tk if band else min(FLASH_TQ, s)
    rs = min(FLASH_STRIP, tq)
    assert s % tq == 0 and tq % tk == 0 and tq % rs == 0 and (not band or WINDOW % tk == 0)
    in_specs = [pl.BlockSpec((1, 1, tq, AUG), lambda bi, hi, i: (bi, hi, i, 0))]
    args = [qa]
    tps = 1
    if extra is not None:
        he, nsup = extra.shape[1], extra.shape[2]
        rep_e = h // he
        assert (s // nsup) % tk == 0
        tps = (s // nsup) // tk
        in_specs.append(pl.BlockSpec((1, 1, nsup, tq, AUG), lambda bi, hi, i: (bi, hi // rep_e, 0, i, 0)))
        args.append(extra)
    in_specs += [pl.BlockSpec((1, 1, AUG, s), lambda bi, hi, i: (bi, hi // rep, 0, 0)),
                 pl.BlockSpec((1, 1, s, AUG), lambda bi, hi, i: (bi, hi // rep, 0, 0))]
    args += [kta, v]
    return pl.pallas_call(
        functools.partial(_flash_kernel, tq=tq, tk=tk, rs=rs, tps=tps, band=band,
                          has_extra=extra is not None),
        grid=(b, h, s // tq),
        in_specs=in_specs,
        out_specs=pl.BlockSpec((1, 1, tq, HEAD_DIM), lambda bi, hi, i: (bi, hi, i, 0)),
        out_shape=jax.ShapeDtypeStruct((b, h, s, HEAD_DIM), F32),
        scratch_shapes=[pltpu.VMEM((tq, LANES), F32), pltpu.VMEM((tq, AUG), F32)],
        compiler_params=_cparams("parallel", "parallel", "parallel"),
        name="flash_band" if band else ("flash_sel" if extra is not None else "flash"),
    )(*args)


def _nsa_combine_kernel(g_ref, oc_ref, os_ref, ow_ref, o_ref):
    g = jax.nn.sigmoid(g_ref[0, 0])
    o = g[:, 0:1] * oc_ref[0, 0] + g[:, 1:2] * os_ref[0, 0] + g[:, 2:3] * ow_ref[0, 0]
    o_ref[0, 0] = o.astype(o_ref.dtype)


def nsa_combine(gl, oc, osl, ow, ts=2048):
    b, h, s, d = oc.shape
    ts = min(ts, s)
    spec = pl.BlockSpec((1, 1, ts, d), lambda bi, hi, i: (bi, hi, i, 0))
    return pl.pallas_call(
        _nsa_combine_kernel,
        grid=(b, h, s // ts),
        in_specs=[pl.BlockSpec((1, 1, ts, 3), lambda bi, hi, i: (bi, hi, i, 0)), spec, spec, spec],
        out_specs=spec,
        out_shape=jax.ShapeDtypeStruct((b, h, s, d), BF16),
        compiler_params=_cparams("parallel", "parallel", "parallel"),
        name="nsa_combine",
    )(gl, oc, osl, ow)


def _proj_norm_res_kernel(a_ref, w_ref, g_ref, x_ref, o_ref):
    y = jnp.dot(a_ref[...], w_ref[...], preferred_element_type=F32)
    o_ref[...] = x_ref[...] + _rms(y, g_ref[...])


def proj_norm_res(a, w, g, x, tm=512):
    n, k = a.shape
    d = w.shape[1]
    return pl.pallas_call(
        _proj_norm_res_kernel,
        grid=(n // tm,),
        in_specs=[pl.BlockSpec((tm, k), lambda i: (i, 0)),
                  pl.BlockSpec((k, d), lambda i: (0, 0)),
                  pl.BlockSpec((1, d), lambda i: (0, 0)),
                  pl.BlockSpec((tm, d), lambda i: (i, 0))],
        out_specs=pl.BlockSpec((tm, d), lambda i: (i, 0)),
        out_shape=jax.ShapeDtypeStruct((n, d), F32),
        compiler_params=_cparams("parallel"),
        name="proj_norm_res",
    )(a, w, g.reshape(1, d), x)


def _ffn_kernel(x_ref, gpre_ref, wg_ref, wu_ref, wd_ref, gpost_ref, o_ref, h_sc, acc_sc):
    f = pl.program_id(1)

    @pl.when(f == 0)
    def _():
        h_sc[...] = _rms(x_ref[...], gpre_ref[...]).astype(BF16)
        acc_sc[...] = jnp.zeros(acc_sc.shape, F32)

    h = h_sc[...]
    a = jnp.dot(h, wg_ref[...], preferred_element_type=F32)
    u = jnp.dot(h, wu_ref[...], preferred_element_type=F32)
    act = (jax.nn.silu(a) * u).astype(BF16)
    acc_sc[...] += jnp.dot(act, wd_ref[...], preferred_element_type=F32)

    @pl.when(f == pl.num_programs(1) - 1)
    def _():
        o_ref[...] = x_ref[...] + _rms(acc_sc[...], gpost_ref[...])


def ffn(x, gpre, wg, wu, wd, gpost, tm=512, tf=256):
    n, d = x.shape
    dff = wg.shape[1]
    assert n % tm == 0 and dff % tf == 0
    return pl.pallas_call(
        _ffn_kernel,
        grid=(n // tm, dff // tf),
        in_specs=[pl.BlockSpec((tm, d), lambda i, f: (i, 0)),
                  pl.BlockSpec((1, d), lambda i, f: (0, 0)),
                  pl.BlockSpec((d, tf), lambda i, f: (0, f)),
                  pl.BlockSpec((d, tf), lambda i, f: (0, f)),
                  pl.BlockSpec((tf, d), lambda i, f: (f, 0)),
                  pl.BlockSpec((1, d), lambda i, f: (0, 0))],
        out_specs=pl.BlockSpec((tm, d), lambda i, f: (i, 0)),
        out_shape=jax.ShapeDtypeStruct((n, d), F32),
        scratch_shapes=[pltpu.VMEM((tm, d), BF16), pltpu.VMEM((tm, d), F32)],
        compiler_params=_cparams("parallel", "arbitrary"),
        name="ffn",
    )(x, gpre.reshape(1, d), wg, wu, wd, gpost.reshape(1, d))


def _rope_tables(s):
    inv = ROPE_THETA ** (-jnp.arange(0, HEAD_DIM, 2, dtype=F32) / HEAD_DIM)
    ang = jnp.arange(s, dtype=F32)[:, None] * inv[None, :]
    cos, sin = jnp.cos(ang), jnp.sin(ang)
    reps = LANES // HEAD_DIM
    return (jnp.tile(jnp.concatenate([cos, cos], -1), (1, reps)),
            jnp.tile(jnp.concatenate([-sin, sin], -1), (1, reps)))


def _heads(t, b, s):
    return t.reshape(b, s, -1, HEAD_DIM).transpose(0, 2, 1, 3)


def _heads_t(t, b, s):
    return t.reshape(b, s, -1, HEAD_DIM).transpose(0, 2, 3, 1)


def _unheads(t):
    b, h, s, d = t.shape
    return t.transpose(0, 2, 1, 3).reshape(b * s, h * d)


def _ones_col(v):
    one = jnp.ones(v.shape[:-1] + (1,), v.dtype)
    return jnp.concatenate([v, one, jnp.zeros(v.shape[:-1] + (HEAD_DIM - 1,), v.dtype)], axis=-1)


def _pad_q(q):
    return jnp.concatenate([q, jnp.zeros_like(q)], axis=-1)


def _block_indicator(n_rows, s, block):
    blk = (np.arange(s) // block) % n_rows
    return jnp.asarray(blk[None, :] == np.arange(n_rows)[:, None], BF16)


def _overlap_matrix(ncp, ns):
    ratio = SLC_BLOCK // CMP_STRIDE
    m = np.arange(ncp)[:, None]
    j = np.arange(ns)[None, :]
    ok = (m >= ratio * j - 1) & (m <= ratio * j + ratio - 1) & (m < ncp - 1)
    return jnp.asarray(ok, BF16)


def _even_mixer(x2, b, s, g_pre, w_in, b_f, pe_k, w1_k, w2_k, pe_v, w1_v, w2_v, rope_tabs):
    hf, hn, g = 8, 8, 2
    hd = HEAD_DIM
    (w_fq, w_fk, w_fv, w_fl, w_nq, w_kc, w_vc, w_ks, w_vs, w_kw, w_vw, w_gl) = jnp.split(
        w_in, list(np.cumsum([hf * hd] * 3 + [hf] + [hn * hd] + [g * hd] * 6)), axis=1)
    scale = hd ** -0.5

    def colscale(widths_scales):
        return jnp.concatenate([jnp.full((w,), v, F32) for w, v in widths_scales])

    w_rope = jnp.concatenate([w_nq, w_kc, w_ks, w_kw], axis=1).astype(BF16)
    p_rope = norm_proj(x2, g_pre, w_rope, colscale([(hn * hd, scale), (3 * g * hd, 1.0)]),
                       rope_tabs, s, BF16)
    w_plain = jnp.concatenate([w_fq, w_fk, w_fv, w_vc, w_vs, w_vw], axis=1).astype(BF16)
    p_plain = norm_proj(x2, g_pre, w_plain, colscale([(hf * hd, scale), (2 * hf * hd + 3 * g * hd, 1.0)]),
                        None, s, BF16)
    n_small = hf + 3 * hn
    w_small = jnp.pad(jnp.concatenate([w_fl, w_gl], axis=1), ((0, 0), (0, LANES - n_small))).astype(BF16)
    p_small = norm_proj(x2, g_pre, w_small, jnp.ones((LANES,), F32), None, s, F32)

    nq, kc, ks, kw = jnp.split(p_rope, [hn * hd, (hn + g) * hd, (hn + 2 * g) * hd], axis=1)
    fq, fk, fv, vc, vs, vw = jnp.split(
        p_plain, list(np.cumsum([hf * hd] * 3 + [g * hd] * 2)), axis=1)

    fl_t = p_small[:, :hf].reshape(b, s, hf).transpose(0, 2, 1)
    dec = fox_decay(fl_t, b_f)
    qx = jnp.concatenate([dec[:, :, 6:7], dec[:, :, 6:7], dec[:, :, 6:7], dec[:, :, 0:3]], axis=2)
    kx = jnp.concatenate([dec[:, :, 3:6], dec[:, :, 6:7], dec[:, :, 6:7], dec[:, :, 6:7]], axis=2)
    pad_q = jnp.zeros((b, hf, s, hd - 6), BF16)
    pad_k = jnp.zeros((b, hf, hd - 6, s), BF16)
    qa = jnp.concatenate([_heads(fq, b, s), qx.transpose(0, 1, 3, 2), pad_q], axis=-1)
    kta = jnp.concatenate([_heads_t(fk, b, s), kx, pad_k], axis=2)
    o_fox = flash(qa, kta, _ones_col(_heads(fv, b, s)))

    q = _heads(nq, b, s)
    ncp = s // CMP_STRIDE
    cw = CMP_STRIDE * hd

    def compress(t, pe, w1, w2):
        tt = _heads(t, b, s).reshape(b, g, ncp, cw)
        return nsa_compress(tt, pe.reshape(2, cw), w1.reshape(2, cw, -1).astype(BF16), w2.astype(BF16))

    kcmp = compress(kc, pe_k, w1_k, w2_k)
    vcmp = compress(vc, pe_v, w1_v, w2_v)
    ns = s // SLC_BLOCK
    o_cmp, selb = nsa_select(q, kcmp.transpose(0, 1, 3, 2), vcmp, _overlap_matrix(ncp, ns))
    qa = _pad_q(q)
    zer = jnp.zeros((b, g, hd, s), BF16)
    ind = jnp.broadcast_to(_block_indicator(hd, s, SLC_BLOCK), (b, g, hd, s))
    o_slc = flash(qa, jnp.concatenate([_heads_t(ks, b, s), ind], axis=2), _ones_col(_heads(vs, b, s)),
                  extra=selb)
    o_win = flash(qa, jnp.concatenate([_heads_t(kw, b, s), zer], axis=2), _ones_col(_heads(vw, b, s)),
                  band=True)
    gl = p_small[:, hf:n_small].reshape(b, s, hn, 3).transpose(0, 2, 1, 3)
    o_nsa = nsa_combine(gl, o_cmp, o_slc, o_win)
    return jnp.concatenate([_unheads(o_fox.astype(BF16)), _unheads(o_nsa)], axis=-1)


def _odd_mixer(x2, b, s, g_pre, w_in, rope_tabs):
    h = 16
    hd = HEAD_DIM
    d = h * hd
    scale = hd ** -0.5
    cs = jnp.concatenate([jnp.full((d,), scale, F32), jnp.ones((d,), F32)])
    p_qk = norm_proj(x2, g_pre, w_in[:, :2 * d].astype(BF16), cs, rope_tabs, s, BF16)
    p_v = norm_proj(x2, g_pre, w_in[:, 2 * d:].astype(BF16), jnp.ones((d,), F32), None, s, BF16)
    q = _heads(p_qk[:, :d], b, s)
    selb = moba_select(q, _heads(p_qk[:, d:], b, s))
    nb = selb.shape[-1]
    if nb < hd:
        selb = jnp.pad(selb, ((0, 0), (0, 0), (0, 0), (0, hd - nb)))
    qa = jnp.concatenate([q, selb], axis=-1)
    ind = jnp.broadcast_to(_block_indicator(hd, s, MOBA_BLOCK), (b, h, hd, s))
    if nb < hd:
        ind = ind * jnp.asarray(np.arange(hd)[:, None] < nb, BF16)
    kta = jnp.concatenate([_heads_t(p_qk[:, d:], b, s), ind], axis=2)
    o = flash(qa, kta, _ones_col(_heads(p_v, b, s)))
    return _unheads(o.astype(BF16))


def kernel(x, ev_w_in, ev_b_f, ev_cmp_pe_k, ev_cmp_w1_k, ev_cmp_w2_k, ev_cmp_pe_v, ev_cmp_w1_v,
           ev_cmp_w2_v, ev_w_out, od_w_in, od_w_out, g_mix_pre, g_mix_post, g_ffn_pre, g_ffn_post,
           ffn_w_gate, ffn_w_up, ffn_w_down):
    b, s, d = x.shape
    depth = g_mix_pre.shape[0]
    rope_tabs = _rope_tables(s)
    x2 = x.reshape(b * s, d)
    for layer in range(depth):
        if layer % 2 == 0:
            e = layer // 2
            o = _even_mixer(x2, b, s, g_mix_pre[layer], ev_w_in[e], ev_b_f[e], ev_cmp_pe_k[e],
                            ev_cmp_w1_k[e], ev_cmp_w2_k[e], ev_cmp_pe_v[e], ev_cmp_w1_v[e],
                            ev_cmp_w2_v[e], rope_tabs)
            w_out = ev_w_out[e]
        else:
            o = _odd_mixer(x2, b, s, g_mix_pre[layer], od_w_in[layer // 2], rope_tabs)
            w_out = od_w_out[layer // 2]
        x2 = proj_norm_res(o, w_out.astype(BF16), g_mix_post[layer], x2)
        x2 = ffn(x2, g_ffn_pre[layer], ffn_w_gate[layer].astype(BF16), ffn_w_up[layer].astype(BF16),
                 ffn_w_down[layer].astype(BF16), g_ffn_post[layer])
    return x2.reshape(b, s, d)
```

```python
import functools

import jax
import jax.numpy as jnp
import numpy as np
from jax import lax
from jax.experimental import pallas as pl
from jax.experimental.pallas import tpu as pltpu

F32 = jnp.float32
BF16 = jnp.bfloat16

HEAD_DIM = 64
LANES = 128
ROPE_THETA = 10000.0
RMS_EPS = 1e-6
CMP_STRIDE = 16
CMP_LEN = 32
SLC_BLOCK = 64
SLC_TOPN = 16
WINDOW = 512
MOBA_BLOCK = 256
MOBA_TOPK = 3
MASKED = -1e30
M_INIT = -1e29
FLASH_TQ = 1024
FLASH_TK = 512
FLASH_STRIP = 512
VMEM_LIMIT = 48 * 1024 * 1024
NT_DIMS = (((1,), (1,)), ((), ()))


def _cparams(*sem):
    return pltpu.CompilerParams(dimension_semantics=sem, vmem_limit_bytes=VMEM_LIMIT)


def _rms(x, g):
    return x * lax.rsqrt(jnp.mean(x * x, axis=-1, keepdims=True) + RMS_EPS) * g


def _split3(x):
    hi = x.astype(BF16).astype(F32)
    r = x - hi
    mid = r.astype(BF16).astype(F32)
    lo = (r - mid).astype(BF16).astype(F32)
    return hi, mid, lo


def _col_tile(nc, cap):
    best = LANES
    for k in range(1, nc // LANES + 1):
        if nc % (k * LANES) == 0 and k * LANES <= cap:
            best = k * LANES
    return best


def _norm_proj_kernel(x_ref, g_ref, w_ref, b_ref, *rest, rope):
    if rope:
        cos_ref, sin_ref, o_ref, h_sc = rest
    else:
        o_ref, h_sc = rest

    @pl.when(pl.program_id(1) == 0)
    def _():
        h_sc[...] = _rms(x_ref[...], g_ref[...]).astype(BF16)

    t = jnp.dot(h_sc[...], w_ref[...], preferred_element_type=F32)
    if rope:
        tn = t.shape[1]
        reps = tn // LANES
        cos = jnp.tile(cos_ref[...], (1, reps))
        sin = jnp.tile(sin_ref[...], (1, reps))
        lane = lax.broadcasted_iota(jnp.int32, t.shape, 1)
        first = (lane & (HEAD_DIM - 1)) < HEAD_DIM // 2
        rot = jnp.where(first, pltpu.roll(t, tn - HEAD_DIM // 2, 1), pltpu.roll(t, HEAD_DIM // 2, 1))
        t = t * cos + rot * sin
    o_ref[...] = (t + b_ref[...]).astype(o_ref.dtype)


def norm_proj(x, g, w, bias, rope_tabs, seq, out_dtype, tm=512):
    n, d = x.shape
    nc = w.shape[1]
    tn = _col_tile(nc, 1536)
    assert n % tm == 0 and seq % tm == 0
    rope = rope_tabs is not None
    in_specs = [
        pl.BlockSpec((tm, d), lambda i, j: (i, 0)),
        pl.BlockSpec((1, d), lambda i, j: (0, 0)),
        pl.BlockSpec((d, tn), lambda i, j: (0, j)),
        pl.BlockSpec((1, tn), lambda i, j: (0, j)),
    ]
    args = [x, g.reshape(1, d), w, bias.reshape(1, nc)]
    if rope:
        spt = seq // tm
        in_specs += [pl.BlockSpec((tm, LANES), lambda i, j: (i % spt, 0))] * 2
        args += list(rope_tabs)
    return pl.pallas_call(
        functools.partial(_norm_proj_kernel, rope=rope),
        grid=(n // tm, nc // tn),
        in_specs=in_specs,
        out_specs=pl.BlockSpec((tm, tn), lambda i, j: (i, j)),
        out_shape=jax.ShapeDtypeStruct((n, nc), out_dtype),
        scratch_shapes=[pltpu.VMEM((tm, d), BF16)],
        compiler_params=_cparams("parallel", "arbitrary"),
        name="norm_proj_rope" if rope else "norm_proj",
    )(*args)


def _norm_proj_t_kernel(x_ref, g_ref, wt_ref, *rest, rope):
    if rope:
        cos_ref, sin_ref, o_ref, h_sc = rest
    else:
        o_ref, h_sc = rest

    @pl.when(pl.program_id(1) == 0)
    def _():
        h_sc[...] = _rms(x_ref[...], g_ref[...]).astype(BF16)

    t = lax.dot_general(wt_ref[...], h_sc[...], NT_DIMS, preferred_element_type=F32)
    if rope:
        tn = t.shape[0]
        reps = tn // LANES
        cos = jnp.tile(cos_ref[...], (reps, 1))
        sin = jnp.tile(sin_ref[...], (reps, 1))
        r = lax.broadcasted_iota(jnp.int32, t.shape, 0)
        first = (r & (HEAD_DIM - 1)) < HEAD_DIM // 2
        rot = jnp.where(first, pltpu.roll(t, tn - HEAD_DIM // 2, 0), pltpu.roll(t, HEAD_DIM // 2, 0))
        t = t * cos + rot * sin
    o_ref[0] = t.astype(o_ref.dtype)


def norm_proj_t(x, g, wt, rope_tabs_t, b, seq, tm=512):
    n, d = x.shape
    nc = wt.shape[0]
    tn = _col_tile(nc, 1024)
    assert n % tm == 0 and seq % tm == 0
    spt = seq // tm
    rope = rope_tabs_t is not None
    in_specs = [
        pl.BlockSpec((tm, d), lambda i, j: (i, 0)),
        pl.BlockSpec((1, d), lambda i, j: (0, 0)),
        pl.BlockSpec((tn, d), lambda i, j: (j, 0)),
    ]
    args = [x, g.reshape(1, d), wt]
    if rope:
        in_specs += [pl.BlockSpec((LANES, tm), lambda i, j: (0, i % spt))] * 2
        args += list(rope_tabs_t)
    return pl.pallas_call(
        functools.partial(_norm_proj_t_kernel, rope=rope),
        grid=(n // tm, nc // tn),
        in_specs=in_specs,
        out_specs=pl.BlockSpec((1, tn, tm), lambda i, j: (i // spt, j, i % spt)),
        out_shape=jax.ShapeDtypeStruct((b, nc, seq), BF16),
        scratch_shapes=[pltpu.VMEM((tm, d), BF16)],
        compiler_params=_cparams("parallel", "arbitrary"),
        name="norm_proj_t_rope" if rope else "norm_proj_t",
    )(*args)


def _fox_decay_kernel(fl_ref, b_ref, pq_ref, pk_ref, oq_ref, ok_ref, carry_sc, *, nh):
    @pl.when(pl.program_id(1) == 0)
    def _():
        carry_sc[...] = jnp.zeros(carry_sc.shape, F32)

    c = jax.nn.log_sigmoid(fl_ref[0] + b_ref[...])
    ts = c.shape[0]
    row = lax.broadcasted_iota(jnp.int32, c.shape, 0)
    lane = lax.broadcasted_iota(jnp.int32, c.shape, 1)
    sh = 1
    while sh < ts:
        c = c + jnp.where(row >= sh, pltpu.roll(c, sh, 0), 0.0)
        sh *= 2
    c = c + carry_sc[0:1, :]
    carry_sc[0:1, :] = c[ts - 1:ts, :]
    hi, mid, lo = _split3(c)
    c3 = jnp.where(lane < nh, hi,
                   jnp.where(lane < 2 * nh, pltpu.roll(mid, nh, 1),
                             jnp.where(lane < 3 * nh, pltpu.roll(lo, 2 * nh, 1),
                                       jnp.where(lane == 3 * nh, 1.0, 0.0)))).astype(BF16)
    for h in range(nh):
        oq_ref[0, h, 0] = jnp.dot(c3, pq_ref[h], preferred_element_type=F32).astype(oq_ref.dtype)
        ok_ref[0, h] = lax.dot_general(pk_ref[h], c3, NT_DIMS,
                                       preferred_element_type=F32).astype(ok_ref.dtype)


def _fox_placement(nh):
    pq = np.zeros((nh, LANES, LANES), np.float32)
    pk = np.zeros((nh, LANES, LANES), np.float32)
    one = 3 * nh
    for h in range(nh):
        for t in range(3):
            pq[h, one, HEAD_DIM + t] = 1.0
            pq[h, t * nh + h, HEAD_DIM + 3 + t] = 1.0
            pk[h, HEAD_DIM + t, t * nh + h] = -1.0
            pk[h, HEAD_DIM + 3 + t, one] = 1.0
    return jnp.asarray(pq, BF16), jnp.asarray(pk, BF16)


def fox_decay(small, b_row, nh, ts=2048):
    b, s, _ = small.shape
    ts = min(ts, s)
    pq, pk = _fox_placement(nh)
    return pl.pallas_call(
        functools.partial(_fox_decay_kernel, nh=nh),
        grid=(b, s // ts),
        in_specs=[pl.BlockSpec((1, ts, LANES), lambda i, j: (i, j, 0)),
                  pl.BlockSpec((1, LANES), lambda i, j: (0, 0)),
                  pl.BlockSpec((nh, LANES, LANES), lambda i, j: (0, 0, 0)),
                  pl.BlockSpec((nh, LANES, LANES), lambda i, j: (0, 0, 0))],
        out_specs=[pl.BlockSpec((1, nh, 1, ts, LANES), lambda i, j: (i, 0, 0, j, 0)),
                   pl.BlockSpec((1, nh, LANES, ts), lambda i, j: (i, 0, 0, j))],
        out_shape=[jax.ShapeDtypeStruct((b, nh, 1, s, LANES), BF16),
                   jax.ShapeDtypeStruct((b, nh, LANES, s), BF16)],
        scratch_shapes=[pltpu.VMEM((8, LANES), F32)],
        compiler_params=_cparams("parallel", "arbitrary"),
        name="fox_decay",
    )(small, b_row, pq, pk)


def _nsa_compress_kernel(t_ref, pe_ref, w1_ref, w2_ref, o_ref):
    t = t_ref[0].astype(F32)
    ncp = t.shape[0]
    a = jnp.dot((t + pe_ref[0:1, :]).astype(BF16), w1_ref[0, 0], preferred_element_type=F32)
    bm = jnp.dot((t + pe_ref[1:2, :]).astype(BF16), w1_ref[0, 1], preferred_element_type=F32)
    pre = a + pltpu.roll(bm, ncp - 1, 0)
    hid = jax.nn.gelu(pre)
    o_ref[0, 0] = jnp.dot(hid.astype(BF16), w2_ref[...], preferred_element_type=F32).astype(o_ref.dtype)


def nsa_compress(t, pe, w1, w2):
    b, ncp, cw = t.shape
    g, _, _, hid = w1.shape
    return pl.pallas_call(
        _nsa_compress_kernel,
        grid=(b, g),
        in_specs=[pl.BlockSpec((1, ncp, cw), lambda i, j: (i, 0, 0)),
                  pl.BlockSpec((2, cw), lambda i, j: (0, 0)),
                  pl.BlockSpec((1, 2, cw, hid), lambda i, j: (j, 0, 0, 0)),
                  pl.BlockSpec((hid, LANES), lambda i, j: (0, 0))],
        out_specs=pl.BlockSpec((1, 1, ncp, LANES), lambda i, j: (i, j, 0, 0)),
        out_shape=jax.ShapeDtypeStruct((b, g, ncp, LANES), BF16),
        compiler_params=_cparams("parallel", "parallel"),
        name="nsa_compress",
    )(t, pe, w1, w2)


def _topk_mask(work, col, k):
    sel = jnp.zeros(work.shape, jnp.bool_)
    col = col.astype(F32)
    for _ in range(k):
        mx = jnp.max(work, axis=-1, keepdims=True)
        first = jnp.min(jnp.where(work == mx, col, 1e9), axis=-1, keepdims=True)
        hit = col == first
        sel = jnp.logical_or(sel, hit)
        work = jnp.where(hit, -jnp.inf, work)
    return sel


def _nsa_select_kernel(q_ref, kct_ref, vc_ref, m_ref, oc_ref, sb_ref, *, tq, rep, nsup):
    i = pl.program_id(2)
    ncp = kct_ref.shape[-1]
    ns = m_ref.shape[-1]
    qpos = i * tq + lax.broadcasted_iota(jnp.int32, (tq, 1), 0)
    cend = lax.broadcasted_iota(jnp.int32, (1, ncp), 1) * CMP_STRIDE + (CMP_LEN - 1)
    cmask = cend <= qpos
    kct = kct_ref[0, 0]
    vc = vc_ref[0, 0]
    pcs = jnp.zeros((tq, ncp), F32)
    for r in range(rep):
        s = jnp.dot(q_ref[0, :, r * LANES:(r + 1) * LANES], kct, preferred_element_type=F32)
        s = jnp.where(cmask, s, MASKED)
        e = jnp.where(cmask, jnp.exp(s - jnp.max(s, axis=-1, keepdims=True)), 0.0)
        p = e / jnp.maximum(jnp.sum(e, axis=-1, keepdims=True), 1e-30)
        oc_ref[0, :, r * LANES:(r + 1) * LANES] = jnp.dot(
            p.astype(BF16), vc, preferred_element_type=F32).astype(oc_ref.dtype)
        pcs = pcs + p
    mm = m_ref[...]
    imp = sum(jnp.dot(part.astype(BF16), mm, preferred_element_type=F32) for part in _split3(pcs))
    sblk = lax.broadcasted_iota(jnp.int32, (1, ns), 1)
    qblk = qpos >> 6
    forced = (sblk == 0) | (sblk == qblk) | (sblk == qblk - 1)
    imp = jnp.where(forced, 1e9, jnp.where(sblk > qblk, -1e9, imp))
    sel = _topk_mask(imp, sblk, min(SLC_TOPN, ns))
    bias = jnp.where(sel & (sblk <= qblk), 0.0, MASKED).astype(sb_ref.dtype)
    zeros = jnp.zeros((tq, HEAD_DIM), sb_ref.dtype)
    for j in range(nsup):
        sb_ref[0, 0, j] = jnp.concatenate([zeros, bias[:, j * HEAD_DIM:(j + 1) * HEAD_DIM]], axis=-1)


def nsa_select(q_all, kct, vc, m, rep, tq=512):
    b, s, _ = q_all.shape
    g = kct.shape[1]
    ncp = kct.shape[-1]
    ns = m.shape[-1]
    nsup = ns // HEAD_DIM
    assert SLC_BLOCK == 64 and ns % HEAD_DIM == 0
    tq = min(tq, s)
    return pl.pallas_call(
        functools.partial(_nsa_select_kernel, tq=tq, rep=rep, nsup=nsup),
        grid=(b, g, s // tq),
        in_specs=[pl.BlockSpec((1, tq, rep * LANES), lambda bi, gi, i: (bi, i, gi)),
                  pl.BlockSpec((1, 1, LANES, ncp), lambda bi, gi, i: (bi, gi, 0, 0)),
                  pl.BlockSpec((1, 1, ncp, LANES), lambda bi, gi, i: (bi, gi, 0, 0)),
                  pl.BlockSpec((ncp, ns), lambda bi, gi, i: (0, 0))],
        out_specs=[pl.BlockSpec((1, tq, rep * LANES), lambda bi, gi, i: (bi, i, gi)),
                   pl.BlockSpec((1, 1, nsup, tq, LANES), lambda bi, gi, i: (bi, gi, 0, i, 0))],
        out_shape=[jax.ShapeDtypeStruct((b, s, g * rep * LANES), BF16),
                   jax.ShapeDtypeStruct((b, g, nsup, s, LANES), BF16)],
        compiler_params=_cparams("parallel", "parallel", "parallel"),
        name="nsa_select",
    )(q_all, kct, vc, m)


def _moba_select_kernel(q_ref, kt_ref, ind_ref, sb_ref, kbar_sc, *, tq):
    i = pl.program_id(2)

    @pl.when(i == 0)
    def _():
        kbar_sc[...] = jnp.dot(kt_ref[0], ind_ref[...], preferred_element_type=F32) * (1.0 / MOBA_BLOCK)

    q = q_ref[0]
    gate = sum(jnp.dot(q, part.astype(BF16), preferred_element_type=F32)
               for part in _split3(kbar_sc[...]))
    qpos = i * tq + lax.broadcasted_iota(jnp.int32, (tq, 1), 0)
    cur = qpos >> 8
    lane = lax.broadcasted_iota(jnp.int32, (1, LANES), 1)
    blk = lane - HEAD_DIM
    past = (blk >= 0) & (blk < cur)
    sel = _topk_mask(jnp.where(past, gate, -jnp.inf), lane, MOBA_TOPK)
    keep = (blk < 0) | (sel & past) | (blk == cur)
    sb_ref[0, 0, 0] = jnp.where(keep, 0.0, MASKED).astype(sb_ref.dtype)


def moba_select(q_all, kt_all, ind, nh, tq=512):
    b, s, _ = q_all.shape
    assert MOBA_BLOCK == 256 and s // MOBA_BLOCK <= HEAD_DIM
    tq = min(tq, s)
    return pl.pallas_call(
        functools.partial(_moba_select_kernel, tq=tq),
        grid=(b, nh, s // tq),
        in_specs=[pl.BlockSpec((1, tq, LANES), lambda bi, hi, i: (bi, i, hi)),
                  pl.BlockSpec((1, LANES, s), lambda bi, hi, i: (bi, hi, 0)),
                  pl.BlockSpec((s, LANES), lambda bi, hi, i: (0, 0))],
        out_specs=pl.BlockSpec((1, 1, 1, tq, LANES), lambda bi, hi, i: (bi, hi, 0, i, 0)),
        out_shape=jax.ShapeDtypeStruct((b, nh, 1, s, LANES), BF16),
        scratch_shapes=[pltpu.VMEM((LANES, LANES), F32)],
        compiler_params=_cparams("parallel", "parallel", "arbitrary"),
        name="moba_select",
    )(q_all, kt_all, ind)


def _flash_kernel(*refs, tq, tk, rs, tps, band, has_extra, has_kx):
    refs = list(refs)
    qa_ref = refs.pop(0)
    ex_ref = refs.pop(0) if has_extra else None
    kt_ref = refs.pop(0)
    kx_ref = refs.pop(0) if has_kx else None
    v_ref, o_ref, m_sc, acc_sc = refs
    i = pl.program_id(2)
    m_sc[...] = jnp.full(m_sc.shape, M_INIT, F32)
    acc_sc[...] = jnp.zeros(acc_sc.shape, F32)
    row = lax.broadcasted_iota(jnp.int32, (rs, tk), 0)
    col = lax.broadcasted_iota(jnp.int32, (rs, tk), 1)
    nst = tq // rs
    kpq = tq // tk

    def run(items):
        tiles = {}

        def operands(j):
            if id(j) not in tiles:
                start = pl.multiple_of(j * tk, tk)
                kt = kt_ref[0, :, pl.ds(start, tk)]
                if kx_ref is not None:
                    kt = kt + kx_ref[0, 0, :, pl.ds(start, tk)]
                tiles[id(j)] = (kt, v_ref[0, pl.ds(start, tk), :])
            return tiles[id(j)]

        def logits(item):
            j, r, _ = item
            rows = pl.ds(r * rs, rs)
            qa = qa_ref[0, rows, :]
            if ex_ref is not None:
                qa = qa + ex_ref[0, 0, j // tps, rows, :]
            return jnp.dot(qa, operands(j)[0], preferred_element_type=F32)

        s_next = logits(items[0])
        for n, (j, r, mask) in enumerate(items):
            rows = pl.ds(r * rs, rs)
            s = s_next
            if n + 1 < len(items):
                s_next = logits(items[n + 1])
            if mask is not None:
                s = jnp.where(mask, s, MASKED)
            m_prev = m_sc[rows, :]
            m_new = jnp.maximum(m_prev, jnp.max(s, axis=-1, keepdims=True))
            p = jnp.exp(s - jnp.tile(m_new, (1, tk // LANES)))
            alpha = jnp.exp(m_prev - m_new)
            acc_sc[rows, :] = alpha * acc_sc[rows, :] + jnp.dot(
                p.astype(BF16), operands(j)[1], preferred_element_type=F32)
            m_sc[rows, :] = m_new

    def edge_items(dj_list, j_of):
        items = []
        for dj in dj_list:
            j = j_of(dj)
            for r in range(nst):
                off, ko = r * rs, dj * tk
                lo = off - WINDOW + 1 if band else None
                if ko > off + rs - 1 or (band and ko + tk - 1 < lo):
                    continue
                full = ko + tk - 1 <= off and (not band or ko > off + rs - 1 - WINDOW)
                mask = None
                if not full:
                    mask = col + ko <= row + off
                    if band:
                        mask = mask & (col + ko > row + (off - WINDOW))
                items.append((j, r, mask))
        return items

    if band:
        for dj in range(-(WINDOW // tk), 0):
            @pl.when(i * kpq + dj >= 0)
            def _():
                jj = i * kpq + dj
                run(edge_items([dj], lambda _: jj))
    else:
        def body(jj, carry):
            tiles = [jj * kpq + dj for dj in range(kpq)]
            run([(j, r, None) for j in tiles for r in range(nst)])
            return carry
        lax.fori_loop(0, i, body, 0)
    diag = [i * kpq + dj for dj in range(kpq)]
    run(edge_items(list(range(kpq)), lambda dj: diag[dj]))
    acc = acc_sc[...]
    o_ref[0] = (acc / acc[:, HEAD_DIM:HEAD_DIM + 1]).astype(o_ref.dtype)


def flash(name, q_all, q0, kt_all, k0, v_all, v0, nh, rep, extra=None, kx=None, band=False):
    b, s, _ = q_all.shape
    tk = min(WINDOW if band else FLASH_TK, s)
    tq = tk if band else min(FLASH_TQ, s)
    rs = min(FLASH_STRIP, tq)
    assert s % tq == 0 and tq % tk == 0 and tq % rs == 0 and (not band or WINDOW % tk == 0)
    in_specs = [pl.BlockSpec((1, tq, LANES), lambda bi, hi, i: (bi, i, q0 + hi))]
    args = [q_all]
    tps = 1
    if extra is not None:
        he, nsup = extra.shape[1], extra.shape[2]
        rep_e = nh // he
        assert (s // nsup) % tk == 0
        tps = (s // nsup) // tk
        in_specs.append(pl.BlockSpec((1, 1, nsup, tq, LANES), lambda bi, hi, i: (bi, hi // rep_e, 0, i, 0)))
        args.append(extra)
    in_specs.append(pl.BlockSpec((1, LANES, s), lambda bi, hi, i: (bi, k0 + hi // rep, 0)))
    args.append(kt_all)
    if kx is not None:
        bx, hx = kx.shape[0], kx.shape[1]
        in_specs.append(pl.BlockSpec(
            (1, 1, LANES, s), lambda bi, hi, i: (bi if bx > 1 else 0, hi if hx > 1 else 0, 0, 0)))
        args.append(kx)
    in_specs.append(pl.BlockSpec((1, s, LANES), lambda bi, hi, i: (bi, 0, v0 + hi // rep)))
    args.append(v_all)
    return pl.pallas_call(
        functools.partial(_flash_kernel, tq=tq, tk=tk, rs=rs, tps=tps, band=band,
                          has_extra=extra is not None, has_kx=kx is not None),
        grid=(b, nh, s // tq),
        in_specs=in_specs,
        out_specs=pl.BlockSpec((1, tq, LANES), lambda bi, hi, i: (bi, i, hi)),
        out_shape=jax.ShapeDtypeStruct((b, s, nh * LANES), BF16),
        scratch_shapes=[pltpu.VMEM((tq, LANES), F32), pltpu.VMEM((tq, LANES), F32)],
        compiler_params=_cparams("parallel", "parallel", "parallel"),
        name=name,
    )(*args)


def _proj_norm_res_kernel(a_ref, w_ref, g_ref, x_ref, o_ref):
    y = jnp.dot(a_ref[...], w_ref[...], preferred_element_type=F32)
    o_ref[...] = x_ref[...] + _rms(y, g_ref[...])


def proj_norm_res(a, w, g, x, tm=512):
    n, k = a.shape
    d = w.shape[1]
    return pl.pallas_call(
        _proj_norm_res_kernel,
        grid=(n // tm,),
        in_specs=[pl.BlockSpec((tm, k), lambda i: (i, 0)),
                  pl.BlockSpec((k, d), lambda i: (0, 0)),
                  pl.BlockSpec((1, d), lambda i: (0, 0)),
                  pl.BlockSpec((tm, d), lambda i: (i, 0))],
        out_specs=pl.BlockSpec((tm, d), lambda i: (i, 0)),
        out_shape=jax.ShapeDtypeStruct((n, d), F32),
        compiler_params=_cparams("parallel"),
        name="proj_norm_res",
    )(a, w, g.reshape(1, d), x)


def _even_out_kernel(of_ref, oc_ref, os_ref, ow_ref, gl_ref, wf_ref, wn_ref, g_ref, x_ref, o_ref,
                     *, nh, g0):
    gate = jax.nn.sigmoid(gl_ref[...])
    parts = []
    for h in range(nh):
        sl = slice(h * LANES, (h + 1) * LANES)
        c = g0 + 3 * h
        parts.append(gate[:, c:c + 1] * oc_ref[:, sl].astype(F32)
                     + gate[:, c + 1:c + 2] * os_ref[:, sl].astype(F32)
                     + gate[:, c + 2:c + 3] * ow_ref[:, sl].astype(F32))
    a = jnp.concatenate(parts, axis=-1).astype(BF16)
    y = (jnp.dot(of_ref[...], wf_ref[...], preferred_element_type=F32)
         + jnp.dot(a, wn_ref[...], preferred_element_type=F32))
    o_ref[...] = x_ref[...] + _rms(y, g_ref[...])


def even_out(o_fox, o_cmp, o_slc, o_win, small, wf, wn, g, x, nh, g0, tm=512):
    n, k = o_fox.shape
    d = wf.shape[1]
    act = pl.BlockSpec((tm, k), lambda i: (i, 0))
    wspec = pl.BlockSpec((k, d), lambda i: (0, 0))
    return pl.pallas_call(
        functools.partial(_even_out_kernel, nh=nh, g0=g0),
        grid=(n // tm,),
        in_specs=[act, act, act, act,
                  pl.BlockSpec((tm, LANES), lambda i: (i, 0)),
                  wspec, wspec,
                  pl.BlockSpec((1, d), lambda i: (0, 0)),
                  pl.BlockSpec((tm, d), lambda i: (i, 0))],
        out_specs=pl.BlockSpec((tm, d), lambda i: (i, 0)),
        out_shape=jax.ShapeDtypeStruct((n, d), F32),
        compiler_params=_cparams("parallel"),
        name="even_out",
    )(o_fox, o_cmp, o_slc, o_win, small, wf, wn, g.reshape(1, d), x)


def _ffn_kernel(x_ref, gpre_ref, wg_ref, wu_ref, wd_ref, gpost_ref, o_ref, h_sc, acc_sc):
    f = pl.program_id(1)

    @pl.when(f == 0)
    def _():
        h_sc[...] = _rms(x_ref[...], gpre_ref[...]).astype(BF16)
        acc_sc[...] = jnp.zeros(acc_sc.shape, F32)

    h = h_sc[...]
    a = jnp.dot(h, wg_ref[...], preferred_element_type=F32)
    u = jnp.dot(h, wu_ref[...], preferred_element_type=F32)
    act = (jax.nn.silu(a) * u).astype(BF16)
    acc_sc[...] += jnp.dot(act, wd_ref[...], preferred_element_type=F32)

    @pl.when(f == pl.num_programs(1) - 1)
    def _():
        o_ref[...] = x_ref[...] + _rms(acc_sc[...], gpost_ref[...])


def ffn(x, gpre, wg, wu, wd, gpost, tm=512, tf=256):
    n, d = x.shape
    dff = wg.shape[1]
    assert n % tm == 0 and dff % tf == 0
    return pl.pallas_call(
        _ffn_kernel,
        grid=(n // tm, dff // tf),
        in_specs=[pl.BlockSpec((tm, d), lambda i, f: (i, 0)),
                  pl.BlockSpec((1, d), lambda i, f: (0, 0)),
                  pl.BlockSpec((d, tf), lambda i, f: (0, f)),
                  pl.BlockSpec((d, tf), lambda i, f: (0, f)),
                  pl.BlockSpec((tf, d), lambda i, f: (f, 0)),
                  pl.BlockSpec((1, d), lambda i, f: (0, 0))],
        out_specs=pl.BlockSpec((tm, d), lambda i, f: (i, 0)),
        out_shape=jax.ShapeDtypeStruct((n, d), F32),
        scratch_shapes=[pltpu.VMEM((tm, d), BF16), pltpu.VMEM((tm, d), F32)],
        compiler_params=_cparams("parallel", "arbitrary"),
        name="ffn",
    )(x, gpre.reshape(1, d), wg, wu, wd, gpost.reshape(1, d))


def _rope_tables(s):
    inv = ROPE_THETA ** (-jnp.arange(0, HEAD_DIM, 2, dtype=F32) / HEAD_DIM)
    ang = jnp.arange(s, dtype=F32)[:, None] * inv[None, :]
    cos, sin = jnp.cos(ang), jnp.sin(ang)
    reps = LANES // HEAD_DIM
    cos2 = jnp.tile(jnp.concatenate([cos, cos], -1), (1, reps))
    sin2 = jnp.tile(jnp.concatenate([-sin, sin], -1), (1, reps))
    return (cos2, sin2), (cos2.T, sin2.T)


def _slab_cols(w, scale=1.0):
    d, c = w.shape
    w3 = (w * scale).reshape(d, c // HEAD_DIM, HEAD_DIM)
    return jnp.concatenate([w3, jnp.zeros_like(w3)], axis=-1).reshape(d, 2 * c)


def _slab_rows(w):
    return _slab_cols(w.T).T


def _ones_lane(n_slabs):
    one = np.zeros((n_slabs, LANES), np.float32)
    one[:, HEAD_DIM] = 1.0
    return jnp.asarray(one.reshape(-1))


def _block_indicator_rows(s, block):
    blk = (np.arange(s) // block) % HEAD_DIM
    ind = np.zeros((LANES, s), np.float32)
    ind[HEAD_DIM + blk, np.arange(s)] = 1.0
    return jnp.asarray(ind, BF16)[None, None]


def _overlap_matrix(ncp, ns):
    ratio = SLC_BLOCK // CMP_STRIDE
    m = np.arange(ncp)[:, None]
    j = np.arange(ns)[None, :]
    ok = (m >= ratio * j - 1) & (m <= ratio * j + ratio - 1) & (m < ncp - 1)
    return jnp.asarray(ok, BF16)


def _compress_weights(pe, w1, w2, g):
    hid = w1.shape[-1]
    w1r = w1.reshape(2, CMP_STRIDE, HEAD_DIM, hid)
    w1g = jnp.zeros((g, 2, CMP_STRIDE, g, HEAD_DIM, hid), F32)
    for gi in range(g):
        w1g = w1g.at[gi, :, :, gi].set(w1r)
    w1g = w1g.reshape(g, 2, CMP_STRIDE * g * HEAD_DIM, hid).astype(BF16)
    pe2 = jnp.broadcast_to(pe.reshape(2, CMP_STRIDE, 1, HEAD_DIM), (2, CMP_STRIDE, g, HEAD_DIM))
    w2p = jnp.concatenate([w2, jnp.zeros_like(w2)], axis=-1).astype(BF16)
    return pe2.reshape(2, CMP_STRIDE * g * HEAD_DIM), w1g, w2p


def _out_rows(w):
    return _slab_rows(w).astype(BF16)


def _even_layer(x2, b, s, g_pre, g_post, w_in, b_f, pe_k, w1_k, w2_k, pe_v, w1_v, w2_v, w_out, tabs, tabs_t):
    hf, hn, g = 8, 8, 2
    hd = HEAD_DIM
    rep = hn // g
    (w_fq, w_fk, w_fv, w_fl, w_nq, w_kc, w_vc, w_ks, w_vs, w_kw, w_vw, w_gl) = jnp.split(
        w_in, list(np.cumsum([hf * hd] * 3 + [hf] + [hn * hd] + [g * hd] * 6)), axis=1)
    scale = hd ** -0.5

    w_plain = jnp.concatenate([_slab_cols(w_fq, scale), _slab_cols(w_fv), _slab_cols(w_vs),
                               _slab_cols(w_vw), w_vc], axis=1).astype(BF16)
    bias_plain = jnp.concatenate([jnp.zeros((hf * LANES,), F32), _ones_lane(hf + 2 * g),
                                  jnp.zeros((g * hd,), F32)])
    p_plain = norm_proj(x2, g_pre, w_plain, bias_plain, None, s, BF16).reshape(b, s, -1)
    fq0, fv0, vs0, vw0 = 0, hf, 2 * hf, 2 * hf + g
    vc_col = (2 * hf + 2 * g) * LANES
    w_rope = jnp.concatenate([_slab_cols(w_nq, scale), w_kc], axis=1).astype(BF16)
    p_rope = norm_proj(x2, g_pre, w_rope, jnp.zeros((w_rope.shape[1],), F32), tabs, s, BF16
                       ).reshape(b, s, -1)
    n_small = hf + 3 * hn
    w_small = jnp.pad(jnp.concatenate([w_fl, w_gl], axis=1), ((0, 0), (0, LANES - n_small))).astype(BF16)
    p_small = norm_proj(x2, g_pre, w_small, jnp.zeros((LANES,), F32), None, s, F32)
    kt_fox = norm_proj_t(x2, g_pre, _slab_rows(w_fk.T).astype(BF16), None, b, s)
    kt_nsa = norm_proj_t(x2, g_pre, _slab_rows(jnp.concatenate([w_ks, w_kw], axis=1).T).astype(BF16),
                         tabs_t, b, s)

    qx, kx = fox_decay(p_small.reshape(b, s, LANES), jnp.pad(b_f, (0, LANES - hf)).reshape(1, LANES), hf)
    o_fox = flash("flash_fox", p_plain, fq0, kt_fox, 0, p_plain, fv0, hf, 1, extra=qx, kx=kx)

    ncp = s // CMP_STRIDE
    ns = s // SLC_BLOCK
    kc = p_rope[:, :, hn * LANES:].reshape(b, ncp, CMP_STRIDE * g * hd)
    vc = p_plain[:, :, vc_col:].reshape(b, ncp, CMP_STRIDE * g * hd)
    kcmp = nsa_compress(kc, *_compress_weights(pe_k, w1_k, w2_k, g))
    vcmp = nsa_compress(vc, *_compress_weights(pe_v, w1_v, w2_v, g))
    o_cmp, selb = nsa_select(p_rope, kcmp.transpose(0, 1, 3, 2), vcmp, _overlap_matrix(ncp, ns), rep)
    o_slc = flash("flash_sel", p_rope, 0, kt_nsa, 0, p_plain, vs0, hn, rep, extra=selb,
                  kx=_block_indicator_rows(s, SLC_BLOCK))
    o_win = flash("flash_band", p_rope, 0, kt_nsa, g, p_plain, vw0, hn, rep, band=True)

    n = b * s
    return even_out(o_fox.reshape(n, -1), o_cmp.reshape(n, -1), o_slc.reshape(n, -1),
                    o_win.reshape(n, -1), p_small, _out_rows(w_out[:hf * hd]), _out_rows(w_out[hf * hd:]),
                    g_post, x2, hn, hf)


def _odd_layer(x2, b, s, g_pre, g_post, w_in, w_out, tabs, tabs_t):
    h = 16
    hd = HEAD_DIM
    d = h * hd
    scale = hd ** -0.5
    w_q, w_k, w_v = w_in[:, :d], w_in[:, d:2 * d], w_in[:, 2 * d:]
    q_all = norm_proj(x2, g_pre, _slab_cols(w_q, scale).astype(BF16), jnp.zeros((h * LANES,), F32),
                      tabs, s, BF16).reshape(b, s, -1)
    v_all = norm_proj(x2, g_pre, _slab_cols(w_v).astype(BF16), _ones_lane(h), None, s, BF16
                      ).reshape(b, s, -1)
    kt_all = norm_proj_t(x2, g_pre, _slab_rows(w_k.T).astype(BF16), tabs_t, b, s)
    ind = _block_indicator_rows(s, MOBA_BLOCK)
    selb = moba_select(q_all, kt_all, ind[0, 0].T, h)
    o = flash("flash_moba", q_all, 0, kt_all, 0, v_all, 0, h, 1, extra=selb, kx=ind)
    return proj_norm_res(o.reshape(b * s, -1), _out_rows(w_out), g_post, x2)


def kernel(x, ev_w_in, ev_b_f, ev_cmp_pe_k, ev_cmp_w1_k, ev_cmp_w2_k, ev_cmp_pe_v, ev_cmp_w1_v,
           ev_cmp_w2_v, ev_w_out, od_w_in, od_w_out, g_mix_pre, g_mix_post, g_ffn_pre, g_ffn_post,
           ffn_w_gate, ffn_w_up, ffn_w_down):
    b, s, d = x.shape
    depth = g_mix_pre.shape[0]
    tabs, tabs_t = _rope_tables(s)
    x2 = x.reshape(b * s, d)
    for layer in range(depth):
        if layer % 2 == 0:
            e = layer // 2
            x2 = _even_layer(x2, b, s, g_mix_pre[layer], g_mix_post[layer], ev_w_in[e], ev_b_f[e],
                             ev_cmp_pe_k[e], ev_cmp_w1_k[e], ev_cmp_w2_k[e], ev_cmp_pe_v[e],
                             ev_cmp_w1_v[e], ev_cmp_w2_v[e], ev_w_out[e], tabs, tabs_t)
        else:
            o = layer // 2
            x2 = _odd_layer(x2, b, s, g_mix_pre[layer], g_mix_post[layer], od_w_in[o], od_w_out[o],
                            tabs, tabs_t)
        x2 = ffn(x2, g_ffn_pre[layer], ffn_w_gate[layer].astype(BF16), ffn_w_up[layer].astype(BF16),
                 ffn_w_down[layer].astype(BF16), g_ffn_post[layer])
    return x2.reshape(b, s, d)
```

```python
import functools

import jax
import jax.numpy as jnp
import numpy as np
from jax import lax
from jax.experimental import pallas as pl
from jax.experimental.pallas import tpu as pltpu

F32 = jnp.float32
BF16 = jnp.bfloat16

HEAD_DIM = 64
LANES = 128
ROPE_THETA = 10000.0
RMS_EPS = 1e-6
CMP_STRIDE = 16
CMP_LEN = 32
SLC_BLOCK = 64
SLC_TOPN = 16
WINDOW = 512
MOBA_BLOCK = 256
MOBA_TOPK = 3
MASKED = -1e30
M_INIT = -1e29
DECAY_CUT = 100.0
FLASH_TQ = 1024
FLASH_TK = 512
FLASH_STRIP = 512
VMEM_LIMIT = 48 * 1024 * 1024
NT_DIMS = (((1,), (1,)), ((), ()))


def _cparams(*sem):
    return pltpu.CompilerParams(dimension_semantics=sem, vmem_limit_bytes=VMEM_LIMIT)


def _rms(x, g):
    return x * lax.rsqrt(jnp.mean(x * x, axis=-1, keepdims=True) + RMS_EPS) * g


def _split3(x):
    hi = x.astype(BF16).astype(F32)
    r = x - hi
    mid = r.astype(BF16).astype(F32)
    lo = (r - mid).astype(BF16).astype(F32)
    return hi, mid, lo


def _col_tile(nc, cap):
    best = LANES
    for k in range(1, nc // LANES + 1):
        if nc % (k * LANES) == 0 and k * LANES <= cap:
            best = k * LANES
    return best


def _norm_proj_kernel(x_ref, g_ref, w_ref, b_ref, *rest, rope):
    if rope:
        cos_ref, sin_ref, o_ref, h_sc = rest
    else:
        o_ref, h_sc = rest

    @pl.when(pl.program_id(1) == 0)
    def _():
        h_sc[...] = _rms(x_ref[...], g_ref[...]).astype(BF16)

    t = jnp.dot(h_sc[...], w_ref[...], preferred_element_type=F32)
    if rope:
        tn = t.shape[1]
        reps = tn // LANES
        cos = jnp.tile(cos_ref[...], (1, reps))
        sin = jnp.tile(sin_ref[...], (1, reps))
        lane = lax.broadcasted_iota(jnp.int32, t.shape, 1)
        first = (lane & (HEAD_DIM - 1)) < HEAD_DIM // 2
        rot = jnp.where(first, pltpu.roll(t, tn - HEAD_DIM // 2, 1), pltpu.roll(t, HEAD_DIM // 2, 1))
        t = t * cos + rot * sin
    o_ref[...] = (t + b_ref[...]).astype(o_ref.dtype)


def norm_proj(x, g, w, bias, rope_tabs, seq, out_dtype, tm=512):
    n, d = x.shape
    nc = w.shape[1]
    tn = _col_tile(nc, 1536)
    assert n % tm == 0 and seq % tm == 0
    rope = rope_tabs is not None
    in_specs = [
        pl.BlockSpec((tm, d), lambda i, j: (i, 0)),
        pl.BlockSpec((1, d), lambda i, j: (0, 0)),
        pl.BlockSpec((d, tn), lambda i, j: (0, j)),
        pl.BlockSpec((1, tn), lambda i, j: (0, j)),
    ]
    args = [x, g.reshape(1, d), w, bias.reshape(1, nc)]
    if rope:
        spt = seq // tm
        in_specs += [pl.BlockSpec((tm, LANES), lambda i, j: (i % spt, 0))] * 2
        args += list(rope_tabs)
    return pl.pallas_call(
        functools.partial(_norm_proj_kernel, rope=rope),
        grid=(n // tm, nc // tn),
        in_specs=in_specs,
        out_specs=pl.BlockSpec((tm, tn), lambda i, j: (i, j)),
        out_shape=jax.ShapeDtypeStruct((n, nc), out_dtype),
        scratch_shapes=[pltpu.VMEM((tm, d), BF16)],
        compiler_params=_cparams("parallel", "arbitrary"),
        name="norm_proj_rope" if rope else "norm_proj",
    )(*args)


def _norm_proj_t_kernel(x_ref, g_ref, wt_ref, *rest, rope):
    if rope:
        cos_ref, sin_ref, o_ref, h_sc = rest
    else:
        o_ref, h_sc = rest

    @pl.when(pl.program_id(1) == 0)
    def _():
        h_sc[...] = _rms(x_ref[...], g_ref[...]).astype(BF16)

    t = lax.dot_general(wt_ref[...], h_sc[...], NT_DIMS, preferred_element_type=F32)
    if rope:
        tn = t.shape[0]
        reps = tn // LANES
        cos = jnp.tile(cos_ref[...], (reps, 1))
        sin = jnp.tile(sin_ref[...], (reps, 1))
        r = lax.broadcasted_iota(jnp.int32, t.shape, 0)
        first = (r & (HEAD_DIM - 1)) < HEAD_DIM // 2
        rot = jnp.where(first, pltpu.roll(t, tn - HEAD_DIM // 2, 0), pltpu.roll(t, HEAD_DIM // 2, 0))
        t = t * cos + rot * sin
    o_ref[0] = t.astype(o_ref.dtype)


def norm_proj_t(x, g, wt, rope_tabs_t, b, seq, tm=512):
    n, d = x.shape
    nc = wt.shape[0]
    tn = _col_tile(nc, 1024)
    assert n % tm == 0 and seq % tm == 0
    spt = seq // tm
    rope = rope_tabs_t is not None
    in_specs = [
        pl.BlockSpec((tm, d), lambda i, j: (i, 0)),
        pl.BlockSpec((1, d), lambda i, j: (0, 0)),
        pl.BlockSpec((tn, d), lambda i, j: (j, 0)),
    ]
    args = [x, g.reshape(1, d), wt]
    if rope:
        in_specs += [pl.BlockSpec((LANES, tm), lambda i, j: (0, i % spt))] * 2
        args += list(rope_tabs_t)
    return pl.pallas_call(
        functools.partial(_norm_proj_t_kernel, rope=rope),
        grid=(n // tm, nc // tn),
        in_specs=in_specs,
        out_specs=pl.BlockSpec((1, tn, tm), lambda i, j: (i // spt, j, i % spt)),
        out_shape=jax.ShapeDtypeStruct((b, nc, seq), BF16),
        scratch_shapes=[pltpu.VMEM((tm, d), BF16)],
        compiler_params=_cparams("parallel", "arbitrary"),
        name="norm_proj_t_rope" if rope else "norm_proj_t",
    )(*args)


def _fox_decay_kernel(fl_ref, b_ref, pq_ref, pk_ref, oq_ref, ok_ref, carry_sc, *, nh):
    @pl.when(pl.program_id(1) == 0)
    def _():
        carry_sc[...] = jnp.zeros(carry_sc.shape, F32)

    c = jax.nn.log_sigmoid(fl_ref[0] + b_ref[...])
    ts = c.shape[0]
    row = lax.broadcasted_iota(jnp.int32, c.shape, 0)
    lane = lax.broadcasted_iota(jnp.int32, c.shape, 1)
    sh = 1
    while sh < ts:
        c = c + jnp.where(row >= sh, pltpu.roll(c, sh, 0), 0.0)
        sh *= 2
    c = c + carry_sc[0:1, :]
    carry_sc[0:1, :] = c[ts - 1:ts, :]
    hi, mid, lo = _split3(c)
    c3 = jnp.where(lane < nh, hi,
                   jnp.where(lane < 2 * nh, pltpu.roll(mid, nh, 1),
                             jnp.where(lane < 3 * nh, pltpu.roll(lo, 2 * nh, 1),
                                       jnp.where(lane == 3 * nh, 1.0, 0.0)))).astype(BF16)
    for h in range(nh):
        oq_ref[0, h, 0] = jnp.dot(c3, pq_ref[h], preferred_element_type=F32).astype(oq_ref.dtype)
        ok_ref[0, h] = lax.dot_general(pk_ref[h], c3, NT_DIMS,
                                       preferred_element_type=F32).astype(ok_ref.dtype)


def _fox_placement(nh):
    pq = np.zeros((nh, LANES, LANES), np.float32)
    pk = np.zeros((nh, LANES, LANES), np.float32)
    one = 3 * nh
    for h in range(nh):
        for t in range(3):
            pq[h, one, HEAD_DIM + t] = 1.0
            pq[h, t * nh + h, HEAD_DIM + 3 + t] = 1.0
            pk[h, HEAD_DIM + t, t * nh + h] = -1.0
            pk[h, HEAD_DIM + 3 + t, one] = 1.0
    return jnp.asarray(pq, BF16), jnp.asarray(pk, BF16)


def fox_decay(small, b_row, nh, ts=2048):
    b, s, _ = small.shape
    ts = min(ts, s)
    pq, pk = _fox_placement(nh)
    return pl.pallas_call(
        functools.partial(_fox_decay_kernel, nh=nh),
        grid=(b, s // ts),
        in_specs=[pl.BlockSpec((1, ts, LANES), lambda i, j: (i, j, 0)),
                  pl.BlockSpec((1, LANES), lambda i, j: (0, 0)),
                  pl.BlockSpec((nh, LANES, LANES), lambda i, j: (0, 0, 0)),
                  pl.BlockSpec((nh, LANES, LANES), lambda i, j: (0, 0, 0))],
        out_specs=[pl.BlockSpec((1, nh, 1, ts, LANES), lambda i, j: (i, 0, 0, j, 0)),
                   pl.BlockSpec((1, nh, LANES, ts), lambda i, j: (i, 0, 0, j))],
        out_shape=[jax.ShapeDtypeStruct((b, nh, 1, s, LANES), BF16),
                   jax.ShapeDtypeStruct((b, nh, LANES, s), BF16)],
        scratch_shapes=[pltpu.VMEM((8, LANES), F32)],
        compiler_params=_cparams("parallel", "arbitrary"),
        name="fox_decay",
    )(small, b_row, pq, pk)


def _nsa_compress_kernel(t_ref, pe_ref, w1_ref, w2_ref, o_ref):
    t = t_ref[0].astype(F32)
    ncp = t.shape[0]
    a = jnp.dot((t + pe_ref[0:1, :]).astype(BF16), w1_ref[0, 0], preferred_element_type=F32)
    bm = jnp.dot((t + pe_ref[1:2, :]).astype(BF16), w1_ref[0, 1], preferred_element_type=F32)
    pre = a + pltpu.roll(bm, ncp - 1, 0)
    hid = jax.nn.gelu(pre)
    o_ref[0, 0] = jnp.dot(hid.astype(BF16), w2_ref[...], preferred_element_type=F32).astype(o_ref.dtype)


def nsa_compress(t, pe, w1, w2):
    b, ncp, cw = t.shape
    g, _, _, hid = w1.shape
    return pl.pallas_call(
        _nsa_compress_kernel,
        grid=(b, g),
        in_specs=[pl.BlockSpec((1, ncp, cw), lambda i, j: (i, 0, 0)),
                  pl.BlockSpec((2, cw), lambda i, j: (0, 0)),
                  pl.BlockSpec((1, 2, cw, hid), lambda i, j: (j, 0, 0, 0)),
                  pl.BlockSpec((hid, LANES), lambda i, j: (0, 0))],
        out_specs=pl.BlockSpec((1, 1, ncp, LANES), lambda i, j: (i, j, 0, 0)),
        out_shape=jax.ShapeDtypeStruct((b, g, ncp, LANES), BF16),
        compiler_params=_cparams("parallel", "parallel"),
        name="nsa_compress",
    )(t, pe, w1, w2)


def _topk_mask(work, col, k):
    sel = jnp.zeros(work.shape, jnp.bool_)
    col = col.astype(F32)
    for _ in range(k):
        mx = jnp.max(work, axis=-1, keepdims=True)
        first = jnp.min(jnp.where(work == mx, col, 1e9), axis=-1, keepdims=True)
        hit = col == first
        sel = jnp.logical_or(sel, hit)
        work = jnp.where(hit, -jnp.inf, work)
    return sel


def _nsa_select_kernel(q_ref, kct_ref, vc_ref, m_ref, oc_ref, sb_ref, *, tq, rep, nsup):
    i = pl.program_id(2)
    ncp = kct_ref.shape[-1]
    ns = m_ref.shape[-1]
    qpos = i * tq + lax.broadcasted_iota(jnp.int32, (tq, 1), 0)
    cend = lax.broadcasted_iota(jnp.int32, (1, ncp), 1) * CMP_STRIDE + (CMP_LEN - 1)
    cmask = cend <= qpos
    kct = kct_ref[0, 0]
    vc = vc_ref[0, 0]
    pcs = jnp.zeros((tq, ncp), F32)
    for r in range(rep):
        s = jnp.dot(q_ref[0, :, r * LANES:(r + 1) * LANES], kct, preferred_element_type=F32)
        s = jnp.where(cmask, s, MASKED)
        e = jnp.where(cmask, jnp.exp(s - jnp.max(s, axis=-1, keepdims=True)), 0.0)
        p = e / jnp.maximum(jnp.sum(e, axis=-1, keepdims=True), 1e-30)
        oc_ref[0, :, r * LANES:(r + 1) * LANES] = jnp.dot(
            p.astype(BF16), vc, preferred_element_type=F32).astype(oc_ref.dtype)
        pcs = pcs + p
    mm = m_ref[...]
    imp = sum(jnp.dot(part.astype(BF16), mm, preferred_element_type=F32) for part in _split3(pcs))
    sblk = lax.broadcasted_iota(jnp.int32, (1, ns), 1)
    qblk = qpos >> 6
    forced = (sblk == 0) | (sblk == qblk) | (sblk == qblk - 1)
    imp = jnp.where(forced, 1e9, jnp.where(sblk > qblk, -1e9, imp))
    sel = _topk_mask(imp, sblk, min(SLC_TOPN, ns))
    bias = jnp.where(sel & (sblk <= qblk), 0.0, MASKED).astype(sb_ref.dtype)
    zeros = jnp.zeros((tq, HEAD_DIM), sb_ref.dtype)
    for j in range(nsup):
        sb_ref[0, 0, j] = jnp.concatenate([zeros, bias[:, j * HEAD_DIM:(j + 1) * HEAD_DIM]], axis=-1)


def nsa_select(q_all, kct, vc, m, rep, tq=512):
    b, s, _ = q_all.shape
    g = kct.shape[1]
    ncp = kct.shape[-1]
    ns = m.shape[-1]
    nsup = ns // HEAD_DIM
    assert SLC_BLOCK == 64 and ns % HEAD_DIM == 0
    tq = min(tq, s)
    return pl.pallas_call(
        functools.partial(_nsa_select_kernel, tq=tq, rep=rep, nsup=nsup),
        grid=(b, g, s // tq),
        in_specs=[pl.BlockSpec((1, tq, rep * LANES), lambda bi, gi, i: (bi, i, gi)),
                  pl.BlockSpec((1, 1, LANES, ncp), lambda bi, gi, i: (bi, gi, 0, 0)),
                  pl.BlockSpec((1, 1, ncp, LANES), lambda bi, gi, i: (bi, gi, 0, 0)),
                  pl.BlockSpec((ncp, ns), lambda bi, gi, i: (0, 0))],
        out_specs=[pl.BlockSpec((1, tq, rep * LANES), lambda bi, gi, i: (bi, i, gi)),
                   pl.BlockSpec((1, 1, nsup, tq, LANES), lambda bi, gi, i: (bi, gi, 0, i, 0))],
        out_shape=[jax.ShapeDtypeStruct((b, s, g * rep * LANES), BF16),
                   jax.ShapeDtypeStruct((b, g, nsup, s, LANES), BF16)],
        compiler_params=_cparams("parallel", "parallel", "parallel"),
        name="nsa_select",
    )(q_all, kct, vc, m)


def _moba_select_kernel(q_ref, kt_ref, ind_ref, sb_ref, kbar_sc, *, tq):
    i = pl.program_id(2)

    @pl.when(i == 0)
    def _():
        kbar_sc[...] = jnp.dot(kt_ref[0], ind_ref[...], preferred_element_type=F32) * (1.0 / MOBA_BLOCK)

    q = q_ref[0]
    gate = sum(jnp.dot(q, part.astype(BF16), preferred_element_type=F32)
               for part in _split3(kbar_sc[...]))
    qpos = i * tq + lax.broadcasted_iota(jnp.int32, (tq, 1), 0)
    cur = qpos >> 8
    lane = lax.broadcasted_iota(jnp.int32, (1, LANES), 1)
    blk = lane - HEAD_DIM
    past = (blk >= 0) & (blk < cur)
    sel = _topk_mask(jnp.where(past, gate, -jnp.inf), lane, MOBA_TOPK)
    keep = (blk < 0) | (sel & past) | (blk == cur)
    sb_ref[0, 0, 0] = jnp.where(keep, 0.0, MASKED).astype(sb_ref.dtype)


def moba_select(q_all, kt_all, ind, nh, tq=2048):
    b, s, _ = q_all.shape
    assert MOBA_BLOCK == 256 and s // MOBA_BLOCK <= HEAD_DIM
    tq = min(tq, s)
    return pl.pallas_call(
        functools.partial(_moba_select_kernel, tq=tq),
        grid=(b, nh, s // tq),
        in_specs=[pl.BlockSpec((1, tq, LANES), lambda bi, hi, i: (bi, i, hi)),
                  pl.BlockSpec((1, LANES, s), lambda bi, hi, i: (bi, hi, 0)),
                  pl.BlockSpec((s, LANES), lambda bi, hi, i: (0, 0))],
        out_specs=pl.BlockSpec((1, 1, 1, tq, LANES), lambda bi, hi, i: (bi, hi, 0, i, 0)),
        out_shape=jax.ShapeDtypeStruct((b, nh, 1, s, LANES), BF16),
        scratch_shapes=[pltpu.VMEM((LANES, LANES), F32)],
        compiler_params=_cparams("parallel", "parallel", "arbitrary"),
        name="moba_select",
    )(q_all, kt_all, ind)


def _flash_kernel(*refs, tq, tk, rs, tps, band, decay, has_extra, has_kx):
    refs = list(refs)
    qa_ref = refs.pop(0)
    ex_ref = refs.pop(0) if has_extra else None
    kt_ref = refs.pop(0)
    kx_ref = refs.pop(0) if has_kx else None
    kmax_sc = refs.pop() if decay else None
    v_ref, o_ref, m_sc, acc_sc = refs
    i = pl.program_id(2)
    m_sc[...] = jnp.full(m_sc.shape, M_INIT, F32)
    acc_sc[...] = jnp.zeros(acc_sc.shape, F32)
    row = lax.broadcasted_iota(jnp.int32, (rs, tk), 0)
    col = lax.broadcasted_iota(jnp.int32, (rs, tk), 1)
    nst = tq // rs
    kpq = tq // tk

    def run(items):
        tiles = {}

        def operands(j):
            if id(j) not in tiles:
                start = pl.multiple_of(j * tk, tk)
                kt = kt_ref[0, :, pl.ds(start, tk)]
                if kx_ref is not None:
                    kt = kt + kx_ref[0, 0, :, pl.ds(start, tk)]
                tiles[id(j)] = (kt, v_ref[0, pl.ds(start, tk), :])
            return tiles[id(j)]

        def logits(item):
            j, r, _ = item
            rows = pl.ds(r * rs, rs)
            qa = qa_ref[0, rows, :]
            if ex_ref is not None:
                qa = qa + ex_ref[0, 0, j // tps, rows, :]
            return jnp.dot(qa, operands(j)[0], preferred_element_type=F32)

        s_next = logits(items[0])
        for n, (j, r, mask) in enumerate(items):
            rows = pl.ds(r * rs, rs)
            s = s_next
            if n + 1 < len(items):
                s_next = logits(items[n + 1])
            if mask is not None:
                s = jnp.where(mask, s, MASKED)
            m_prev = m_sc[rows, :]
            m_new = jnp.maximum(m_prev, jnp.max(s, axis=-1, keepdims=True))
            p = jnp.exp(s - jnp.tile(m_new, (1, tk // LANES)))
            alpha = jnp.exp(m_prev - m_new)
            acc_sc[rows, :] = alpha * acc_sc[rows, :] + jnp.dot(
                p.astype(BF16), operands(j)[1], preferred_element_type=F32)
            m_sc[rows, :] = m_new

    def edge_items(dj_list, j_of):
        items = []
        for dj in dj_list:
            j = j_of(dj)
            for r in range(nst):
                off, ko = r * rs, dj * tk
                lo = off - WINDOW + 1 if band else None
                if ko > off + rs - 1 or (band and ko + tk - 1 < lo):
                    continue
                full = ko + tk - 1 <= off and (not band or ko > off + rs - 1 - WINDOW)
                mask = None
                if not full:
                    mask = col + ko <= row + off
                    if band:
                        mask = mask & (col + ko > row + (off - WINDOW))
                items.append((j, r, mask))
        return items

    if band:
        for dj in range(-(WINDOW // tk), 0):
            @pl.when(i * kpq + dj >= 0)
            def _():
                jj = i * kpq + dj
                run(edge_items([dj], lambda _: jj))
    def body(jj, carry):
        tiles = [jj * kpq + dj for dj in range(kpq)]
        run([(j, r, None) for j in tiles for r in range(nst)])
        return carry

    def run_diag():
        diag = [i * kpq + dj for dj in range(kpq)]
        run(edge_items(list(range(kpq)), lambda dj: diag[dj]))

    if decay:
        s_len = kt_ref.shape[-1]

        @pl.when(i == 0)
        def _():
            def chunk(c, best):
                kk = kt_ref[0, :, pl.ds(pl.multiple_of(c * tq, tq), tq)].astype(F32)
                return jnp.maximum(best, jnp.sum(kk * kk, axis=0, keepdims=True))
            ksq = lax.fori_loop(0, s_len // tq, chunk, jnp.zeros((1, tq), F32))
            kmax_sc[...] = jnp.broadcast_to(jnp.sqrt(jnp.max(ksq, axis=1, keepdims=True)), kmax_sc.shape)

        run_diag()
        q = qa_ref[0].astype(F32)
        qn = jnp.sqrt(jnp.sum(q * q, axis=1, keepdims=True))
        slack = jnp.max(qn * kmax_sc[0:1, 0:1] - m_sc[:, 0:1], axis=0, keepdims=True)
        cvec = -jnp.sum(kx_ref[0, 0, HEAD_DIM:HEAD_DIM + 3, :].astype(F32), axis=0, keepdims=True)
        pos = lax.broadcasted_iota(jnp.int32, (1, s_len), 1)
        q0 = i * tq
        c_q0 = jnp.sum(jnp.where(pos == q0, cvec, 0.0), axis=1, keepdims=True)
        dead = (pos < q0) & (slack + c_q0 - cvec <= -DECAY_CUT)
        n_dead = jnp.sum(jnp.where(dead, 1.0, 0.0)).astype(jnp.int32) // tq
        lax.fori_loop(0, i - n_dead, lambda t, carry: body(i - 1 - t, carry), 0)
    elif band:
        run_diag()
    else:
        lax.fori_loop(0, i, body, 0)
        run_diag()
    acc = acc_sc[...]
    o_ref[0] = (acc / acc[:, HEAD_DIM:HEAD_DIM + 1]).astype(o_ref.dtype)


def flash(name, q_all, q0, kt_all, k0, v_all, v0, nh, rep, extra=None, kx=None, band=False,
          decay=False):
    assert not decay or (kx is not None and not band)
    b, s, _ = q_all.shape
    tk = min(WINDOW if band else FLASH_TK, s)
    tq = tk if band else min(FLASH_TQ, s)
    rs = min(FLASH_STRIP, tq)
    assert s % tq == 0 and tq % tk == 0 and tq % rs == 0 and (not band or WINDOW % tk == 0)
    in_specs = [pl.BlockSpec((1, tq, LANES), lambda bi, hi, i: (bi, i, q0 + hi))]
    args = [q_all]
    tps = 1
    if extra is not None:
        he, nsup = extra.shape[1], extra.shape[2]
        rep_e = nh // he
        assert (s // nsup) % tk == 0
        tps = (s // nsup) // tk
        in_specs.append(pl.BlockSpec((1, 1, nsup, tq, LANES), lambda bi, hi, i: (bi, hi // rep_e, 0, i, 0)))
        args.append(extra)
    in_specs.append(pl.BlockSpec((1, LANES, s), lambda bi, hi, i: (bi, k0 + hi // rep, 0)))
    args.append(kt_all)
    if kx is not None:
        bx, hx = kx.shape[0], kx.shape[1]
        in_specs.append(pl.BlockSpec(
            (1, 1, LANES, s), lambda bi, hi, i: (bi if bx > 1 else 0, hi if hx > 1 else 0, 0, 0)))
        args.append(kx)
    in_specs.append(pl.BlockSpec((1, s, LANES), lambda bi, hi, i: (bi, 0, v0 + hi // rep)))
    args.append(v_all)
    return pl.pallas_call(
        functools.partial(_flash_kernel, tq=tq, tk=tk, rs=rs, tps=tps, band=band, decay=decay,
                          has_extra=extra is not None, has_kx=kx is not None),
        grid=(b, nh, s // tq),
        in_specs=in_specs,
        out_specs=pl.BlockSpec((1, tq, LANES), lambda bi, hi, i: (bi, i, hi)),
        out_shape=jax.ShapeDtypeStruct((b, s, nh * LANES), BF16),
        scratch_shapes=[pltpu.VMEM((tq, LANES), F32), pltpu.VMEM((tq, LANES), F32)]
        + ([pltpu.VMEM((8, LANES), F32)] if decay else []),
        compiler_params=_cparams("parallel", "parallel", "arbitrary" if decay else "parallel"),
        name=name,
    )(*args)


def _proj_norm_res_kernel(a_ref, w_ref, g_ref, x_ref, o_ref):
    y = jnp.dot(a_ref[...], w_ref[...], preferred_element_type=F32)
    o_ref[...] = x_ref[...] + _rms(y, g_ref[...])


def proj_norm_res(a, w, g, x, tm=512):
    n, k = a.shape
    d = w.shape[1]
    return pl.pallas_call(
        _proj_norm_res_kernel,
        grid=(n // tm,),
        in_specs=[pl.BlockSpec((tm, k), lambda i: (i, 0)),
                  pl.BlockSpec((k, d), lambda i: (0, 0)),
                  pl.BlockSpec((1, d), lambda i: (0, 0)),
                  pl.BlockSpec((tm, d), lambda i: (i, 0))],
        out_specs=pl.BlockSpec((tm, d), lambda i: (i, 0)),
        out_shape=jax.ShapeDtypeStruct((n, d), F32),
        compiler_params=_cparams("parallel"),
        name="proj_norm_res",
    )(a, w, g.reshape(1, d), x)


def _even_out_kernel(of_ref, oc_ref, os_ref, ow_ref, gl_ref, wf_ref, wn_ref, g_ref, x_ref, o_ref,
                     *, nh, g0):
    gate = jax.nn.sigmoid(gl_ref[...])
    parts = []
    for h in range(nh):
        sl = slice(h * LANES, (h + 1) * LANES)
        c = g0 + 3 * h
        parts.append(gate[:, c:c + 1] * oc_ref[:, sl].astype(F32)
                     + gate[:, c + 1:c + 2] * os_ref[:, sl].astype(F32)
                     + gate[:, c + 2:c + 3] * ow_ref[:, sl].astype(F32))
    a = jnp.concatenate(parts, axis=-1).astype(BF16)
    y = (jnp.dot(of_ref[...], wf_ref[...], preferred_element_type=F32)
         + jnp.dot(a, wn_ref[...], preferred_element_type=F32))
    o_ref[...] = x_ref[...] + _rms(y, g_ref[...])


def even_out(o_fox, o_cmp, o_slc, o_win, small, wf, wn, g, x, nh, g0, tm=512):
    n, k = o_fox.shape
    d = wf.shape[1]
    act = pl.BlockSpec((tm, k), lambda i: (i, 0))
    wspec = pl.BlockSpec((k, d), lambda i: (0, 0))
    return pl.pallas_call(
        functools.partial(_even_out_kernel, nh=nh, g0=g0),
        grid=(n // tm,),
        in_specs=[act, act, act, act,
                  pl.BlockSpec((tm, LANES), lambda i: (i, 0)),
                  wspec, wspec,
                  pl.BlockSpec((1, d), lambda i: (0, 0)),
                  pl.BlockSpec((tm, d), lambda i: (i, 0))],
        out_specs=pl.BlockSpec((tm, d), lambda i: (i, 0)),
        out_shape=jax.ShapeDtypeStruct((n, d), F32),
        compiler_params=_cparams("parallel"),
        name="even_out",
    )(o_fox, o_cmp, o_slc, o_win, small, wf, wn, g.reshape(1, d), x)


def _ffn_kernel(x_ref, gpre_ref, wg_ref, wu_ref, wd_ref, gpost_ref, o_ref, h_sc, acc_sc):
    f = pl.program_id(1)

    @pl.when(f == 0)
    def _():
        h_sc[...] = _rms(x_ref[...], gpre_ref[...]).astype(BF16)
        acc_sc[...] = jnp.zeros(acc_sc.shape, F32)

    h = h_sc[...]
    a = jnp.dot(h, wg_ref[...], preferred_element_type=F32)
    u = jnp.dot(h, wu_ref[...], preferred_element_type=F32)
    act = (jax.nn.silu(a) * u).astype(BF16)
    acc_sc[...] += jnp.dot(act, wd_ref[...], preferred_element_type=F32)

    @pl.when(f == pl.num_programs(1) - 1)
    def _():
        o_ref[...] = x_ref[...] + _rms(acc_sc[...], gpost_ref[...])


def ffn(x, gpre, wg, wu, wd, gpost, tm=512, tf=256):
    n, d = x.shape
    dff = wg.shape[1]
    assert n % tm == 0 and dff % tf == 0
    return pl.pallas_call(
        _ffn_kernel,
        grid=(n // tm, dff // tf),
        in_specs=[pl.BlockSpec((tm, d), lambda i, f: (i, 0)),
                  pl.BlockSpec((1, d), lambda i, f: (0, 0)),
                  pl.BlockSpec((d, tf), lambda i, f: (0, f)),
                  pl.BlockSpec((d, tf), lambda i, f: (0, f)),
                  pl.BlockSpec((tf, d), lambda i, f: (f, 0)),
                  pl.BlockSpec((1, d), lambda i, f: (0, 0))],
        out_specs=pl.BlockSpec((tm, d), lambda i, f: (i, 0)),
        out_shape=jax.ShapeDtypeStruct((n, d), F32),
        scratch_shapes=[pltpu.VMEM((tm, d), BF16), pltpu.VMEM((tm, d), F32)],
        compiler_params=_cparams("parallel", "arbitrary"),
        name="ffn",
    )(x, gpre.reshape(1, d), wg, wu, wd, gpost.reshape(1, d))


def _rope_tables(s):
    inv = ROPE_THETA ** (-jnp.arange(0, HEAD_DIM, 2, dtype=F32) / HEAD_DIM)
    ang = jnp.arange(s, dtype=F32)[:, None] * inv[None, :]
    cos, sin = jnp.cos(ang), jnp.sin(ang)
    reps = LANES // HEAD_DIM
    cos2 = jnp.tile(jnp.concatenate([cos, cos], -1), (1, reps))
    sin2 = jnp.tile(jnp.concatenate([-sin, sin], -1), (1, reps))
    return (cos2, sin2), (cos2.T, sin2.T)


def _slab_cols(w, scale=1.0):
    d, c = w.shape
    w3 = (w * scale).reshape(d, c // HEAD_DIM, HEAD_DIM)
    return jnp.concatenate([w3, jnp.zeros_like(w3)], axis=-1).reshape(d, 2 * c)


def _slab_rows(w):
    return _slab_cols(w.T).T


def _ones_lane(n_slabs):
    one = np.zeros((n_slabs, LANES), np.float32)
    one[:, HEAD_DIM] = 1.0
    return jnp.asarray(one.reshape(-1))


def _block_indicator_rows(s, block):
    blk = (np.arange(s) // block) % HEAD_DIM
    ind = np.zeros((LANES, s), np.float32)
    ind[HEAD_DIM + blk, np.arange(s)] = 1.0
    return jnp.asarray(ind, BF16)[None, None]


def _overlap_matrix(ncp, ns):
    ratio = SLC_BLOCK // CMP_STRIDE
    m = np.arange(ncp)[:, None]
    j = np.arange(ns)[None, :]
    ok = (m >= ratio * j - 1) & (m <= ratio * j + ratio - 1) & (m < ncp - 1)
    return jnp.asarray(ok, BF16)


def _compress_weights(pe, w1, w2, g):
    hid = w1.shape[-1]
    w1r = w1.reshape(2, CMP_STRIDE, HEAD_DIM, hid)
    w1g = jnp.zeros((g, 2, CMP_STRIDE, g, HEAD_DIM, hid), F32)
    for gi in range(g):
        w1g = w1g.at[gi, :, :, gi].set(w1r)
    w1g = w1g.reshape(g, 2, CMP_STRIDE * g * HEAD_DIM, hid).astype(BF16)
    pe2 = jnp.broadcast_to(pe.reshape(2, CMP_STRIDE, 1, HEAD_DIM), (2, CMP_STRIDE, g, HEAD_DIM))
    w2p = jnp.concatenate([w2, jnp.zeros_like(w2)], axis=-1).astype(BF16)
    return pe2.reshape(2, CMP_STRIDE * g * HEAD_DIM), w1g, w2p


def _out_rows(w):
    return _slab_rows(w).astype(BF16)


def _even_layer(x2, b, s, g_pre, g_post, w_in, b_f, pe_k, w1_k, w2_k, pe_v, w1_v, w2_v, w_out, tabs, tabs_t):
    hf, hn, g = 8, 8, 2
    hd = HEAD_DIM
    rep = hn // g
    (w_fq, w_fk, w_fv, w_fl, w_nq, w_kc, w_vc, w_ks, w_vs, w_kw, w_vw, w_gl) = jnp.split(
        w_in, list(np.cumsum([hf * hd] * 3 + [hf] + [hn * hd] + [g * hd] * 6)), axis=1)
    scale = hd ** -0.5

    w_plain = jnp.concatenate([_slab_cols(w_fq, scale), _slab_cols(w_fv), _slab_cols(w_vs),
                               _slab_cols(w_vw), w_vc], axis=1).astype(BF16)
    bias_plain = jnp.concatenate([jnp.zeros((hf * LANES,), F32), _ones_lane(hf + 2 * g),
                                  jnp.zeros((g * hd,), F32)])
    p_plain = norm_proj(x2, g_pre, w_plain, bias_plain, None, s, BF16).reshape(b, s, -1)
    fq0, fv0, vs0, vw0 = 0, hf, 2 * hf, 2 * hf + g
    vc_col = (2 * hf + 2 * g) * LANES
    w_rope = jnp.concatenate([_slab_cols(w_nq, scale), w_kc], axis=1).astype(BF16)
    p_rope = norm_proj(x2, g_pre, w_rope, jnp.zeros((w_rope.shape[1],), F32), tabs, s, BF16
                       ).reshape(b, s, -1)
    n_small = hf + 3 * hn
    w_small = jnp.pad(jnp.concatenate([w_fl, w_gl], axis=1), ((0, 0), (0, LANES - n_small))).astype(BF16)
    p_small = norm_proj(x2, g_pre, w_small, jnp.zeros((LANES,), F32), None, s, F32)
    kt_fox = norm_proj_t(x2, g_pre, _slab_rows(w_fk.T).astype(BF16), None, b, s)
    kt_nsa = norm_proj_t(x2, g_pre, _slab_rows(jnp.concatenate([w_ks, w_kw], axis=1).T).astype(BF16),
                         tabs_t, b, s)

    qx, kx = fox_decay(p_small.reshape(b, s, LANES), jnp.pad(b_f, (0, LANES - hf)).reshape(1, LANES), hf)
    o_fox = flash("flash_fox", p_plain, fq0, kt_fox, 0, p_plain, fv0, hf, 1, extra=qx, kx=kx, decay=True)

    ncp = s // CMP_STRIDE
    ns = s // SLC_BLOCK
    kc = p_rope[:, :, hn * LANES:].reshape(b, ncp, CMP_STRIDE * g * hd)
    vc = p_plain[:, :, vc_col:].reshape(b, ncp, CMP_STRIDE * g * hd)
    kcmp = nsa_compress(kc, *_compress_weights(pe_k, w1_k, w2_k, g))
    vcmp = nsa_compress(vc, *_compress_weights(pe_v, w1_v, w2_v, g))
    o_cmp, selb = nsa_select(p_rope, kcmp.transpose(0, 1, 3, 2), vcmp, _overlap_matrix(ncp, ns), rep)
    o_slc = flash("flash_sel", p_rope, 0, kt_nsa, 0, p_plain, vs0, hn, rep, extra=selb,
                  kx=_block_indicator_rows(s, SLC_BLOCK))
    o_win = flash("flash_band", p_rope, 0, kt_nsa, g, p_plain, vw0, hn, rep, band=True)

    n = b * s
    return even_out(o_fox.reshape(n, -1), o_cmp.reshape(n, -1), o_slc.reshape(n, -1),
                    o_win.reshape(n, -1), p_small, _out_rows(w_out[:hf * hd]), _out_rows(w_out[hf * hd:]),
                    g_post, x2, hn, hf)


def _odd_layer(x2, b, s, g_pre, g_post, w_in, w_out, tabs, tabs_t):
    h = 16
    hd = HEAD_DIM
    d = h * hd
    scale = hd ** -0.5
    w_q, w_k, w_v = w_in[:, :d], w_in[:, d:2 * d], w_in[:, 2 * d:]
    q_all = norm_proj(x2, g_pre, _slab_cols(w_q, scale).astype(BF16), jnp.zeros((h * LANES,), F32),
                      tabs, s, BF16).reshape(b, s, -1)
    v_all = norm_proj(x2, g_pre, _slab_cols(w_v).astype(BF16), _ones_lane(h), None, s, BF16
                      ).reshape(b, s, -1)
    kt_all = norm_proj_t(x2, g_pre, _slab_rows(w_k.T).astype(BF16), tabs_t, b, s)
    ind = _block_indicator_rows(s, MOBA_BLOCK)
    selb = moba_select(q_all, kt_all, ind[0, 0].T, h)
    o = flash("flash_moba", q_all, 0, kt_all, 0, v_all, 0, h, 1, extra=selb, kx=ind)
    return proj_norm_res(o.reshape(b * s, -1), _out_rows(w_out), g_post, x2)


def kernel(x, ev_w_in, ev_b_f, ev_cmp_pe_k, ev_cmp_w1_k, ev_cmp_w2_k, ev_cmp_pe_v, ev_cmp_w1_v,
           ev_cmp_w2_v, ev_w_out, od_w_in, od_w_out, g_mix_pre, g_mix_post, g_ffn_pre, g_ffn_post,
           ffn_w_gate, ffn_w_up, ffn_w_down):
    b, s, d = x.shape
    depth = g_mix_pre.shape[0]
    tabs, tabs_t = _rope_tables(s)
    x2 = x.reshape(b * s, d)
    for layer in range(depth):
        if layer % 2 == 0:
            e = layer // 2
            x2 = _even_layer(x2, b, s, g_mix_pre[layer], g_mix_post[layer], ev_w_in[e], ev_b_f[e],
                             ev_cmp_pe_k[e], ev_cmp_w1_k[e], ev_cmp_w2_k[e], ev_cmp_pe_v[e],
                             ev_cmp_w1_v[e], ev_cmp_w2_v[e], ev_w_out[e], tabs, tabs_t)
        else:
            o = layer // 2
            x2 = _odd_layer(x2, b, s, g_mix_pre[layer], g_mix_post[layer], od_w_in[o], od_w_out[o],
                            tabs, tabs_t)
        x2 = ffn(x2, g_ffn_pre[layer], ffn_w_gate[layer].astype(BF16), ffn_w_up[layer].astype(BF16),
                 ffn_w_down[layer].astype(BF16), g_ffn_post[layer])
    return x2.reshape(b, s, d)
```

```python
import functools

import jax
import jax.numpy as jnp
import numpy as np
from jax import lax
from jax.experimental import pallas as pl
from jax.experimental.pallas import tpu as pltpu

F32 = jnp.float32
BF16 = jnp.bfloat16

HEAD_DIM = 64
LANES = 128
ROPE_THETA = 10000.0
RMS_EPS = 1e-6
CMP_STRIDE = 16
CMP_LEN = 32
SLC_BLOCK = 64
SLC_TOPN = 16
WINDOW = 512
MOBA_BLOCK = 256
MOBA_TOPK = 3
MASKED = -1e30
M_INIT = -1e29
DECAY_CUT = 100.0
FLASH_TQ = 1024
FLASH_TK = 512
FLASH_STRIP = 512
FLASH_AHEAD = 2
VMEM_LIMIT = 48 * 1024 * 1024
NT_DIMS = (((1,), (1,)), ((), ()))


def _cparams(*sem):
    return pltpu.CompilerParams(dimension_semantics=sem, vmem_limit_bytes=VMEM_LIMIT)


def _rms(x, g):
    return x * lax.rsqrt(jnp.mean(x * x, axis=-1, keepdims=True) + RMS_EPS) * g


def _split3(x):
    hi = x.astype(BF16).astype(F32)
    r = x - hi
    mid = r.astype(BF16).astype(F32)
    lo = (r - mid).astype(BF16).astype(F32)
    return hi, mid, lo


def _col_tile(nc, cap):
    best = LANES
    for k in range(1, nc // LANES + 1):
        if nc % (k * LANES) == 0 and k * LANES <= cap:
            best = k * LANES
    return best


def _norm_proj_kernel(x_ref, g_ref, w_ref, b_ref, *rest, rope):
    if rope:
        cos_ref, sin_ref, o_ref, h_sc = rest
    else:
        o_ref, h_sc = rest

    @pl.when(pl.program_id(1) == 0)
    def _():
        h_sc[...] = _rms(x_ref[...], g_ref[...]).astype(BF16)

    t = jnp.dot(h_sc[...], w_ref[...], preferred_element_type=F32)
    if rope:
        tn = t.shape[1]
        reps = tn // LANES
        cos = jnp.tile(cos_ref[...], (1, reps))
        sin = jnp.tile(sin_ref[...], (1, reps))
        lane = lax.broadcasted_iota(jnp.int32, t.shape, 1)
        first = (lane & (HEAD_DIM - 1)) < HEAD_DIM // 2
        rot = jnp.where(first, pltpu.roll(t, tn - HEAD_DIM // 2, 1), pltpu.roll(t, HEAD_DIM // 2, 1))
        t = t * cos + rot * sin
    o_ref[...] = (t + b_ref[...]).astype(o_ref.dtype)


def norm_proj(x, g, w, bias, rope_tabs, seq, out_dtype, tm=512):
    n, d = x.shape
    nc = w.shape[1]
    tn = _col_tile(nc, 1536)
    assert n % tm == 0 and seq % tm == 0
    rope = rope_tabs is not None
    in_specs = [
        pl.BlockSpec((tm, d), lambda i, j: (i, 0)),
        pl.BlockSpec((1, d), lambda i, j: (0, 0)),
        pl.BlockSpec((d, tn), lambda i, j: (0, j)),
        pl.BlockSpec((1, tn), lambda i, j: (0, j)),
    ]
    args = [x, g.reshape(1, d), w, bias.reshape(1, nc)]
    if rope:
        spt = seq // tm
        in_specs += [pl.BlockSpec((tm, LANES), lambda i, j: (i % spt, 0))] * 2
        args += list(rope_tabs)
    return pl.pallas_call(
        functools.partial(_norm_proj_kernel, rope=rope),
        grid=(n // tm, nc // tn),
        in_specs=in_specs,
        out_specs=pl.BlockSpec((tm, tn), lambda i, j: (i, j)),
        out_shape=jax.ShapeDtypeStruct((n, nc), out_dtype),
        scratch_shapes=[pltpu.VMEM((tm, d), BF16)],
        compiler_params=_cparams("parallel", "arbitrary"),
        name="norm_proj_rope" if rope else "norm_proj",
    )(*args)


def _norm_proj_t_kernel(x_ref, g_ref, wt_ref, *rest, rope):
    if rope:
        cos_ref, sin_ref, o_ref, h_sc = rest
    else:
        o_ref, h_sc = rest

    @pl.when(pl.program_id(1) == 0)
    def _():
        h_sc[...] = _rms(x_ref[...], g_ref[...]).astype(BF16)

    t = lax.dot_general(wt_ref[...], h_sc[...], NT_DIMS, preferred_element_type=F32)
    tn, tm = t.shape
    nheads = tn // HEAD_DIM
    if rope:
        cos = jnp.tile(cos_ref[...], (nheads, 1))
        sin = jnp.tile(sin_ref[...], (nheads, 1))
        r = lax.broadcasted_iota(jnp.int32, t.shape, 0)
        first = (r & (HEAD_DIM - 1)) < HEAD_DIM // 2
        rot = jnp.where(first, pltpu.roll(t, tn - HEAD_DIM // 2, 0), pltpu.roll(t, HEAD_DIM // 2, 0))
        t = t * cos + rot * sin
    spare = jnp.zeros((LANES - HEAD_DIM, tm), o_ref.dtype)
    for h in range(nheads):
        o_ref[0, h * LANES:h * LANES + HEAD_DIM, :] = t[h * HEAD_DIM:(h + 1) * HEAD_DIM].astype(o_ref.dtype)
        o_ref[0, h * LANES + HEAD_DIM:(h + 1) * LANES, :] = spare


def norm_proj_t(x, g, wt, rope_tabs_t, b, seq, tm=512):
    n, d = x.shape
    nc = wt.shape[0]
    tn = _col_tile(nc, 512)
    assert n % tm == 0 and seq % tm == 0 and nc % tn == 0
    spt = seq // tm
    rope = rope_tabs_t is not None
    in_specs = [
        pl.BlockSpec((tm, d), lambda i, j: (i, 0)),
        pl.BlockSpec((1, d), lambda i, j: (0, 0)),
        pl.BlockSpec((tn, d), lambda i, j: (j, 0)),
    ]
    args = [x, g.reshape(1, d), wt]
    if rope:
        in_specs += [pl.BlockSpec((HEAD_DIM, tm), lambda i, j: (0, i % spt))] * 2
        args += list(rope_tabs_t)
    return pl.pallas_call(
        functools.partial(_norm_proj_t_kernel, rope=rope),
        grid=(n // tm, nc // tn),
        in_specs=in_specs,
        out_specs=pl.BlockSpec((1, 2 * tn, tm), lambda i, j: (i // spt, j, i % spt)),
        out_shape=jax.ShapeDtypeStruct((b, 2 * nc, seq), BF16),
        scratch_shapes=[pltpu.VMEM((tm, d), BF16)],
        compiler_params=_cparams("parallel", "arbitrary"),
        name="norm_proj_t_rope" if rope else "norm_proj_t",
    )(*args)


def _fox_decay_kernel(fl_ref, b_ref, pq_ref, pk_ref, oq_ref, ok_ref, carry_sc, *, nh):
    @pl.when(pl.program_id(1) == 0)
    def _():
        carry_sc[...] = jnp.zeros(carry_sc.shape, F32)

    c = jax.nn.log_sigmoid(fl_ref[0] + b_ref[...])
    ts = c.shape[0]
    row = lax.broadcasted_iota(jnp.int32, c.shape, 0)
    lane = lax.broadcasted_iota(jnp.int32, c.shape, 1)
    sh = 1
    while sh < ts:
        c = c + jnp.where(row >= sh, pltpu.roll(c, sh, 0), 0.0)
        sh *= 2
    c = c + carry_sc[0:1, :]
    carry_sc[0:1, :] = c[ts - 1:ts, :]
    hi, mid, lo = _split3(c)
    c3 = jnp.where(lane < nh, hi,
                   jnp.where(lane < 2 * nh, pltpu.roll(mid, nh, 1),
                             jnp.where(lane < 3 * nh, pltpu.roll(lo, 2 * nh, 1),
                                       jnp.where(lane == 3 * nh, 1.0, 0.0)))).astype(BF16)
    for h in range(nh):
        oq_ref[0, h, 0] = jnp.dot(c3, pq_ref[h], preferred_element_type=F32).astype(oq_ref.dtype)
        ok_ref[0, h] = lax.dot_general(pk_ref[h], c3, NT_DIMS,
                                       preferred_element_type=F32).astype(ok_ref.dtype)


def _fox_placement(nh):
    pq = np.zeros((nh, LANES, LANES), np.float32)
    pk = np.zeros((nh, LANES, LANES), np.float32)
    one = 3 * nh
    for h in range(nh):
        for t in range(3):
            pq[h, one, HEAD_DIM + t] = 1.0
            pq[h, t * nh + h, HEAD_DIM + 3 + t] = 1.0
            pk[h, HEAD_DIM + t, t * nh + h] = -1.0
            pk[h, HEAD_DIM + 3 + t, one] = 1.0
    return jnp.asarray(pq, BF16), jnp.asarray(pk, BF16)


def fox_decay(small, b_row, nh, ts=2048):
    b, s, _ = small.shape
    ts = min(ts, s)
    pq, pk = _fox_placement(nh)
    return pl.pallas_call(
        functools.partial(_fox_decay_kernel, nh=nh),
        grid=(b, s // ts),
        in_specs=[pl.BlockSpec((1, ts, LANES), lambda i, j: (i, j, 0)),
                  pl.BlockSpec((1, LANES), lambda i, j: (0, 0)),
                  pl.BlockSpec((nh, LANES, LANES), lambda i, j: (0, 0, 0)),
                  pl.BlockSpec((nh, LANES, LANES), lambda i, j: (0, 0, 0))],
        out_specs=[pl.BlockSpec((1, nh, 1, ts, LANES), lambda i, j: (i, 0, 0, j, 0)),
                   pl.BlockSpec((1, nh, LANES, ts), lambda i, j: (i, 0, 0, j))],
        out_shape=[jax.ShapeDtypeStruct((b, nh, 1, s, LANES), BF16),
                   jax.ShapeDtypeStruct((b, nh, LANES, s), BF16)],
        scratch_shapes=[pltpu.VMEM((8, LANES), F32)],
        compiler_params=_cparams("parallel", "arbitrary"),
        name="fox_decay",
    )(small, b_row, pq, pk)


def _nsa_compress_kernel(t_ref, pe_ref, w1_ref, w2_ref, o_ref):
    t = t_ref[0].astype(F32)
    ncp = t.shape[0]
    a = jnp.dot((t + pe_ref[0:1, :]).astype(BF16), w1_ref[0, 0], preferred_element_type=F32)
    bm = jnp.dot((t + pe_ref[1:2, :]).astype(BF16), w1_ref[0, 1], preferred_element_type=F32)
    pre = a + pltpu.roll(bm, ncp - 1, 0)
    hid = jax.nn.gelu(pre)
    o_ref[0, 0] = jnp.dot(hid.astype(BF16), w2_ref[...], preferred_element_type=F32).astype(o_ref.dtype)


def nsa_compress(t, pe, w1, w2):
    b, ncp, cw = t.shape
    g, _, _, hid = w1.shape
    return pl.pallas_call(
        _nsa_compress_kernel,
        grid=(b, g),
        in_specs=[pl.BlockSpec((1, ncp, cw), lambda i, j: (i, 0, 0)),
                  pl.BlockSpec((2, cw), lambda i, j: (0, 0)),
                  pl.BlockSpec((1, 2, cw, hid), lambda i, j: (j, 0, 0, 0)),
                  pl.BlockSpec((hid, LANES), lambda i, j: (0, 0))],
        out_specs=pl.BlockSpec((1, 1, ncp, LANES), lambda i, j: (i, j, 0, 0)),
        out_shape=jax.ShapeDtypeStruct((b, g, ncp, LANES), BF16),
        compiler_params=_cparams("parallel", "parallel"),
        name="nsa_compress",
    )(t, pe, w1, w2)


def _topk_mask(work, col, k):
    sel = jnp.zeros(work.shape, jnp.bool_)
    col = col.astype(F32)
    for _ in range(k):
        mx = jnp.max(work, axis=-1, keepdims=True)
        first = jnp.min(jnp.where(work == mx, col, 1e9), axis=-1, keepdims=True)
        hit = col == first
        sel = jnp.logical_or(sel, hit)
        work = jnp.where(hit, -jnp.inf, work)
    return sel


def _nsa_select_kernel(q_ref, kct_ref, vc_ref, m_ref, oc_ref, sb_ref, *, tq, rep, nsup):
    i = pl.program_id(2)
    ncp = kct_ref.shape[-1]
    ns = m_ref.shape[-1]
    qpos = i * tq + lax.broadcasted_iota(jnp.int32, (tq, 1), 0)
    cend = lax.broadcasted_iota(jnp.int32, (1, ncp), 1) * CMP_STRIDE + (CMP_LEN - 1)
    cmask = cend <= qpos
    has_block = jnp.where(qpos >= CMP_LEN - 1, 1.0, 0.0)
    kct = kct_ref[0, 0]
    vc = vc_ref[0, 0]
    pcs = jnp.zeros((tq, ncp), F32)
    for r in range(rep):
        s = jnp.dot(q_ref[0, :, r * LANES:(r + 1) * LANES], kct, preferred_element_type=F32)
        s = jnp.where(cmask, s, MASKED)
        e = jnp.exp(s - jnp.max(s, axis=-1, keepdims=True))
        p = e * (has_block / jnp.maximum(jnp.sum(e, axis=-1, keepdims=True), 1e-30))
        oc_ref[0, :, r * LANES:(r + 1) * LANES] = jnp.dot(
            p.astype(BF16), vc, preferred_element_type=F32).astype(oc_ref.dtype)
        pcs = pcs + p
    mm = m_ref[...]
    imp = sum(jnp.dot(part.astype(BF16), mm, preferred_element_type=F32) for part in _split3(pcs))
    sblk = lax.broadcasted_iota(jnp.int32, (1, ns), 1)
    qblk = qpos >> 6
    forced = (sblk == 0) | (sblk == qblk) | (sblk == qblk - 1)
    free = jnp.where(forced | (sblk > qblk), -jnp.inf, imp)
    sel = _topk_mask(free, sblk, min(SLC_TOPN, ns) - 3)
    bias = jnp.where((sel | forced) & (sblk <= qblk), 0.0, MASKED).astype(sb_ref.dtype)
    zeros = jnp.zeros((tq, HEAD_DIM), sb_ref.dtype)
    for j in range(nsup):
        sb_ref[0, 0, j] = jnp.concatenate([zeros, bias[:, j * HEAD_DIM:(j + 1) * HEAD_DIM]], axis=-1)


def nsa_select(q_all, kct, vc, m, rep, tq=512):
    b, s, _ = q_all.shape
    g = kct.shape[1]
    ncp = kct.shape[-1]
    ns = m.shape[-1]
    nsup = ns // HEAD_DIM
    assert SLC_BLOCK == 64 and ns % HEAD_DIM == 0
    tq = min(tq, s)
    return pl.pallas_call(
        functools.partial(_nsa_select_kernel, tq=tq, rep=rep, nsup=nsup),
        grid=(b, g, s // tq),
        in_specs=[pl.BlockSpec((1, tq, rep * LANES), lambda bi, gi, i: (bi, i, gi)),
                  pl.BlockSpec((1, 1, LANES, ncp), lambda bi, gi, i: (bi, gi, 0, 0)),
                  pl.BlockSpec((1, 1, ncp, LANES), lambda bi, gi, i: (bi, gi, 0, 0)),
                  pl.BlockSpec((ncp, ns), lambda bi, gi, i: (0, 0))],
        out_specs=[pl.BlockSpec((1, tq, rep * LANES), lambda bi, gi, i: (bi, i, gi)),
                   pl.BlockSpec((1, 1, nsup, tq, LANES), lambda bi, gi, i: (bi, gi, 0, i, 0))],
        out_shape=[jax.ShapeDtypeStruct((b, s, g * rep * LANES), BF16),
                   jax.ShapeDtypeStruct((b, g, nsup, s, LANES), BF16)],
        compiler_params=_cparams("parallel", "parallel", "parallel"),
        name="nsa_select",
    )(q_all, kct, vc, m)


def _moba_select_kernel(q_ref, kt_ref, ind_ref, sb_ref, kbar_sc, *, tq):
    i = pl.program_id(2)

    @pl.when(i == 0)
    def _():
        kbar_sc[...] = jnp.dot(kt_ref[0], ind_ref[...], preferred_element_type=F32) * (1.0 / MOBA_BLOCK)

    q = q_ref[0]
    gate = sum(jnp.dot(q, part.astype(BF16), preferred_element_type=F32)
               for part in _split3(kbar_sc[...]))
    qpos = i * tq + lax.broadcasted_iota(jnp.int32, (tq, 1), 0)
    cur = qpos >> 8
    lane = lax.broadcasted_iota(jnp.int32, (1, LANES), 1)
    blk = lane - HEAD_DIM
    past = (blk >= 0) & (blk < cur)
    sel = _topk_mask(jnp.where(past, gate, -jnp.inf), lane, MOBA_TOPK)
    keep = (blk < 0) | (sel & past) | (blk == cur)
    sb_ref[0, 0, 0] = jnp.where(keep, 0.0, MASKED).astype(sb_ref.dtype)


def moba_select(q_all, kt_all, ind, nh, tq=2048):
    b, s, _ = q_all.shape
    assert MOBA_BLOCK == 256 and s // MOBA_BLOCK <= HEAD_DIM
    tq = min(tq, s)
    return pl.pallas_call(
        functools.partial(_moba_select_kernel, tq=tq),
        grid=(b, nh, s // tq),
        in_specs=[pl.BlockSpec((1, tq, LANES), lambda bi, hi, i: (bi, i, hi)),
                  pl.BlockSpec((1, LANES, s), lambda bi, hi, i: (bi, hi, 0)),
                  pl.BlockSpec((s, LANES), lambda bi, hi, i: (0, 0))],
        out_specs=pl.BlockSpec((1, 1, 1, tq, LANES), lambda bi, hi, i: (bi, hi, 0, i, 0)),
        out_shape=jax.ShapeDtypeStruct((b, nh, 1, s, LANES), BF16),
        scratch_shapes=[pltpu.VMEM((LANES, LANES), F32)],
        compiler_params=_cparams("parallel", "parallel", "arbitrary"),
        name="moba_select",
    )(q_all, kt_all, ind)


def _flash_kernel(*refs, tq, tk, rs, tps, band, decay, has_extra, has_kx):
    refs = list(refs)
    qa_ref = refs.pop(0)
    ex_ref = refs.pop(0) if has_extra else None
    kt_ref = refs.pop(0)
    kx_ref = refs.pop(0) if has_kx else None
    kmax_sc = refs.pop() if decay else None
    v_ref, o_ref, m_sc, acc_sc = refs
    i = pl.program_id(2)
    m_sc[...] = jnp.full(m_sc.shape, M_INIT, F32)
    acc_sc[...] = jnp.zeros(acc_sc.shape, F32)
    row = lax.broadcasted_iota(jnp.int32, (rs, tk), 0)
    col = lax.broadcasted_iota(jnp.int32, (rs, tk), 1)
    nst = tq // rs
    kpq = tq // tk

    def run(items):
        tiles = {}

        def operands(j):
            if id(j) not in tiles:
                start = pl.multiple_of(j * tk, tk)
                kt = kt_ref[0, :, pl.ds(start, tk)]
                if kx_ref is not None:
                    kt = kt + kx_ref[0, 0, :, pl.ds(start, tk)]
                tiles[id(j)] = (kt, v_ref[0, pl.ds(start, tk), :])
            return tiles[id(j)]

        def logits(item):
            j, r, _ = item
            rows = pl.ds(r * rs, rs)
            qa = qa_ref[0, rows, :]
            if ex_ref is not None:
                qa = qa + ex_ref[0, 0, j // tps, rows, :]
            return jnp.dot(qa, operands(j)[0], preferred_element_type=F32)

        pending = [logits(it) for it in items[:FLASH_AHEAD]]
        for n, (j, r, mask) in enumerate(items):
            rows = pl.ds(r * rs, rs)
            s = pending.pop(0)
            if n + FLASH_AHEAD < len(items):
                pending.append(logits(items[n + FLASH_AHEAD]))
            if mask is not None:
                s = jnp.where(mask, s, MASKED)
            m_prev = m_sc[rows, :]
            m_new = jnp.maximum(m_prev, jnp.max(s, axis=-1, keepdims=True))
            p = jnp.exp(s - jnp.tile(m_new, (1, tk // LANES)))
            alpha = jnp.exp(m_prev - m_new)
            acc_sc[rows, :] = alpha * acc_sc[rows, :] + jnp.dot(
                p.astype(BF16), operands(j)[1], preferred_element_type=F32)
            m_sc[rows, :] = m_new

    def edge_items(dj_list, j_of):
        items = []
        for dj in dj_list:
            j = j_of(dj)
            for r in range(nst):
                off, ko = r * rs, dj * tk
                lo = off - WINDOW + 1 if band else None
                if ko > off + rs - 1 or (band and ko + tk - 1 < lo):
                    continue
                full = ko + tk - 1 <= off and (not band or ko > off + rs - 1 - WINDOW)
                mask = None
                if not full:
                    mask = col + ko <= row + off
                    if band:
                        mask = mask & (col + ko > row + (off - WINDOW))
                items.append((j, r, mask))
        return items

    if band:
        for dj in range(-(WINDOW // tk), 0):
            @pl.when(i * kpq + dj >= 0)
            def _():
                jj = i * kpq + dj
                run(edge_items([dj], lambda _: jj))
    def body(jj, carry):
        tiles = [jj * kpq + dj for dj in range(kpq)]
        run([(j, r, None) for j in tiles for r in range(nst)])
        return carry

    def run_diag():
        diag = [i * kpq + dj for dj in range(kpq)]
        run(edge_items(list(range(kpq)), lambda dj: diag[dj]))

    if decay:
        s_len = kt_ref.shape[-1]

        @pl.when(i == 0)
        def _():
            def chunk(c, best):
                kk = kt_ref[0, :, pl.ds(pl.multiple_of(c * tq, tq), tq)].astype(F32)
                return jnp.maximum(best, jnp.sum(kk * kk, axis=0, keepdims=True))
            ksq = lax.fori_loop(0, s_len // tq, chunk, jnp.zeros((1, tq), F32))
            kmax_sc[...] = jnp.broadcast_to(jnp.sqrt(jnp.max(ksq, axis=1, keepdims=True)), kmax_sc.shape)

        run_diag()
        q = qa_ref[0].astype(F32)
        qn = jnp.sqrt(jnp.sum(q * q, axis=1, keepdims=True))
        slack = jnp.max(qn * kmax_sc[0:1, 0:1] - m_sc[:, 0:1], axis=0, keepdims=True)
        cvec = -jnp.sum(kx_ref[0, 0, HEAD_DIM:HEAD_DIM + 3, :].astype(F32), axis=0, keepdims=True)
        pos = lax.broadcasted_iota(jnp.int32, (1, s_len), 1)
        q0 = i * tq
        c_q0 = jnp.sum(jnp.where(pos == q0, cvec, 0.0), axis=1, keepdims=True)
        dead = (pos < q0) & (slack + c_q0 - cvec <= -DECAY_CUT)
        n_dead = jnp.sum(jnp.where(dead, 1.0, 0.0)).astype(jnp.int32) // tq
        lax.fori_loop(0, i - n_dead, lambda t, carry: body(i - 1 - t, carry), 0)
    elif band:
        run_diag()
    else:
        lax.fori_loop(0, i, body, 0)
        run_diag()
    acc = acc_sc[...]
    o_ref[0] = (acc / acc[:, HEAD_DIM:HEAD_DIM + 1]).astype(o_ref.dtype)


def flash(name, q_all, q0, kt_all, k0, v_all, v0, nh, rep, extra=None, kx=None, band=False,
          decay=False):
    assert not decay or (kx is not None and not band)
    b, s, _ = q_all.shape
    tk = min(WINDOW if band else FLASH_TK, s)
    tq = tk if band else min(FLASH_TQ, s)
    rs = min(FLASH_STRIP, tq)
    assert s % tq == 0 and tq % tk == 0 and tq % rs == 0 and (not band or WINDOW % tk == 0)
    in_specs = [pl.BlockSpec((1, tq, LANES), lambda bi, hi, i: (bi, i, q0 + hi))]
    args = [q_all]
    tps = 1
    if extra is not None:
        he, nsup = extra.shape[1], extra.shape[2]
        rep_e = nh // he
        assert (s // nsup) % tk == 0
        tps = (s // nsup) // tk
        in_specs.append(pl.BlockSpec((1, 1, nsup, tq, LANES), lambda bi, hi, i: (bi, hi // rep_e, 0, i, 0)))
        args.append(extra)
    in_specs.append(pl.BlockSpec((1, LANES, s), lambda bi, hi, i: (bi, k0 + hi // rep, 0)))
    args.append(kt_all)
    if kx is not None:
        bx, hx = kx.shape[0], kx.shape[1]
        in_specs.append(pl.BlockSpec(
            (1, 1, LANES, s), lambda bi, hi, i: (bi if bx > 1 else 0, hi if hx > 1 else 0, 0, 0)))
        args.append(kx)
    in_specs.append(pl.BlockSpec((1, s, LANES), lambda bi, hi, i: (bi, 0, v0 + hi // rep)))
    args.append(v_all)
    return pl.pallas_call(
        functools.partial(_flash_kernel, tq=tq, tk=tk, rs=rs, tps=tps, band=band, decay=decay,
                          has_extra=extra is not None, has_kx=kx is not None),
        grid=(b, nh, s // tq),
        in_specs=in_specs,
        out_specs=pl.BlockSpec((1, tq, LANES), lambda bi, hi, i: (bi, i, hi)),
        out_shape=jax.ShapeDtypeStruct((b, s, nh * LANES), BF16),
        scratch_shapes=[pltpu.VMEM((tq, LANES), F32), pltpu.VMEM((tq, LANES), F32)]
        + ([pltpu.VMEM((8, LANES), F32)] if decay else []),
        compiler_params=_cparams("parallel", "parallel", "arbitrary" if decay else "parallel"),
        name=name,
    )(*args)


def _proj_norm_res_kernel(a_ref, w_ref, g_ref, x_ref, o_ref):
    y = jnp.dot(a_ref[...], w_ref[...], preferred_element_type=F32)
    o_ref[...] = x_ref[...] + _rms(y, g_ref[...])


def proj_norm_res(a, w, g, x, tm=512):
    n, k = a.shape
    d = w.shape[1]
    return pl.pallas_call(
        _proj_norm_res_kernel,
        grid=(n // tm,),
        in_specs=[pl.BlockSpec((tm, k), lambda i: (i, 0)),
                  pl.BlockSpec((k, d), lambda i: (0, 0)),
                  pl.BlockSpec((1, d), lambda i: (0, 0)),
                  pl.BlockSpec((tm, d), lambda i: (i, 0))],
        out_specs=pl.BlockSpec((tm, d), lambda i: (i, 0)),
        out_shape=jax.ShapeDtypeStruct((n, d), F32),
        compiler_params=_cparams("parallel"),
        name="proj_norm_res",
    )(a, w, g.reshape(1, d), x)


def _even_out_kernel(of_ref, oc_ref, os_ref, ow_ref, gl_ref, wf_ref, wn_ref, g_ref, x_ref, o_ref,
                     *, nh, g0):
    gate = jax.nn.sigmoid(gl_ref[...])
    parts = []
    for h in range(nh):
        sl = slice(h * LANES, (h + 1) * LANES)
        c = g0 + 3 * h
        parts.append(gate[:, c:c + 1] * oc_ref[:, sl].astype(F32)
                     + gate[:, c + 1:c + 2] * os_ref[:, sl].astype(F32)
                     + gate[:, c + 2:c + 3] * ow_ref[:, sl].astype(F32))
    a = jnp.concatenate(parts, axis=-1).astype(BF16)
    y = (jnp.dot(of_ref[...], wf_ref[...], preferred_element_type=F32)
         + jnp.dot(a, wn_ref[...], preferred_element_type=F32))
    o_ref[...] = x_ref[...] + _rms(y, g_ref[...])


def even_out(o_fox, o_cmp, o_slc, o_win, small, wf, wn, g, x, nh, g0, tm=512):
    n, k = o_fox.shape
    d = wf.shape[1]
    act = pl.BlockSpec((tm, k), lambda i: (i, 0))
    wspec = pl.BlockSpec((k, d), lambda i: (0, 0))
    return pl.pallas_call(
        functools.partial(_even_out_kernel, nh=nh, g0=g0),
        grid=(n // tm,),
        in_specs=[act, act, act, act,
                  pl.BlockSpec((tm, LANES), lambda i: (i, 0)),
                  wspec, wspec,
                  pl.BlockSpec((1, d), lambda i: (0, 0)),
                  pl.BlockSpec((tm, d), lambda i: (i, 0))],
        out_specs=pl.BlockSpec((tm, d), lambda i: (i, 0)),
        out_shape=jax.ShapeDtypeStruct((n, d), F32),
        compiler_params=_cparams("parallel"),
        name="even_out",
    )(o_fox, o_cmp, o_slc, o_win, small, wf, wn, g.reshape(1, d), x)


def _ffn_kernel(x_ref, gpre_ref, wg_ref, wu_ref, wd_ref, gpost_ref, o_ref, h_sc, acc_sc):
    f = pl.program_id(1)

    @pl.when(f == 0)
    def _():
        h_sc[...] = _rms(x_ref[...], gpre_ref[...]).astype(BF16)
        acc_sc[...] = jnp.zeros(acc_sc.shape, F32)

    h = h_sc[...]
    a = jnp.dot(h, wg_ref[...], preferred_element_type=F32)
    u = jnp.dot(h, wu_ref[...], preferred_element_type=F32)
    act = (jax.nn.silu(a) * u).astype(BF16)
    acc_sc[...] += jnp.dot(act, wd_ref[...], preferred_element_type=F32)

    @pl.when(f == pl.num_programs(1) - 1)
    def _():
        o_ref[...] = x_ref[...] + _rms(acc_sc[...], gpost_ref[...])


def ffn(x, gpre, wg, wu, wd, gpost, tm=512, tf=1408):
    n, d = x.shape
    dff = wg.shape[1]
    assert n % tm == 0 and dff % tf == 0
    return pl.pallas_call(
        _ffn_kernel,
        grid=(n // tm, dff // tf),
        in_specs=[pl.BlockSpec((tm, d), lambda i, f: (i, 0)),
                  pl.BlockSpec((1, d), lambda i, f: (0, 0)),
                  pl.BlockSpec((d, tf), lambda i, f: (0, f)),
                  pl.BlockSpec((d, tf), lambda i, f: (0, f)),
                  pl.BlockSpec((tf, d), lambda i, f: (f, 0)),
                  pl.BlockSpec((1, d), lambda i, f: (0, 0))],
        out_specs=pl.BlockSpec((tm, d), lambda i, f: (i, 0)),
        out_shape=jax.ShapeDtypeStruct((n, d), F32),
        scratch_shapes=[pltpu.VMEM((tm, d), BF16), pltpu.VMEM((tm, d), F32)],
        compiler_params=_cparams("parallel", "arbitrary"),
        name="ffn",
    )(x, gpre.reshape(1, d), wg, wu, wd, gpost.reshape(1, d))


def _rope_tables(s):
    inv = ROPE_THETA ** (-jnp.arange(0, HEAD_DIM, 2, dtype=F32) / HEAD_DIM)
    ang = jnp.arange(s, dtype=F32)[:, None] * inv[None, :]
    cos, sin = jnp.cos(ang), jnp.sin(ang)
    reps = LANES // HEAD_DIM
    cos2 = jnp.tile(jnp.concatenate([cos, cos], -1), (1, reps))
    sin2 = jnp.tile(jnp.concatenate([-sin, sin], -1), (1, reps))
    return (cos2, sin2), (cos2.T[:HEAD_DIM], sin2.T[:HEAD_DIM])


def _slab_cols(w, scale=1.0):
    d, c = w.shape
    w3 = (w * scale).reshape(d, c // HEAD_DIM, HEAD_DIM)
    return jnp.concatenate([w3, jnp.zeros_like(w3)], axis=-1).reshape(d, 2 * c)


def _slab_rows(w):
    return _slab_cols(w.T).T


def _ones_lane(n_slabs):
    one = np.zeros((n_slabs, LANES), np.float32)
    one[:, HEAD_DIM] = 1.0
    return jnp.asarray(one.reshape(-1))


def _block_indicator_rows(s, block):
    blk = (np.arange(s) // block) % HEAD_DIM
    ind = np.zeros((LANES, s), np.float32)
    ind[HEAD_DIM + blk, np.arange(s)] = 1.0
    return jnp.asarray(ind, BF16)[None, None]


def _overlap_matrix(ncp, ns):
    ratio = SLC_BLOCK // CMP_STRIDE
    m = np.arange(ncp)[:, None]
    j = np.arange(ns)[None, :]
    ok = (m >= ratio * j - 1) & (m <= ratio * j + ratio - 1) & (m < ncp - 1)
    return jnp.asarray(ok, BF16)


def _compress_weights(pe, w1, w2, g):
    hid = w1.shape[-1]
    w1r = w1.reshape(2, CMP_STRIDE, HEAD_DIM, hid)
    w1g = jnp.zeros((g, 2, CMP_STRIDE, g, HEAD_DIM, hid), F32)
    for gi in range(g):
        w1g = w1g.at[gi, :, :, gi].set(w1r)
    w1g = w1g.reshape(g, 2, CMP_STRIDE * g * HEAD_DIM, hid).astype(BF16)
    pe2 = jnp.broadcast_to(pe.reshape(2, CMP_STRIDE, 1, HEAD_DIM), (2, CMP_STRIDE, g, HEAD_DIM))
    w2p = jnp.concatenate([w2, jnp.zeros_like(w2)], axis=-1).astype(BF16)
    return pe2.reshape(2, CMP_STRIDE * g * HEAD_DIM), w1g, w2p


def _out_rows(w):
    return _slab_rows(w).astype(BF16)


def _even_layer(x2, b, s, g_pre, g_post, w_in, b_f, pe_k, w1_k, w2_k, pe_v, w1_v, w2_v, w_out, tabs, tabs_t):
    hf, hn, g = 8, 8, 2
    hd = HEAD_DIM
    rep = hn // g
    (w_fq, w_fk, w_fv, w_fl, w_nq, w_kc, w_vc, w_ks, w_vs, w_kw, w_vw, w_gl) = jnp.split(
        w_in, list(np.cumsum([hf * hd] * 3 + [hf] + [hn * hd] + [g * hd] * 6)), axis=1)
    scale = hd ** -0.5

    w_plain = jnp.concatenate([_slab_cols(w_fq, scale), _slab_cols(w_fv), _slab_cols(w_vs),
                               _slab_cols(w_vw), w_vc], axis=1).astype(BF16)
    bias_plain = jnp.concatenate([jnp.zeros((hf * LANES,), F32), _ones_lane(hf + 2 * g),
                                  jnp.zeros((g * hd,), F32)])
    p_plain = norm_proj(x2, g_pre, w_plain, bias_plain, None, s, BF16).reshape(b, s, -1)
    fq0, fv0, vs0, vw0 = 0, hf, 2 * hf, 2 * hf + g
    vc_col = (2 * hf + 2 * g) * LANES
    w_rope = jnp.concatenate([_slab_cols(w_nq, scale), w_kc], axis=1).astype(BF16)
    p_rope = norm_proj(x2, g_pre, w_rope, jnp.zeros((w_rope.shape[1],), F32), tabs, s, BF16
                       ).reshape(b, s, -1)
    n_small = hf + 3 * hn
    w_small = jnp.pad(jnp.concatenate([w_fl, w_gl], axis=1), ((0, 0), (0, LANES - n_small))).astype(BF16)
    p_small = norm_proj(x2, g_pre, w_small, jnp.zeros((LANES,), F32), None, s, F32)
    kt_fox = norm_proj_t(x2, g_pre, w_fk.T.astype(BF16), None, b, s)
    kt_nsa = norm_proj_t(x2, g_pre, jnp.concatenate([w_ks, w_kw], axis=1).T.astype(BF16),
                         tabs_t, b, s)

    qx, kx = fox_decay(p_small.reshape(b, s, LANES), jnp.pad(b_f, (0, LANES - hf)).reshape(1, LANES), hf)
    o_fox = flash("flash_fox", p_plain, fq0, kt_fox, 0, p_plain, fv0, hf, 1, extra=qx, kx=kx, decay=True)

    ncp = s // CMP_STRIDE
    ns = s // SLC_BLOCK
    kc = p_rope[:, :, hn * LANES:].reshape(b, ncp, CMP_STRIDE * g * hd)
    vc = p_plain[:, :, vc_col:].reshape(b, ncp, CMP_STRIDE * g * hd)
    kcmp = nsa_compress(kc, *_compress_weights(pe_k, w1_k, w2_k, g))
    vcmp = nsa_compress(vc, *_compress_weights(pe_v, w1_v, w2_v, g))
    o_cmp, selb = nsa_select(p_rope, kcmp.transpose(0, 1, 3, 2), vcmp, _overlap_matrix(ncp, ns), rep)
    o_slc = flash("flash_sel", p_rope, 0, kt_nsa, 0, p_plain, vs0, hn, rep, extra=selb,
                  kx=_block_indicator_rows(s, SLC_BLOCK))
    o_win = flash("flash_band", p_rope, 0, kt_nsa, g, p_plain, vw0, hn, rep, band=True)

    n = b * s
    return even_out(o_fox.reshape(n, -1), o_cmp.reshape(n, -1), o_slc.reshape(n, -1),
                    o_win.reshape(n, -1), p_small, _out_rows(w_out[:hf * hd]), _out_rows(w_out[hf * hd:]),
                    g_post, x2, hn, hf)


def _odd_layer(x2, b, s, g_pre, g_post, w_in, w_out, tabs, tabs_t):
    h = 16
    hd = HEAD_DIM
    d = h * hd
    scale = hd ** -0.5
    w_q, w_k, w_v = w_in[:, :d], w_in[:, d:2 * d], w_in[:, 2 * d:]
    q_all = norm_proj(x2, g_pre, _slab_cols(w_q, scale).astype(BF16), jnp.zeros((h * LANES,), F32),
                      tabs, s, BF16).reshape(b, s, -1)
    v_all = norm_proj(x2, g_pre, _slab_cols(w_v).astype(BF16), _ones_lane(h), None, s, BF16
                      ).reshape(b, s, -1)
    kt_all = norm_proj_t(x2, g_pre, w_k.T.astype(BF16), tabs_t, b, s)
    ind = _block_indicator_rows(s, MOBA_BLOCK)
    selb = moba_select(q_all, kt_all, ind[0, 0].T, h)
    o = flash("flash_moba", q_all, 0, kt_all, 0, v_all, 0, h, 1, extra=selb, kx=ind)
    return proj_norm_res(o.reshape(b * s, -1), _out_rows(w_out), g_post, x2)


def kernel(x, ev_w_in, ev_b_f, ev_cmp_pe_k, ev_cmp_w1_k, ev_cmp_w2_k, ev_cmp_pe_v, ev_cmp_w1_v,
           ev_cmp_w2_v, ev_w_out, od_w_in, od_w_out, g_mix_pre, g_mix_post, g_ffn_pre, g_ffn_post,
           ffn_w_gate, ffn_w_up, ffn_w_down):
    b, s, d = x.shape
    depth = g_mix_pre.shape[0]
    tabs, tabs_t = _rope_tables(s)
    x2 = x.reshape(b * s, d)
    for layer in range(depth):
        if layer % 2 == 0:
            e = layer // 2
            x2 = _even_layer(x2, b, s, g_mix_pre[layer], g_mix_post[layer], ev_w_in[e], ev_b_f[e],
                             ev_cmp_pe_k[e], ev_cmp_w1_k[e], ev_cmp_w2_k[e], ev_cmp_pe_v[e],
                             ev_cmp_w1_v[e], ev_cmp_w2_v[e], ev_w_out[e], tabs, tabs_t)
        else:
            o = layer // 2
            x2 = _odd_layer(x2, b, s, g_mix_pre[layer], g_mix_post[layer], od_w_in[o], od_w_out[o],
                            tabs, tabs_t)
        x2 = ffn(x2, g_ffn_pre[layer], ffn_w_gate[layer].astype(BF16), ffn_w_up[layer].astype(BF16),
                 ffn_w_down[layer].astype(BF16), g_ffn_post[layer])
    return x2.reshape(b, s, d)
```

```python
import functools

import jax
import jax.numpy as jnp
import numpy as np
from jax import lax
from jax.experimental import pallas as pl
from jax.experimental.pallas import tpu as pltpu

F32 = jnp.float32
BF16 = jnp.bfloat16

HEAD_DIM = 64
LANES = 128
ROPE_THETA = 10000.0
RMS_EPS = 1e-6
CMP_STRIDE = 16
CMP_LEN = 32
SLC_BLOCK = 64
SLC_TOPN = 16
WINDOW = 512
MOBA_BLOCK = 256
MOBA_TOPK = 3
MASKED = -1e30
M_INIT = -1e29
DECAY_CUT = 100.0
FLASH_TQ = 1024
FLASH_TK = 512
FLASH_STRIP = 512
FLASH_STRIP_MAIN = 1024
FLASH_AHEAD = 2
VMEM_LIMIT = 48 * 1024 * 1024
NT_DIMS = (((1,), (1,)), ((), ()))


def _cparams(*sem):
    return pltpu.CompilerParams(dimension_semantics=sem, vmem_limit_bytes=VMEM_LIMIT)


def _rms(x, g):
    return x * lax.rsqrt(jnp.mean(x * x, axis=-1, keepdims=True) + RMS_EPS) * g


def _split3(x):
    hi = x.astype(BF16).astype(F32)
    r = x - hi
    mid = r.astype(BF16).astype(F32)
    lo = (r - mid).astype(BF16).astype(F32)
    return hi, mid, lo


def _col_tile(nc, cap):
    best = LANES
    for k in range(1, nc // LANES + 1):
        if nc % (k * LANES) == 0 and k * LANES <= cap:
            best = k * LANES
    return best


def _norm_proj_kernel(x_ref, g_ref, w_ref, b_ref, *rest, rope, slabs):
    if rope:
        cos_ref, sin_ref, o_ref, h_sc = rest
    else:
        o_ref, h_sc = rest

    @pl.when(pl.program_id(1) == 0)
    def _():
        h_sc[...] = _rms(x_ref[...], g_ref[...]).astype(BF16)

    t = jnp.dot(h_sc[...], w_ref[...], preferred_element_type=F32)
    tn = t.shape[1]
    lane = lax.broadcasted_iota(jnp.int32, t.shape, 1)
    if rope:
        reps = tn // LANES
        cos = jnp.tile(cos_ref[...], (1, reps))
        sin = jnp.tile(sin_ref[...], (1, reps))
        first = (lane & (HEAD_DIM - 1)) < HEAD_DIM // 2
        rot = jnp.where(first, pltpu.roll(t, tn - HEAD_DIM // 2, 1), pltpu.roll(t, HEAD_DIM // 2, 1))
        t = t * cos + rot * sin
    if slabs:
        low = lax.broadcasted_iota(jnp.int32, (t.shape[0], LANES), 1) < HEAD_DIM
        parts = []
        for pair in range(tn // LANES):
            v = t[:, pair * LANES:(pair + 1) * LANES]
            parts.append(jnp.where(low, v, 0.0))
            parts.append(jnp.where(low, pltpu.roll(v, HEAD_DIM, 1), 0.0))
        t = jnp.concatenate(parts, axis=1)
    o_ref[...] = (t + b_ref[...]).astype(o_ref.dtype)


def norm_proj(x, g, w, bias, rope_tabs, seq, out_dtype, slabs=True, tm=512):
    n, d = x.shape
    nc = w.shape[1]
    tn = _col_tile(nc, 1408)
    wide = 2 if slabs else 1
    assert n % tm == 0 and seq % tm == 0 and nc % tn == 0
    rope = rope_tabs is not None
    in_specs = [
        pl.BlockSpec((tm, d), lambda i, j: (i, 0)),
        pl.BlockSpec((1, d), lambda i, j: (0, 0)),
        pl.BlockSpec((d, tn), lambda i, j: (0, j)),
        pl.BlockSpec((1, wide * tn), lambda i, j: (0, j)),
    ]
    args = [x, g.reshape(1, d), w, bias.reshape(1, wide * nc)]
    if rope:
        spt = seq // tm
        in_specs += [pl.BlockSpec((tm, LANES), lambda i, j: (i % spt, 0))] * 2
        args += list(rope_tabs)
    return pl.pallas_call(
        functools.partial(_norm_proj_kernel, rope=rope, slabs=slabs),
        grid=(n // tm, nc // tn),
        in_specs=in_specs,
        out_specs=pl.BlockSpec((tm, wide * tn), lambda i, j: (i, j)),
        out_shape=jax.ShapeDtypeStruct((n, wide * nc), out_dtype),
        scratch_shapes=[pltpu.VMEM((tm, d), BF16)],
        compiler_params=_cparams("parallel", "arbitrary"),
        name="norm_proj_rope" if rope else "norm_proj",
    )(*args)


def _norm_proj_t_kernel(x_ref, g_ref, wt_ref, *rest, rope):
    if rope:
        cos_ref, sin_ref, o_ref, h_sc = rest
    else:
        o_ref, h_sc = rest

    @pl.when(pl.program_id(1) == 0)
    def _():
        h_sc[...] = _rms(x_ref[...], g_ref[...]).astype(BF16)

    t = lax.dot_general(wt_ref[...], h_sc[...], NT_DIMS, preferred_element_type=F32)
    tn, tm = t.shape
    nheads = tn // HEAD_DIM
    if rope:
        cos = jnp.tile(cos_ref[...], (nheads, 1))
        sin = jnp.tile(sin_ref[...], (nheads, 1))
        r = lax.broadcasted_iota(jnp.int32, t.shape, 0)
        first = (r & (HEAD_DIM - 1)) < HEAD_DIM // 2
        rot = jnp.where(first, pltpu.roll(t, tn - HEAD_DIM // 2, 0), pltpu.roll(t, HEAD_DIM // 2, 0))
        t = t * cos + rot * sin
    spare = jnp.zeros((LANES - HEAD_DIM, tm), o_ref.dtype)
    for h in range(nheads):
        o_ref[0, h * LANES:h * LANES + HEAD_DIM, :] = t[h * HEAD_DIM:(h + 1) * HEAD_DIM].astype(o_ref.dtype)
        o_ref[0, h * LANES + HEAD_DIM:(h + 1) * LANES, :] = spare


def norm_proj_t(x, g, wt, rope_tabs_t, b, seq, tm=512):
    n, d = x.shape
    nc = wt.shape[0]
    tn = _col_tile(nc, 512)
    assert n % tm == 0 and seq % tm == 0 and nc % tn == 0
    spt = seq // tm
    rope = rope_tabs_t is not None
    in_specs = [
        pl.BlockSpec((tm, d), lambda i, j: (i, 0)),
        pl.BlockSpec((1, d), lambda i, j: (0, 0)),
        pl.BlockSpec((tn, d), lambda i, j: (j, 0)),
    ]
    args = [x, g.reshape(1, d), wt]
    if rope:
        in_specs += [pl.BlockSpec((HEAD_DIM, tm), lambda i, j: (0, i % spt))] * 2
        args += list(rope_tabs_t)
    return pl.pallas_call(
        functools.partial(_norm_proj_t_kernel, rope=rope),
        grid=(n // tm, nc // tn),
        in_specs=in_specs,
        out_specs=pl.BlockSpec((1, 2 * tn, tm), lambda i, j: (i // spt, j, i % spt)),
        out_shape=jax.ShapeDtypeStruct((b, 2 * nc, seq), BF16),
        scratch_shapes=[pltpu.VMEM((tm, d), BF16)],
        compiler_params=_cparams("parallel", "arbitrary"),
        name="norm_proj_t_rope" if rope else "norm_proj_t",
    )(*args)


def _fox_decay_kernel(fl_ref, b_ref, pq_ref, pk_ref, oq_ref, ok_ref, carry_sc, *, nh):
    @pl.when(pl.program_id(1) == 0)
    def _():
        carry_sc[...] = jnp.zeros(carry_sc.shape, F32)

    c = jax.nn.log_sigmoid(fl_ref[0] + b_ref[...])
    ts = c.shape[0]
    row = lax.broadcasted_iota(jnp.int32, c.shape, 0)
    lane = lax.broadcasted_iota(jnp.int32, c.shape, 1)
    sh = 1
    while sh < ts:
        c = c + jnp.where(row >= sh, pltpu.roll(c, sh, 0), 0.0)
        sh *= 2
    c = c + carry_sc[0:1, :]
    carry_sc[0:1, :] = c[ts - 1:ts, :]
    hi, mid, lo = _split3(c)
    c3 = jnp.where(lane < nh, hi,
                   jnp.where(lane < 2 * nh, pltpu.roll(mid, nh, 1),
                             jnp.where(lane < 3 * nh, pltpu.roll(lo, 2 * nh, 1),
                                       jnp.where(lane == 3 * nh, 1.0, 0.0)))).astype(BF16)
    for h in range(nh):
        oq_ref[0, h, 0] = jnp.dot(c3, pq_ref[h], preferred_element_type=F32).astype(oq_ref.dtype)
        ok_ref[0, h] = lax.dot_general(pk_ref[h], c3, NT_DIMS,
                                       preferred_element_type=F32).astype(ok_ref.dtype)


def _fox_placement(nh):
    pq = np.zeros((nh, LANES, LANES), np.float32)
    pk = np.zeros((nh, LANES, LANES), np.float32)
    one = 3 * nh
    for h in range(nh):
        for t in range(3):
            pq[h, one, HEAD_DIM + t] = 1.0
            pq[h, t * nh + h, HEAD_DIM + 3 + t] = 1.0
            pk[h, HEAD_DIM + t, t * nh + h] = -1.0
            pk[h, HEAD_DIM + 3 + t, one] = 1.0
    return jnp.asarray(pq, BF16), jnp.asarray(pk, BF16)


def fox_decay(small, b_row, nh, ts=2048):
    b, s, _ = small.shape
    ts = min(ts, s)
    pq, pk = _fox_placement(nh)
    return pl.pallas_call(
        functools.partial(_fox_decay_kernel, nh=nh),
        grid=(b, s // ts),
        in_specs=[pl.BlockSpec((1, ts, LANES), lambda i, j: (i, j, 0)),
                  pl.BlockSpec((1, LANES), lambda i, j: (0, 0)),
                  pl.BlockSpec((nh, LANES, LANES), lambda i, j: (0, 0, 0)),
                  pl.BlockSpec((nh, LANES, LANES), lambda i, j: (0, 0, 0))],
        out_specs=[pl.BlockSpec((1, nh, 1, ts, LANES), lambda i, j: (i, 0, 0, j, 0)),
                   pl.BlockSpec((1, nh, LANES, ts), lambda i, j: (i, 0, 0, j))],
        out_shape=[jax.ShapeDtypeStruct((b, nh, 1, s, LANES), BF16),
                   jax.ShapeDtypeStruct((b, nh, LANES, s), BF16)],
        scratch_shapes=[pltpu.VMEM((8, LANES), F32)],
        compiler_params=_cparams("parallel", "arbitrary"),
        name="fox_decay",
    )(small, b_row, pq, pk)


def _nsa_compress_kernel(t_ref, pe_ref, w1_ref, w2_ref, o_ref):
    t = t_ref[0].astype(F32)
    ncp = t.shape[0]
    a = jnp.dot((t + pe_ref[0:1, :]).astype(BF16), w1_ref[0, 0], preferred_element_type=F32)
    bm = jnp.dot((t + pe_ref[1:2, :]).astype(BF16), w1_ref[0, 1], preferred_element_type=F32)
    pre = a + pltpu.roll(bm, ncp - 1, 0)
    hid = jax.nn.gelu(pre)
    o_ref[0, 0] = jnp.dot(hid.astype(BF16), w2_ref[...], preferred_element_type=F32).astype(o_ref.dtype)


def nsa_compress(t, pe, w1, w2):
    b, ncp, cw = t.shape
    g, _, _, hid = w1.shape
    return pl.pallas_call(
        _nsa_compress_kernel,
        grid=(b, g),
        in_specs=[pl.BlockSpec((1, ncp, cw), lambda i, j: (i, 0, 0)),
                  pl.BlockSpec((2, cw), lambda i, j: (0, 0)),
                  pl.BlockSpec((1, 2, cw, hid), lambda i, j: (j, 0, 0, 0)),
                  pl.BlockSpec((hid, LANES), lambda i, j: (0, 0))],
        out_specs=pl.BlockSpec((1, 1, ncp, LANES), lambda i, j: (i, j, 0, 0)),
        out_shape=jax.ShapeDtypeStruct((b, g, ncp, LANES), BF16),
        compiler_params=_cparams("parallel", "parallel"),
        name="nsa_compress",
    )(t, pe, w1, w2)


def _topk_mask(work, col, k):
    sel = jnp.zeros(work.shape, jnp.bool_)
    col = col.astype(F32)
    for _ in range(k):
        mx = jnp.max(work, axis=-1, keepdims=True)
        first = jnp.min(jnp.where(work == mx, col, 1e9), axis=-1, keepdims=True)
        hit = col == first
        sel = jnp.logical_or(sel, hit)
        work = jnp.where(hit, -jnp.inf, work)
    return sel


def _nsa_select_kernel(q_ref, kct_ref, vc_ref, m_ref, oc_ref, sb_ref, *, tq, rep, nsup):
    i = pl.program_id(2)
    ncp = kct_ref.shape[-1]
    ns = m_ref.shape[-1]
    qpos = i * tq + lax.broadcasted_iota(jnp.int32, (tq, 1), 0)
    cend = lax.broadcasted_iota(jnp.int32, (1, ncp), 1) * CMP_STRIDE + (CMP_LEN - 1)
    cmask = cend <= qpos
    has_block = jnp.where(qpos >= CMP_LEN - 1, 1.0, 0.0)
    kct = kct_ref[0, 0]
    vc = vc_ref[0, 0]
    pcs = jnp.zeros((tq, ncp), F32)
    for r in range(rep):
        s = jnp.dot(q_ref[0, :, r * LANES:(r + 1) * LANES], kct, preferred_element_type=F32)
        s = jnp.where(cmask, s, MASKED)
        e = jnp.exp(s - jnp.max(s, axis=-1, keepdims=True))
        p = e * (has_block / jnp.maximum(jnp.sum(e, axis=-1, keepdims=True), 1e-30))
        oc_ref[0, :, r * LANES:(r + 1) * LANES] = jnp.dot(
            p.astype(BF16), vc, preferred_element_type=F32).astype(oc_ref.dtype)
        pcs = pcs + p
    mm = m_ref[...]
    imp = sum(jnp.dot(part.astype(BF16), mm, preferred_element_type=F32) for part in _split3(pcs))
    sblk = lax.broadcasted_iota(jnp.int32, (1, ns), 1)
    qblk = qpos >> 6
    forced = (sblk == 0) | (sblk == qblk) | (sblk == qblk - 1)
    free = jnp.where(forced | (sblk > qblk), -jnp.inf, imp)
    sel = _topk_mask(free, sblk, min(SLC_TOPN, ns) - 3)
    bias = jnp.where((sel | forced) & (sblk <= qblk), 0.0, MASKED).astype(sb_ref.dtype)
    zeros = jnp.zeros((tq, HEAD_DIM), sb_ref.dtype)
    for j in range(nsup):
        sb_ref[0, 0, j] = jnp.concatenate([zeros, bias[:, j * HEAD_DIM:(j + 1) * HEAD_DIM]], axis=-1)


def nsa_select(q_all, kct, vc, m, rep, tq=512):
    b, s, _ = q_all.shape
    g = kct.shape[1]
    ncp = kct.shape[-1]
    ns = m.shape[-1]
    nsup = ns // HEAD_DIM
    assert SLC_BLOCK == 64 and ns % HEAD_DIM == 0
    tq = min(tq, s)
    return pl.pallas_call(
        functools.partial(_nsa_select_kernel, tq=tq, rep=rep, nsup=nsup),
        grid=(b, g, s // tq),
        in_specs=[pl.BlockSpec((1, tq, rep * LANES), lambda bi, gi, i: (bi, i, gi)),
                  pl.BlockSpec((1, 1, LANES, ncp), lambda bi, gi, i: (bi, gi, 0, 0)),
                  pl.BlockSpec((1, 1, ncp, LANES), lambda bi, gi, i: (bi, gi, 0, 0)),
                  pl.BlockSpec((ncp, ns), lambda bi, gi, i: (0, 0))],
        out_specs=[pl.BlockSpec((1, tq, rep * LANES), lambda bi, gi, i: (bi, i, gi)),
                   pl.BlockSpec((1, 1, nsup, tq, LANES), lambda bi, gi, i: (bi, gi, 0, i, 0))],
        out_shape=[jax.ShapeDtypeStruct((b, s, g * rep * LANES), BF16),
                   jax.ShapeDtypeStruct((b, g, nsup, s, LANES), BF16)],
        compiler_params=_cparams("parallel", "parallel", "parallel"),
        name="nsa_select",
    )(q_all, kct, vc, m)


def _moba_select_kernel(q_ref, kt_ref, ind_ref, sb_ref, kbar_sc, *, tq):
    i = pl.program_id(2)

    @pl.when(i == 0)
    def _():
        kbar_sc[...] = jnp.dot(kt_ref[0], ind_ref[...], preferred_element_type=F32) * (1.0 / MOBA_BLOCK)

    q = q_ref[0]
    gate = sum(jnp.dot(q, part.astype(BF16), preferred_element_type=F32)
               for part in _split3(kbar_sc[...]))
    qpos = i * tq + lax.broadcasted_iota(jnp.int32, (tq, 1), 0)
    cur = qpos >> 8
    lane = lax.broadcasted_iota(jnp.int32, (1, LANES), 1)
    blk = lane - HEAD_DIM
    past = (blk >= 0) & (blk < cur)
    sel = _topk_mask(jnp.where(past, gate, -jnp.inf), lane, MOBA_TOPK)
    keep = (blk < 0) | (sel & past) | (blk == cur)
    sb_ref[0, 0, 0] = jnp.where(keep, 0.0, MASKED).astype(sb_ref.dtype)


def moba_select(q_all, kt_all, ind, nh, tq=2048):
    b, s, _ = q_all.shape
    assert MOBA_BLOCK == 256 and s // MOBA_BLOCK <= HEAD_DIM
    tq = min(tq, s)
    return pl.pallas_call(
        functools.partial(_moba_select_kernel, tq=tq),
        grid=(b, nh, s // tq),
        in_specs=[pl.BlockSpec((1, tq, LANES), lambda bi, hi, i: (bi, i, hi)),
                  pl.BlockSpec((1, LANES, s), lambda bi, hi, i: (bi, hi, 0)),
                  pl.BlockSpec((s, LANES), lambda bi, hi, i: (0, 0))],
        out_specs=pl.BlockSpec((1, 1, 1, tq, LANES), lambda bi, hi, i: (bi, hi, 0, i, 0)),
        out_shape=jax.ShapeDtypeStruct((b, nh, 1, s, LANES), BF16),
        scratch_shapes=[pltpu.VMEM((LANES, LANES), F32)],
        compiler_params=_cparams("parallel", "parallel", "arbitrary"),
        name="moba_select",
    )(q_all, kt_all, ind)


def _flash_kernel(*refs, tq, tk, rs, rs_main, tps, band, decay, has_extra, has_kx):
    refs = list(refs)
    qa_ref = refs.pop(0)
    ex_ref = refs.pop(0) if has_extra else None
    kt_ref = refs.pop(0)
    kx_ref = refs.pop(0) if has_kx else None
    kmax_sc = refs.pop() if decay else None
    v_ref, o_ref, m_sc, acc_sc = refs
    i = pl.program_id(2)
    m_sc[...] = jnp.full(m_sc.shape, M_INIT, F32)
    acc_sc[...] = jnp.zeros(acc_sc.shape, F32)
    row = lax.broadcasted_iota(jnp.int32, (rs, tk), 0)
    col = lax.broadcasted_iota(jnp.int32, (rs, tk), 1)
    nst = tq // rs
    kpq = tq // tk

    def run(items, rs=rs):
        tiles = {}

        def operands(j):
            if id(j) not in tiles:
                start = pl.multiple_of(j * tk, tk)
                kt = kt_ref[0, :, pl.ds(start, tk)]
                if kx_ref is not None:
                    kt = kt + kx_ref[0, 0, :, pl.ds(start, tk)]
                tiles[id(j)] = (kt, v_ref[0, pl.ds(start, tk), :])
            return tiles[id(j)]

        def logits(item):
            j, r, _ = item
            rows = pl.ds(r * rs, rs)
            qa = qa_ref[0, rows, :]
            if ex_ref is not None:
                qa = qa + ex_ref[0, 0, j // tps, rows, :]
            return jnp.dot(qa, operands(j)[0], preferred_element_type=F32)

        pending = [logits(it) for it in items[:FLASH_AHEAD]]
        for n, (j, r, mask) in enumerate(items):
            rows = pl.ds(r * rs, rs)
            s = pending.pop(0)
            if n + FLASH_AHEAD < len(items):
                pending.append(logits(items[n + FLASH_AHEAD]))
            if mask is not None:
                s = jnp.where(mask, s, MASKED)
            m_prev = m_sc[rows, :]
            m_new = jnp.maximum(m_prev, jnp.max(s, axis=-1, keepdims=True))
            p = jnp.exp(s - jnp.tile(m_new, (1, tk // LANES)))
            alpha = jnp.exp(m_prev - m_new)
            acc_sc[rows, :] = alpha * acc_sc[rows, :] + jnp.dot(
                p.astype(BF16), operands(j)[1], preferred_element_type=F32)
            m_sc[rows, :] = m_new

    def edge_items(dj_list, j_of):
        items = []
        for dj in dj_list:
            j = j_of(dj)
            for r in range(nst):
                off, ko = r * rs, dj * tk
                lo = off - WINDOW + 1 if band else None
                if ko > off + rs - 1 or (band and ko + tk - 1 < lo):
                    continue
                full = ko + tk - 1 <= off and (not band or ko > off + rs - 1 - WINDOW)
                mask = None
                if not full:
                    mask = col + ko <= row + off
                    if band:
                        mask = mask & (col + ko > row + (off - WINDOW))
                items.append((j, r, mask))
        return items

    if band:
        for dj in range(-(WINDOW // tk), 0):
            @pl.when(i * kpq + dj >= 0)
            def _():
                jj = i * kpq + dj
                run(edge_items([dj], lambda _: jj))
    def body(jjs):
        tiles = [jj * kpq + dj for jj in jjs for dj in range(kpq)]
        run([(j, r, None) for j in tiles for r in range(tq // rs_main)], rs_main)

    def sweep(n, jj_of):
        def pair(t, carry):
            body([jj_of(2 * t), jj_of(2 * t + 1)])
            return carry
        lax.fori_loop(0, n // 2, pair, 0)

        @pl.when(n % 2 == 1)
        def _():
            body([jj_of(n - 1)])

    def run_diag():
        diag = [i * kpq + dj for dj in range(kpq)]
        run(edge_items(list(range(kpq)), lambda dj: diag[dj]))

    if decay:
        s_len = kt_ref.shape[-1]

        @pl.when(i == 0)
        def _():
            def chunk(c, best):
                kk = kt_ref[0, :, pl.ds(pl.multiple_of(c * tq, tq), tq)].astype(F32)
                return jnp.maximum(best, jnp.sum(kk * kk, axis=0, keepdims=True))
            ksq = lax.fori_loop(0, s_len // tq, chunk, jnp.zeros((1, tq), F32))
            kmax_sc[...] = jnp.broadcast_to(jnp.sqrt(jnp.max(ksq, axis=1, keepdims=True)), kmax_sc.shape)

        run_diag()
        q = qa_ref[0].astype(F32)
        qn = jnp.sqrt(jnp.sum(q * q, axis=1, keepdims=True))
        slack = jnp.max(qn * kmax_sc[0:1, 0:1] - m_sc[:, 0:1], axis=0, keepdims=True)
        cvec = -jnp.sum(kx_ref[0, 0, HEAD_DIM:HEAD_DIM + 3, :].astype(F32), axis=0, keepdims=True)
        pos = lax.broadcasted_iota(jnp.int32, (1, s_len), 1)
        q0 = i * tq
        c_q0 = jnp.sum(jnp.where(pos == q0, cvec, 0.0), axis=1, keepdims=True)
        dead = (pos < q0) & (slack + c_q0 - cvec <= -DECAY_CUT)
        n_dead = jnp.sum(jnp.where(dead, 1.0, 0.0)).astype(jnp.int32) // tq
        sweep(i - n_dead, lambda t: i - 1 - t)
    elif band:
        run_diag()
    else:
        sweep(i, lambda t: t)
        run_diag()
    acc = acc_sc[...]
    o_ref[0] = (acc / acc[:, HEAD_DIM:HEAD_DIM + 1]).astype(o_ref.dtype)


def flash(name, q_all, q0, kt_all, k0, v_all, v0, nh, rep, extra=None, kx=None, band=False,
          decay=False):
    assert not decay or (kx is not None and not band)
    b, s, _ = q_all.shape
    tk = min(WINDOW if band else FLASH_TK, s)
    tq = tk if band else min(FLASH_TQ, s)
    rs = min(FLASH_STRIP, tq)
    assert s % tq == 0 and tq % tk == 0 and tq % rs == 0 and (not band or WINDOW % tk == 0)
    in_specs = [pl.BlockSpec((1, tq, LANES), lambda bi, hi, i: (bi, i, q0 + hi))]
    args = [q_all]
    tps = 1
    if extra is not None:
        he, nsup = extra.shape[1], extra.shape[2]
        rep_e = nh // he
        assert (s // nsup) % tk == 0
        tps = (s // nsup) // tk
        in_specs.append(pl.BlockSpec((1, 1, nsup, tq, LANES), lambda bi, hi, i: (bi, hi // rep_e, 0, i, 0)))
        args.append(extra)
    in_specs.append(pl.BlockSpec((1, LANES, s), lambda bi, hi, i: (bi, k0 + hi // rep, 0)))
    args.append(kt_all)
    if kx is not None:
        bx, hx = kx.shape[0], kx.shape[1]
        in_specs.append(pl.BlockSpec(
            (1, 1, LANES, s), lambda bi, hi, i: (bi if bx > 1 else 0, hi if hx > 1 else 0, 0, 0)))
        args.append(kx)
    in_specs.append(pl.BlockSpec((1, s, LANES), lambda bi, hi, i: (bi, 0, v0 + hi // rep)))
    args.append(v_all)
    return pl.pallas_call(
        functools.partial(_flash_kernel, tq=tq, tk=tk, rs=rs, rs_main=min(FLASH_STRIP_MAIN, tq),
                          tps=tps, band=band, decay=decay,
                          has_extra=extra is not None, has_kx=kx is not None),
        grid=(b, nh, s // tq),
        in_specs=in_specs,
        out_specs=pl.BlockSpec((1, tq, LANES), lambda bi, hi, i: (bi, i, hi)),
        out_shape=jax.ShapeDtypeStruct((b, s, nh * LANES), BF16),
        scratch_shapes=[pltpu.VMEM((tq, LANES), F32), pltpu.VMEM((tq, LANES), F32)]
        + ([pltpu.VMEM((8, LANES), F32)] if decay else []),
        compiler_params=_cparams("parallel", "parallel", "arbitrary" if decay else "parallel"),
        name=name,
    )(*args)


def _proj_norm_res_kernel(a_ref, w_ref, g_ref, x_ref, o_ref):
    y = jnp.dot(a_ref[...], w_ref[...], preferred_element_type=F32)
    o_ref[...] = x_ref[...] + _rms(y, g_ref[...])


def proj_norm_res(a, w, g, x, tm=512):
    n, k = a.shape
    d = w.shape[1]
    return pl.pallas_call(
        _proj_norm_res_kernel,
        grid=(n // tm,),
        in_specs=[pl.BlockSpec((tm, k), lambda i: (i, 0)),
                  pl.BlockSpec((k, d), lambda i: (0, 0)),
                  pl.BlockSpec((1, d), lambda i: (0, 0)),
                  pl.BlockSpec((tm, d), lambda i: (i, 0))],
        out_specs=pl.BlockSpec((tm, d), lambda i: (i, 0)),
        out_shape=jax.ShapeDtypeStruct((n, d), F32),
        compiler_params=_cparams("parallel"),
        name="proj_norm_res",
    )(a, w, g.reshape(1, d), x)


def _even_out_kernel(of_ref, oc_ref, os_ref, ow_ref, gl_ref, wf_ref, wn_ref, g_ref, x_ref, o_ref,
                     *, nh, g0):
    gate = jax.nn.sigmoid(gl_ref[...])
    parts = []
    for h in range(nh):
        sl = slice(h * LANES, (h + 1) * LANES)
        c = g0 + 3 * h
        parts.append(gate[:, c:c + 1] * oc_ref[:, sl].astype(F32)
                     + gate[:, c + 1:c + 2] * os_ref[:, sl].astype(F32)
                     + gate[:, c + 2:c + 3] * ow_ref[:, sl].astype(F32))
    a = jnp.concatenate(parts, axis=-1).astype(BF16)
    y = (jnp.dot(of_ref[...], wf_ref[...], preferred_element_type=F32)
         + jnp.dot(a, wn_ref[...], preferred_element_type=F32))
    o_ref[...] = x_ref[...] + _rms(y, g_ref[...])


def even_out(o_fox, o_cmp, o_slc, o_win, small, wf, wn, g, x, nh, g0, tm=512):
    n, k = o_fox.shape
    d = wf.shape[1]
    act = pl.BlockSpec((tm, k), lambda i: (i, 0))
    wspec = pl.BlockSpec((k, d), lambda i: (0, 0))
    return pl.pallas_call(
        functools.partial(_even_out_kernel, nh=nh, g0=g0),
        grid=(n // tm,),
        in_specs=[act, act, act, act,
                  pl.BlockSpec((tm, LANES), lambda i: (i, 0)),
                  wspec, wspec,
                  pl.BlockSpec((1, d), lambda i: (0, 0)),
                  pl.BlockSpec((tm, d), lambda i: (i, 0))],
        out_specs=pl.BlockSpec((tm, d), lambda i: (i, 0)),
        out_shape=jax.ShapeDtypeStruct((n, d), F32),
        compiler_params=_cparams("parallel"),
        name="even_out",
    )(o_fox, o_cmp, o_slc, o_win, small, wf, wn, g.reshape(1, d), x)


def _ffn_kernel(x_ref, gpre_ref, wg_ref, wu_ref, wd_ref, gpost_ref, o_ref, h_sc, acc_sc):
    f = pl.program_id(1)

    @pl.when(f == 0)
    def _():
        h_sc[...] = _rms(x_ref[...], gpre_ref[...]).astype(BF16)
        acc_sc[...] = jnp.zeros(acc_sc.shape, F32)

    h = h_sc[...]
    a = jnp.dot(h, wg_ref[...], preferred_element_type=F32)
    u = jnp.dot(h, wu_ref[...], preferred_element_type=F32)
    act = (jax.nn.silu(a) * u).astype(BF16)
    acc_sc[...] += jnp.dot(act, wd_ref[...], preferred_element_type=F32)

    @pl.when(f == pl.num_programs(1) - 1)
    def _():
        o_ref[...] = x_ref[...] + _rms(acc_sc[...], gpost_ref[...])


def ffn(x, gpre, wg, wu, wd, gpost, tm=512, tf=1408):
    n, d = x.shape
    dff = wg.shape[1]
    assert n % tm == 0 and dff % tf == 0
    return pl.pallas_call(
        _ffn_kernel,
        grid=(n // tm, dff // tf),
        in_specs=[pl.BlockSpec((tm, d), lambda i, f: (i, 0)),
                  pl.BlockSpec((1, d), lambda i, f: (0, 0)),
                  pl.BlockSpec((d, tf), lambda i, f: (0, f)),
                  pl.BlockSpec((d, tf), lambda i, f: (0, f)),
                  pl.BlockSpec((tf, d), lambda i, f: (f, 0)),
                  pl.BlockSpec((1, d), lambda i, f: (0, 0))],
        out_specs=pl.BlockSpec((tm, d), lambda i, f: (i, 0)),
        out_shape=jax.ShapeDtypeStruct((n, d), F32),
        scratch_shapes=[pltpu.VMEM((tm, d), BF16), pltpu.VMEM((tm, d), F32)],
        compiler_params=_cparams("parallel", "arbitrary"),
        name="ffn",
    )(x, gpre.reshape(1, d), wg, wu, wd, gpost.reshape(1, d))


def _rope_tables(s):
    inv = ROPE_THETA ** (-jnp.arange(0, HEAD_DIM, 2, dtype=F32) / HEAD_DIM)
    ang = jnp.arange(s, dtype=F32)[:, None] * inv[None, :]
    cos, sin = jnp.cos(ang), jnp.sin(ang)
    reps = LANES // HEAD_DIM
    cos2 = jnp.tile(jnp.concatenate([cos, cos], -1), (1, reps))
    sin2 = jnp.tile(jnp.concatenate([-sin, sin], -1), (1, reps))
    return (cos2, sin2), (cos2.T[:HEAD_DIM], sin2.T[:HEAD_DIM])


def _slab_cols(w, scale=1.0):
    d, c = w.shape
    w3 = (w * scale).reshape(d, c // HEAD_DIM, HEAD_DIM)
    return jnp.concatenate([w3, jnp.zeros_like(w3)], axis=-1).reshape(d, 2 * c)


def _slab_rows(w):
    return _slab_cols(w.T).T


def _ones_lane(n_slabs):
    one = np.zeros((n_slabs, LANES), np.float32)
    one[:, HEAD_DIM] = 1.0
    return jnp.asarray(one.reshape(-1))


def _block_indicator_rows(s, block):
    blk = (np.arange(s) // block) % HEAD_DIM
    ind = np.zeros((LANES, s), np.float32)
    ind[HEAD_DIM + blk, np.arange(s)] = 1.0
    return jnp.asarray(ind, BF16)[None, None]


def _overlap_matrix(ncp, ns):
    ratio = SLC_BLOCK // CMP_STRIDE
    m = np.arange(ncp)[:, None]
    j = np.arange(ns)[None, :]
    ok = (m >= ratio * j - 1) & (m <= ratio * j + ratio - 1) & (m < ncp - 1)
    return jnp.asarray(ok, BF16)


def _compress_weights(pe, w1, w2, g):
    hid = w1.shape[-1]
    w1r = w1.reshape(2, CMP_STRIDE, HEAD_DIM, hid)
    w1g = jnp.zeros((g, 2, CMP_STRIDE, g, LANES, hid), F32)
    for gi in range(g):
        w1g = w1g.at[gi, :, :, gi, :HEAD_DIM].set(w1r)
    w1g = w1g.reshape(g, 2, CMP_STRIDE * g * LANES, hid).astype(BF16)
    pe2 = jnp.zeros((2, CMP_STRIDE, g, LANES), F32).at[..., :HEAD_DIM].set(
        pe.reshape(2, CMP_STRIDE, 1, HEAD_DIM))
    w2p = jnp.concatenate([w2, jnp.zeros_like(w2)], axis=-1).astype(BF16)
    return pe2.reshape(2, CMP_STRIDE * g * LANES), w1g, w2p


def _out_rows(w):
    return _slab_rows(w).astype(BF16)


def _even_layer(x2, b, s, g_pre, g_post, w_in, b_f, pe_k, w1_k, w2_k, pe_v, w1_v, w2_v, w_out, tabs, tabs_t):
    hf, hn, g = 8, 8, 2
    hd = HEAD_DIM
    rep = hn // g
    (w_fq, w_fk, w_fv, w_fl, w_nq, w_kc, w_vc, w_ks, w_vs, w_kw, w_vw, w_gl) = jnp.split(
        w_in, list(np.cumsum([hf * hd] * 3 + [hf] + [hn * hd] + [g * hd] * 6)), axis=1)
    scale = hd ** -0.5

    w_plain = jnp.concatenate([w_fq * scale, w_fv, w_vs, w_vw, w_vc], axis=1).astype(BF16)
    bias_plain = jnp.concatenate([jnp.zeros((hf * LANES,), F32), _ones_lane(hf + 2 * g),
                                  jnp.zeros((g * LANES,), F32)])
    p_plain = norm_proj(x2, g_pre, w_plain, bias_plain, None, s, BF16).reshape(b, s, -1)
    fq0, fv0, vs0, vw0 = 0, hf, 2 * hf, 2 * hf + g
    vc_col = (2 * hf + 2 * g) * LANES
    w_rope = jnp.concatenate([w_nq * scale, w_kc], axis=1).astype(BF16)
    p_rope = norm_proj(x2, g_pre, w_rope, jnp.zeros((2 * w_rope.shape[1],), F32), tabs, s, BF16
                       ).reshape(b, s, -1)
    n_small = hf + 3 * hn
    w_small = jnp.pad(jnp.concatenate([w_fl, w_gl], axis=1), ((0, 0), (0, LANES - n_small))).astype(BF16)
    p_small = norm_proj(x2, g_pre, w_small, jnp.zeros((LANES,), F32), None, s, F32, slabs=False)
    kt_fox = norm_proj_t(x2, g_pre, w_fk.T.astype(BF16), None, b, s)
    kt_nsa = norm_proj_t(x2, g_pre, jnp.concatenate([w_ks, w_kw], axis=1).T.astype(BF16),
                         tabs_t, b, s)

    qx, kx = fox_decay(p_small.reshape(b, s, LANES), jnp.pad(b_f, (0, LANES - hf)).reshape(1, LANES), hf)
    o_fox = flash("flash_fox", p_plain, fq0, kt_fox, 0, p_plain, fv0, hf, 1, extra=qx, kx=kx, decay=True)

    ncp = s // CMP_STRIDE
    ns = s // SLC_BLOCK
    kc = p_rope[:, :, hn * LANES:].reshape(b, ncp, CMP_STRIDE * g * LANES)
    vc = p_plain[:, :, vc_col:].reshape(b, ncp, CMP_STRIDE * g * LANES)
    kcmp = nsa_compress(kc, *_compress_weights(pe_k, w1_k, w2_k, g))
    vcmp = nsa_compress(vc, *_compress_weights(pe_v, w1_v, w2_v, g))
    o_cmp, selb = nsa_select(p_rope, kcmp.transpose(0, 1, 3, 2), vcmp, _overlap_matrix(ncp, ns), rep)
    o_slc = flash("flash_sel", p_rope, 0, kt_nsa, 0, p_plain, vs0, hn, rep, extra=selb,
                  kx=_block_indicator_rows(s, SLC_BLOCK))
    o_win = flash("flash_band", p_rope, 0, kt_nsa, g, p_plain, vw0, hn, rep, band=True)

    n = b * s
    return even_out(o_fox.reshape(n, -1), o_cmp.reshape(n, -1), o_slc.reshape(n, -1),
                    o_win.reshape(n, -1), p_small, _out_rows(w_out[:hf * hd]), _out_rows(w_out[hf * hd:]),
                    g_post, x2, hn, hf)


def _odd_layer(x2, b, s, g_pre, g_post, w_in, w_out, tabs, tabs_t):
    h = 16
    hd = HEAD_DIM
    d = h * hd
    scale = hd ** -0.5
    w_q, w_k, w_v = w_in[:, :d], w_in[:, d:2 * d], w_in[:, 2 * d:]
    q_all = norm_proj(x2, g_pre, (w_q * scale).astype(BF16), jnp.zeros((h * LANES,), F32),
                      tabs, s, BF16).reshape(b, s, -1)
    v_all = norm_proj(x2, g_pre, w_v.astype(BF16), _ones_lane(h), None, s, BF16).reshape(b, s, -1)
    kt_all = norm_proj_t(x2, g_pre, w_k.T.astype(BF16), tabs_t, b, s)
    ind = _block_indicator_rows(s, MOBA_BLOCK)
    selb = moba_select(q_all, kt_all, ind[0, 0].T, h)
    o = flash("flash_moba", q_all, 0, kt_all, 0, v_all, 0, h, 1, extra=selb, kx=ind)
    return proj_norm_res(o.reshape(b * s, -1), _out_rows(w_out), g_post, x2)


def kernel(x, ev_w_in, ev_b_f, ev_cmp_pe_k, ev_cmp_w1_k, ev_cmp_w2_k, ev_cmp_pe_v, ev_cmp_w1_v,
           ev_cmp_w2_v, ev_w_out, od_w_in, od_w_out, g_mix_pre, g_mix_post, g_ffn_pre, g_ffn_post,
           ffn_w_gate, ffn_w_up, ffn_w_down):
    b, s, d = x.shape
    depth = g_mix_pre.shape[0]
    tabs, tabs_t = _rope_tables(s)
    x2 = x.reshape(b * s, d)
    for layer in range(depth):
        if layer % 2 == 0:
            e = layer // 2
            x2 = _even_layer(x2, b, s, g_mix_pre[layer], g_mix_post[layer], ev_w_in[e], ev_b_f[e],
                             ev_cmp_pe_k[e], ev_cmp_w1_k[e], ev_cmp_w2_k[e], ev_cmp_pe_v[e],
                             ev_cmp_w1_v[e], ev_cmp_w2_v[e], ev_w_out[e], tabs, tabs_t)
        else:
            o = layer // 2
            x2 = _odd_layer(x2, b, s, g_mix_pre[layer], g_mix_post[layer], od_w_in[o], od_w_out[o],
                            tabs, tabs_t)
        x2 = ffn(x2, g_ffn_pre[layer], ffn_w_gate[layer].astype(BF16), ffn_w_up[layer].astype(BF16),
                 ffn_w_down[layer].astype(BF16), g_ffn_post[layer])
    return x2.reshape(b, s, d)
```

```python
import functools

import jax
import jax.numpy as jnp
import numpy as np
from jax import lax
from jax.experimental import pallas as pl
from jax.experimental.pallas import tpu as pltpu

F32 = jnp.float32
BF16 = jnp.bfloat16

HEAD_DIM = 64
LANES = 128
ROPE_THETA = 10000.0
RMS_EPS = 1e-6
CMP_STRIDE = 16
CMP_LEN = 32
SLC_BLOCK = 64
SLC_TOPN = 16
WINDOW = 512
MOBA_BLOCK = 256
MOBA_TOPK = 3
H_FOX, H_NSA, NSA_GROUPS, H_MOBA = 8, 8, 2, 16
SLC_SHIFT = SLC_BLOCK.bit_length() - 1
MOBA_SHIFT = MOBA_BLOCK.bit_length() - 1
PROJ_TILE_COLS = 1408
PROJ_T_TILE_ROWS = 512
MASKED = -1e30
M_INIT = -1e29
DECAY_CUT = 100.0
FLASH_TQ = 1024
FLASH_TK = 512
FLASH_STRIP = 512
FLASH_STRIP_MAIN = 1024
FLASH_AHEAD = 2
VMEM_LIMIT = 48 * 1024 * 1024
NT_DIMS = (((1,), (1,)), ((), ()))


def _cparams(*sem):
    return pltpu.CompilerParams(dimension_semantics=sem, vmem_limit_bytes=VMEM_LIMIT)


def _rms(x, g):
    return x * lax.rsqrt(jnp.mean(x * x, axis=-1, keepdims=True) + RMS_EPS) * g


def _split3(x):
    hi = x.astype(BF16).astype(F32)
    r = x - hi
    mid = r.astype(BF16).astype(F32)
    lo = (r - mid).astype(BF16).astype(F32)
    return hi, mid, lo


def _col_tile(nc, cap):
    best = LANES
    for k in range(1, nc // LANES + 1):
        if nc % (k * LANES) == 0 and k * LANES <= cap:
            best = k * LANES
    return best


def _norm_proj_kernel(x_ref, g_ref, w_ref, b_ref, *rest, rope, slabs):
    if rope:
        cos_ref, sin_ref, o_ref, h_sc = rest
    else:
        o_ref, h_sc = rest

    @pl.when(pl.program_id(1) == 0)
    def _():
        h_sc[...] = _rms(x_ref[...], g_ref[...]).astype(BF16)

    t = jnp.dot(h_sc[...], w_ref[...], preferred_element_type=F32)
    tn = t.shape[1]
    lane = lax.broadcasted_iota(jnp.int32, t.shape, 1)
    if rope:
        reps = tn // LANES
        cos = jnp.tile(cos_ref[...], (1, reps))
        sin = jnp.tile(sin_ref[...], (1, reps))
        first = (lane & (HEAD_DIM - 1)) < HEAD_DIM // 2
        rot = jnp.where(first, pltpu.roll(t, tn - HEAD_DIM // 2, 1), pltpu.roll(t, HEAD_DIM // 2, 1))
        t = t * cos + rot * sin
    if slabs:
        low = lax.broadcasted_iota(jnp.int32, (t.shape[0], LANES), 1) < HEAD_DIM
        parts = []
        for pair in range(tn // LANES):
            v = t[:, pair * LANES:(pair + 1) * LANES]
            parts.append(jnp.where(low, v, 0.0))
            parts.append(jnp.where(low, pltpu.roll(v, HEAD_DIM, 1), 0.0))
        t = jnp.concatenate(parts, axis=1)
    o_ref[...] = (t + b_ref[...]).astype(o_ref.dtype)


def norm_proj(x, g, w, bias, rope_tabs, seq, out_dtype, slabs=True, tm=512):
    n, d = x.shape
    nc = w.shape[1]
    tn = _col_tile(nc, PROJ_TILE_COLS)
    wide = 2 if slabs else 1
    assert n % tm == 0 and seq % tm == 0 and nc % tn == 0
    rope = rope_tabs is not None
    in_specs = [
        pl.BlockSpec((tm, d), lambda i, j: (i, 0)),
        pl.BlockSpec((1, d), lambda i, j: (0, 0)),
        pl.BlockSpec((d, tn), lambda i, j: (0, j)),
        pl.BlockSpec((1, wide * tn), lambda i, j: (0, j)),
    ]
    args = [x, g.reshape(1, d), w, bias.reshape(1, wide * nc)]
    if rope:
        spt = seq // tm
        in_specs += [pl.BlockSpec((tm, LANES), lambda i, j: (i % spt, 0))] * 2
        args += list(rope_tabs)
    return pl.pallas_call(
        functools.partial(_norm_proj_kernel, rope=rope, slabs=slabs),
        grid=(n // tm, nc // tn),
        in_specs=in_specs,
        out_specs=pl.BlockSpec((tm, wide * tn), lambda i, j: (i, j)),
        out_shape=jax.ShapeDtypeStruct((n, wide * nc), out_dtype),
        scratch_shapes=[pltpu.VMEM((tm, d), BF16)],
        compiler_params=_cparams("parallel", "arbitrary"),
        name="norm_proj_rope" if rope else "norm_proj",
    )(*args)


def _norm_proj_t_kernel(x_ref, g_ref, wt_ref, *rest, rope):
    if rope:
        cos_ref, sin_ref, o_ref, h_sc = rest
    else:
        o_ref, h_sc = rest

    @pl.when(pl.program_id(1) == 0)
    def _():
        h_sc[...] = _rms(x_ref[...], g_ref[...]).astype(BF16)

    t = lax.dot_general(wt_ref[...], h_sc[...], NT_DIMS, preferred_element_type=F32)
    tn, tm = t.shape
    nheads = tn // HEAD_DIM
    if rope:
        cos = jnp.tile(cos_ref[...], (nheads, 1))
        sin = jnp.tile(sin_ref[...], (nheads, 1))
        r = lax.broadcasted_iota(jnp.int32, t.shape, 0)
        first = (r & (HEAD_DIM - 1)) < HEAD_DIM // 2
        rot = jnp.where(first, pltpu.roll(t, tn - HEAD_DIM // 2, 0), pltpu.roll(t, HEAD_DIM // 2, 0))
        t = t * cos + rot * sin
    spare = jnp.zeros((LANES - HEAD_DIM, tm), o_ref.dtype)
    for h in range(nheads):
        o_ref[0, h * LANES:h * LANES + HEAD_DIM, :] = t[h * HEAD_DIM:(h + 1) * HEAD_DIM].astype(o_ref.dtype)
        o_ref[0, h * LANES + HEAD_DIM:(h + 1) * LANES, :] = spare


def norm_proj_t(x, g, wt, rope_tabs_t, b, seq, tm=512):
    n, d = x.shape
    nc = wt.shape[0]
    tn = _col_tile(nc, PROJ_T_TILE_ROWS)
    assert n % tm == 0 and seq % tm == 0 and nc % tn == 0
    spt = seq // tm
    rope = rope_tabs_t is not None
    in_specs = [
        pl.BlockSpec((tm, d), lambda i, j: (i, 0)),
        pl.BlockSpec((1, d), lambda i, j: (0, 0)),
        pl.BlockSpec((tn, d), lambda i, j: (j, 0)),
    ]
    args = [x, g.reshape(1, d), wt]
    if rope:
        in_specs += [pl.BlockSpec((HEAD_DIM, tm), lambda i, j: (0, i % spt))] * 2
        args += list(rope_tabs_t)
    return pl.pallas_call(
        functools.partial(_norm_proj_t_kernel, rope=rope),
        grid=(n // tm, nc // tn),
        in_specs=in_specs,
        out_specs=pl.BlockSpec((1, 2 * tn, tm), lambda i, j: (i // spt, j, i % spt)),
        out_shape=jax.ShapeDtypeStruct((b, 2 * nc, seq), BF16),
        scratch_shapes=[pltpu.VMEM((tm, d), BF16)],
        compiler_params=_cparams("parallel", "arbitrary"),
        name="norm_proj_t_rope" if rope else "norm_proj_t",
    )(*args)


def _fox_decay_kernel(fl_ref, b_ref, pq_ref, pk_ref, oq_ref, ok_ref, carry_sc, *, nh):
    @pl.when(pl.program_id(1) == 0)
    def _():
        carry_sc[...] = jnp.zeros(carry_sc.shape, F32)

    c = jax.nn.log_sigmoid(fl_ref[0] + b_ref[...])
    ts = c.shape[0]
    row = lax.broadcasted_iota(jnp.int32, c.shape, 0)
    lane = lax.broadcasted_iota(jnp.int32, c.shape, 1)
    sh = 1
    while sh < ts:
        c = c + jnp.where(row >= sh, pltpu.roll(c, sh, 0), 0.0)
        sh *= 2
    c = c + carry_sc[0:1, :]
    carry_sc[0:1, :] = c[ts - 1:ts, :]
    hi, mid, lo = _split3(c)
    c3 = jnp.where(lane < nh, hi,
                   jnp.where(lane < 2 * nh, pltpu.roll(mid, nh, 1),
                             jnp.where(lane < 3 * nh, pltpu.roll(lo, 2 * nh, 1),
                                       jnp.where(lane == 3 * nh, 1.0, 0.0)))).astype(BF16)
    for h in range(nh):
        oq_ref[0, h, 0] = jnp.dot(c3, pq_ref[h], preferred_element_type=F32).astype(oq_ref.dtype)
        ok_ref[0, h] = lax.dot_general(pk_ref[h], c3, NT_DIMS,
                                       preferred_element_type=F32).astype(ok_ref.dtype)


def _fox_placement(nh):
    pq = np.zeros((nh, LANES, LANES), np.float32)
    pk = np.zeros((nh, LANES, LANES), np.float32)
    one = 3 * nh
    for h in range(nh):
        for t in range(3):
            pq[h, one, HEAD_DIM + t] = 1.0
            pq[h, t * nh + h, HEAD_DIM + 3 + t] = 1.0
            pk[h, HEAD_DIM + t, t * nh + h] = -1.0
            pk[h, HEAD_DIM + 3 + t, one] = 1.0
    return jnp.asarray(pq, BF16), jnp.asarray(pk, BF16)


def fox_decay(small, b_row, nh, ts=2048):
    b, s, _ = small.shape
    ts = min(ts, s)
    pq, pk = _fox_placement(nh)
    return pl.pallas_call(
        functools.partial(_fox_decay_kernel, nh=nh),
        grid=(b, s // ts),
        in_specs=[pl.BlockSpec((1, ts, LANES), lambda i, j: (i, j, 0)),
                  pl.BlockSpec((1, LANES), lambda i, j: (0, 0)),
                  pl.BlockSpec((nh, LANES, LANES), lambda i, j: (0, 0, 0)),
                  pl.BlockSpec((nh, LANES, LANES), lambda i, j: (0, 0, 0))],
        out_specs=[pl.BlockSpec((1, nh, 1, ts, LANES), lambda i, j: (i, 0, 0, j, 0)),
                   pl.BlockSpec((1, nh, LANES, ts), lambda i, j: (i, 0, 0, j))],
        out_shape=[jax.ShapeDtypeStruct((b, nh, 1, s, LANES), BF16),
                   jax.ShapeDtypeStruct((b, nh, LANES, s), BF16)],
        scratch_shapes=[pltpu.VMEM((8, LANES), F32)],
        compiler_params=_cparams("parallel", "arbitrary"),
        name="fox_decay",
    )(small, b_row, pq, pk)


def _nsa_compress_kernel(t_ref, pe_ref, w1_ref, w2_ref, o_ref):
    t = t_ref[0].astype(F32)
    ncp = t.shape[0]
    a = jnp.dot((t + pe_ref[0:1, :]).astype(BF16), w1_ref[0, 0], preferred_element_type=F32)
    bm = jnp.dot((t + pe_ref[1:2, :]).astype(BF16), w1_ref[0, 1], preferred_element_type=F32)
    pre = a + pltpu.roll(bm, ncp - 1, 0)
    hid = jax.nn.gelu(pre)
    o_ref[0, 0] = jnp.dot(hid.astype(BF16), w2_ref[...], preferred_element_type=F32).astype(o_ref.dtype)


def nsa_compress(t, pe, w1, w2):
    b, ncp, cw = t.shape
    g, _, _, hid = w1.shape
    return pl.pallas_call(
        _nsa_compress_kernel,
        grid=(b, g),
        in_specs=[pl.BlockSpec((1, ncp, cw), lambda i, j: (i, 0, 0)),
                  pl.BlockSpec((2, cw), lambda i, j: (0, 0)),
                  pl.BlockSpec((1, 2, cw, hid), lambda i, j: (j, 0, 0, 0)),
                  pl.BlockSpec((hid, LANES), lambda i, j: (0, 0))],
        out_specs=pl.BlockSpec((1, 1, ncp, LANES), lambda i, j: (i, j, 0, 0)),
        out_shape=jax.ShapeDtypeStruct((b, g, ncp, LANES), BF16),
        compiler_params=_cparams("parallel", "parallel"),
        name="nsa_compress",
    )(t, pe, w1, w2)


def _topk_mask(work, col, k):
    sel = jnp.zeros(work.shape, jnp.bool_)
    col = col.astype(F32)
    for _ in range(k):
        mx = jnp.max(work, axis=-1, keepdims=True)
        first = jnp.min(jnp.where(work == mx, col, jnp.inf), axis=-1, keepdims=True)
        hit = col == first
        sel = jnp.logical_or(sel, hit)
        work = jnp.where(hit, -jnp.inf, work)
    return sel


def _nsa_select_kernel(q_ref, kct_ref, vc_ref, m_ref, oc_ref, sb_ref, *, tq, rep, nsup, sps):
    i = pl.program_id(2)
    ncp = kct_ref.shape[-1]
    ns = m_ref.shape[-1]
    qpos = i * tq + lax.broadcasted_iota(jnp.int32, (tq, 1), 0)
    has_block = jnp.where(qpos >= CMP_LEN - 1, 1.0, 0.0)
    qblk = qpos >> SLC_SHIFT
    zeros = jnp.zeros((tq, HEAD_DIM), sb_ref.dtype)
    hidden = jnp.concatenate([zeros, jnp.full((tq, HEAD_DIM), MASKED, sb_ref.dtype)], axis=-1)

    def prefix(nv):
        wc, ws = ncp * nv // nsup, ns * nv // nsup
        cend = lax.broadcasted_iota(jnp.int32, (1, wc), 1) * CMP_STRIDE + (CMP_LEN - 1)
        cmask = cend <= qpos
        kct = kct_ref[0, 0, :, :wc]
        vc = vc_ref[0, 0, :wc, :]
        pcs = jnp.zeros((tq, wc), F32)
        for r in range(rep):
            s = jnp.dot(q_ref[0, :, r * LANES:(r + 1) * LANES], kct, preferred_element_type=F32)
            s = jnp.where(cmask, s, MASKED)
            e = jnp.exp(s - jnp.max(s, axis=-1, keepdims=True))
            p = e * (has_block / jnp.maximum(jnp.sum(e, axis=-1, keepdims=True), 1e-30))
            oc_ref[0, :, r * LANES:(r + 1) * LANES] = jnp.dot(
                p.astype(BF16), vc, preferred_element_type=F32).astype(oc_ref.dtype)
            pcs = pcs + p
        mm = m_ref[:wc, :ws]
        imp = sum(jnp.dot(part.astype(BF16), mm, preferred_element_type=F32) for part in _split3(pcs))
        sblk = lax.broadcasted_iota(jnp.int32, (1, ws), 1)
        forced = (sblk == 0) | (sblk == qblk) | (sblk == qblk - 1)
        free = jnp.where(forced | (sblk > qblk), -jnp.inf, imp)
        sel = _topk_mask(free, sblk, min(SLC_TOPN, ns) - 3)
        bias = jnp.where((sel | forced) & (sblk <= qblk), 0.0, MASKED).astype(sb_ref.dtype)
        for j in range(nsup):
            sb_ref[0, 0, j] = hidden if j >= nv else jnp.concatenate(
                [zeros, bias[:, j * HEAD_DIM:(j + 1) * HEAD_DIM]], axis=-1)

    for nv in range(1, nsup + 1):
        pl.when(i // sps == nv - 1)(functools.partial(prefix, nv))


def nsa_select(q_all, kct, vc, m, rep, tq=512):
    b, s, _ = q_all.shape
    g = kct.shape[1]
    ncp = kct.shape[-1]
    ns = m.shape[-1]
    nsup = ns // HEAD_DIM
    tq = min(tq, s)
    assert SLC_BLOCK == 64 and ns % HEAD_DIM == 0 and (s // tq) % nsup == 0
    return pl.pallas_call(
        functools.partial(_nsa_select_kernel, tq=tq, rep=rep, nsup=nsup, sps=(s // tq) // nsup),
        grid=(b, g, s // tq),
        in_specs=[pl.BlockSpec((1, tq, rep * LANES), lambda bi, gi, i: (bi, i, gi)),
                  pl.BlockSpec((1, 1, LANES, ncp), lambda bi, gi, i: (bi, gi, 0, 0)),
                  pl.BlockSpec((1, 1, ncp, LANES), lambda bi, gi, i: (bi, gi, 0, 0)),
                  pl.BlockSpec((ncp, ns), lambda bi, gi, i: (0, 0))],
        out_specs=[pl.BlockSpec((1, tq, rep * LANES), lambda bi, gi, i: (bi, i, gi)),
                   pl.BlockSpec((1, 1, nsup, tq, LANES), lambda bi, gi, i: (bi, gi, 0, i, 0))],
        out_shape=[jax.ShapeDtypeStruct((b, s, g * rep * LANES), BF16),
                   jax.ShapeDtypeStruct((b, g, nsup, s, LANES), BF16)],
        compiler_params=_cparams("parallel", "parallel", "parallel"),
        name="nsa_select",
    )(q_all, kct, vc, m)


def _moba_select_kernel(q_ref, kt_ref, ind_ref, sb_ref, kbar_sc, *, tq):
    i = pl.program_id(2)

    @pl.when(i == 0)
    def _():
        kbar_sc[...] = jnp.dot(kt_ref[0], ind_ref[...], preferred_element_type=F32) * (1.0 / MOBA_BLOCK)

    q = q_ref[0]
    gate = sum(jnp.dot(q, part.astype(BF16), preferred_element_type=F32)
               for part in _split3(kbar_sc[...]))
    qpos = i * tq + lax.broadcasted_iota(jnp.int32, (tq, 1), 0)
    cur = qpos >> MOBA_SHIFT
    lane = lax.broadcasted_iota(jnp.int32, (1, LANES), 1)
    blk = lane - HEAD_DIM
    past = (blk >= 0) & (blk < cur)
    sel = _topk_mask(jnp.where(past, gate, -jnp.inf), lane, MOBA_TOPK)
    keep = (blk < 0) | (sel & past) | (blk == cur)
    sb_ref[0, 0, 0] = jnp.where(keep, 0.0, MASKED).astype(sb_ref.dtype)


def moba_select(q_all, kt_all, ind, nh, tq=2048):
    b, s, _ = q_all.shape
    assert MOBA_BLOCK == 256 and s // MOBA_BLOCK <= HEAD_DIM
    tq = min(tq, s)
    return pl.pallas_call(
        functools.partial(_moba_select_kernel, tq=tq),
        grid=(b, nh, s // tq),
        in_specs=[pl.BlockSpec((1, tq, LANES), lambda bi, hi, i: (bi, i, hi)),
                  pl.BlockSpec((1, LANES, s), lambda bi, hi, i: (bi, hi, 0)),
                  pl.BlockSpec((s, LANES), lambda bi, hi, i: (0, 0))],
        out_specs=pl.BlockSpec((1, 1, 1, tq, LANES), lambda bi, hi, i: (bi, hi, 0, i, 0)),
        out_shape=jax.ShapeDtypeStruct((b, nh, 1, s, LANES), BF16),
        scratch_shapes=[pltpu.VMEM((LANES, LANES), F32)],
        compiler_params=_cparams("parallel", "parallel", "arbitrary"),
        name="moba_select",
    )(q_all, kt_all, ind)


def _flash_kernel(*refs, tq, tk, rs, rs_main, tps, band, decay, has_extra, has_kx):
    refs = list(refs)
    qa_ref = refs.pop(0)
    ex_ref = refs.pop(0) if has_extra else None
    kt_ref = refs.pop(0)
    kx_ref = refs.pop(0) if has_kx else None
    kmax_sc = refs.pop() if decay else None
    v_ref, o_ref, m_sc, acc_sc = refs
    i = pl.program_id(2)
    m_sc[...] = jnp.full(m_sc.shape, M_INIT, F32)
    acc_sc[...] = jnp.zeros(acc_sc.shape, F32)
    row = lax.broadcasted_iota(jnp.int32, (rs, tk), 0)
    col = lax.broadcasted_iota(jnp.int32, (rs, tk), 1)
    nst = tq // rs
    kpq = tq // tk

    def run(items, rs=rs):
        tiles = {}

        def operands(j):
            if id(j) not in tiles:
                start = pl.multiple_of(j * tk, tk)
                kt = kt_ref[0, :, pl.ds(start, tk)]
                if kx_ref is not None:
                    kt = kt + kx_ref[0, 0, :, pl.ds(start, tk)]
                tiles[id(j)] = (kt, v_ref[0, pl.ds(start, tk), :])
            return tiles[id(j)]

        def logits(item):
            j, r, _ = item
            rows = pl.ds(r * rs, rs)
            qa = qa_ref[0, rows, :]
            if ex_ref is not None:
                qa = qa + ex_ref[0, 0, j // tps, rows, :]
            return jnp.dot(qa, operands(j)[0], preferred_element_type=F32)

        pending = [logits(it) for it in items[:FLASH_AHEAD]]
        for n, (j, r, mask) in enumerate(items):
            rows = pl.ds(r * rs, rs)
            s = pending.pop(0)
            if n + FLASH_AHEAD < len(items):
                pending.append(logits(items[n + FLASH_AHEAD]))
            if mask is not None:
                s = jnp.where(mask, s, MASKED)
            m_prev = m_sc[rows, :]
            m_new = jnp.maximum(m_prev, jnp.max(s, axis=-1, keepdims=True))
            p = jnp.exp(s - jnp.tile(m_new, (1, tk // LANES)))
            alpha = jnp.exp(m_prev - m_new)
            acc_sc[rows, :] = alpha * acc_sc[rows, :] + jnp.dot(
                p.astype(BF16), operands(j)[1], preferred_element_type=F32)
            m_sc[rows, :] = m_new

    def edge_items(dj_list, j_of):
        items = []
        for dj in dj_list:
            j = j_of(dj)
            for r in range(nst):
                off, ko = r * rs, dj * tk
                lo = off - WINDOW + 1 if band else None
                if ko > off + rs - 1 or (band and ko + tk - 1 < lo):
                    continue
                full = ko + tk - 1 <= off and (not band or ko > off + rs - 1 - WINDOW)
                mask = None
                if not full:
                    mask = col + ko <= row + off
                    if band:
                        mask = mask & (col + ko > row + (off - WINDOW))
                items.append((j, r, mask))
        return items

    if band:
        for dj in range(-(WINDOW // tk), 0):
            @pl.when(i * kpq + dj >= 0)
            def _():
                jj = i * kpq + dj
                run(edge_items([dj], lambda _: jj))
    def body(jjs):
        tiles = [jj * kpq + dj for jj in jjs for dj in range(kpq)]
        run([(j, r, None) for j in tiles for r in range(tq // rs_main)], rs_main)

    def sweep(n, jj_of):
        def pair(t, carry):
            body([jj_of(2 * t), jj_of(2 * t + 1)])
            return carry
        lax.fori_loop(0, n // 2, pair, 0)

        @pl.when(n % 2 == 1)
        def _():
            body([jj_of(n - 1)])

    def run_diag():
        diag = [i * kpq + dj for dj in range(kpq)]
        run(edge_items(list(range(kpq)), lambda dj: diag[dj]))

    if decay:
        s_len = kt_ref.shape[-1]

        @pl.when(i == 0)
        def _():
            def chunk(c, best):
                kk = kt_ref[0, :, pl.ds(pl.multiple_of(c * tq, tq), tq)].astype(F32)
                return jnp.maximum(best, jnp.sum(kk * kk, axis=0, keepdims=True))
            ksq = lax.fori_loop(0, s_len // tq, chunk, jnp.zeros((1, tq), F32))
            kmax_sc[...] = jnp.broadcast_to(jnp.sqrt(jnp.max(ksq, axis=1, keepdims=True)), kmax_sc.shape)

        run_diag()
        q = qa_ref[0].astype(F32)
        qn = jnp.sqrt(jnp.sum(q * q, axis=1, keepdims=True))
        slack = jnp.max(qn * kmax_sc[0:1, 0:1] - m_sc[:, 0:1], axis=0, keepdims=True)
        cvec = -jnp.sum(kx_ref[0, 0, HEAD_DIM:HEAD_DIM + 3, :].astype(F32), axis=0, keepdims=True)
        pos = lax.broadcasted_iota(jnp.int32, (1, s_len), 1)
        q0 = i * tq
        c_q0 = jnp.sum(jnp.where(pos == q0, cvec, 0.0), axis=1, keepdims=True)
        dead = (pos < q0) & (slack + c_q0 - cvec <= -DECAY_CUT)
        n_dead = jnp.sum(jnp.where(dead, 1.0, 0.0)).astype(jnp.int32) // tq
        sweep(i - n_dead, lambda t: i - 1 - t)
    elif band:
        run_diag()
    else:
        sweep(i, lambda t: t)
        run_diag()
    acc = acc_sc[...]
    o_ref[0] = (acc / acc[:, HEAD_DIM:HEAD_DIM + 1]).astype(o_ref.dtype)


def flash(name, q_all, q0, kt_all, k0, v_all, v0, nh, rep, extra=None, kx=None, band=False,
          decay=False):
    assert not decay or (kx is not None and not band)
    b, s, _ = q_all.shape
    tk = min(WINDOW if band else FLASH_TK, s)
    tq = tk if band else min(FLASH_TQ, s)
    rs = min(FLASH_STRIP, tq)
    assert s % tq == 0 and tq % tk == 0 and tq % rs == 0 and (not band or WINDOW % tk == 0)
    in_specs = [pl.BlockSpec((1, tq, LANES), lambda bi, hi, i: (bi, i, q0 + hi))]
    args = [q_all]
    tps = 1
    if extra is not None:
        he, nsup = extra.shape[1], extra.shape[2]
        rep_e = nh // he
        assert (s // nsup) % tk == 0
        tps = (s // nsup) // tk
        in_specs.append(pl.BlockSpec((1, 1, nsup, tq, LANES), lambda bi, hi, i: (bi, hi // rep_e, 0, i, 0)))
        args.append(extra)
    in_specs.append(pl.BlockSpec((1, LANES, s), lambda bi, hi, i: (bi, k0 + hi // rep, 0)))
    args.append(kt_all)
    if kx is not None:
        bx, hx = kx.shape[0], kx.shape[1]
        in_specs.append(pl.BlockSpec(
            (1, 1, LANES, s), lambda bi, hi, i: (bi if bx > 1 else 0, hi if hx > 1 else 0, 0, 0)))
        args.append(kx)
    in_specs.append(pl.BlockSpec((1, s, LANES), lambda bi, hi, i: (bi, 0, v0 + hi // rep)))
    args.append(v_all)
    return pl.pallas_call(
        functools.partial(_flash_kernel, tq=tq, tk=tk, rs=rs, rs_main=min(FLASH_STRIP_MAIN, tq),
                          tps=tps, band=band, decay=decay,
                          has_extra=extra is not None, has_kx=kx is not None),
        grid=(b, nh, s // tq),
        in_specs=in_specs,
        out_specs=pl.BlockSpec((1, tq, LANES), lambda bi, hi, i: (bi, i, hi)),
        out_shape=jax.ShapeDtypeStruct((b, s, nh * LANES), BF16),
        scratch_shapes=[pltpu.VMEM((tq, LANES), F32), pltpu.VMEM((tq, LANES), F32)]
        + ([pltpu.VMEM((8, LANES), F32)] if decay else []),
        compiler_params=_cparams("parallel", "parallel", "arbitrary" if decay else "parallel"),
        name=name,
    )(*args)


def _proj_norm_res_kernel(a_ref, w_ref, g_ref, x_ref, o_ref):
    y = jnp.dot(a_ref[...], w_ref[...], preferred_element_type=F32)
    o_ref[...] = x_ref[...] + _rms(y, g_ref[...])


def proj_norm_res(a, w, g, x, tm=512):
    n, k = a.shape
    d = w.shape[1]
    return pl.pallas_call(
        _proj_norm_res_kernel,
        grid=(n // tm,),
        in_specs=[pl.BlockSpec((tm, k), lambda i: (i, 0)),
                  pl.BlockSpec((k, d), lambda i: (0, 0)),
                  pl.BlockSpec((1, d), lambda i: (0, 0)),
                  pl.BlockSpec((tm, d), lambda i: (i, 0))],
        out_specs=pl.BlockSpec((tm, d), lambda i: (i, 0)),
        out_shape=jax.ShapeDtypeStruct((n, d), F32),
        compiler_params=_cparams("parallel"),
        name="proj_norm_res",
    )(a, w, g.reshape(1, d), x)


def _even_out_kernel(of_ref, oc_ref, os_ref, ow_ref, gl_ref, wf_ref, wn_ref, g_ref, x_ref, o_ref,
                     *, nh, g0):
    gate = jax.nn.sigmoid(gl_ref[...])
    parts = []
    for h in range(nh):
        sl = slice(h * LANES, (h + 1) * LANES)
        c = g0 + 3 * h
        parts.append(gate[:, c:c + 1] * oc_ref[:, sl].astype(F32)
                     + gate[:, c + 1:c + 2] * os_ref[:, sl].astype(F32)
                     + gate[:, c + 2:c + 3] * ow_ref[:, sl].astype(F32))
    a = jnp.concatenate(parts, axis=-1).astype(BF16)
    y = (jnp.dot(of_ref[...], wf_ref[...], preferred_element_type=F32)
         + jnp.dot(a, wn_ref[...], preferred_element_type=F32))
    o_ref[...] = x_ref[...] + _rms(y, g_ref[...])


def even_out(o_fox, o_cmp, o_slc, o_win, small, wf, wn, g, x, nh, g0, tm=512):
    n, k = o_fox.shape
    d = wf.shape[1]
    act = pl.BlockSpec((tm, k), lambda i: (i, 0))
    wspec = pl.BlockSpec((k, d), lambda i: (0, 0))
    return pl.pallas_call(
        functools.partial(_even_out_kernel, nh=nh, g0=g0),
        grid=(n // tm,),
        in_specs=[act, act, act, act,
                  pl.BlockSpec((tm, LANES), lambda i: (i, 0)),
                  wspec, wspec,
                  pl.BlockSpec((1, d), lambda i: (0, 0)),
                  pl.BlockSpec((tm, d), lambda i: (i, 0))],
        out_specs=pl.BlockSpec((tm, d), lambda i: (i, 0)),
        out_shape=jax.ShapeDtypeStruct((n, d), F32),
        compiler_params=_cparams("parallel"),
        name="even_out",
    )(o_fox, o_cmp, o_slc, o_win, small, wf, wn, g.reshape(1, d), x)


def _ffn_kernel(x_ref, gpre_ref, wg_ref, wu_ref, wd_ref, gpost_ref, o_ref, h_sc, acc_sc):
    f = pl.program_id(1)

    @pl.when(f == 0)
    def _():
        h_sc[...] = _rms(x_ref[...], gpre_ref[...]).astype(BF16)
        acc_sc[...] = jnp.zeros(acc_sc.shape, F32)

    h = h_sc[...]
    a = jnp.dot(h, wg_ref[...], preferred_element_type=F32)
    u = jnp.dot(h, wu_ref[...], preferred_element_type=F32)
    act = (jax.nn.silu(a) * u).astype(BF16)
    acc_sc[...] += jnp.dot(act, wd_ref[...], preferred_element_type=F32)

    @pl.when(f == pl.num_programs(1) - 1)
    def _():
        o_ref[...] = x_ref[...] + _rms(acc_sc[...], gpost_ref[...])


def ffn(x, gpre, wg, wu, wd, gpost, tm=512, tf=1408):
    n, d = x.shape
    dff = wg.shape[1]
    assert n % tm == 0 and dff % tf == 0
    return pl.pallas_call(
        _ffn_kernel,
        grid=(n // tm, dff // tf),
        in_specs=[pl.BlockSpec((tm, d), lambda i, f: (i, 0)),
                  pl.BlockSpec((1, d), lambda i, f: (0, 0)),
                  pl.BlockSpec((d, tf), lambda i, f: (0, f)),
                  pl.BlockSpec((d, tf), lambda i, f: (0, f)),
                  pl.BlockSpec((tf, d), lambda i, f: (f, 0)),
                  pl.BlockSpec((1, d), lambda i, f: (0, 0))],
        out_specs=pl.BlockSpec((tm, d), lambda i, f: (i, 0)),
        out_shape=jax.ShapeDtypeStruct((n, d), F32),
        scratch_shapes=[pltpu.VMEM((tm, d), BF16), pltpu.VMEM((tm, d), F32)],
        compiler_params=_cparams("parallel", "arbitrary"),
        name="ffn",
    )(x, gpre.reshape(1, d), wg, wu, wd, gpost.reshape(1, d))


def _rope_tables(s):
    inv = ROPE_THETA ** (-jnp.arange(0, HEAD_DIM, 2, dtype=F32) / HEAD_DIM)
    ang = jnp.arange(s, dtype=F32)[:, None] * inv[None, :]
    cos, sin = jnp.cos(ang), jnp.sin(ang)
    reps = LANES // HEAD_DIM
    cos2 = jnp.tile(jnp.concatenate([cos, cos], -1), (1, reps))
    sin2 = jnp.tile(jnp.concatenate([-sin, sin], -1), (1, reps))
    return (cos2, sin2), (cos2.T[:HEAD_DIM], sin2.T[:HEAD_DIM])


def _slab_cols(w, scale=1.0):
    d, c = w.shape
    w3 = (w * scale).reshape(d, c // HEAD_DIM, HEAD_DIM)
    return jnp.concatenate([w3, jnp.zeros_like(w3)], axis=-1).reshape(d, 2 * c)


def _slab_rows(w):
    return _slab_cols(w.T).T


def _ones_lane(n_slabs):
    one = np.zeros((n_slabs, LANES), np.float32)
    one[:, HEAD_DIM] = 1.0
    return jnp.asarray(one.reshape(-1))


def _block_indicator_rows(s, block):
    blk = (np.arange(s) // block) % HEAD_DIM
    ind = np.zeros((LANES, s), np.float32)
    ind[HEAD_DIM + blk, np.arange(s)] = 1.0
    return jnp.asarray(ind, BF16)[None, None]


def _overlap_matrix(ncp, ns):
    ratio = SLC_BLOCK // CMP_STRIDE
    m = np.arange(ncp)[:, None]
    j = np.arange(ns)[None, :]
    ok = (m >= ratio * j - 1) & (m <= ratio * j + ratio - 1) & (m < ncp - 1)
    return jnp.asarray(ok, BF16)


def _compress_weights(pe, w1, w2, g):
    hid = w1.shape[-1]
    w1r = w1.reshape(2, CMP_STRIDE, HEAD_DIM, hid)
    w1g = jnp.zeros((g, 2, CMP_STRIDE, g, LANES, hid), F32)
    for gi in range(g):
        w1g = w1g.at[gi, :, :, gi, :HEAD_DIM].set(w1r)
    w1g = w1g.reshape(g, 2, CMP_STRIDE * g * LANES, hid).astype(BF16)
    pe2 = jnp.zeros((2, CMP_STRIDE, g, LANES), F32).at[..., :HEAD_DIM].set(
        pe.reshape(2, CMP_STRIDE, 1, HEAD_DIM))
    w2p = jnp.concatenate([w2, jnp.zeros_like(w2)], axis=-1).astype(BF16)
    return pe2.reshape(2, CMP_STRIDE * g * LANES), w1g, w2p


def _out_rows(w):
    return _slab_rows(w).astype(BF16)


def _even_layer(x2, b, s, g_pre, g_post, w_in, b_f, pe_k, w1_k, w2_k, pe_v, w1_v, w2_v, w_out, tabs, tabs_t):
    hf, hn, g = H_FOX, H_NSA, NSA_GROUPS
    hd = HEAD_DIM
    rep = hn // g
    (w_fq, w_fk, w_fv, w_fl, w_nq, w_kc, w_vc, w_ks, w_vs, w_kw, w_vw, w_gl) = jnp.split(
        w_in, list(np.cumsum([hf * hd] * 3 + [hf] + [hn * hd] + [g * hd] * 6)), axis=1)
    scale = hd ** -0.5

    w_plain = jnp.concatenate([w_fq * scale, w_fv, w_vs, w_vw, w_vc], axis=1).astype(BF16)
    bias_plain = jnp.concatenate([jnp.zeros((hf * LANES,), F32), _ones_lane(hf + 2 * g),
                                  jnp.zeros((g * LANES,), F32)])
    p_plain = norm_proj(x2, g_pre, w_plain, bias_plain, None, s, BF16).reshape(b, s, -1)
    fq0, fv0, vs0, vw0 = 0, hf, 2 * hf, 2 * hf + g
    vc_col = (2 * hf + 2 * g) * LANES
    w_rope = jnp.concatenate([w_nq * scale, w_kc], axis=1).astype(BF16)
    p_rope = norm_proj(x2, g_pre, w_rope, jnp.zeros((2 * w_rope.shape[1],), F32), tabs, s, BF16
                       ).reshape(b, s, -1)
    n_small = hf + 3 * hn
    w_small = jnp.pad(jnp.concatenate([w_fl, w_gl], axis=1), ((0, 0), (0, LANES - n_small))).astype(BF16)
    p_small = norm_proj(x2, g_pre, w_small, jnp.zeros((LANES,), F32), None, s, F32, slabs=False)
    kt_fox = norm_proj_t(x2, g_pre, w_fk.T.astype(BF16), None, b, s)
    kt_nsa = norm_proj_t(x2, g_pre, jnp.concatenate([w_ks, w_kw], axis=1).T.astype(BF16),
                         tabs_t, b, s)

    qx, kx = fox_decay(p_small.reshape(b, s, LANES), jnp.pad(b_f, (0, LANES - hf)).reshape(1, LANES), hf)
    o_fox = flash("flash_fox", p_plain, fq0, kt_fox, 0, p_plain, fv0, hf, 1, extra=qx, kx=kx, decay=True)

    ncp = s // CMP_STRIDE
    ns = s // SLC_BLOCK
    kc = p_rope[:, :, hn * LANES:].reshape(b, ncp, CMP_STRIDE * g * LANES)
    vc = p_plain[:, :, vc_col:].reshape(b, ncp, CMP_STRIDE * g * LANES)
    kcmp = nsa_compress(kc, *_compress_weights(pe_k, w1_k, w2_k, g))
    vcmp = nsa_compress(vc, *_compress_weights(pe_v, w1_v, w2_v, g))
    o_cmp, selb = nsa_select(p_rope, kcmp.transpose(0, 1, 3, 2), vcmp, _overlap_matrix(ncp, ns), rep)
    o_slc = flash("flash_sel", p_rope, 0, kt_nsa, 0, p_plain, vs0, hn, rep, extra=selb,
                  kx=_block_indicator_rows(s, SLC_BLOCK))
    o_win = flash("flash_band", p_rope, 0, kt_nsa, g, p_plain, vw0, hn, rep, band=True)

    n = b * s
    return even_out(o_fox.reshape(n, -1), o_cmp.reshape(n, -1), o_slc.reshape(n, -1),
                    o_win.reshape(n, -1), p_small, _out_rows(w_out[:hf * hd]), _out_rows(w_out[hf * hd:]),
                    g_post, x2, hn, hf)


def _odd_layer(x2, b, s, g_pre, g_post, w_in, w_out, tabs, tabs_t):
    h = H_MOBA
    hd = HEAD_DIM
    d = h * hd
    scale = hd ** -0.5
    w_q, w_k, w_v = w_in[:, :d], w_in[:, d:2 * d], w_in[:, 2 * d:]
    q_all = norm_proj(x2, g_pre, (w_q * scale).astype(BF16), jnp.zeros((h * LANES,), F32),
                      tabs, s, BF16).reshape(b, s, -1)
    v_all = norm_proj(x2, g_pre, w_v.astype(BF16), _ones_lane(h), None, s, BF16).reshape(b, s, -1)
    kt_all = norm_proj_t(x2, g_pre, w_k.T.astype(BF16), tabs_t, b, s)
    ind = _block_indicator_rows(s, MOBA_BLOCK)
    selb = moba_select(q_all, kt_all, ind[0, 0].T, h)
    o = flash("flash_moba", q_all, 0, kt_all, 0, v_all, 0, h, 1, extra=selb, kx=ind)
    return proj_norm_res(o.reshape(b * s, -1), _out_rows(w_out), g_post, x2)


def kernel(x, ev_w_in, ev_b_f, ev_cmp_pe_k, ev_cmp_w1_k, ev_cmp_w2_k, ev_cmp_pe_v, ev_cmp_w1_v,
           ev_cmp_w2_v, ev_w_out, od_w_in, od_w_out, g_mix_pre, g_mix_post, g_ffn_pre, g_ffn_post,
           ffn_w_gate, ffn_w_up, ffn_w_down):
    b, s, d = x.shape
    depth = g_mix_pre.shape[0]
    tabs, tabs_t = _rope_tables(s)
    x2 = x.reshape(b * s, d)
    for layer in range(depth):
        if layer % 2 == 0:
            e = layer // 2
            x2 = _even_layer(x2, b, s, g_mix_pre[layer], g_mix_post[layer], ev_w_in[e], ev_b_f[e],
                             ev_cmp_pe_k[e], ev_cmp_w1_k[e], ev_cmp_w2_k[e], ev_cmp_pe_v[e],
                             ev_cmp_w1_v[e], ev_cmp_w2_v[e], ev_w_out[e], tabs, tabs_t)
        else:
            o = layer // 2
            x2 = _odd_layer(x2, b, s, g_mix_pre[layer], g_mix_post[layer], od_w_in[o], od_w_out[o],
                            tabs, tabs_t)
        x2 = ffn(x2, g_ffn_pre[layer], ffn_w_gate[layer].astype(BF16), ffn_w_up[layer].astype(BF16),
                 ffn_w_down[layer].astype(BF16), g_ffn_post[layer])
    return x2.reshape(b, s, d)
```

```python
import functools

import jax
import jax.numpy as jnp
import numpy as np
from jax import lax
from jax.experimental import pallas as pl
from jax.experimental.pallas import tpu as pltpu

F32 = jnp.float32
BF16 = jnp.bfloat16

HEAD_DIM = 64
LANES = 128
ROPE_THETA = 10000.0
RMS_EPS = 1e-6
CMP_STRIDE = 16
CMP_LEN = 32
SLC_BLOCK = 64
SLC_TOPN = 16
WINDOW = 512
MOBA_BLOCK = 256
MOBA_TOPK = 3
H_FOX, H_NSA, NSA_GROUPS, H_MOBA = 8, 8, 2, 16
SLC_SHIFT = SLC_BLOCK.bit_length() - 1
MOBA_SHIFT = MOBA_BLOCK.bit_length() - 1
PROJ_TILE_COLS = 1408
PROJ_T_TILE_ROWS = 512
MASKED = -1e30
M_INIT = -1e29
DECAY_CUT = 100.0
FLASH_TQ = 1024
FLASH_TK = 512
FLASH_STRIP = 512
FLASH_STRIP_MAIN = 1024
FLASH_GROUP = 2
FLASH_AHEAD = 2
VMEM_LIMIT = 48 * 1024 * 1024
NT_DIMS = (((1,), (1,)), ((), ()))


def _cparams(*sem):
    return pltpu.CompilerParams(dimension_semantics=sem, vmem_limit_bytes=VMEM_LIMIT)


def _rms(x, g):
    return x * lax.rsqrt(jnp.mean(x * x, axis=-1, keepdims=True) + RMS_EPS) * g


def _split3(x):
    hi = x.astype(BF16).astype(F32)
    r = x - hi
    mid = r.astype(BF16).astype(F32)
    lo = (r - mid).astype(BF16).astype(F32)
    return hi, mid, lo


def _col_tile(nc, cap):
    best = LANES
    for k in range(1, nc // LANES + 1):
        if nc % (k * LANES) == 0 and k * LANES <= cap:
            best = k * LANES
    return best


def _norm_proj_kernel(x_ref, g_ref, w_ref, b_ref, *rest, rope, slabs):
    if rope:
        cos_ref, sin_ref, o_ref, h_sc = rest
    else:
        o_ref, h_sc = rest

    @pl.when(pl.program_id(1) == 0)
    def _():
        h_sc[...] = _rms(x_ref[...], g_ref[...]).astype(BF16)

    t = jnp.dot(h_sc[...], w_ref[...], preferred_element_type=F32)
    tn = t.shape[1]
    lane = lax.broadcasted_iota(jnp.int32, t.shape, 1)
    if rope:
        reps = tn // LANES
        cos = jnp.tile(cos_ref[...], (1, reps))
        sin = jnp.tile(sin_ref[...], (1, reps))
        first = (lane & (HEAD_DIM - 1)) < HEAD_DIM // 2
        rot = jnp.where(first, pltpu.roll(t, tn - HEAD_DIM // 2, 1), pltpu.roll(t, HEAD_DIM // 2, 1))
        t = t * cos + rot * sin
    if slabs:
        low = lax.broadcasted_iota(jnp.int32, (t.shape[0], LANES), 1) < HEAD_DIM
        parts = []
        for pair in range(tn // LANES):
            v = t[:, pair * LANES:(pair + 1) * LANES]
            parts.append(jnp.where(low, v, 0.0))
            parts.append(jnp.where(low, pltpu.roll(v, HEAD_DIM, 1), 0.0))
        t = jnp.concatenate(parts, axis=1)
    o_ref[...] = (t + b_ref[...]).astype(o_ref.dtype)


def norm_proj(x, g, w, bias, rope_tabs, seq, out_dtype, slabs=True, tm=512):
    n, d = x.shape
    nc = w.shape[1]
    tn = _col_tile(nc, PROJ_TILE_COLS)
    wide = 2 if slabs else 1
    assert n % tm == 0 and seq % tm == 0 and nc % tn == 0
    rope = rope_tabs is not None
    in_specs = [
        pl.BlockSpec((tm, d), lambda i, j: (i, 0)),
        pl.BlockSpec((1, d), lambda i, j: (0, 0)),
        pl.BlockSpec((d, tn), lambda i, j: (0, j)),
        pl.BlockSpec((1, wide * tn), lambda i, j: (0, j)),
    ]
    args = [x, g.reshape(1, d), w, bias.reshape(1, wide * nc)]
    if rope:
        spt = seq // tm
        in_specs += [pl.BlockSpec((tm, LANES), lambda i, j: (i % spt, 0))] * 2
        args += list(rope_tabs)
    return pl.pallas_call(
        functools.partial(_norm_proj_kernel, rope=rope, slabs=slabs),
        grid=(n // tm, nc // tn),
        in_specs=in_specs,
        out_specs=pl.BlockSpec((tm, wide * tn), lambda i, j: (i, j)),
        out_shape=jax.ShapeDtypeStruct((n, wide * nc), out_dtype),
        scratch_shapes=[pltpu.VMEM((tm, d), BF16)],
        compiler_params=_cparams("parallel", "arbitrary"),
        name="norm_proj_rope" if rope else "norm_proj",
    )(*args)


def _norm_proj_t_kernel(x_ref, g_ref, wt_ref, *rest, rope):
    if rope:
        cos_ref, sin_ref, o_ref, h_sc = rest
    else:
        o_ref, h_sc = rest

    @pl.when(pl.program_id(1) == 0)
    def _():
        h_sc[...] = _rms(x_ref[...], g_ref[...]).astype(BF16)

    t = lax.dot_general(wt_ref[...], h_sc[...], NT_DIMS, preferred_element_type=F32)
    tn, tm = t.shape
    nheads = tn // HEAD_DIM
    if rope:
        cos = jnp.tile(cos_ref[...], (nheads, 1))
        sin = jnp.tile(sin_ref[...], (nheads, 1))
        r = lax.broadcasted_iota(jnp.int32, t.shape, 0)
        first = (r & (HEAD_DIM - 1)) < HEAD_DIM // 2
        rot = jnp.where(first, pltpu.roll(t, tn - HEAD_DIM // 2, 0), pltpu.roll(t, HEAD_DIM // 2, 0))
        t = t * cos + rot * sin
    spare = jnp.zeros((LANES - HEAD_DIM, tm), o_ref.dtype)
    for h in range(nheads):
        o_ref[0, h * LANES:h * LANES + HEAD_DIM, :] = t[h * HEAD_DIM:(h + 1) * HEAD_DIM].astype(o_ref.dtype)
        o_ref[0, h * LANES + HEAD_DIM:(h + 1) * LANES, :] = spare


def norm_proj_t(x, g, wt, rope_tabs_t, b, seq, tm=512):
    n, d = x.shape
    nc = wt.shape[0]
    tn = _col_tile(nc, PROJ_T_TILE_ROWS)
    assert n % tm == 0 and seq % tm == 0 and nc % tn == 0
    spt = seq // tm
    rope = rope_tabs_t is not None
    in_specs = [
        pl.BlockSpec((tm, d), lambda i, j: (i, 0)),
        pl.BlockSpec((1, d), lambda i, j: (0, 0)),
        pl.BlockSpec((tn, d), lambda i, j: (j, 0)),
    ]
    args = [x, g.reshape(1, d), wt]
    if rope:
        in_specs += [pl.BlockSpec((HEAD_DIM, tm), lambda i, j: (0, i % spt))] * 2
        args += list(rope_tabs_t)
    return pl.pallas_call(
        functools.partial(_norm_proj_t_kernel, rope=rope),
        grid=(n // tm, nc // tn),
        in_specs=in_specs,
        out_specs=pl.BlockSpec((1, 2 * tn, tm), lambda i, j: (i // spt, j, i % spt)),
        out_shape=jax.ShapeDtypeStruct((b, 2 * nc, seq), BF16),
        scratch_shapes=[pltpu.VMEM((tm, d), BF16)],
        compiler_params=_cparams("parallel", "arbitrary"),
        name="norm_proj_t_rope" if rope else "norm_proj_t",
    )(*args)


def _fox_decay_kernel(fl_ref, b_ref, pq_ref, pk_ref, oq_ref, ok_ref, carry_sc, *, nh):
    @pl.when(pl.program_id(1) == 0)
    def _():
        carry_sc[...] = jnp.zeros(carry_sc.shape, F32)

    c = jax.nn.log_sigmoid(fl_ref[0] + b_ref[...])
    ts = c.shape[0]
    row = lax.broadcasted_iota(jnp.int32, c.shape, 0)
    lane = lax.broadcasted_iota(jnp.int32, c.shape, 1)
    sh = 1
    while sh < ts:
        c = c + jnp.where(row >= sh, pltpu.roll(c, sh, 0), 0.0)
        sh *= 2
    c = c + carry_sc[0:1, :]
    carry_sc[0:1, :] = c[ts - 1:ts, :]
    hi, mid, lo = _split3(c)
    c3 = jnp.where(lane < nh, hi,
                   jnp.where(lane < 2 * nh, pltpu.roll(mid, nh, 1),
                             jnp.where(lane < 3 * nh, pltpu.roll(lo, 2 * nh, 1),
                                       jnp.where(lane == 3 * nh, 1.0, 0.0)))).astype(BF16)
    for h in range(nh):
        oq_ref[0, h, 0] = jnp.dot(c3, pq_ref[h], preferred_element_type=F32).astype(oq_ref.dtype)
        ok_ref[0, h] = lax.dot_general(pk_ref[h], c3, NT_DIMS,
                                       preferred_element_type=F32).astype(ok_ref.dtype)


def _fox_placement(nh):
    pq = np.zeros((nh, LANES, LANES), np.float32)
    pk = np.zeros((nh, LANES, LANES), np.float32)
    one = 3 * nh
    for h in range(nh):
        for t in range(3):
            pq[h, one, HEAD_DIM + t] = 1.0
            pq[h, t * nh + h, HEAD_DIM + 3 + t] = 1.0
            pk[h, HEAD_DIM + t, t * nh + h] = -1.0
            pk[h, HEAD_DIM + 3 + t, one] = 1.0
    return jnp.asarray(pq, BF16), jnp.asarray(pk, BF16)


def fox_decay(small, b_row, nh, ts=2048):
    b, s, _ = small.shape
    ts = min(ts, s)
    pq, pk = _fox_placement(nh)
    return pl.pallas_call(
        functools.partial(_fox_decay_kernel, nh=nh),
        grid=(b, s // ts),
        in_specs=[pl.BlockSpec((1, ts, LANES), lambda i, j: (i, j, 0)),
                  pl.BlockSpec((1, LANES), lambda i, j: (0, 0)),
                  pl.BlockSpec((nh, LANES, LANES), lambda i, j: (0, 0, 0)),
                  pl.BlockSpec((nh, LANES, LANES), lambda i, j: (0, 0, 0))],
        out_specs=[pl.BlockSpec((1, nh, 1, ts, LANES), lambda i, j: (i, 0, 0, j, 0)),
                   pl.BlockSpec((1, nh, LANES, ts), lambda i, j: (i, 0, 0, j))],
        out_shape=[jax.ShapeDtypeStruct((b, nh, 1, s, LANES), BF16),
                   jax.ShapeDtypeStruct((b, nh, LANES, s), BF16)],
        scratch_shapes=[pltpu.VMEM((8, LANES), F32)],
        compiler_params=_cparams("parallel", "arbitrary"),
        name="fox_decay",
    )(small, b_row, pq, pk)


def _nsa_compress_kernel(t_ref, pe_ref, w1_ref, w2_ref, o_ref):
    t = t_ref[0].astype(F32)
    ncp = t.shape[0]
    a = jnp.dot((t + pe_ref[0:1, :]).astype(BF16), w1_ref[0, 0], preferred_element_type=F32)
    bm = jnp.dot((t + pe_ref[1:2, :]).astype(BF16), w1_ref[0, 1], preferred_element_type=F32)
    pre = a + pltpu.roll(bm, ncp - 1, 0)
    hid = jax.nn.gelu(pre)
    o_ref[0, 0] = jnp.dot(hid.astype(BF16), w2_ref[...], preferred_element_type=F32).astype(o_ref.dtype)


def nsa_compress(t, pe, w1, w2):
    b, ncp, cw = t.shape
    g, _, _, hid = w1.shape
    return pl.pallas_call(
        _nsa_compress_kernel,
        grid=(b, g),
        in_specs=[pl.BlockSpec((1, ncp, cw), lambda i, j: (i, 0, 0)),
                  pl.BlockSpec((2, cw), lambda i, j: (0, 0)),
                  pl.BlockSpec((1, 2, cw, hid), lambda i, j: (j, 0, 0, 0)),
                  pl.BlockSpec((hid, LANES), lambda i, j: (0, 0))],
        out_specs=pl.BlockSpec((1, 1, ncp, LANES), lambda i, j: (i, j, 0, 0)),
        out_shape=jax.ShapeDtypeStruct((b, g, ncp, LANES), BF16),
        compiler_params=_cparams("parallel", "parallel"),
        name="nsa_compress",
    )(t, pe, w1, w2)


def _topk_mask(work, col, k):
    sel = jnp.zeros(work.shape, jnp.bool_)
    col = col.astype(F32)
    for _ in range(k):
        mx = jnp.max(work, axis=-1, keepdims=True)
        first = jnp.min(jnp.where(work == mx, col, jnp.inf), axis=-1, keepdims=True)
        hit = col == first
        sel = jnp.logical_or(sel, hit)
        work = jnp.where(hit, -jnp.inf, work)
    return sel


def _nsa_select_kernel(q_ref, kct_ref, vc_ref, m_ref, oc_ref, sb_ref, *, tq, rep, nsup, sps):
    i = pl.program_id(2)
    ncp = kct_ref.shape[-1]
    ns = m_ref.shape[-1]
    qpos = i * tq + lax.broadcasted_iota(jnp.int32, (tq, 1), 0)
    has_block = jnp.where(qpos >= CMP_LEN - 1, 1.0, 0.0)
    qblk = qpos >> SLC_SHIFT
    zeros = jnp.zeros((tq, HEAD_DIM), sb_ref.dtype)
    hidden = jnp.concatenate([zeros, jnp.full((tq, HEAD_DIM), MASKED, sb_ref.dtype)], axis=-1)

    def prefix(nv):
        wc, ws = ncp * nv // nsup, ns * nv // nsup
        cend = lax.broadcasted_iota(jnp.int32, (1, wc), 1) * CMP_STRIDE + (CMP_LEN - 1)
        cmask = cend <= qpos
        kct = kct_ref[0, 0, :, :wc]
        vc = vc_ref[0, 0, :wc, :]
        pcs = jnp.zeros((tq, wc), F32)
        for r in range(rep):
            s = jnp.dot(q_ref[0, :, r * LANES:(r + 1) * LANES], kct, preferred_element_type=F32)
            s = jnp.where(cmask, s, MASKED)
            e = jnp.exp(s - jnp.max(s, axis=-1, keepdims=True))
            p = e * (has_block / jnp.maximum(jnp.sum(e, axis=-1, keepdims=True), 1e-30))
            oc_ref[0, :, r * LANES:(r + 1) * LANES] = jnp.dot(
                p.astype(BF16), vc, preferred_element_type=F32).astype(oc_ref.dtype)
            pcs = pcs + p
        mm = m_ref[:wc, :ws]
        imp = sum(jnp.dot(part.astype(BF16), mm, preferred_element_type=F32) for part in _split3(pcs))
        sblk = lax.broadcasted_iota(jnp.int32, (1, ws), 1)
        forced = (sblk == 0) | (sblk == qblk) | (sblk == qblk - 1)
        free = jnp.where(forced | (sblk > qblk), -jnp.inf, imp)
        sel = _topk_mask(free, sblk, min(SLC_TOPN, ns) - 3)
        bias = jnp.where((sel | forced) & (sblk <= qblk), 0.0, MASKED).astype(sb_ref.dtype)
        for j in range(nsup):
            sb_ref[0, 0, j] = hidden if j >= nv else jnp.concatenate(
                [zeros, bias[:, j * HEAD_DIM:(j + 1) * HEAD_DIM]], axis=-1)

    for nv in range(1, nsup + 1):
        pl.when(i // sps == nv - 1)(functools.partial(prefix, nv))


def nsa_select(q_all, kct, vc, m, rep, tq=512):
    b, s, _ = q_all.shape
    g = kct.shape[1]
    ncp = kct.shape[-1]
    ns = m.shape[-1]
    nsup = ns // HEAD_DIM
    tq = min(tq, s)
    assert SLC_BLOCK == 64 and ns % HEAD_DIM == 0 and (s // tq) % nsup == 0
    return pl.pallas_call(
        functools.partial(_nsa_select_kernel, tq=tq, rep=rep, nsup=nsup, sps=(s // tq) // nsup),
        grid=(b, g, s // tq),
        in_specs=[pl.BlockSpec((1, tq, rep * LANES), lambda bi, gi, i: (bi, i, gi)),
                  pl.BlockSpec((1, 1, LANES, ncp), lambda bi, gi, i: (bi, gi, 0, 0)),
                  pl.BlockSpec((1, 1, ncp, LANES), lambda bi, gi, i: (bi, gi, 0, 0)),
                  pl.BlockSpec((ncp, ns), lambda bi, gi, i: (0, 0))],
        out_specs=[pl.BlockSpec((1, tq, rep * LANES), lambda bi, gi, i: (bi, i, gi)),
                   pl.BlockSpec((1, 1, nsup, tq, LANES), lambda bi, gi, i: (bi, gi, 0, i, 0))],
        out_shape=[jax.ShapeDtypeStruct((b, s, g * rep * LANES), BF16),
                   jax.ShapeDtypeStruct((b, g, nsup, s, LANES), BF16)],
        compiler_params=_cparams("parallel", "parallel", "parallel"),
        name="nsa_select",
    )(q_all, kct, vc, m)


def _moba_select_kernel(q_ref, kt_ref, ind_ref, sb_ref, kbar_sc, *, tq):
    i = pl.program_id(2)

    @pl.when(i == 0)
    def _():
        kbar_sc[...] = jnp.dot(kt_ref[0], ind_ref[...], preferred_element_type=F32) * (1.0 / MOBA_BLOCK)

    q = q_ref[0]
    gate = sum(jnp.dot(q, part.astype(BF16), preferred_element_type=F32)
               for part in _split3(kbar_sc[...]))
    qpos = i * tq + lax.broadcasted_iota(jnp.int32, (tq, 1), 0)
    cur = qpos >> MOBA_SHIFT
    lane = lax.broadcasted_iota(jnp.int32, (1, LANES), 1)
    blk = lane - HEAD_DIM
    past = (blk >= 0) & (blk < cur)
    sel = _topk_mask(jnp.where(past, gate, -jnp.inf), lane, MOBA_TOPK)
    keep = (blk < 0) | (sel & past) | (blk == cur)
    sb_ref[0, 0, 0] = jnp.where(keep, 0.0, MASKED).astype(sb_ref.dtype)


def moba_select(q_all, kt_all, ind, nh, tq=2048):
    b, s, _ = q_all.shape
    assert MOBA_BLOCK == 256 and s // MOBA_BLOCK <= HEAD_DIM
    tq = min(tq, s)
    return pl.pallas_call(
        functools.partial(_moba_select_kernel, tq=tq),
        grid=(b, nh, s // tq),
        in_specs=[pl.BlockSpec((1, tq, LANES), lambda bi, hi, i: (bi, i, hi)),
                  pl.BlockSpec((1, LANES, s), lambda bi, hi, i: (bi, hi, 0)),
                  pl.BlockSpec((s, LANES), lambda bi, hi, i: (0, 0))],
        out_specs=pl.BlockSpec((1, 1, 1, tq, LANES), lambda bi, hi, i: (bi, hi, 0, i, 0)),
        out_shape=jax.ShapeDtypeStruct((b, nh, 1, s, LANES), BF16),
        scratch_shapes=[pltpu.VMEM((LANES, LANES), F32)],
        compiler_params=_cparams("parallel", "parallel", "arbitrary"),
        name="moba_select",
    )(q_all, kt_all, ind)


def _flash_kernel(*refs, tq, tk, rs, rs_main, tps, band, decay, has_extra, has_kx):
    refs = list(refs)
    qa_ref = refs.pop(0)
    ex_ref = refs.pop(0) if has_extra else None
    kt_ref = refs.pop(0)
    kx_ref = refs.pop(0) if has_kx else None
    kmax_sc = refs.pop() if decay else None
    v_ref, o_ref, m_sc, acc_sc = refs
    i = pl.program_id(2)
    m_sc[...] = jnp.full(m_sc.shape, M_INIT, F32)
    acc_sc[...] = jnp.zeros(acc_sc.shape, F32)
    row = lax.broadcasted_iota(jnp.int32, (rs, tk), 0)
    col = lax.broadcasted_iota(jnp.int32, (rs, tk), 1)
    nst = tq // rs
    kpq = tq // tk

    def run(items, rs=rs):
        tiles = {}

        def operands(j):
            if id(j) not in tiles:
                start = pl.multiple_of(j * tk, tk)
                kt = kt_ref[0, :, pl.ds(start, tk)]
                if kx_ref is not None:
                    kt = kt + kx_ref[0, 0, :, pl.ds(start, tk)]
                tiles[id(j)] = (kt, v_ref[0, pl.ds(start, tk), :])
            return tiles[id(j)]

        def logits(item):
            j, r, _ = item
            rows = pl.ds(r * rs, rs)
            qa = qa_ref[0, rows, :]
            if ex_ref is not None:
                qa = qa + ex_ref[0, 0, j // tps, rows, :]
            return jnp.dot(qa, operands(j)[0], preferred_element_type=F32)

        def accumulate(rows, alpha, p, vv):
            acc_sc[rows, :] = alpha * acc_sc[rows, :] + jnp.dot(p, vv, preferred_element_type=F32)

        pending = [logits(it) for it in items[:FLASH_AHEAD]]
        held = None
        for n, (j, r, mask) in enumerate(items):
            rows = pl.ds(r * rs, rs)
            s = pending.pop(0)
            if n + FLASH_AHEAD < len(items):
                pending.append(logits(items[n + FLASH_AHEAD]))
            if held is not None:
                accumulate(*held)
            if mask is not None:
                s = jnp.where(mask, s, MASKED)
            m_prev = m_sc[rows, :]
            m_new = jnp.maximum(m_prev, jnp.max(s, axis=-1, keepdims=True))
            p = jnp.exp(s - jnp.tile(m_new, (1, tk // LANES)))
            m_sc[rows, :] = m_new
            held = (rows, jnp.exp(m_prev - m_new), p.astype(BF16), operands(j)[1])
        accumulate(*held)

    def edge_items(dj_list, j_of):
        items = []
        for dj in dj_list:
            j = j_of(dj)
            for r in range(nst):
                off, ko = r * rs, dj * tk
                lo = off - WINDOW + 1 if band else None
                if ko > off + rs - 1 or (band and ko + tk - 1 < lo):
                    continue
                full = ko + tk - 1 <= off and (not band or ko > off + rs - 1 - WINDOW)
                mask = None
                if not full:
                    mask = col + ko <= row + off
                    if band:
                        mask = mask & (col + ko > row + (off - WINDOW))
                items.append((j, r, mask))
        return items

    if band:
        for dj in range(-(WINDOW // tk), 0):
            @pl.when(i * kpq + dj >= 0)
            def _():
                jj = i * kpq + dj
                run(edge_items([dj], lambda _: jj))
    def body(jjs):
        tiles = [jj * kpq + dj for jj in jjs for dj in range(kpq)]
        run([(j, r, None) for j in tiles for r in range(tq // rs_main)], rs_main)

    def sweep(n, jj_of):
        def group(t, carry):
            body([jj_of(FLASH_GROUP * t + u) for u in range(FLASH_GROUP)])
            return carry
        lax.fori_loop(0, n // FLASH_GROUP, group, 0)

        def single(t, carry):
            body([jj_of(n - n % FLASH_GROUP + t)])
            return carry
        lax.fori_loop(0, n % FLASH_GROUP, single, 0)

    def run_diag():
        diag = [i * kpq + dj for dj in range(kpq)]
        run(edge_items(list(range(kpq)), lambda dj: diag[dj]))

    if decay:
        s_len = kt_ref.shape[-1]

        @pl.when(i == 0)
        def _():
            def chunk(c, best):
                kk = kt_ref[0, :, pl.ds(pl.multiple_of(c * tq, tq), tq)].astype(F32)
                return jnp.maximum(best, jnp.sum(kk * kk, axis=0, keepdims=True))
            ksq = lax.fori_loop(0, s_len // tq, chunk, jnp.zeros((1, tq), F32))
            kmax_sc[...] = jnp.broadcast_to(jnp.sqrt(jnp.max(ksq, axis=1, keepdims=True)), kmax_sc.shape)

        run_diag()
        q = qa_ref[0].astype(F32)
        qn = jnp.sqrt(jnp.sum(q * q, axis=1, keepdims=True))
        slack = jnp.max(qn * kmax_sc[0:1, 0:1] - m_sc[:, 0:1], axis=0, keepdims=True)
        cvec = -jnp.sum(kx_ref[0, 0, HEAD_DIM:HEAD_DIM + 3, :].astype(F32), axis=0, keepdims=True)
        pos = lax.broadcasted_iota(jnp.int32, (1, s_len), 1)
        q0 = i * tq
        c_q0 = jnp.sum(jnp.where(pos == q0, cvec, 0.0), axis=1, keepdims=True)
        dead = (pos < q0) & (slack + c_q0 - cvec <= -DECAY_CUT)
        n_dead = jnp.sum(jnp.where(dead, 1.0, 0.0)).astype(jnp.int32) // tq
        sweep(i - n_dead, lambda t: i - 1 - t)
    elif band:
        run_diag()
    else:
        sweep(i, lambda t: t)
        run_diag()
    acc = acc_sc[...]
    o_ref[0] = (acc / acc[:, HEAD_DIM:HEAD_DIM + 1]).astype(o_ref.dtype)


def flash(name, q_all, q0, kt_all, k0, v_all, v0, nh, rep, extra=None, kx=None, band=False,
          decay=False):
    assert not decay or (kx is not None and not band)
    b, s, _ = q_all.shape
    tk = min(WINDOW if band else FLASH_TK, s)
    tq = tk if band else min(FLASH_TQ, s)
    rs = min(FLASH_STRIP, tq)
    assert s % tq == 0 and tq % tk == 0 and tq % rs == 0 and (not band or WINDOW % tk == 0)
    in_specs = [pl.BlockSpec((1, tq, LANES), lambda bi, hi, i: (bi, i, q0 + hi))]
    args = [q_all]
    tps = 1
    if extra is not None:
        he, nsup = extra.shape[1], extra.shape[2]
        rep_e = nh // he
        assert (s // nsup) % tk == 0
        tps = (s // nsup) // tk
        in_specs.append(pl.BlockSpec((1, 1, nsup, tq, LANES), lambda bi, hi, i: (bi, hi // rep_e, 0, i, 0)))
        args.append(extra)
    in_specs.append(pl.BlockSpec((1, LANES, s), lambda bi, hi, i: (bi, k0 + hi // rep, 0)))
    args.append(kt_all)
    if kx is not None:
        bx, hx = kx.shape[0], kx.shape[1]
        in_specs.append(pl.BlockSpec(
            (1, 1, LANES, s), lambda bi, hi, i: (bi if bx > 1 else 0, hi if hx > 1 else 0, 0, 0)))
        args.append(kx)
    in_specs.append(pl.BlockSpec((1, s, LANES), lambda bi, hi, i: (bi, 0, v0 + hi // rep)))
    args.append(v_all)
    return pl.pallas_call(
        functools.partial(_flash_kernel, tq=tq, tk=tk, rs=rs, rs_main=min(FLASH_STRIP_MAIN, tq),
                          tps=tps, band=band, decay=decay,
                          has_extra=extra is not None, has_kx=kx is not None),
        grid=(b, nh, s // tq),
        in_specs=in_specs,
        out_specs=pl.BlockSpec((1, tq, LANES), lambda bi, hi, i: (bi, i, hi)),
        out_shape=jax.ShapeDtypeStruct((b, s, nh * LANES), BF16),
        scratch_shapes=[pltpu.VMEM((tq, LANES), F32), pltpu.VMEM((tq, LANES), F32)]
        + ([pltpu.VMEM((8, LANES), F32)] if decay else []),
        compiler_params=_cparams("parallel", "parallel", "arbitrary" if decay else "parallel"),
        name=name,
    )(*args)


def _proj_norm_res_kernel(a_ref, w_ref, g_ref, x_ref, o_ref):
    y = jnp.dot(a_ref[...], w_ref[...], preferred_element_type=F32)
    o_ref[...] = x_ref[...] + _rms(y, g_ref[...])


def proj_norm_res(a, w, g, x, tm=512):
    n, k = a.shape
    d = w.shape[1]
    return pl.pallas_call(
        _proj_norm_res_kernel,
        grid=(n // tm,),
        in_specs=[pl.BlockSpec((tm, k), lambda i: (i, 0)),
                  pl.BlockSpec((k, d), lambda i: (0, 0)),
                  pl.BlockSpec((1, d), lambda i: (0, 0)),
                  pl.BlockSpec((tm, d), lambda i: (i, 0))],
        out_specs=pl.BlockSpec((tm, d), lambda i: (i, 0)),
        out_shape=jax.ShapeDtypeStruct((n, d), F32),
        compiler_params=_cparams("parallel"),
        name="proj_norm_res",
    )(a, w, g.reshape(1, d), x)


def _even_out_kernel(of_ref, oc_ref, os_ref, ow_ref, gl_ref, wf_ref, wn_ref, g_ref, x_ref, o_ref,
                     *, nh, g0):
    gate = jax.nn.sigmoid(gl_ref[...])
    parts = []
    for h in range(nh):
        sl = slice(h * LANES, (h + 1) * LANES)
        c = g0 + 3 * h
        parts.append(gate[:, c:c + 1] * oc_ref[:, sl].astype(F32)
                     + gate[:, c + 1:c + 2] * os_ref[:, sl].astype(F32)
                     + gate[:, c + 2:c + 3] * ow_ref[:, sl].astype(F32))
    a = jnp.concatenate(parts, axis=-1).astype(BF16)
    y = (jnp.dot(of_ref[...], wf_ref[...], preferred_element_type=F32)
         + jnp.dot(a, wn_ref[...], preferred_element_type=F32))
    o_ref[...] = x_ref[...] + _rms(y, g_ref[...])


def even_out(o_fox, o_cmp, o_slc, o_win, small, wf, wn, g, x, nh, g0, tm=512):
    n, k = o_fox.shape
    d = wf.shape[1]
    act = pl.BlockSpec((tm, k), lambda i: (i, 0))
    wspec = pl.BlockSpec((k, d), lambda i: (0, 0))
    return pl.pallas_call(
        functools.partial(_even_out_kernel, nh=nh, g0=g0),
        grid=(n // tm,),
        in_specs=[act, act, act, act,
                  pl.BlockSpec((tm, LANES), lambda i: (i, 0)),
                  wspec, wspec,
                  pl.BlockSpec((1, d), lambda i: (0, 0)),
                  pl.BlockSpec((tm, d), lambda i: (i, 0))],
        out_specs=pl.BlockSpec((tm, d), lambda i: (i, 0)),
        out_shape=jax.ShapeDtypeStruct((n, d), F32),
        compiler_params=_cparams("parallel"),
        name="even_out",
    )(o_fox, o_cmp, o_slc, o_win, small, wf, wn, g.reshape(1, d), x)


def _ffn_kernel(x_ref, gpre_ref, wg_ref, wu_ref, wd_ref, gpost_ref, o_ref, h_sc, acc_sc):
    f = pl.program_id(1)

    @pl.when(f == 0)
    def _():
        h_sc[...] = _rms(x_ref[...], gpre_ref[...]).astype(BF16)
        acc_sc[...] = jnp.zeros(acc_sc.shape, F32)

    h = h_sc[...]
    a = jnp.dot(h, wg_ref[...], preferred_element_type=F32)
    u = jnp.dot(h, wu_ref[...], preferred_element_type=F32)
    act = (jax.nn.silu(a) * u).astype(BF16)
    acc_sc[...] += jnp.dot(act, wd_ref[...], preferred_element_type=F32)

    @pl.when(f == pl.num_programs(1) - 1)
    def _():
        o_ref[...] = x_ref[...] + _rms(acc_sc[...], gpost_ref[...])


def ffn(x, gpre, wg, wu, wd, gpost, tm=1024, tf=1408):
    n, d = x.shape
    dff = wg.shape[1]
    assert n % tm == 0 and dff % tf == 0
    return pl.pallas_call(
        _ffn_kernel,
        grid=(n // tm, dff // tf),
        in_specs=[pl.BlockSpec((tm, d), lambda i, f: (i, 0)),
                  pl.BlockSpec((1, d), lambda i, f: (0, 0)),
                  pl.BlockSpec((d, tf), lambda i, f: (0, f)),
                  pl.BlockSpec((d, tf), lambda i, f: (0, f)),
                  pl.BlockSpec((tf, d), lambda i, f: (f, 0)),
                  pl.BlockSpec((1, d), lambda i, f: (0, 0))],
        out_specs=pl.BlockSpec((tm, d), lambda i, f: (i, 0)),
        out_shape=jax.ShapeDtypeStruct((n, d), F32),
        scratch_shapes=[pltpu.VMEM((tm, d), BF16), pltpu.VMEM((tm, d), F32)],
        compiler_params=_cparams("parallel", "arbitrary"),
        name="ffn",
    )(x, gpre.reshape(1, d), wg, wu, wd, gpost.reshape(1, d))


def _rope_tables(s):
    inv = ROPE_THETA ** (-jnp.arange(0, HEAD_DIM, 2, dtype=F32) / HEAD_DIM)
    ang = jnp.arange(s, dtype=F32)[:, None] * inv[None, :]
    cos, sin = jnp.cos(ang), jnp.sin(ang)
    reps = LANES // HEAD_DIM
    cos2 = jnp.tile(jnp.concatenate([cos, cos], -1), (1, reps))
    sin2 = jnp.tile(jnp.concatenate([-sin, sin], -1), (1, reps))
    return (cos2, sin2), (cos2.T[:HEAD_DIM], sin2.T[:HEAD_DIM])


def _slab_cols(w, scale=1.0):
    d, c = w.shape
    w3 = (w * scale).reshape(d, c // HEAD_DIM, HEAD_DIM)
    return jnp.concatenate([w3, jnp.zeros_like(w3)], axis=-1).reshape(d, 2 * c)


def _slab_rows(w):
    return _slab_cols(w.T).T


def _ones_lane(n_slabs):
    one = np.zeros((n_slabs, LANES), np.float32)
    one[:, HEAD_DIM] = 1.0
    return jnp.asarray(one.reshape(-1))


def _block_indicator_rows(s, block):
    blk = (np.arange(s) // block) % HEAD_DIM
    ind = np.zeros((LANES, s), np.float32)
    ind[HEAD_DIM + blk, np.arange(s)] = 1.0
    return jnp.asarray(ind, BF16)[None, None]


def _overlap_matrix(ncp, ns):
    ratio = SLC_BLOCK // CMP_STRIDE
    m = np.arange(ncp)[:, None]
    j = np.arange(ns)[None, :]
    ok = (m >= ratio * j - 1) & (m <= ratio * j + ratio - 1) & (m < ncp - 1)
    return jnp.asarray(ok, BF16)


def _compress_weights(pe, w1, w2, g):
    hid = w1.shape[-1]
    w1r = w1.reshape(2, CMP_STRIDE, HEAD_DIM, hid)
    w1g = jnp.zeros((g, 2, CMP_STRIDE, g, LANES, hid), F32)
    for gi in range(g):
        w1g = w1g.at[gi, :, :, gi, :HEAD_DIM].set(w1r)
    w1g = w1g.reshape(g, 2, CMP_STRIDE * g * LANES, hid).astype(BF16)
    pe2 = jnp.zeros((2, CMP_STRIDE, g, LANES), F32).at[..., :HEAD_DIM].set(
        pe.reshape(2, CMP_STRIDE, 1, HEAD_DIM))
    w2p = jnp.concatenate([w2, jnp.zeros_like(w2)], axis=-1).astype(BF16)
    return pe2.reshape(2, CMP_STRIDE * g * LANES), w1g, w2p


def _out_rows(w):
    return _slab_rows(w).astype(BF16)


def _even_layer(x2, b, s, g_pre, g_post, w_in, b_f, pe_k, w1_k, w2_k, pe_v, w1_v, w2_v, w_out, tabs, tabs_t):
    hf, hn, g = H_FOX, H_NSA, NSA_GROUPS
    hd = HEAD_DIM
    rep = hn // g
    (w_fq, w_fk, w_fv, w_fl, w_nq, w_kc, w_vc, w_ks, w_vs, w_kw, w_vw, w_gl) = jnp.split(
        w_in, list(np.cumsum([hf * hd] * 3 + [hf] + [hn * hd] + [g * hd] * 6)), axis=1)
    scale = hd ** -0.5

    w_plain = jnp.concatenate([w_fq * scale, w_fv, w_vs, w_vw, w_vc], axis=1).astype(BF16)
    bias_plain = jnp.concatenate([jnp.zeros((hf * LANES,), F32), _ones_lane(hf + 2 * g),
                                  jnp.zeros((g * LANES,), F32)])
    p_plain = norm_proj(x2, g_pre, w_plain, bias_plain, None, s, BF16).reshape(b, s, -1)
    fq0, fv0, vs0, vw0 = 0, hf, 2 * hf, 2 * hf + g
    vc_col = (2 * hf + 2 * g) * LANES
    w_rope = jnp.concatenate([w_nq * scale, w_kc], axis=1).astype(BF16)
    p_rope = norm_proj(x2, g_pre, w_rope, jnp.zeros((2 * w_rope.shape[1],), F32), tabs, s, BF16
                       ).reshape(b, s, -1)
    n_small = hf + 3 * hn
    w_small = jnp.pad(jnp.concatenate([w_fl, w_gl], axis=1), ((0, 0), (0, LANES - n_small))).astype(BF16)
    p_small = norm_proj(x2, g_pre, w_small, jnp.zeros((LANES,), F32), None, s, F32, slabs=False)
    kt_fox = norm_proj_t(x2, g_pre, w_fk.T.astype(BF16), None, b, s)
    kt_nsa = norm_proj_t(x2, g_pre, jnp.concatenate([w_ks, w_kw], axis=1).T.astype(BF16),
                         tabs_t, b, s)

    qx, kx = fox_decay(p_small.reshape(b, s, LANES), jnp.pad(b_f, (0, LANES - hf)).reshape(1, LANES), hf)
    o_fox = flash("flash_fox", p_plain, fq0, kt_fox, 0, p_plain, fv0, hf, 1, extra=qx, kx=kx, decay=True)

    ncp = s // CMP_STRIDE
    ns = s // SLC_BLOCK
    kc = p_rope[:, :, hn * LANES:].reshape(b, ncp, CMP_STRIDE * g * LANES)
    vc = p_plain[:, :, vc_col:].reshape(b, ncp, CMP_STRIDE * g * LANES)
    kcmp = nsa_compress(kc, *_compress_weights(pe_k, w1_k, w2_k, g))
    vcmp = nsa_compress(vc, *_compress_weights(pe_v, w1_v, w2_v, g))
    o_cmp, selb = nsa_select(p_rope, kcmp.transpose(0, 1, 3, 2), vcmp, _overlap_matrix(ncp, ns), rep)
    o_slc = flash("flash_sel", p_rope, 0, kt_nsa, 0, p_plain, vs0, hn, rep, extra=selb,
                  kx=_block_indicator_rows(s, SLC_BLOCK))
    o_win = flash("flash_band", p_rope, 0, kt_nsa, g, p_plain, vw0, hn, rep, band=True)

    n = b * s
    return even_out(o_fox.reshape(n, -1), o_cmp.reshape(n, -1), o_slc.reshape(n, -1),
                    o_win.reshape(n, -1), p_small, _out_rows(w_out[:hf * hd]), _out_rows(w_out[hf * hd:]),
                    g_post, x2, hn, hf)


def _odd_layer(x2, b, s, g_pre, g_post, w_in, w_out, tabs, tabs_t):
    h = H_MOBA
    hd = HEAD_DIM
    d = h * hd
    scale = hd ** -0.5
    w_q, w_k, w_v = w_in[:, :d], w_in[:, d:2 * d], w_in[:, 2 * d:]
    q_all = norm_proj(x2, g_pre, (w_q * scale).astype(BF16), jnp.zeros((h * LANES,), F32),
                      tabs, s, BF16).reshape(b, s, -1)
    v_all = norm_proj(x2, g_pre, w_v.astype(BF16), _ones_lane(h), None, s, BF16).reshape(b, s, -1)
    kt_all = norm_proj_t(x2, g_pre, w_k.T.astype(BF16), tabs_t, b, s)
    ind = _block_indicator_rows(s, MOBA_BLOCK)
    selb = moba_select(q_all, kt_all, ind[0, 0].T, h)
    o = flash("flash_moba", q_all, 0, kt_all, 0, v_all, 0, h, 1, extra=selb, kx=ind)
    return proj_norm_res(o.reshape(b * s, -1), _out_rows(w_out), g_post, x2)


def kernel(x, ev_w_in, ev_b_f, ev_cmp_pe_k, ev_cmp_w1_k, ev_cmp_w2_k, ev_cmp_pe_v, ev_cmp_w1_v,
           ev_cmp_w2_v, ev_w_out, od_w_in, od_w_out, g_mix_pre, g_mix_post, g_ffn_pre, g_ffn_post,
           ffn_w_gate, ffn_w_up, ffn_w_down):
    b, s, d = x.shape
    depth = g_mix_pre.shape[0]
    tabs, tabs_t = _rope_tables(s)
    x2 = x.reshape(b * s, d)
    for layer in range(depth):
        if layer % 2 == 0:
            e = layer // 2
            x2 = _even_layer(x2, b, s, g_mix_pre[layer], g_mix_post[layer], ev_w_in[e], ev_b_f[e],
                             ev_cmp_pe_k[e], ev_cmp_w1_k[e], ev_cmp_w2_k[e], ev_cmp_pe_v[e],
                             ev_cmp_w1_v[e], ev_cmp_w2_v[e], ev_w_out[e], tabs, tabs_t)
        else:
            o = layer // 2
            x2 = _odd_layer(x2, b, s, g_mix_pre[layer], g_mix_post[layer], od_w_in[o], od_w_out[o],
                            tabs, tabs_t)
        x2 = ffn(x2, g_ffn_pre[layer], ffn_w_gate[layer].astype(BF16), ffn_w_up[layer].astype(BF16),
                 ffn_w_down[layer].astype(BF16), g_ffn_post[layer])
    return x2.reshape(b, s, d)
```

```python
import functools

import jax
import jax.numpy as jnp
import numpy as np
from jax import lax
from jax.experimental import pallas as pl
from jax.experimental.pallas import tpu as pltpu

F32 = jnp.float32
BF16 = jnp.bfloat16

HEAD_DIM = 64
LANES = 128
ROPE_THETA = 10000.0
RMS_EPS = 1e-6
CMP_STRIDE = 16
CMP_LEN = 32
SLC_BLOCK = 64
SLC_TOPN = 16
WINDOW = 512
MOBA_BLOCK = 256
MOBA_TOPK = 3
H_FOX, H_NSA, NSA_GROUPS, H_MOBA = 8, 8, 2, 16
SLC_SHIFT = SLC_BLOCK.bit_length() - 1
MOBA_SHIFT = MOBA_BLOCK.bit_length() - 1
PROJ_TILE_COLS = 1408
PROJ_T_TILE_ROWS = 1024
MASKED = -1e30
M_INIT = -1e29
DECAY_CUT = 100.0
FLASH_TQ = 1024
FLASH_TK = 512
FLASH_STRIP = 512
FLASH_STRIP_MAIN = 1024
FLASH_GROUP = 2
FLASH_AHEAD = 2
VMEM_LIMIT = 48 * 1024 * 1024
NT_DIMS = (((1,), (1,)), ((), ()))


def _cparams(*sem):
    return pltpu.CompilerParams(dimension_semantics=sem, vmem_limit_bytes=VMEM_LIMIT)


def _rms(x, g):
    return x * lax.rsqrt(jnp.mean(x * x, axis=-1, keepdims=True) + RMS_EPS) * g


def _split3(x):
    hi = x.astype(BF16).astype(F32)
    r = x - hi
    mid = r.astype(BF16).astype(F32)
    lo = (r - mid).astype(BF16).astype(F32)
    return hi, mid, lo


def _col_tile(nc, cap):
    best = LANES
    for k in range(1, nc // LANES + 1):
        if nc % (k * LANES) == 0 and k * LANES <= cap:
            best = k * LANES
    return best


def _norm_proj_kernel(x_ref, g_ref, w_ref, b_ref, *rest, rope, slabs):
    if rope:
        cos_ref, sin_ref, o_ref, h_sc = rest
    else:
        o_ref, h_sc = rest

    @pl.when(pl.program_id(1) == 0)
    def _():
        h_sc[...] = _rms(x_ref[...], g_ref[...]).astype(BF16)

    t = jnp.dot(h_sc[...], w_ref[...], preferred_element_type=F32)
    tn = t.shape[1]
    lane = lax.broadcasted_iota(jnp.int32, t.shape, 1)
    if rope:
        reps = tn // LANES
        cos = jnp.tile(cos_ref[...], (1, reps))
        sin = jnp.tile(sin_ref[...], (1, reps))
        first = (lane & (HEAD_DIM - 1)) < HEAD_DIM // 2
        rot = jnp.where(first, pltpu.roll(t, tn - HEAD_DIM // 2, 1), pltpu.roll(t, HEAD_DIM // 2, 1))
        t = t * cos + rot * sin
    if slabs:
        low = lax.broadcasted_iota(jnp.int32, (t.shape[0], LANES), 1) < HEAD_DIM
        parts = []
        for pair in range(tn // LANES):
            v = t[:, pair * LANES:(pair + 1) * LANES]
            parts.append(jnp.where(low, v, 0.0))
            parts.append(jnp.where(low, pltpu.roll(v, HEAD_DIM, 1), 0.0))
        t = jnp.concatenate(parts, axis=1)
    o_ref[...] = (t + b_ref[...]).astype(o_ref.dtype)


def norm_proj(x, g, w, bias, rope_tabs, seq, out_dtype, slabs=True, tm=512):
    n, d = x.shape
    nc = w.shape[1]
    tn = _col_tile(nc, PROJ_TILE_COLS)
    wide = 2 if slabs else 1
    assert n % tm == 0 and seq % tm == 0 and nc % tn == 0
    rope = rope_tabs is not None
    in_specs = [
        pl.BlockSpec((tm, d), lambda i, j: (i, 0)),
        pl.BlockSpec((1, d), lambda i, j: (0, 0)),
        pl.BlockSpec((d, tn), lambda i, j: (0, j)),
        pl.BlockSpec((1, wide * tn), lambda i, j: (0, j)),
    ]
    args = [x, g.reshape(1, d), w, bias.reshape(1, wide * nc)]
    if rope:
        spt = seq // tm
        in_specs += [pl.BlockSpec((tm, LANES), lambda i, j: (i % spt, 0))] * 2
        args += list(rope_tabs)
    return pl.pallas_call(
        functools.partial(_norm_proj_kernel, rope=rope, slabs=slabs),
        grid=(n // tm, nc // tn),
        in_specs=in_specs,
        out_specs=pl.BlockSpec((tm, wide * tn), lambda i, j: (i, j)),
        out_shape=jax.ShapeDtypeStruct((n, wide * nc), out_dtype),
        scratch_shapes=[pltpu.VMEM((tm, d), BF16)],
        compiler_params=_cparams("parallel", "arbitrary"),
        name="norm_proj_rope" if rope else "norm_proj",
    )(*args)


def _norm_proj_t_kernel(x_ref, g_ref, wt_ref, *rest, rope):
    if rope:
        cos_ref, sin_ref, o_ref, h_sc = rest
    else:
        o_ref, h_sc = rest

    @pl.when(pl.program_id(1) == 0)
    def _():
        h_sc[...] = _rms(x_ref[...], g_ref[...]).astype(BF16)

    t = lax.dot_general(wt_ref[...], h_sc[...], NT_DIMS, preferred_element_type=F32)
    tn, tm = t.shape
    nheads = tn // HEAD_DIM
    if rope:
        cos = jnp.tile(cos_ref[...], (nheads, 1))
        sin = jnp.tile(sin_ref[...], (nheads, 1))
        r = lax.broadcasted_iota(jnp.int32, t.shape, 0)
        first = (r & (HEAD_DIM - 1)) < HEAD_DIM // 2
        rot = jnp.where(first, pltpu.roll(t, tn - HEAD_DIM // 2, 0), pltpu.roll(t, HEAD_DIM // 2, 0))
        t = t * cos + rot * sin
    spare = jnp.zeros((LANES - HEAD_DIM, tm), o_ref.dtype)
    for h in range(nheads):
        o_ref[0, h * LANES:h * LANES + HEAD_DIM, :] = t[h * HEAD_DIM:(h + 1) * HEAD_DIM].astype(o_ref.dtype)
        o_ref[0, h * LANES + HEAD_DIM:(h + 1) * LANES, :] = spare


def norm_proj_t(x, g, wt, rope_tabs_t, b, seq, tm=512):
    n, d = x.shape
    nc = wt.shape[0]
    tn = _col_tile(nc, PROJ_T_TILE_ROWS)
    assert n % tm == 0 and seq % tm == 0 and nc % tn == 0
    spt = seq // tm
    rope = rope_tabs_t is not None
    in_specs = [
        pl.BlockSpec((tm, d), lambda i, j: (i, 0)),
        pl.BlockSpec((1, d), lambda i, j: (0, 0)),
        pl.BlockSpec((tn, d), lambda i, j: (j, 0)),
    ]
    args = [x, g.reshape(1, d), wt]
    if rope:
        in_specs += [pl.BlockSpec((HEAD_DIM, tm), lambda i, j: (0, i % spt))] * 2
        args += list(rope_tabs_t)
    return pl.pallas_call(
        functools.partial(_norm_proj_t_kernel, rope=rope),
        grid=(n // tm, nc // tn),
        in_specs=in_specs,
        out_specs=pl.BlockSpec((1, 2 * tn, tm), lambda i, j: (i // spt, j, i % spt)),
        out_shape=jax.ShapeDtypeStruct((b, 2 * nc, seq), BF16),
        scratch_shapes=[pltpu.VMEM((tm, d), BF16)],
        compiler_params=_cparams("parallel", "arbitrary"),
        name="norm_proj_t_rope" if rope else "norm_proj_t",
    )(*args)


def _fox_decay_kernel(fl_ref, b_ref, pq_ref, pk_ref, oq_ref, ok_ref, carry_sc, *, nh):
    @pl.when(pl.program_id(1) == 0)
    def _():
        carry_sc[...] = jnp.zeros(carry_sc.shape, F32)

    c = jax.nn.log_sigmoid(fl_ref[0] + b_ref[...])
    ts = c.shape[0]
    row = lax.broadcasted_iota(jnp.int32, c.shape, 0)
    lane = lax.broadcasted_iota(jnp.int32, c.shape, 1)
    sh = 1
    while sh < ts:
        c = c + jnp.where(row >= sh, pltpu.roll(c, sh, 0), 0.0)
        sh *= 2
    c = c + carry_sc[0:1, :]
    carry_sc[0:1, :] = c[ts - 1:ts, :]
    hi, mid, lo = _split3(c)
    c3 = jnp.where(lane < nh, hi,
                   jnp.where(lane < 2 * nh, pltpu.roll(mid, nh, 1),
                             jnp.where(lane < 3 * nh, pltpu.roll(lo, 2 * nh, 1),
                                       jnp.where(lane == 3 * nh, 1.0, 0.0)))).astype(BF16)
    for h in range(nh):
        oq_ref[0, h, 0] = jnp.dot(c3, pq_ref[h], preferred_element_type=F32).astype(oq_ref.dtype)
        ok_ref[0, h] = lax.dot_general(pk_ref[h], c3, NT_DIMS,
                                       preferred_element_type=F32).astype(ok_ref.dtype)


def _fox_placement(nh):
    pq = np.zeros((nh, LANES, LANES), np.float32)
    pk = np.zeros((nh, LANES, LANES), np.float32)
    one = 3 * nh
    for h in range(nh):
        for t in range(3):
            pq[h, one, HEAD_DIM + t] = 1.0
            pq[h, t * nh + h, HEAD_DIM + 3 + t] = 1.0
            pk[h, HEAD_DIM + t, t * nh + h] = -1.0
            pk[h, HEAD_DIM + 3 + t, one] = 1.0
    return jnp.asarray(pq, BF16), jnp.asarray(pk, BF16)


def fox_decay(small, b_row, nh, ts=2048):
    b, s, _ = small.shape
    ts = min(ts, s)
    pq, pk = _fox_placement(nh)
    return pl.pallas_call(
        functools.partial(_fox_decay_kernel, nh=nh),
        grid=(b, s // ts),
        in_specs=[pl.BlockSpec((1, ts, LANES), lambda i, j: (i, j, 0)),
                  pl.BlockSpec((1, LANES), lambda i, j: (0, 0)),
                  pl.BlockSpec((nh, LANES, LANES), lambda i, j: (0, 0, 0)),
                  pl.BlockSpec((nh, LANES, LANES), lambda i, j: (0, 0, 0))],
        out_specs=[pl.BlockSpec((1, nh, 1, ts, LANES), lambda i, j: (i, 0, 0, j, 0)),
                   pl.BlockSpec((1, nh, LANES, ts), lambda i, j: (i, 0, 0, j))],
        out_shape=[jax.ShapeDtypeStruct((b, nh, 1, s, LANES), BF16),
                   jax.ShapeDtypeStruct((b, nh, LANES, s), BF16)],
        scratch_shapes=[pltpu.VMEM((8, LANES), F32)],
        compiler_params=_cparams("parallel", "arbitrary"),
        name="fox_decay",
    )(small, b_row, pq, pk)


def _nsa_compress_kernel(t_ref, pe_ref, w1_ref, w2_ref, o_ref):
    t = t_ref[0].astype(F32)
    ncp = t.shape[0]
    a = jnp.dot((t + pe_ref[0:1, :]).astype(BF16), w1_ref[0, 0], preferred_element_type=F32)
    bm = jnp.dot((t + pe_ref[1:2, :]).astype(BF16), w1_ref[0, 1], preferred_element_type=F32)
    pre = a + pltpu.roll(bm, ncp - 1, 0)
    hid = jax.nn.gelu(pre)
    o_ref[0, 0] = jnp.dot(hid.astype(BF16), w2_ref[...], preferred_element_type=F32).astype(o_ref.dtype)


def nsa_compress(t, pe, w1, w2):
    b, ncp, cw = t.shape
    g, _, _, hid = w1.shape
    return pl.pallas_call(
        _nsa_compress_kernel,
        grid=(b, g),
        in_specs=[pl.BlockSpec((1, ncp, cw), lambda i, j: (i, 0, 0)),
                  pl.BlockSpec((2, cw), lambda i, j: (0, 0)),
                  pl.BlockSpec((1, 2, cw, hid), lambda i, j: (j, 0, 0, 0)),
                  pl.BlockSpec((hid, LANES), lambda i, j: (0, 0))],
        out_specs=pl.BlockSpec((1, 1, ncp, LANES), lambda i, j: (i, j, 0, 0)),
        out_shape=jax.ShapeDtypeStruct((b, g, ncp, LANES), BF16),
        compiler_params=_cparams("parallel", "parallel"),
        name="nsa_compress",
    )(t, pe, w1, w2)


def _topk_mask(work, col, k):
    sel = jnp.zeros(work.shape, jnp.bool_)
    col = col.astype(F32)
    for _ in range(k):
        mx = jnp.max(work, axis=-1, keepdims=True)
        first = jnp.min(jnp.where(work == mx, col, jnp.inf), axis=-1, keepdims=True)
        hit = col == first
        sel = jnp.logical_or(sel, hit)
        work = jnp.where(hit, -jnp.inf, work)
    return sel


def _nsa_select_kernel(q_ref, kct_ref, vc_ref, m_ref, oc_ref, sb_ref, *, tq, rep, nsup, sps):
    i = pl.program_id(2)
    ncp = kct_ref.shape[-1]
    ns = m_ref.shape[-1]
    qpos = i * tq + lax.broadcasted_iota(jnp.int32, (tq, 1), 0)
    has_block = jnp.where(qpos >= CMP_LEN - 1, 1.0, 0.0)
    qblk = qpos >> SLC_SHIFT
    zeros = jnp.zeros((tq, HEAD_DIM), sb_ref.dtype)
    hidden = jnp.concatenate([zeros, jnp.full((tq, HEAD_DIM), MASKED, sb_ref.dtype)], axis=-1)

    def prefix(nv):
        wc, ws = ncp * nv // nsup, ns * nv // nsup
        cend = lax.broadcasted_iota(jnp.int32, (1, wc), 1) * CMP_STRIDE + (CMP_LEN - 1)
        cmask = cend <= qpos
        kct = kct_ref[0, 0, :, :wc]
        vc = vc_ref[0, 0, :wc, :]
        pcs = jnp.zeros((tq, wc), F32)
        for r in range(rep):
            s = jnp.dot(q_ref[0, :, r * LANES:(r + 1) * LANES], kct, preferred_element_type=F32)
            s = jnp.where(cmask, s, MASKED)
            e = jnp.exp(s - jnp.max(s, axis=-1, keepdims=True))
            p = e * (has_block / jnp.maximum(jnp.sum(e, axis=-1, keepdims=True), 1e-30))
            oc_ref[0, :, r * LANES:(r + 1) * LANES] = jnp.dot(
                p.astype(BF16), vc, preferred_element_type=F32).astype(oc_ref.dtype)
            pcs = pcs + p
        mm = m_ref[:wc, :ws]
        imp = sum(jnp.dot(part.astype(BF16), mm, preferred_element_type=F32) for part in _split3(pcs))
        sblk = lax.broadcasted_iota(jnp.int32, (1, ws), 1)
        forced = (sblk == 0) | (sblk == qblk) | (sblk == qblk - 1)
        free = jnp.where(forced | (sblk > qblk), -jnp.inf, imp)
        sel = _topk_mask(free, sblk, min(SLC_TOPN, ns) - 3)
        bias = jnp.where((sel | forced) & (sblk <= qblk), 0.0, MASKED).astype(sb_ref.dtype)
        for j in range(nsup):
            sb_ref[0, 0, j] = hidden if j >= nv else jnp.concatenate(
                [zeros, bias[:, j * HEAD_DIM:(j + 1) * HEAD_DIM]], axis=-1)

    for nv in range(1, nsup + 1):
        pl.when(i // sps == nv - 1)(functools.partial(prefix, nv))


def nsa_select(q_all, kct, vc, m, rep, tq=512):
    b, s, _ = q_all.shape
    g = kct.shape[1]
    ncp = kct.shape[-1]
    ns = m.shape[-1]
    nsup = ns // HEAD_DIM
    tq = min(tq, s)
    assert SLC_BLOCK == 64 and ns % HEAD_DIM == 0 and (s // tq) % nsup == 0
    return pl.pallas_call(
        functools.partial(_nsa_select_kernel, tq=tq, rep=rep, nsup=nsup, sps=(s // tq) // nsup),
        grid=(b, g, s // tq),
        in_specs=[pl.BlockSpec((1, tq, rep * LANES), lambda bi, gi, i: (bi, i, gi)),
                  pl.BlockSpec((1, 1, LANES, ncp), lambda bi, gi, i: (bi, gi, 0, 0)),
                  pl.BlockSpec((1, 1, ncp, LANES), lambda bi, gi, i: (bi, gi, 0, 0)),
                  pl.BlockSpec((ncp, ns), lambda bi, gi, i: (0, 0))],
        out_specs=[pl.BlockSpec((1, tq, rep * LANES), lambda bi, gi, i: (bi, i, gi)),
                   pl.BlockSpec((1, 1, nsup, tq, LANES), lambda bi, gi, i: (bi, gi, 0, i, 0))],
        out_shape=[jax.ShapeDtypeStruct((b, s, g * rep * LANES), BF16),
                   jax.ShapeDtypeStruct((b, g, nsup, s, LANES), BF16)],
        compiler_params=_cparams("parallel", "parallel", "parallel"),
        name="nsa_select",
    )(q_all, kct, vc, m)


def _moba_select_kernel(q_ref, kt_ref, ind_ref, sb_ref, kbar_sc, *, tq):
    i = pl.program_id(2)

    @pl.when(i == 0)
    def _():
        kbar_sc[...] = jnp.dot(kt_ref[0], ind_ref[...], preferred_element_type=F32) * (1.0 / MOBA_BLOCK)

    q = q_ref[0]
    gate = sum(jnp.dot(q, part.astype(BF16), preferred_element_type=F32)
               for part in _split3(kbar_sc[...]))
    qpos = i * tq + lax.broadcasted_iota(jnp.int32, (tq, 1), 0)
    cur = qpos >> MOBA_SHIFT
    lane = lax.broadcasted_iota(jnp.int32, (1, LANES), 1)
    blk = lane - HEAD_DIM
    past = (blk >= 0) & (blk < cur)
    sel = _topk_mask(jnp.where(past, gate, -jnp.inf), lane, MOBA_TOPK)
    keep = (blk < 0) | (sel & past) | (blk == cur)
    sb_ref[0, 0, 0] = jnp.where(keep, 0.0, MASKED).astype(sb_ref.dtype)


def moba_select(q_all, kt_all, ind, nh, tq=2048):
    b, s, _ = q_all.shape
    assert MOBA_BLOCK == 256 and s // MOBA_BLOCK <= HEAD_DIM
    tq = min(tq, s)
    return pl.pallas_call(
        functools.partial(_moba_select_kernel, tq=tq),
        grid=(b, nh, s // tq),
        in_specs=[pl.BlockSpec((1, tq, LANES), lambda bi, hi, i: (bi, i, hi)),
                  pl.BlockSpec((1, LANES, s), lambda bi, hi, i: (bi, hi, 0)),
                  pl.BlockSpec((s, LANES), lambda bi, hi, i: (0, 0))],
        out_specs=pl.BlockSpec((1, 1, 1, tq, LANES), lambda bi, hi, i: (bi, hi, 0, i, 0)),
        out_shape=jax.ShapeDtypeStruct((b, nh, 1, s, LANES), BF16),
        scratch_shapes=[pltpu.VMEM((LANES, LANES), F32)],
        compiler_params=_cparams("parallel", "parallel", "arbitrary"),
        name="moba_select",
    )(q_all, kt_all, ind)


def _flash_kernel(*refs, tq, tk, rs, rs_main, tps, band, decay, has_extra, has_kx):
    refs = list(refs)
    qa_ref = refs.pop(0)
    ex_ref = refs.pop(0) if has_extra else None
    kt_ref = refs.pop(0)
    kx_ref = refs.pop(0) if has_kx else None
    kmax_sc = refs.pop() if decay else None
    v_ref, o_ref, m_sc, acc_sc = refs
    i = pl.program_id(2)
    m_sc[...] = jnp.full(m_sc.shape, M_INIT, F32)
    acc_sc[...] = jnp.zeros(acc_sc.shape, F32)
    row = lax.broadcasted_iota(jnp.int32, (rs, tk), 0)
    col = lax.broadcasted_iota(jnp.int32, (rs, tk), 1)
    nst = tq // rs
    kpq = tq // tk

    def run(items, rs=rs):
        tiles = {}

        def operands(j):
            if id(j) not in tiles:
                start = pl.multiple_of(j * tk, tk)
                kt = kt_ref[0, :, pl.ds(start, tk)]
                if kx_ref is not None:
                    kt = kt + kx_ref[0, 0, :, pl.ds(start, tk)]
                tiles[id(j)] = (kt, v_ref[0, pl.ds(start, tk), :])
            return tiles[id(j)]

        def logits(item):
            j, r, _ = item
            rows = pl.ds(r * rs, rs)
            qa = qa_ref[0, rows, :]
            if ex_ref is not None:
                qa = qa + ex_ref[0, 0, j // tps, rows, :]
            return jnp.dot(qa, operands(j)[0], preferred_element_type=F32)

        def accumulate(rows, alpha, p, vv):
            acc_sc[rows, :] = alpha * acc_sc[rows, :] + jnp.dot(p, vv, preferred_element_type=F32)

        pending = [logits(it) for it in items[:FLASH_AHEAD]]
        held = None
        for n, (j, r, mask) in enumerate(items):
            rows = pl.ds(r * rs, rs)
            s = pending.pop(0)
            if n + FLASH_AHEAD < len(items):
                pending.append(logits(items[n + FLASH_AHEAD]))
            if held is not None:
                accumulate(*held)
            if mask is not None:
                s = jnp.where(mask, s, MASKED)
            m_prev = m_sc[rows, :]
            m_new = jnp.maximum(m_prev, jnp.max(s, axis=-1, keepdims=True))
            p = jnp.exp(s - jnp.tile(m_new, (1, tk // LANES)))
            m_sc[rows, :] = m_new
            held = (rows, jnp.exp(m_prev - m_new), p.astype(BF16), operands(j)[1])
        accumulate(*held)

    def edge_items(dj_list, j_of):
        items = []
        for dj in dj_list:
            j = j_of(dj)
            for r in range(nst):
                off, ko = r * rs, dj * tk
                lo = off - WINDOW + 1 if band else None
                if ko > off + rs - 1 or (band and ko + tk - 1 < lo):
                    continue
                full = ko + tk - 1 <= off and (not band or ko > off + rs - 1 - WINDOW)
                mask = None
                if not full:
                    mask = col + ko <= row + off
                    if band:
                        mask = mask & (col + ko > row + (off - WINDOW))
                items.append((j, r, mask))
        return items

    def body(jjs):
        tiles = [jj * kpq + dj for jj in jjs for dj in range(kpq)]
        run([(j, r, None) for j in tiles for r in range(tq // rs_main)], rs_main)

    def sweep(n, jj_of):
        def group(t, carry):
            body([jj_of(FLASH_GROUP * t + u) for u in range(FLASH_GROUP)])
            return carry
        lax.fori_loop(0, n // FLASH_GROUP, group, 0)

        def single(t, carry):
            body([jj_of(n - n % FLASH_GROUP + t)])
            return carry
        lax.fori_loop(0, n % FLASH_GROUP, single, 0)

    def run_diag():
        diag = [i * kpq + dj for dj in range(kpq)]
        run(edge_items(list(range(kpq)), lambda dj: diag[dj]))

    if decay:
        s_len = kt_ref.shape[-1]

        @pl.when(i == 0)
        def _():
            def chunk(c, best):
                kk = kt_ref[0, :, pl.ds(pl.multiple_of(c * tq, tq), tq)].astype(F32)
                return jnp.maximum(best, jnp.sum(kk * kk, axis=0, keepdims=True))
            ksq = lax.fori_loop(0, s_len // tq, chunk, jnp.zeros((1, tq), F32))
            kmax_sc[...] = jnp.broadcast_to(jnp.sqrt(jnp.max(ksq, axis=1, keepdims=True)), kmax_sc.shape)

        run_diag()
        q = qa_ref[0].astype(F32)
        qn = jnp.sqrt(jnp.sum(q * q, axis=1, keepdims=True))
        slack = jnp.max(qn * kmax_sc[0:1, 0:1] - m_sc[:, 0:1], axis=0, keepdims=True)
        cvec = -jnp.sum(kx_ref[0, 0, HEAD_DIM:HEAD_DIM + 3, :].astype(F32), axis=0, keepdims=True)
        pos = lax.broadcasted_iota(jnp.int32, (1, s_len), 1)
        q0 = i * tq
        c_q0 = jnp.sum(jnp.where(pos == q0, cvec, 0.0), axis=1, keepdims=True)
        dead = (pos < q0) & (slack + c_q0 - cvec <= -DECAY_CUT)
        n_dead = jnp.sum(jnp.where(dead, 1.0, 0.0)).astype(jnp.int32) // tq
        sweep(i - n_dead, lambda t: i - 1 - t)
    elif band:
        pl.when(i == 0)(run_diag)

        @pl.when(i > 0)
        def _():
            near = {dj: i * kpq + dj for dj in range(-(WINDOW // tk), kpq)}
            run(edge_items(sorted(near), near.get))
    else:
        sweep(i, lambda t: t)
        run_diag()
    acc = acc_sc[...]
    o_ref[0] = (acc / acc[:, HEAD_DIM:HEAD_DIM + 1]).astype(o_ref.dtype)


def flash(name, q_all, q0, kt_all, k0, v_all, v0, nh, rep, extra=None, kx=None, band=False,
          decay=False):
    assert not decay or (kx is not None and not band)
    b, s, _ = q_all.shape
    tk = min(WINDOW if band else FLASH_TK, s)
    tq = min(FLASH_TQ, s)
    rs = min(FLASH_STRIP, tq)
    assert s % tq == 0 and tq % tk == 0 and tq % rs == 0 and (not band or WINDOW % tk == 0)
    in_specs = [pl.BlockSpec((1, tq, LANES), lambda bi, hi, i: (bi, i, q0 + hi))]
    args = [q_all]
    tps = 1
    if extra is not None:
        he, nsup = extra.shape[1], extra.shape[2]
        rep_e = nh // he
        assert (s // nsup) % tk == 0
        tps = (s // nsup) // tk
        in_specs.append(pl.BlockSpec((1, 1, nsup, tq, LANES), lambda bi, hi, i: (bi, hi // rep_e, 0, i, 0)))
        args.append(extra)
    in_specs.append(pl.BlockSpec((1, LANES, s), lambda bi, hi, i: (bi, k0 + hi // rep, 0)))
    args.append(kt_all)
    if kx is not None:
        bx, hx = kx.shape[0], kx.shape[1]
        in_specs.append(pl.BlockSpec(
            (1, 1, LANES, s), lambda bi, hi, i: (bi if bx > 1 else 0, hi if hx > 1 else 0, 0, 0)))
        args.append(kx)
    in_specs.append(pl.BlockSpec((1, s, LANES), lambda bi, hi, i: (bi, 0, v0 + hi // rep)))
    args.append(v_all)
    return pl.pallas_call(
        functools.partial(_flash_kernel, tq=tq, tk=tk, rs=rs, rs_main=min(FLASH_STRIP_MAIN, tq),
                          tps=tps, band=band, decay=decay,
                          has_extra=extra is not None, has_kx=kx is not None),
        grid=(b, nh, s // tq),
        in_specs=in_specs,
        out_specs=pl.BlockSpec((1, tq, LANES), lambda bi, hi, i: (bi, i, hi)),
        out_shape=jax.ShapeDtypeStruct((b, s, nh * LANES), BF16),
        scratch_shapes=[pltpu.VMEM((tq, LANES), F32), pltpu.VMEM((tq, LANES), F32)]
        + ([pltpu.VMEM((8, LANES), F32)] if decay else []),
        compiler_params=_cparams("parallel", "parallel", "arbitrary" if decay else "parallel"),
        name=name,
    )(*args)


def _proj_norm_res_kernel(a_ref, w_ref, g_ref, x_ref, o_ref):
    y = jnp.dot(a_ref[...], w_ref[...], preferred_element_type=F32)
    o_ref[...] = x_ref[...] + _rms(y, g_ref[...])


def proj_norm_res(a, w, g, x, tm=512):
    n, k = a.shape
    d = w.shape[1]
    return pl.pallas_call(
        _proj_norm_res_kernel,
        grid=(n // tm,),
        in_specs=[pl.BlockSpec((tm, k), lambda i: (i, 0)),
                  pl.BlockSpec((k, d), lambda i: (0, 0)),
                  pl.BlockSpec((1, d), lambda i: (0, 0)),
                  pl.BlockSpec((tm, d), lambda i: (i, 0))],
        out_specs=pl.BlockSpec((tm, d), lambda i: (i, 0)),
        out_shape=jax.ShapeDtypeStruct((n, d), F32),
        compiler_params=_cparams("parallel"),
        name="proj_norm_res",
    )(a, w, g.reshape(1, d), x)


def _even_out_kernel(of_ref, oc_ref, os_ref, ow_ref, gl_ref, wf_ref, wn_ref, g_ref, x_ref, o_ref,
                     *, nh, g0):
    gate = jax.nn.sigmoid(gl_ref[...])
    parts = []
    for h in range(nh):
        sl = slice(h * LANES, (h + 1) * LANES)
        c = g0 + 3 * h
        parts.append(gate[:, c:c + 1] * oc_ref[:, sl].astype(F32)
                     + gate[:, c + 1:c + 2] * os_ref[:, sl].astype(F32)
                     + gate[:, c + 2:c + 3] * ow_ref[:, sl].astype(F32))
    a = jnp.concatenate(parts, axis=-1).astype(BF16)
    y = (jnp.dot(of_ref[...], wf_ref[...], preferred_element_type=F32)
         + jnp.dot(a, wn_ref[...], preferred_element_type=F32))
    o_ref[...] = x_ref[...] + _rms(y, g_ref[...])


def even_out(o_fox, o_cmp, o_slc, o_win, small, wf, wn, g, x, nh, g0, tm=512):
    n, k = o_fox.shape
    d = wf.shape[1]
    act = pl.BlockSpec((tm, k), lambda i: (i, 0))
    wspec = pl.BlockSpec((k, d), lambda i: (0, 0))
    return pl.pallas_call(
        functools.partial(_even_out_kernel, nh=nh, g0=g0),
        grid=(n // tm,),
        in_specs=[act, act, act, act,
                  pl.BlockSpec((tm, LANES), lambda i: (i, 0)),
                  wspec, wspec,
                  pl.BlockSpec((1, d), lambda i: (0, 0)),
                  pl.BlockSpec((tm, d), lambda i: (i, 0))],
        out_specs=pl.BlockSpec((tm, d), lambda i: (i, 0)),
        out_shape=jax.ShapeDtypeStruct((n, d), F32),
        compiler_params=_cparams("parallel"),
        name="even_out",
    )(o_fox, o_cmp, o_slc, o_win, small, wf, wn, g.reshape(1, d), x)


def _ffn_kernel(x_ref, gpre_ref, wg_ref, wu_ref, wd_ref, gpost_ref, o_ref, h_sc, acc_sc):
    f = pl.program_id(1)

    @pl.when(f == 0)
    def _():
        h_sc[...] = _rms(x_ref[...], gpre_ref[...]).astype(BF16)
        acc_sc[...] = jnp.zeros(acc_sc.shape, F32)

    h = h_sc[...]
    a = jnp.dot(h, wg_ref[...], preferred_element_type=F32)
    u = jnp.dot(h, wu_ref[...], preferred_element_type=F32)
    act = (jax.nn.silu(a) * u).astype(BF16)
    acc_sc[...] += jnp.dot(act, wd_ref[...], preferred_element_type=F32)

    @pl.when(f == pl.num_programs(1) - 1)
    def _():
        o_ref[...] = x_ref[...] + _rms(acc_sc[...], gpost_ref[...])


def ffn(x, gpre, wg, wu, wd, gpost, tm=1024, tf=1408):
    n, d = x.shape
    dff = wg.shape[1]
    assert n % tm == 0 and dff % tf == 0
    return pl.pallas_call(
        _ffn_kernel,
        grid=(n // tm, dff // tf),
        in_specs=[pl.BlockSpec((tm, d), lambda i, f: (i, 0)),
                  pl.BlockSpec((1, d), lambda i, f: (0, 0)),
                  pl.BlockSpec((d, tf), lambda i, f: (0, f)),
                  pl.BlockSpec((d, tf), lambda i, f: (0, f)),
                  pl.BlockSpec((tf, d), lambda i, f: (f, 0)),
                  pl.BlockSpec((1, d), lambda i, f: (0, 0))],
        out_specs=pl.BlockSpec((tm, d), lambda i, f: (i, 0)),
        out_shape=jax.ShapeDtypeStruct((n, d), F32),
        scratch_shapes=[pltpu.VMEM((tm, d), BF16), pltpu.VMEM((tm, d), F32)],
        compiler_params=_cparams("parallel", "arbitrary"),
        name="ffn",
    )(x, gpre.reshape(1, d), wg, wu, wd, gpost.reshape(1, d))


def _rope_tables(s):
    inv = ROPE_THETA ** (-jnp.arange(0, HEAD_DIM, 2, dtype=F32) / HEAD_DIM)
    ang = jnp.arange(s, dtype=F32)[:, None] * inv[None, :]
    cos, sin = jnp.cos(ang), jnp.sin(ang)
    reps = LANES // HEAD_DIM
    cos2 = jnp.tile(jnp.concatenate([cos, cos], -1), (1, reps))
    sin2 = jnp.tile(jnp.concatenate([-sin, sin], -1), (1, reps))
    return (cos2, sin2), (cos2.T[:HEAD_DIM], sin2.T[:HEAD_DIM])


def _slab_cols(w, scale=1.0):
    d, c = w.shape
    w3 = (w * scale).reshape(d, c // HEAD_DIM, HEAD_DIM)
    return jnp.concatenate([w3, jnp.zeros_like(w3)], axis=-1).reshape(d, 2 * c)


def _slab_rows(w):
    return _slab_cols(w.T).T


def _ones_lane(n_slabs):
    one = np.zeros((n_slabs, LANES), np.float32)
    one[:, HEAD_DIM] = 1.0
    return jnp.asarray(one.reshape(-1))


def _block_indicator_rows(s, block):
    blk = (np.arange(s) // block) % HEAD_DIM
    ind = np.zeros((LANES, s), np.float32)
    ind[HEAD_DIM + blk, np.arange(s)] = 1.0
    return jnp.asarray(ind, BF16)[None, None]


def _overlap_matrix(ncp, ns):
    ratio = SLC_BLOCK // CMP_STRIDE
    m = np.arange(ncp)[:, None]
    j = np.arange(ns)[None, :]
    ok = (m >= ratio * j - 1) & (m <= ratio * j + ratio - 1) & (m < ncp - 1)
    return jnp.asarray(ok, BF16)


def _compress_weights(pe, w1, w2, g):
    hid = w1.shape[-1]
    w1r = w1.reshape(2, CMP_STRIDE, HEAD_DIM, hid)
    w1g = jnp.zeros((g, 2, CMP_STRIDE, g, LANES, hid), F32)
    for gi in range(g):
        w1g = w1g.at[gi, :, :, gi, :HEAD_DIM].set(w1r)
    w1g = w1g.reshape(g, 2, CMP_STRIDE * g * LANES, hid).astype(BF16)
    pe2 = jnp.zeros((2, CMP_STRIDE, g, LANES), F32).at[..., :HEAD_DIM].set(
        pe.reshape(2, CMP_STRIDE, 1, HEAD_DIM))
    w2p = jnp.concatenate([w2, jnp.zeros_like(w2)], axis=-1).astype(BF16)
    return pe2.reshape(2, CMP_STRIDE * g * LANES), w1g, w2p


def _out_rows(w):
    return _slab_rows(w).astype(BF16)


def _even_layer(x2, b, s, g_pre, g_post, w_in, b_f, pe_k, w1_k, w2_k, pe_v, w1_v, w2_v, w_out, tabs, tabs_t):
    hf, hn, g = H_FOX, H_NSA, NSA_GROUPS
    hd = HEAD_DIM
    rep = hn // g
    (w_fq, w_fk, w_fv, w_fl, w_nq, w_kc, w_vc, w_ks, w_vs, w_kw, w_vw, w_gl) = jnp.split(
        w_in, list(np.cumsum([hf * hd] * 3 + [hf] + [hn * hd] + [g * hd] * 6)), axis=1)
    scale = hd ** -0.5

    w_plain = jnp.concatenate([w_fq * scale, w_fv, w_vs, w_vw, w_vc], axis=1).astype(BF16)
    bias_plain = jnp.concatenate([jnp.zeros((hf * LANES,), F32), _ones_lane(hf + 2 * g),
                                  jnp.zeros((g * LANES,), F32)])
    p_plain = norm_proj(x2, g_pre, w_plain, bias_plain, None, s, BF16).reshape(b, s, -1)
    fq0, fv0, vs0, vw0 = 0, hf, 2 * hf, 2 * hf + g
    vc_col = (2 * hf + 2 * g) * LANES
    w_rope = jnp.concatenate([w_nq * scale, w_kc], axis=1).astype(BF16)
    p_rope = norm_proj(x2, g_pre, w_rope, jnp.zeros((2 * w_rope.shape[1],), F32), tabs, s, BF16
                       ).reshape(b, s, -1)
    n_small = hf + 3 * hn
    w_small = jnp.pad(jnp.concatenate([w_fl, w_gl], axis=1), ((0, 0), (0, LANES - n_small))).astype(BF16)
    p_small = norm_proj(x2, g_pre, w_small, jnp.zeros((LANES,), F32), None, s, F32, slabs=False)
    kt_fox = norm_proj_t(x2, g_pre, w_fk.T.astype(BF16), None, b, s)
    kt_nsa = norm_proj_t(x2, g_pre, jnp.concatenate([w_ks, w_kw], axis=1).T.astype(BF16),
                         tabs_t, b, s)

    qx, kx = fox_decay(p_small.reshape(b, s, LANES), jnp.pad(b_f, (0, LANES - hf)).reshape(1, LANES), hf)
    o_fox = flash("flash_fox", p_plain, fq0, kt_fox, 0, p_plain, fv0, hf, 1, extra=qx, kx=kx, decay=True)

    ncp = s // CMP_STRIDE
    ns = s // SLC_BLOCK
    kc = p_rope[:, :, hn * LANES:].reshape(b, ncp, CMP_STRIDE * g * LANES)
    vc = p_plain[:, :, vc_col:].reshape(b, ncp, CMP_STRIDE * g * LANES)
    kcmp = nsa_compress(kc, *_compress_weights(pe_k, w1_k, w2_k, g))
    vcmp = nsa_compress(vc, *_compress_weights(pe_v, w1_v, w2_v, g))
    o_cmp, selb = nsa_select(p_rope, kcmp.transpose(0, 1, 3, 2), vcmp, _overlap_matrix(ncp, ns), rep)
    o_slc = flash("flash_sel", p_rope, 0, kt_nsa, 0, p_plain, vs0, hn, rep, extra=selb,
                  kx=_block_indicator_rows(s, SLC_BLOCK))
    o_win = flash("flash_band", p_rope, 0, kt_nsa, g, p_plain, vw0, hn, rep, band=True)

    n = b * s
    return even_out(o_fox.reshape(n, -1), o_cmp.reshape(n, -1), o_slc.reshape(n, -1),
                    o_win.reshape(n, -1), p_small, _out_rows(w_out[:hf * hd]), _out_rows(w_out[hf * hd:]),
                    g_post, x2, hn, hf)


def _odd_layer(x2, b, s, g_pre, g_post, w_in, w_out, tabs, tabs_t):
    h = H_MOBA
    hd = HEAD_DIM
    d = h * hd
    scale = hd ** -0.5
    w_q, w_k, w_v = w_in[:, :d], w_in[:, d:2 * d], w_in[:, 2 * d:]
    q_all = norm_proj(x2, g_pre, (w_q * scale).astype(BF16), jnp.zeros((h * LANES,), F32),
                      tabs, s, BF16).reshape(b, s, -1)
    v_all = norm_proj(x2, g_pre, w_v.astype(BF16), _ones_lane(h), None, s, BF16).reshape(b, s, -1)
    kt_all = norm_proj_t(x2, g_pre, w_k.T.astype(BF16), tabs_t, b, s)
    ind = _block_indicator_rows(s, MOBA_BLOCK)
    selb = moba_select(q_all, kt_all, ind[0, 0].T, h)
    o = flash("flash_moba", q_all, 0, kt_all, 0, v_all, 0, h, 1, extra=selb, kx=ind)
    return proj_norm_res(o.reshape(b * s, -1), _out_rows(w_out), g_post, x2)


def kernel(x, ev_w_in, ev_b_f, ev_cmp_pe_k, ev_cmp_w1_k, ev_cmp_w2_k, ev_cmp_pe_v, ev_cmp_w1_v,
           ev_cmp_w2_v, ev_w_out, od_w_in, od_w_out, g_mix_pre, g_mix_post, g_ffn_pre, g_ffn_post,
           ffn_w_gate, ffn_w_up, ffn_w_down):
    b, s, d = x.shape
    depth = g_mix_pre.shape[0]
    tabs, tabs_t = _rope_tables(s)
    x2 = x.reshape(b * s, d)
    for layer in range(depth):
        if layer % 2 == 0:
            e = layer // 2
            x2 = _even_layer(x2, b, s, g_mix_pre[layer], g_mix_post[layer], ev_w_in[e], ev_b_f[e],
                             ev_cmp_pe_k[e], ev_cmp_w1_k[e], ev_cmp_w2_k[e], ev_cmp_pe_v[e],
                             ev_cmp_w1_v[e], ev_cmp_w2_v[e], ev_w_out[e], tabs, tabs_t)
        else:
            o = layer // 2
            x2 = _odd_layer(x2, b, s, g_mix_pre[layer], g_mix_post[layer], od_w_in[o], od_w_out[o],
                            tabs, tabs_t)
        x2 = ffn(x2, g_ffn_pre[layer], ffn_w_gate[layer].astype(BF16), ffn_w_up[layer].astype(BF16),
                 ffn_w_down[layer].astype(BF16), g_ffn_post[layer])
    return x2.reshape(b, s, d)
```

```python
import functools

import jax
import jax.numpy as jnp
import numpy as np
from jax import lax
from jax.experimental import pallas as pl
from jax.experimental.pallas import tpu as pltpu

F32 = jnp.float32
BF16 = jnp.bfloat16

HEAD_DIM = 64
LANES = 128
ROPE_THETA = 10000.0
RMS_EPS = 1e-6
CMP_STRIDE = 16
CMP_LEN = 32
SLC_BLOCK = 64
SLC_TOPN = 16
WINDOW = 512
MOBA_BLOCK = 256
MOBA_TOPK = 3
H_FOX, H_NSA, NSA_GROUPS, H_MOBA = 8, 8, 2, 16
SLC_SHIFT = SLC_BLOCK.bit_length() - 1
MOBA_SHIFT = MOBA_BLOCK.bit_length() - 1
MASKED = -1e30
M_INIT = -1e29
DECAY_CUT = 100.0
FLASH_TQ = 1024
FLASH_TK = 512
FLASH_STRIP = 512
FLASH_STRIP_MAIN = 1024
FLASH_GROUP = 2
FLASH_AHEAD = 2
VMEM_LIMIT = 48 * 1024 * 1024
NT_DIMS = (((1,), (1,)), ((), ()))


def _cparams(*sem):
    return pltpu.CompilerParams(dimension_semantics=sem, vmem_limit_bytes=VMEM_LIMIT)


def _rms(x, g):
    return x * lax.rsqrt(jnp.mean(x * x, axis=-1, keepdims=True) + RMS_EPS) * g


def _split3(x):
    hi = x.astype(BF16).astype(F32)
    r = x - hi
    mid = r.astype(BF16).astype(F32)
    lo = (r - mid).astype(BF16).astype(F32)
    return hi, mid, lo


def _slab_project(h, w_ref, b_ref, cos_ref, sin_ref):
    t = jnp.dot(h, w_ref[...], preferred_element_type=F32)
    tm, tn = t.shape
    if cos_ref is not None:
        reps = tn // LANES
        cos = jnp.tile(cos_ref[...], (1, reps))
        sin = jnp.tile(sin_ref[...], (1, reps))
        lane = lax.broadcasted_iota(jnp.int32, t.shape, 1)
        first = (lane & (HEAD_DIM - 1)) < HEAD_DIM // 2
        rot = jnp.where(first, pltpu.roll(t, tn - HEAD_DIM // 2, 1), pltpu.roll(t, HEAD_DIM // 2, 1))
        t = t * cos + rot * sin
    low = lax.broadcasted_iota(jnp.int32, (tm, LANES), 1) < HEAD_DIM
    parts = []
    for pair in range(tn // LANES):
        v = t[:, pair * LANES:(pair + 1) * LANES]
        parts.append(jnp.where(low, v, 0.0))
        parts.append(jnp.where(low, pltpu.roll(v, HEAD_DIM, 1), 0.0))
    return jnp.concatenate(parts, axis=1) + b_ref[...]


def _key_project(h, wt_ref, cos_ref, sin_ref, o_ref):
    t = lax.dot_general(wt_ref[...], h, NT_DIMS, preferred_element_type=F32)
    tn, tm = t.shape
    nheads = tn // HEAD_DIM
    if cos_ref is not None:
        cos = jnp.tile(cos_ref[...], (nheads, 1))
        sin = jnp.tile(sin_ref[...], (nheads, 1))
        r = lax.broadcasted_iota(jnp.int32, t.shape, 0)
        first = (r & (HEAD_DIM - 1)) < HEAD_DIM // 2
        rot = jnp.where(first, pltpu.roll(t, tn - HEAD_DIM // 2, 0), pltpu.roll(t, HEAD_DIM // 2, 0))
        t = t * cos + rot * sin
    spare = jnp.zeros((LANES - HEAD_DIM, tm), o_ref.dtype)
    for hh in range(nheads):
        o_ref[0, hh * LANES:hh * LANES + HEAD_DIM, :] = t[hh * HEAD_DIM:(hh + 1) * HEAD_DIM].astype(o_ref.dtype)
        o_ref[0, hh * LANES + HEAD_DIM:(hh + 1) * LANES, :] = spare


def _project_kernel(x_ref, g_ref, cos_ref, sin_ref, cos_t_ref, sin_t_ref, *refs, kinds):
    n_out = len(kinds)
    ins, outs = list(refs[:-n_out]), refs[-n_out:]
    h = _rms(x_ref[...], g_ref[...]).astype(BF16)
    for (kind, rope), o_ref in zip(kinds, outs):
        if kind == "slab":
            w_ref, b_ref = ins.pop(0), ins.pop(0)
            tabs = (cos_ref, sin_ref) if rope else (None, None)
            o_ref[...] = _slab_project(h, w_ref, b_ref, *tabs).astype(o_ref.dtype)
        elif kind == "plain":
            o_ref[...] = jnp.dot(h, ins.pop(0)[...], preferred_element_type=F32).astype(o_ref.dtype)
        else:
            tabs = (cos_t_ref, sin_t_ref) if rope else (None, None)
            _key_project(h, ins.pop(0), *tabs, o_ref)


def project(x, g, tabs, tabs_t, b, seq, outputs, tm=512):
    n, d = x.shape
    assert n % tm == 0 and seq % tm == 0
    spt = seq // tm
    in_specs = [pl.BlockSpec((tm, d), lambda i: (i, 0)),
                pl.BlockSpec((1, d), lambda i: (0, 0)),
                pl.BlockSpec((tm, LANES), lambda i: (i % spt, 0)),
                pl.BlockSpec((tm, LANES), lambda i: (i % spt, 0)),
                pl.BlockSpec((HEAD_DIM, tm), lambda i: (0, i % spt)),
                pl.BlockSpec((HEAD_DIM, tm), lambda i: (0, i % spt))]
    args = [x, g.reshape(1, d), *tabs, *tabs_t]
    kinds, out_specs, out_shapes = [], [], []
    for out in outputs:
        w = out[1]
        in_specs.append(pl.BlockSpec(w.shape, lambda i: (0, 0)))
        args.append(w)
        if out[0] == "slab":
            nc = 2 * w.shape[1]
            in_specs.append(pl.BlockSpec((1, nc), lambda i: (0, 0)))
            args.append(out[2].reshape(1, nc))
            kinds.append(("slab", out[3]))
            out_specs.append(pl.BlockSpec((tm, nc), lambda i: (i, 0)))
            out_shapes.append(jax.ShapeDtypeStruct((n, nc), BF16))
        elif out[0] == "plain":
            kinds.append(("plain", False))
            out_specs.append(pl.BlockSpec((tm, w.shape[1]), lambda i: (i, 0)))
            out_shapes.append(jax.ShapeDtypeStruct((n, w.shape[1]), out[2]))
        else:
            rows = 2 * w.shape[0]
            kinds.append(("keys", out[2]))
            out_specs.append(pl.BlockSpec((1, rows, tm), lambda i: (i // spt, 0, i % spt)))
            out_shapes.append(jax.ShapeDtypeStruct((b, rows, seq), BF16))
    return pl.pallas_call(
        functools.partial(_project_kernel, kinds=tuple(kinds)),
        grid=(n // tm,),
        in_specs=in_specs,
        out_specs=out_specs,
        out_shape=out_shapes,
        compiler_params=_cparams("parallel"),
        name="project",
    )(*args)


def _fox_decay_kernel(fl_ref, b_ref, pq_ref, pk_ref, oq_ref, ok_ref, carry_sc, *, nh):
    @pl.when(pl.program_id(1) == 0)
    def _():
        carry_sc[...] = jnp.zeros(carry_sc.shape, F32)

    c = jax.nn.log_sigmoid(fl_ref[0] + b_ref[...])
    ts = c.shape[0]
    row = lax.broadcasted_iota(jnp.int32, c.shape, 0)
    lane = lax.broadcasted_iota(jnp.int32, c.shape, 1)
    sh = 1
    while sh < ts:
        c = c + jnp.where(row >= sh, pltpu.roll(c, sh, 0), 0.0)
        sh *= 2
    c = c + carry_sc[0:1, :]
    carry_sc[0:1, :] = c[ts - 1:ts, :]
    hi, mid, lo = _split3(c)
    c3 = jnp.where(lane < nh, hi,
                   jnp.where(lane < 2 * nh, pltpu.roll(mid, nh, 1),
                             jnp.where(lane < 3 * nh, pltpu.roll(lo, 2 * nh, 1),
                                       jnp.where(lane == 3 * nh, 1.0, 0.0)))).astype(BF16)
    for h in range(nh):
        oq_ref[0, h, 0] = jnp.dot(c3, pq_ref[h], preferred_element_type=F32).astype(oq_ref.dtype)
        ok_ref[0, h] = lax.dot_general(pk_ref[h], c3, NT_DIMS,
                                       preferred_element_type=F32).astype(ok_ref.dtype)


def _fox_placement(nh):
    pq = np.zeros((nh, LANES, LANES), np.float32)
    pk = np.zeros((nh, LANES, LANES), np.float32)
    one = 3 * nh
    for h in range(nh):
        for t in range(3):
            pq[h, one, HEAD_DIM + t] = 1.0
            pq[h, t * nh + h, HEAD_DIM + 3 + t] = 1.0
            pk[h, HEAD_DIM + t, t * nh + h] = -1.0
            pk[h, HEAD_DIM + 3 + t, one] = 1.0
    return jnp.asarray(pq, BF16), jnp.asarray(pk, BF16)


def fox_decay(small, b_row, nh, ts=2048):
    b, s, _ = small.shape
    ts = min(ts, s)
    pq, pk = _fox_placement(nh)
    return pl.pallas_call(
        functools.partial(_fox_decay_kernel, nh=nh),
        grid=(b, s // ts),
        in_specs=[pl.BlockSpec((1, ts, LANES), lambda i, j: (i, j, 0)),
                  pl.BlockSpec((1, LANES), lambda i, j: (0, 0)),
                  pl.BlockSpec((nh, LANES, LANES), lambda i, j: (0, 0, 0)),
                  pl.BlockSpec((nh, LANES, LANES), lambda i, j: (0, 0, 0))],
        out_specs=[pl.BlockSpec((1, nh, 1, ts, LANES), lambda i, j: (i, 0, 0, j, 0)),
                   pl.BlockSpec((1, nh, LANES, ts), lambda i, j: (i, 0, 0, j))],
        out_shape=[jax.ShapeDtypeStruct((b, nh, 1, s, LANES), BF16),
                   jax.ShapeDtypeStruct((b, nh, LANES, s), BF16)],
        scratch_shapes=[pltpu.VMEM((8, LANES), F32)],
        compiler_params=_cparams("parallel", "arbitrary"),
        name="fox_decay",
    )(small, b_row, pq, pk)


def _nsa_compress_kernel(t_ref, pe_ref, w1_ref, w2_ref, o_ref):
    t = t_ref[0].astype(F32)
    ncp = t.shape[0]
    a = jnp.dot((t + pe_ref[0:1, :]).astype(BF16), w1_ref[0, 0], preferred_element_type=F32)
    bm = jnp.dot((t + pe_ref[1:2, :]).astype(BF16), w1_ref[0, 1], preferred_element_type=F32)
    pre = a + pltpu.roll(bm, ncp - 1, 0)
    hid = jax.nn.gelu(pre)
    o_ref[0, 0] = jnp.dot(hid.astype(BF16), w2_ref[...], preferred_element_type=F32).astype(o_ref.dtype)


def nsa_compress(t, pe, w1, w2):
    b, ncp, cw = t.shape
    g, _, _, hid = w1.shape
    return pl.pallas_call(
        _nsa_compress_kernel,
        grid=(b, g),
        in_specs=[pl.BlockSpec((1, ncp, cw), lambda i, j: (i, 0, 0)),
                  pl.BlockSpec((2, cw), lambda i, j: (0, 0)),
                  pl.BlockSpec((1, 2, cw, hid), lambda i, j: (j, 0, 0, 0)),
                  pl.BlockSpec((hid, LANES), lambda i, j: (0, 0))],
        out_specs=pl.BlockSpec((1, 1, ncp, LANES), lambda i, j: (i, j, 0, 0)),
        out_shape=jax.ShapeDtypeStruct((b, g, ncp, LANES), BF16),
        compiler_params=_cparams("parallel", "parallel"),
        name="nsa_compress",
    )(t, pe, w1, w2)


def _topk_mask(work, col, k):
    sel = jnp.zeros(work.shape, jnp.bool_)
    col = col.astype(F32)
    for _ in range(k):
        mx = jnp.max(work, axis=-1, keepdims=True)
        first = jnp.min(jnp.where(work == mx, col, jnp.inf), axis=-1, keepdims=True)
        hit = col == first
        sel = jnp.logical_or(sel, hit)
        work = jnp.where(hit, -jnp.inf, work)
    return sel


def _nsa_select_kernel(q_ref, kct_ref, vc_ref, m_ref, oc_ref, sb_ref, *, tq, rep, nsup, sps):
    i = pl.program_id(2)
    ncp = kct_ref.shape[-1]
    ns = m_ref.shape[-1]
    qpos = i * tq + lax.broadcasted_iota(jnp.int32, (tq, 1), 0)
    has_block = jnp.where(qpos >= CMP_LEN - 1, 1.0, 0.0)
    qblk = qpos >> SLC_SHIFT
    zeros = jnp.zeros((tq, HEAD_DIM), sb_ref.dtype)
    hidden = jnp.concatenate([zeros, jnp.full((tq, HEAD_DIM), MASKED, sb_ref.dtype)], axis=-1)

    def prefix(nv):
        wc, ws = ncp * nv // nsup, ns * nv // nsup
        cend = lax.broadcasted_iota(jnp.int32, (1, wc), 1) * CMP_STRIDE + (CMP_LEN - 1)
        cmask = cend <= qpos
        kct = kct_ref[0, 0, :, :wc]
        vc = vc_ref[0, 0, :wc, :]
        pcs = jnp.zeros((tq, wc), F32)
        for r in range(rep):
            s = jnp.dot(q_ref[0, :, r * LANES:(r + 1) * LANES], kct, preferred_element_type=F32)
            s = jnp.where(cmask, s, MASKED)
            e = jnp.exp(s - jnp.max(s, axis=-1, keepdims=True))
            p = e * (has_block / jnp.maximum(jnp.sum(e, axis=-1, keepdims=True), 1e-30))
            oc_ref[0, :, r * LANES:(r + 1) * LANES] = jnp.dot(
                p.astype(BF16), vc, preferred_element_type=F32).astype(oc_ref.dtype)
            pcs = pcs + p
        mm = m_ref[:wc, :ws]
        imp = sum(jnp.dot(part.astype(BF16), mm, preferred_element_type=F32) for part in _split3(pcs))
        sblk = lax.broadcasted_iota(jnp.int32, (1, ws), 1)
        forced = (sblk == 0) | (sblk == qblk) | (sblk == qblk - 1)
        free = jnp.where(forced | (sblk > qblk), -jnp.inf, imp)
        sel = _topk_mask(free, sblk, min(SLC_TOPN, ns) - 3)
        bias = jnp.where((sel | forced) & (sblk <= qblk), 0.0, MASKED).astype(sb_ref.dtype)
        for j in range(nsup):
            sb_ref[0, 0, j] = hidden if j >= nv else jnp.concatenate(
                [zeros, bias[:, j * HEAD_DIM:(j + 1) * HEAD_DIM]], axis=-1)

    for nv in range(1, nsup + 1):
        pl.when(i // sps == nv - 1)(functools.partial(prefix, nv))


def nsa_select(q_all, kct, vc, m, rep, tq=512):
    b, s, _ = q_all.shape
    g = kct.shape[1]
    ncp = kct.shape[-1]
    ns = m.shape[-1]
    nsup = ns // HEAD_DIM
    tq = min(tq, s)
    assert SLC_BLOCK == 64 and ns % HEAD_DIM == 0 and (s // tq) % nsup == 0
    return pl.pallas_call(
        functools.partial(_nsa_select_kernel, tq=tq, rep=rep, nsup=nsup, sps=(s // tq) // nsup),
        grid=(b, g, s // tq),
        in_specs=[pl.BlockSpec((1, tq, rep * LANES), lambda bi, gi, i: (bi, i, gi)),
                  pl.BlockSpec((1, 1, LANES, ncp), lambda bi, gi, i: (bi, gi, 0, 0)),
                  pl.BlockSpec((1, 1, ncp, LANES), lambda bi, gi, i: (bi, gi, 0, 0)),
                  pl.BlockSpec((ncp, ns), lambda bi, gi, i: (0, 0))],
        out_specs=[pl.BlockSpec((1, tq, rep * LANES), lambda bi, gi, i: (bi, i, gi)),
                   pl.BlockSpec((1, 1, nsup, tq, LANES), lambda bi, gi, i: (bi, gi, 0, i, 0))],
        out_shape=[jax.ShapeDtypeStruct((b, s, g * rep * LANES), BF16),
                   jax.ShapeDtypeStruct((b, g, nsup, s, LANES), BF16)],
        compiler_params=_cparams("parallel", "parallel", "parallel"),
        name="nsa_select",
    )(q_all, kct, vc, m)


def _moba_select_kernel(q_ref, kt_ref, ind_ref, sb_ref, kbar_sc, *, tq):
    i = pl.program_id(2)

    @pl.when(i == 0)
    def _():
        kbar_sc[...] = jnp.dot(kt_ref[0], ind_ref[...], preferred_element_type=F32) * (1.0 / MOBA_BLOCK)

    q = q_ref[0]
    gate = sum(jnp.dot(q, part.astype(BF16), preferred_element_type=F32)
               for part in _split3(kbar_sc[...]))
    qpos = i * tq + lax.broadcasted_iota(jnp.int32, (tq, 1), 0)
    cur = qpos >> MOBA_SHIFT
    lane = lax.broadcasted_iota(jnp.int32, (1, LANES), 1)
    blk = lane - HEAD_DIM
    past = (blk >= 0) & (blk < cur)
    sel = _topk_mask(jnp.where(past, gate, -jnp.inf), lane, MOBA_TOPK)
    keep = (blk < 0) | (sel & past) | (blk == cur)
    sb_ref[0, 0, 0] = jnp.where(keep, 0.0, MASKED).astype(sb_ref.dtype)


def moba_select(q_all, kt_all, ind, nh, tq=2048):
    b, s, _ = q_all.shape
    assert MOBA_BLOCK == 256 and s // MOBA_BLOCK <= HEAD_DIM
    tq = min(tq, s)
    return pl.pallas_call(
        functools.partial(_moba_select_kernel, tq=tq),
        grid=(b, nh, s // tq),
        in_specs=[pl.BlockSpec((1, tq, LANES), lambda bi, hi, i: (bi, i, hi)),
                  pl.BlockSpec((1, LANES, s), lambda bi, hi, i: (bi, hi, 0)),
                  pl.BlockSpec((s, LANES), lambda bi, hi, i: (0, 0))],
        out_specs=pl.BlockSpec((1, 1, 1, tq, LANES), lambda bi, hi, i: (bi, hi, 0, i, 0)),
        out_shape=jax.ShapeDtypeStruct((b, nh, 1, s, LANES), BF16),
        scratch_shapes=[pltpu.VMEM((LANES, LANES), F32)],
        compiler_params=_cparams("parallel", "parallel", "arbitrary"),
        name="moba_select",
    )(q_all, kt_all, ind)


def _flash_kernel(*refs, tq, tk, rs, rs_main, tps, band, decay, has_extra, has_kx):
    refs = list(refs)
    qa_ref = refs.pop(0)
    ex_ref = refs.pop(0) if has_extra else None
    kt_ref = refs.pop(0)
    kx_ref = refs.pop(0) if has_kx else None
    kmax_sc = refs.pop() if decay else None
    v_ref, o_ref, m_sc, acc_sc = refs
    i = pl.program_id(2)
    m_sc[...] = jnp.full(m_sc.shape, M_INIT, F32)
    acc_sc[...] = jnp.zeros(acc_sc.shape, F32)
    row = lax.broadcasted_iota(jnp.int32, (rs, tk), 0)
    col = lax.broadcasted_iota(jnp.int32, (rs, tk), 1)
    nst = tq // rs
    kpq = tq // tk

    def run(items, rs=rs):
        tiles = {}

        def operands(j):
            if id(j) not in tiles:
                start = pl.multiple_of(j * tk, tk)
                kt = kt_ref[0, :, pl.ds(start, tk)]
                if kx_ref is not None:
                    kt = kt + kx_ref[0, 0, :, pl.ds(start, tk)]
                tiles[id(j)] = (kt, v_ref[0, pl.ds(start, tk), :])
            return tiles[id(j)]

        def logits(item):
            j, r, _ = item
            rows = pl.ds(r * rs, rs)
            qa = qa_ref[0, rows, :]
            if ex_ref is not None:
                qa = qa + ex_ref[0, 0, j // tps, rows, :]
            return jnp.dot(qa, operands(j)[0], preferred_element_type=F32)

        def accumulate(rows, alpha, p, vv):
            acc_sc[rows, :] = alpha * acc_sc[rows, :] + jnp.dot(p, vv, preferred_element_type=F32)

        pending = [logits(it) for it in items[:FLASH_AHEAD]]
        held = None
        for n, (j, r, mask) in enumerate(items):
            rows = pl.ds(r * rs, rs)
            s = pending.pop(0)
            if n + FLASH_AHEAD < len(items):
                pending.append(logits(items[n + FLASH_AHEAD]))
            if held is not None:
                accumulate(*held)
            if mask is not None:
                s = jnp.where(mask, s, MASKED)
            m_prev = m_sc[rows, :]
            m_new = jnp.maximum(m_prev, jnp.max(s, axis=-1, keepdims=True))
            p = jnp.exp(s - jnp.tile(m_new, (1, tk // LANES)))
            m_sc[rows, :] = m_new
            held = (rows, jnp.exp(m_prev - m_new), p.astype(BF16), operands(j)[1])
        accumulate(*held)

    def edge_items(dj_list, j_of):
        items = []
        for dj in dj_list:
            j = j_of(dj)
            for r in range(nst):
                off, ko = r * rs, dj * tk
                lo = off - WINDOW + 1 if band else None
                if ko > off + rs - 1 or (band and ko + tk - 1 < lo):
                    continue
                full = ko + tk - 1 <= off and (not band or ko > off + rs - 1 - WINDOW)
                mask = None
                if not full:
                    mask = col + ko <= row + off
                    if band:
                        mask = mask & (col + ko > row + (off - WINDOW))
                items.append((j, r, mask))
        return items

    def body(jjs):
        tiles = [jj * kpq + dj for jj in jjs for dj in range(kpq)]
        run([(j, r, None) for j in tiles for r in range(tq // rs_main)], rs_main)

    def sweep(n, jj_of):
        def group(t, carry):
            body([jj_of(FLASH_GROUP * t + u) for u in range(FLASH_GROUP)])
            return carry
        lax.fori_loop(0, n // FLASH_GROUP, group, 0)

        def single(t, carry):
            body([jj_of(n - n % FLASH_GROUP + t)])
            return carry
        lax.fori_loop(0, n % FLASH_GROUP, single, 0)

    def run_diag():
        diag = [i * kpq + dj for dj in range(kpq)]
        run(edge_items(list(range(kpq)), lambda dj: diag[dj]))

    if decay:
        s_len = kt_ref.shape[-1]

        @pl.when(i == 0)
        def _():
            def chunk(c, best):
                kk = kt_ref[0, :, pl.ds(pl.multiple_of(c * tq, tq), tq)].astype(F32)
                return jnp.maximum(best, jnp.sum(kk * kk, axis=0, keepdims=True))
            ksq = lax.fori_loop(0, s_len // tq, chunk, jnp.zeros((1, tq), F32))
            kmax_sc[...] = jnp.broadcast_to(jnp.sqrt(jnp.max(ksq, axis=1, keepdims=True)), kmax_sc.shape)

        run_diag()
        q = qa_ref[0].astype(F32)
        qn = jnp.sqrt(jnp.sum(q * q, axis=1, keepdims=True))
        slack = jnp.max(qn * kmax_sc[0:1, 0:1] - m_sc[:, 0:1], axis=0, keepdims=True)
        cvec = -jnp.sum(kx_ref[0, 0, HEAD_DIM:HEAD_DIM + 3, :].astype(F32), axis=0, keepdims=True)
        pos = lax.broadcasted_iota(jnp.int32, (1, s_len), 1)
        q0 = i * tq
        c_q0 = jnp.sum(jnp.where(pos == q0, cvec, 0.0), axis=1, keepdims=True)
        dead = (pos < q0) & (slack + c_q0 - cvec <= -DECAY_CUT)
        n_dead = jnp.sum(jnp.where(dead, 1.0, 0.0)).astype(jnp.int32) // tq
        sweep(i - n_dead, lambda t: i - 1 - t)
    elif band:
        pl.when(i == 0)(run_diag)

        @pl.when(i > 0)
        def _():
            near = {dj: i * kpq + dj for dj in range(-(WINDOW // tk), kpq)}
            run(edge_items(sorted(near), near.get))
    else:
        sweep(i, lambda t: t)
        run_diag()
    acc = acc_sc[...]
    o_ref[0] = (acc / acc[:, HEAD_DIM:HEAD_DIM + 1]).astype(o_ref.dtype)


def flash(name, q_all, q0, kt_all, k0, v_all, v0, nh, rep, extra=None, kx=None, band=False,
          decay=False):
    assert not decay or (kx is not None and not band)
    b, s, _ = q_all.shape
    tk = min(WINDOW if band else FLASH_TK, s)
    tq = min(FLASH_TQ, s)
    rs = min(FLASH_STRIP, tq)
    assert s % tq == 0 and tq % tk == 0 and tq % rs == 0 and (not band or WINDOW % tk == 0)
    in_specs = [pl.BlockSpec((1, tq, LANES), lambda bi, hi, i: (bi, i, q0 + hi))]
    args = [q_all]
    tps = 1
    if extra is not None:
        he, nsup = extra.shape[1], extra.shape[2]
        rep_e = nh // he
        assert (s // nsup) % tk == 0
        tps = (s // nsup) // tk
        in_specs.append(pl.BlockSpec((1, 1, nsup, tq, LANES), lambda bi, hi, i: (bi, hi // rep_e, 0, i, 0)))
        args.append(extra)
    in_specs.append(pl.BlockSpec((1, LANES, s), lambda bi, hi, i: (bi, k0 + hi // rep, 0)))
    args.append(kt_all)
    if kx is not None:
        bx, hx = kx.shape[0], kx.shape[1]
        in_specs.append(pl.BlockSpec(
            (1, 1, LANES, s), lambda bi, hi, i: (bi if bx > 1 else 0, hi if hx > 1 else 0, 0, 0)))
        args.append(kx)
    in_specs.append(pl.BlockSpec((1, s, LANES), lambda bi, hi, i: (bi, 0, v0 + hi // rep)))
    args.append(v_all)
    return pl.pallas_call(
        functools.partial(_flash_kernel, tq=tq, tk=tk, rs=rs, rs_main=min(FLASH_STRIP_MAIN, tq),
                          tps=tps, band=band, decay=decay,
                          has_extra=extra is not None, has_kx=kx is not None),
        grid=(b, nh, s // tq),
        in_specs=in_specs,
        out_specs=pl.BlockSpec((1, tq, LANES), lambda bi, hi, i: (bi, i, hi)),
        out_shape=jax.ShapeDtypeStruct((b, s, nh * LANES), BF16),
        scratch_shapes=[pltpu.VMEM((tq, LANES), F32), pltpu.VMEM((tq, LANES), F32)]
        + ([pltpu.VMEM((8, LANES), F32)] if decay else []),
        compiler_params=_cparams("parallel", "parallel", "arbitrary" if decay else "parallel"),
        name=name,
    )(*args)


def _proj_norm_res_kernel(a_ref, w_ref, g_ref, x_ref, o_ref):
    y = jnp.dot(a_ref[...], w_ref[...], preferred_element_type=F32)
    o_ref[...] = x_ref[...] + _rms(y, g_ref[...])


def proj_norm_res(a, w, g, x, tm=512):
    n, k = a.shape
    d = w.shape[1]
    return pl.pallas_call(
        _proj_norm_res_kernel,
        grid=(n // tm,),
        in_specs=[pl.BlockSpec((tm, k), lambda i: (i, 0)),
                  pl.BlockSpec((k, d), lambda i: (0, 0)),
                  pl.BlockSpec((1, d), lambda i: (0, 0)),
                  pl.BlockSpec((tm, d), lambda i: (i, 0))],
        out_specs=pl.BlockSpec((tm, d), lambda i: (i, 0)),
        out_shape=jax.ShapeDtypeStruct((n, d), F32),
        compiler_params=_cparams("parallel"),
        name="proj_norm_res",
    )(a, w, g.reshape(1, d), x)


def _even_out_kernel(of_ref, oc_ref, os_ref, ow_ref, gl_ref, wf_ref, wn_ref, g_ref, x_ref, o_ref,
                     *, nh, g0):
    gate = jax.nn.sigmoid(gl_ref[...])
    parts = []
    for h in range(nh):
        sl = slice(h * LANES, (h + 1) * LANES)
        c = g0 + 3 * h
        parts.append(gate[:, c:c + 1] * oc_ref[:, sl].astype(F32)
                     + gate[:, c + 1:c + 2] * os_ref[:, sl].astype(F32)
                     + gate[:, c + 2:c + 3] * ow_ref[:, sl].astype(F32))
    a = jnp.concatenate(parts, axis=-1).astype(BF16)
    y = (jnp.dot(of_ref[...], wf_ref[...], preferred_element_type=F32)
         + jnp.dot(a, wn_ref[...], preferred_element_type=F32))
    o_ref[...] = x_ref[...] + _rms(y, g_ref[...])


def even_out(o_fox, o_cmp, o_slc, o_win, small, wf, wn, g, x, nh, g0, tm=512):
    n, k = o_fox.shape
    d = wf.shape[1]
    act = pl.BlockSpec((tm, k), lambda i: (i, 0))
    wspec = pl.BlockSpec((k, d), lambda i: (0, 0))
    return pl.pallas_call(
        functools.partial(_even_out_kernel, nh=nh, g0=g0),
        grid=(n // tm,),
        in_specs=[act, act, act, act,
                  pl.BlockSpec((tm, LANES), lambda i: (i, 0)),
                  wspec, wspec,
                  pl.BlockSpec((1, d), lambda i: (0, 0)),
                  pl.BlockSpec((tm, d), lambda i: (i, 0))],
        out_specs=pl.BlockSpec((tm, d), lambda i: (i, 0)),
        out_shape=jax.ShapeDtypeStruct((n, d), F32),
        compiler_params=_cparams("parallel"),
        name="even_out",
    )(o_fox, o_cmp, o_slc, o_win, small, wf, wn, g.reshape(1, d), x)


def _ffn_kernel(x_ref, gpre_ref, wg_ref, wu_ref, wd_ref, gpost_ref, o_ref, h_sc, acc_sc):
    f = pl.program_id(1)

    @pl.when(f == 0)
    def _():
        h_sc[...] = _rms(x_ref[...], gpre_ref[...]).astype(BF16)
        acc_sc[...] = jnp.zeros(acc_sc.shape, F32)

    h = h_sc[...]
    a = jnp.dot(h, wg_ref[...], preferred_element_type=F32)
    u = jnp.dot(h, wu_ref[...], preferred_element_type=F32)
    act = (jax.nn.silu(a) * u).astype(BF16)
    acc_sc[...] += jnp.dot(act, wd_ref[...], preferred_element_type=F32)

    @pl.when(f == pl.num_programs(1) - 1)
    def _():
        o_ref[...] = x_ref[...] + _rms(acc_sc[...], gpost_ref[...])


def ffn(x, gpre, wg, wu, wd, gpost, tm=1024, tf=1408):
    n, d = x.shape
    dff = wg.shape[1]
    assert n % tm == 0 and dff % tf == 0
    return pl.pallas_call(
        _ffn_kernel,
        grid=(n // tm, dff // tf),
        in_specs=[pl.BlockSpec((tm, d), lambda i, f: (i, 0)),
                  pl.BlockSpec((1, d), lambda i, f: (0, 0)),
                  pl.BlockSpec((d, tf), lambda i, f: (0, f)),
                  pl.BlockSpec((d, tf), lambda i, f: (0, f)),
                  pl.BlockSpec((tf, d), lambda i, f: (f, 0)),
                  pl.BlockSpec((1, d), lambda i, f: (0, 0))],
        out_specs=pl.BlockSpec((tm, d), lambda i, f: (i, 0)),
        out_shape=jax.ShapeDtypeStruct((n, d), F32),
        scratch_shapes=[pltpu.VMEM((tm, d), BF16), pltpu.VMEM((tm, d), F32)],
        compiler_params=_cparams("parallel", "arbitrary"),
        name="ffn",
    )(x, gpre.reshape(1, d), wg, wu, wd, gpost.reshape(1, d))


def _rope_tables(s):
    inv = ROPE_THETA ** (-jnp.arange(0, HEAD_DIM, 2, dtype=F32) / HEAD_DIM)
    ang = jnp.arange(s, dtype=F32)[:, None] * inv[None, :]
    cos, sin = jnp.cos(ang), jnp.sin(ang)
    reps = LANES // HEAD_DIM
    cos2 = jnp.tile(jnp.concatenate([cos, cos], -1), (1, reps))
    sin2 = jnp.tile(jnp.concatenate([-sin, sin], -1), (1, reps))
    return (cos2, sin2), (cos2.T[:HEAD_DIM], sin2.T[:HEAD_DIM])


def _slab_cols(w, scale=1.0):
    d, c = w.shape
    w3 = (w * scale).reshape(d, c // HEAD_DIM, HEAD_DIM)
    return jnp.concatenate([w3, jnp.zeros_like(w3)], axis=-1).reshape(d, 2 * c)


def _slab_rows(w):
    return _slab_cols(w.T).T


def _ones_lane(n_slabs):
    one = np.zeros((n_slabs, LANES), np.float32)
    one[:, HEAD_DIM] = 1.0
    return jnp.asarray(one.reshape(-1))


def _block_indicator_rows(s, block):
    blk = (np.arange(s) // block) % HEAD_DIM
    ind = np.zeros((LANES, s), np.float32)
    ind[HEAD_DIM + blk, np.arange(s)] = 1.0
    return jnp.asarray(ind, BF16)[None, None]


def _overlap_matrix(ncp, ns):
    ratio = SLC_BLOCK // CMP_STRIDE
    m = np.arange(ncp)[:, None]
    j = np.arange(ns)[None, :]
    ok = (m >= ratio * j - 1) & (m <= ratio * j + ratio - 1) & (m < ncp - 1)
    return jnp.asarray(ok, BF16)


def _compress_weights(pe, w1, w2, g):
    hid = w1.shape[-1]
    w1r = w1.reshape(2, CMP_STRIDE, HEAD_DIM, hid)
    w1g = jnp.zeros((g, 2, CMP_STRIDE, g, LANES, hid), F32)
    for gi in range(g):
        w1g = w1g.at[gi, :, :, gi, :HEAD_DIM].set(w1r)
    w1g = w1g.reshape(g, 2, CMP_STRIDE * g * LANES, hid).astype(BF16)
    pe2 = jnp.zeros((2, CMP_STRIDE, g, LANES), F32).at[..., :HEAD_DIM].set(
        pe.reshape(2, CMP_STRIDE, 1, HEAD_DIM))
    w2p = jnp.concatenate([w2, jnp.zeros_like(w2)], axis=-1).astype(BF16)
    return pe2.reshape(2, CMP_STRIDE * g * LANES), w1g, w2p


def _out_rows(w):
    return _slab_rows(w).astype(BF16)


def _even_layer(x2, b, s, g_pre, g_post, w_in, b_f, pe_k, w1_k, w2_k, pe_v, w1_v, w2_v, w_out, tabs, tabs_t):
    hf, hn, g = H_FOX, H_NSA, NSA_GROUPS
    hd = HEAD_DIM
    rep = hn // g
    (w_fq, w_fk, w_fv, w_fl, w_nq, w_kc, w_vc, w_ks, w_vs, w_kw, w_vw, w_gl) = jnp.split(
        w_in, list(np.cumsum([hf * hd] * 3 + [hf] + [hn * hd] + [g * hd] * 6)), axis=1)
    scale = hd ** -0.5

    w_plain = jnp.concatenate([w_fq * scale, w_fv, w_vs, w_vw, w_vc], axis=1).astype(BF16)
    bias_plain = jnp.concatenate([jnp.zeros((hf * LANES,), F32), _ones_lane(hf + 2 * g),
                                  jnp.zeros((g * LANES,), F32)])
    fq0, fv0, vs0, vw0 = 0, hf, 2 * hf, 2 * hf + g
    vc_col = (2 * hf + 2 * g) * LANES
    w_rope = jnp.concatenate([w_nq * scale, w_kc], axis=1).astype(BF16)
    n_small = hf + 3 * hn
    w_small = jnp.pad(jnp.concatenate([w_fl, w_gl], axis=1), ((0, 0), (0, LANES - n_small))).astype(BF16)
    p_plain, p_rope, p_small, kt_fox, kt_nsa = project(x2, g_pre, tabs, tabs_t, b, s, [
        ("slab", w_plain, bias_plain, False),
        ("slab", w_rope, jnp.zeros((2 * w_rope.shape[1],), F32), True),
        ("plain", w_small, F32),
        ("keys", w_fk.T.astype(BF16), False),
        ("keys", jnp.concatenate([w_ks, w_kw], axis=1).T.astype(BF16), True),
    ])
    p_plain = p_plain.reshape(b, s, -1)
    p_rope = p_rope.reshape(b, s, -1)

    qx, kx = fox_decay(p_small.reshape(b, s, LANES), jnp.pad(b_f, (0, LANES - hf)).reshape(1, LANES), hf)
    o_fox = flash("flash_fox", p_plain, fq0, kt_fox, 0, p_plain, fv0, hf, 1, extra=qx, kx=kx, decay=True)

    ncp = s // CMP_STRIDE
    ns = s // SLC_BLOCK
    kc = p_rope[:, :, hn * LANES:].reshape(b, ncp, CMP_STRIDE * g * LANES)
    vc = p_plain[:, :, vc_col:].reshape(b, ncp, CMP_STRIDE * g * LANES)
    kcmp = nsa_compress(kc, *_compress_weights(pe_k, w1_k, w2_k, g))
    vcmp = nsa_compress(vc, *_compress_weights(pe_v, w1_v, w2_v, g))
    o_cmp, selb = nsa_select(p_rope, kcmp.transpose(0, 1, 3, 2), vcmp, _overlap_matrix(ncp, ns), rep)
    o_slc = flash("flash_sel", p_rope, 0, kt_nsa, 0, p_plain, vs0, hn, rep, extra=selb,
                  kx=_block_indicator_rows(s, SLC_BLOCK))
    o_win = flash("flash_band", p_rope, 0, kt_nsa, g, p_plain, vw0, hn, rep, band=True)

    n = b * s
    return even_out(o_fox.reshape(n, -1), o_cmp.reshape(n, -1), o_slc.reshape(n, -1),
                    o_win.reshape(n, -1), p_small, _out_rows(w_out[:hf * hd]), _out_rows(w_out[hf * hd:]),
                    g_post, x2, hn, hf)


def _odd_layer(x2, b, s, g_pre, g_post, w_in, w_out, tabs, tabs_t):
    h = H_MOBA
    hd = HEAD_DIM
    d = h * hd
    scale = hd ** -0.5
    w_q, w_k, w_v = w_in[:, :d], w_in[:, d:2 * d], w_in[:, 2 * d:]
    q_all, v_all, kt_all = project(x2, g_pre, tabs, tabs_t, b, s, [
        ("slab", (w_q * scale).astype(BF16), jnp.zeros((h * LANES,), F32), True),
        ("slab", w_v.astype(BF16), _ones_lane(h), False),
        ("keys", w_k.T.astype(BF16), True),
    ])
    q_all = q_all.reshape(b, s, -1)
    v_all = v_all.reshape(b, s, -1)
    ind = _block_indicator_rows(s, MOBA_BLOCK)
    selb = moba_select(q_all, kt_all, ind[0, 0].T, h)
    o = flash("flash_moba", q_all, 0, kt_all, 0, v_all, 0, h, 1, extra=selb, kx=ind)
    return proj_norm_res(o.reshape(b * s, -1), _out_rows(w_out), g_post, x2)


def kernel(x, ev_w_in, ev_b_f, ev_cmp_pe_k, ev_cmp_w1_k, ev_cmp_w2_k, ev_cmp_pe_v, ev_cmp_w1_v,
           ev_cmp_w2_v, ev_w_out, od_w_in, od_w_out, g_mix_pre, g_mix_post, g_ffn_pre, g_ffn_post,
           ffn_w_gate, ffn_w_up, ffn_w_down):
    b, s, d = x.shape
    depth = g_mix_pre.shape[0]
    tabs, tabs_t = _rope_tables(s)
    x2 = x.reshape(b * s, d)
    for layer in range(depth):
        if layer % 2 == 0:
            e = layer // 2
            x2 = _even_layer(x2, b, s, g_mix_pre[layer], g_mix_post[layer], ev_w_in[e], ev_b_f[e],
                             ev_cmp_pe_k[e], ev_cmp_w1_k[e], ev_cmp_w2_k[e], ev_cmp_pe_v[e],
                             ev_cmp_w1_v[e], ev_cmp_w2_v[e], ev_w_out[e], tabs, tabs_t)
        else:
            o = layer // 2
            x2 = _odd_layer(x2, b, s, g_mix_pre[layer], g_mix_post[layer], od_w_in[o], od_w_out[o],
                            tabs, tabs_t)
        x2 = ffn(x2, g_ffn_pre[layer], ffn_w_gate[layer].astype(BF16), ffn_w_up[layer].astype(BF16),
                 ffn_w_down[layer].astype(BF16), g_ffn_post[layer])
    return x2.reshape(b, s, d)
```

```python
import functools

import jax
import jax.numpy as jnp
import numpy as np
from jax import lax
from jax.experimental import pallas as pl
from jax.experimental.pallas import tpu as pltpu

F32 = jnp.float32
BF16 = jnp.bfloat16

HEAD_DIM = 64
LANES = 128
ROPE_THETA = 10000.0
RMS_EPS = 1e-6
CMP_STRIDE = 16
CMP_LEN = 32
SLC_BLOCK = 64
SLC_TOPN = 16
WINDOW = 512
MOBA_BLOCK = 256
MOBA_TOPK = 3
H_FOX, H_NSA, NSA_GROUPS, H_MOBA = 8, 8, 2, 16
SLC_SHIFT = SLC_BLOCK.bit_length() - 1
MOBA_SHIFT = MOBA_BLOCK.bit_length() - 1
MASKED = -1e30
M_INIT = -1e29
DECAY_CUT = 100.0
FLASH_TQ = 1024
FLASH_TK = 512
FLASH_STRIP = 512
FLASH_STRIP_MAIN = 1024
FLASH_GROUP = 2
FLASH_AHEAD = 2
VMEM_LIMIT = 48 * 1024 * 1024
NT_DIMS = (((1,), (1,)), ((), ()))


def _cparams(*sem):
    return pltpu.CompilerParams(dimension_semantics=sem, vmem_limit_bytes=VMEM_LIMIT)


def _rms(x, g):
    return x * lax.rsqrt(jnp.mean(x * x, axis=-1, keepdims=True) + RMS_EPS) * g


def _split3(x):
    hi = x.astype(BF16).astype(F32)
    r = x - hi
    mid = r.astype(BF16).astype(F32)
    lo = (r - mid).astype(BF16).astype(F32)
    return hi, mid, lo


def _slab_project(h, w_ref, b_ref, cos_ref, sin_ref):
    t = jnp.dot(h, w_ref[...], preferred_element_type=F32)
    tm, tn = t.shape
    if cos_ref is not None:
        reps = tn // LANES
        cos = jnp.tile(cos_ref[...], (1, reps))
        sin = jnp.tile(sin_ref[...], (1, reps))
        lane = lax.broadcasted_iota(jnp.int32, t.shape, 1)
        first = (lane & (HEAD_DIM - 1)) < HEAD_DIM // 2
        rot = jnp.where(first, pltpu.roll(t, tn - HEAD_DIM // 2, 1), pltpu.roll(t, HEAD_DIM // 2, 1))
        t = t * cos + rot * sin
    low = lax.broadcasted_iota(jnp.int32, (tm, LANES), 1) < HEAD_DIM
    parts = []
    for pair in range(tn // LANES):
        v = t[:, pair * LANES:(pair + 1) * LANES]
        parts.append(jnp.where(low, v, 0.0))
        parts.append(jnp.where(low, pltpu.roll(v, HEAD_DIM, 1), 0.0))
    return jnp.concatenate(parts, axis=1) + b_ref[...]


def _key_project(h, wt_ref, cos_ref, sin_ref, o_ref):
    t = lax.dot_general(wt_ref[...], h, NT_DIMS, preferred_element_type=F32)
    tn, tm = t.shape
    nheads = tn // HEAD_DIM
    if cos_ref is not None:
        cos = jnp.tile(cos_ref[...], (nheads, 1))
        sin = jnp.tile(sin_ref[...], (nheads, 1))
        r = lax.broadcasted_iota(jnp.int32, t.shape, 0)
        first = (r & (HEAD_DIM - 1)) < HEAD_DIM // 2
        rot = jnp.where(first, pltpu.roll(t, tn - HEAD_DIM // 2, 0), pltpu.roll(t, HEAD_DIM // 2, 0))
        t = t * cos + rot * sin
    spare = jnp.zeros((LANES - HEAD_DIM, tm), o_ref.dtype)
    for hh in range(nheads):
        o_ref[0, hh * LANES:hh * LANES + HEAD_DIM, :] = t[hh * HEAD_DIM:(hh + 1) * HEAD_DIM].astype(o_ref.dtype)
        o_ref[0, hh * LANES + HEAD_DIM:(hh + 1) * LANES, :] = spare


def _project_kernel(x_ref, g_ref, cos_ref, sin_ref, cos_t_ref, sin_t_ref, *refs, kinds):
    n_out = len(kinds)
    ins, outs = list(refs[:-n_out]), refs[-n_out:]
    h = _rms(x_ref[...], g_ref[...]).astype(BF16)
    for (kind, rope), o_ref in zip(kinds, outs):
        if kind == "slab":
            w_ref, b_ref = ins.pop(0), ins.pop(0)
            tabs = (cos_ref, sin_ref) if rope else (None, None)
            o_ref[...] = _slab_project(h, w_ref, b_ref, *tabs).astype(o_ref.dtype)
        elif kind == "plain":
            o_ref[...] = jnp.dot(h, ins.pop(0)[...], preferred_element_type=F32).astype(o_ref.dtype)
        else:
            tabs = (cos_t_ref, sin_t_ref) if rope else (None, None)
            _key_project(h, ins.pop(0), *tabs, o_ref)


def project(x, g, tabs, tabs_t, b, seq, outputs, tm=512):
    n, d = x.shape
    assert n % tm == 0 and seq % tm == 0
    spt = seq // tm
    in_specs = [pl.BlockSpec((tm, d), lambda i: (i, 0)),
                pl.BlockSpec((1, d), lambda i: (0, 0)),
                pl.BlockSpec((tm, LANES), lambda i: (i % spt, 0)),
                pl.BlockSpec((tm, LANES), lambda i: (i % spt, 0)),
                pl.BlockSpec((HEAD_DIM, tm), lambda i: (0, i % spt)),
                pl.BlockSpec((HEAD_DIM, tm), lambda i: (0, i % spt))]
    args = [x, g.reshape(1, d), *tabs, *tabs_t]
    kinds, out_specs, out_shapes = [], [], []
    for out in outputs:
        w = out[1]
        in_specs.append(pl.BlockSpec(w.shape, lambda i: (0, 0)))
        args.append(w)
        if out[0] == "slab":
            nc = 2 * w.shape[1]
            in_specs.append(pl.BlockSpec((1, nc), lambda i: (0, 0)))
            args.append(out[2].reshape(1, nc))
            kinds.append(("slab", out[3]))
            out_specs.append(pl.BlockSpec((tm, nc), lambda i: (i, 0)))
            out_shapes.append(jax.ShapeDtypeStruct((n, nc), BF16))
        elif out[0] == "plain":
            kinds.append(("plain", False))
            out_specs.append(pl.BlockSpec((tm, w.shape[1]), lambda i: (i, 0)))
            out_shapes.append(jax.ShapeDtypeStruct((n, w.shape[1]), out[2]))
        else:
            rows = 2 * w.shape[0]
            kinds.append(("keys", out[2]))
            out_specs.append(pl.BlockSpec((1, rows, tm), lambda i: (i // spt, 0, i % spt)))
            out_shapes.append(jax.ShapeDtypeStruct((b, rows, seq), BF16))
    return pl.pallas_call(
        functools.partial(_project_kernel, kinds=tuple(kinds)),
        grid=(n // tm,),
        in_specs=in_specs,
        out_specs=out_specs,
        out_shape=out_shapes,
        compiler_params=_cparams("parallel"),
        name="project",
    )(*args)


def _fox_decay_kernel(fl_ref, b_ref, pq_ref, pk_ref, oq_ref, ok_ref, carry_sc, *, nh):
    @pl.when(pl.program_id(1) == 0)
    def _():
        carry_sc[...] = jnp.zeros(carry_sc.shape, F32)

    c = jax.nn.log_sigmoid(fl_ref[0] + b_ref[...])
    ts = c.shape[0]
    row = lax.broadcasted_iota(jnp.int32, c.shape, 0)
    lane = lax.broadcasted_iota(jnp.int32, c.shape, 1)
    sh = 1
    while sh < ts:
        c = c + jnp.where(row >= sh, pltpu.roll(c, sh, 0), 0.0)
        sh *= 2
    c = c + carry_sc[0:1, :]
    carry_sc[0:1, :] = c[ts - 1:ts, :]
    hi, mid, lo = _split3(c)
    c3 = jnp.where(lane < nh, hi,
                   jnp.where(lane < 2 * nh, pltpu.roll(mid, nh, 1),
                             jnp.where(lane < 3 * nh, pltpu.roll(lo, 2 * nh, 1),
                                       jnp.where(lane == 3 * nh, 1.0, 0.0)))).astype(BF16)
    for h in range(nh):
        oq_ref[0, h, 0] = jnp.dot(c3, pq_ref[h], preferred_element_type=F32).astype(oq_ref.dtype)
        ok_ref[0, h] = lax.dot_general(pk_ref[h], c3, NT_DIMS,
                                       preferred_element_type=F32).astype(ok_ref.dtype)


def _fox_placement(nh):
    pq = np.zeros((nh, LANES, LANES), np.float32)
    pk = np.zeros((nh, LANES, LANES), np.float32)
    one = 3 * nh
    for h in range(nh):
        for t in range(3):
            pq[h, one, HEAD_DIM + t] = 1.0
            pq[h, t * nh + h, HEAD_DIM + 3 + t] = 1.0
            pk[h, HEAD_DIM + t, t * nh + h] = -1.0
            pk[h, HEAD_DIM + 3 + t, one] = 1.0
    return jnp.asarray(pq, BF16), jnp.asarray(pk, BF16)


def fox_decay(small, b_row, nh, ts=2048):
    b, s, _ = small.shape
    ts = min(ts, s)
    pq, pk = _fox_placement(nh)
    return pl.pallas_call(
        functools.partial(_fox_decay_kernel, nh=nh),
        grid=(b, s // ts),
        in_specs=[pl.BlockSpec((1, ts, LANES), lambda i, j: (i, j, 0)),
                  pl.BlockSpec((1, LANES), lambda i, j: (0, 0)),
                  pl.BlockSpec((nh, LANES, LANES), lambda i, j: (0, 0, 0)),
                  pl.BlockSpec((nh, LANES, LANES), lambda i, j: (0, 0, 0))],
        out_specs=[pl.BlockSpec((1, nh, 1, ts, LANES), lambda i, j: (i, 0, 0, j, 0)),
                   pl.BlockSpec((1, nh, LANES, ts), lambda i, j: (i, 0, 0, j))],
        out_shape=[jax.ShapeDtypeStruct((b, nh, 1, s, LANES), BF16),
                   jax.ShapeDtypeStruct((b, nh, LANES, s), BF16)],
        scratch_shapes=[pltpu.VMEM((8, LANES), F32)],
        compiler_params=_cparams("parallel", "arbitrary"),
        name="fox_decay",
    )(small, b_row, pq, pk)


def _nsa_compress_kernel(t_ref, pe_ref, w1_ref, w2_ref, o_ref):
    t = t_ref[0].astype(F32)
    ncp = t.shape[0]
    a = jnp.dot((t + pe_ref[0:1, :]).astype(BF16), w1_ref[0, 0], preferred_element_type=F32)
    bm = jnp.dot((t + pe_ref[1:2, :]).astype(BF16), w1_ref[0, 1], preferred_element_type=F32)
    pre = a + pltpu.roll(bm, ncp - 1, 0)
    hid = jax.nn.gelu(pre)
    o_ref[0, 0] = jnp.dot(hid.astype(BF16), w2_ref[...], preferred_element_type=F32).astype(o_ref.dtype)


def nsa_compress(t, pe, w1, w2):
    b, ncp, cw = t.shape
    g, _, _, hid = w1.shape
    return pl.pallas_call(
        _nsa_compress_kernel,
        grid=(b, g),
        in_specs=[pl.BlockSpec((1, ncp, cw), lambda i, j: (i, 0, 0)),
                  pl.BlockSpec((2, cw), lambda i, j: (0, 0)),
                  pl.BlockSpec((1, 2, cw, hid), lambda i, j: (j, 0, 0, 0)),
                  pl.BlockSpec((hid, LANES), lambda i, j: (0, 0))],
        out_specs=pl.BlockSpec((1, 1, ncp, LANES), lambda i, j: (i, j, 0, 0)),
        out_shape=jax.ShapeDtypeStruct((b, g, ncp, LANES), BF16),
        compiler_params=_cparams("parallel", "parallel"),
        name="nsa_compress",
    )(t, pe, w1, w2)


def _topk_mask(work, col, k, axis=-1):
    sel = jnp.zeros(work.shape, jnp.bool_)
    col = col.astype(F32)
    for _ in range(k):
        mx = jnp.max(work, axis=axis, keepdims=True)
        first = jnp.min(jnp.where(work == mx, col, jnp.inf), axis=axis, keepdims=True)
        hit = col == first
        sel = jnp.logical_or(sel, hit)
        work = jnp.where(hit, -jnp.inf, work)
    return sel


def _nsa_select_kernel(q_ref, kct_ref, vc_ref, m_ref, oc_ref, sb_ref, *, tq, rep, nsup, sps):
    i = pl.program_id(2)
    ncp = kct_ref.shape[-1]
    ns = m_ref.shape[-1]
    qpos = i * tq + lax.broadcasted_iota(jnp.int32, (tq, 1), 0)
    has_block = jnp.where(qpos >= CMP_LEN - 1, 1.0, 0.0)
    qblk = qpos >> SLC_SHIFT
    zeros = jnp.zeros((tq, HEAD_DIM), sb_ref.dtype)
    hidden = jnp.concatenate([zeros, jnp.full((tq, HEAD_DIM), MASKED, sb_ref.dtype)], axis=-1)

    def prefix(nv):
        wc, ws = ncp * nv // nsup, ns * nv // nsup
        cend = lax.broadcasted_iota(jnp.int32, (1, wc), 1) * CMP_STRIDE + (CMP_LEN - 1)
        cmask = cend <= qpos
        kct = kct_ref[0, 0, :, :wc]
        vc = vc_ref[0, 0, :wc, :]
        pcs = jnp.zeros((tq, wc), F32)
        for r in range(rep):
            s = jnp.dot(q_ref[0, :, r * LANES:(r + 1) * LANES], kct, preferred_element_type=F32)
            s = jnp.where(cmask, s, MASKED)
            e = jnp.exp(s - jnp.max(s, axis=-1, keepdims=True))
            p = e * (has_block / jnp.maximum(jnp.sum(e, axis=-1, keepdims=True), 1e-30))
            oc_ref[0, :, r * LANES:(r + 1) * LANES] = jnp.dot(
                p.astype(BF16), vc, preferred_element_type=F32).astype(oc_ref.dtype)
            pcs = pcs + p
        mm = m_ref[:wc, :ws]
        imp = sum(jnp.dot(part.astype(BF16), mm, preferred_element_type=F32) for part in _split3(pcs))
        sblk = lax.broadcasted_iota(jnp.int32, (1, ws), 1)
        forced = (sblk == 0) | (sblk == qblk) | (sblk == qblk - 1)
        free = jnp.where(forced | (sblk > qblk), -jnp.inf, imp)
        sel = _topk_mask(free, sblk, min(SLC_TOPN, ns) - 3)
        bias = jnp.where((sel | forced) & (sblk <= qblk), 0.0, MASKED).astype(sb_ref.dtype)
        for j in range(nsup):
            sb_ref[0, 0, j] = hidden if j >= nv else jnp.concatenate(
                [zeros, bias[:, j * HEAD_DIM:(j + 1) * HEAD_DIM]], axis=-1)

    for nv in range(1, nsup + 1):
        pl.when(i // sps == nv - 1)(functools.partial(prefix, nv))


def nsa_select(q_all, kct, vc, m, rep, tq=512):
    b, s, _ = q_all.shape
    g = kct.shape[1]
    ncp = kct.shape[-1]
    ns = m.shape[-1]
    nsup = ns // HEAD_DIM
    tq = min(tq, s)
    assert SLC_BLOCK == 64 and ns % HEAD_DIM == 0 and (s // tq) % nsup == 0
    return pl.pallas_call(
        functools.partial(_nsa_select_kernel, tq=tq, rep=rep, nsup=nsup, sps=(s // tq) // nsup),
        grid=(b, g, s // tq),
        in_specs=[pl.BlockSpec((1, tq, rep * LANES), lambda bi, gi, i: (bi, i, gi)),
                  pl.BlockSpec((1, 1, LANES, ncp), lambda bi, gi, i: (bi, gi, 0, 0)),
                  pl.BlockSpec((1, 1, ncp, LANES), lambda bi, gi, i: (bi, gi, 0, 0)),
                  pl.BlockSpec((ncp, ns), lambda bi, gi, i: (0, 0))],
        out_specs=[pl.BlockSpec((1, tq, rep * LANES), lambda bi, gi, i: (bi, i, gi)),
                   pl.BlockSpec((1, 1, nsup, tq, LANES), lambda bi, gi, i: (bi, gi, 0, i, 0))],
        out_shape=[jax.ShapeDtypeStruct((b, s, g * rep * LANES), BF16),
                   jax.ShapeDtypeStruct((b, g, nsup, s, LANES), BF16)],
        compiler_params=_cparams("parallel", "parallel", "parallel"),
        name="nsa_select",
    )(q_all, kct, vc, m)


def _moba_select_kernel(q_ref, kt_ref, ind_ref, sb_ref, kbar_sc, *, tq):
    i = pl.program_id(2)

    @pl.when(i == 0)
    def _():
        kbar = jnp.dot(kt_ref[0], ind_ref[...], preferred_element_type=F32) * (1.0 / MOBA_BLOCK)
        kbar_sc[...] = kbar.T

    q = q_ref[0]
    gate = sum(lax.dot_general(part.astype(BF16), q, NT_DIMS, preferred_element_type=F32)
               for part in _split3(kbar_sc[...]))
    qpos = i * tq + lax.broadcasted_iota(jnp.int32, (1, tq), 1)
    cur = qpos >> MOBA_SHIFT
    row = lax.broadcasted_iota(jnp.int32, (LANES, 1), 0)
    blk = row - HEAD_DIM
    past = (blk >= 0) & (blk < cur)
    sel = _topk_mask(jnp.where(past, gate, -jnp.inf), row, MOBA_TOPK, axis=0)
    keep = (blk < 0) | (sel & past) | (blk == cur)
    sb_ref[0, 0, 0] = jnp.where(keep, 0.0, MASKED).T.astype(sb_ref.dtype)


def moba_select(q_all, kt_all, ind, nh, tq=2048):
    b, s, _ = q_all.shape
    assert MOBA_BLOCK == 256 and s // MOBA_BLOCK <= HEAD_DIM
    tq = min(tq, s)
    return pl.pallas_call(
        functools.partial(_moba_select_kernel, tq=tq),
        grid=(b, nh, s // tq),
        in_specs=[pl.BlockSpec((1, tq, LANES), lambda bi, hi, i: (bi, i, hi)),
                  pl.BlockSpec((1, LANES, s), lambda bi, hi, i: (bi, hi, 0)),
                  pl.BlockSpec((s, LANES), lambda bi, hi, i: (0, 0))],
        out_specs=pl.BlockSpec((1, 1, 1, tq, LANES), lambda bi, hi, i: (bi, hi, 0, i, 0)),
        out_shape=jax.ShapeDtypeStruct((b, nh, 1, s, LANES), BF16),
        scratch_shapes=[pltpu.VMEM((LANES, LANES), F32)],
        compiler_params=_cparams("parallel", "parallel", "arbitrary"),
        name="moba_select",
    )(q_all, kt_all, ind)


def _flash_kernel(*refs, tq, tk, rs, rs_main, tps, band, decay, has_extra, has_kx):
    refs = list(refs)
    qa_ref = refs.pop(0)
    ex_ref = refs.pop(0) if has_extra else None
    kt_ref = refs.pop(0)
    kx_ref = refs.pop(0) if has_kx else None
    kmax_sc = refs.pop() if decay else None
    v_ref, o_ref, m_sc, acc_sc = refs
    i = pl.program_id(2)
    m_sc[...] = jnp.full(m_sc.shape, M_INIT, F32)
    acc_sc[...] = jnp.zeros(acc_sc.shape, F32)
    row = lax.broadcasted_iota(jnp.int32, (rs, tk), 0)
    col = lax.broadcasted_iota(jnp.int32, (rs, tk), 1)
    nst = tq // rs
    kpq = tq // tk

    def run(items, rs=rs):
        tiles = {}

        def operands(j):
            if id(j) not in tiles:
                start = pl.multiple_of(j * tk, tk)
                kt = kt_ref[0, :, pl.ds(start, tk)]
                if kx_ref is not None:
                    kt = kt + kx_ref[0, 0, :, pl.ds(start, tk)]
                tiles[id(j)] = (kt, v_ref[0, pl.ds(start, tk), :])
            return tiles[id(j)]

        def logits(item):
            j, r, _ = item
            rows = pl.ds(r * rs, rs)
            qa = qa_ref[0, rows, :]
            if ex_ref is not None:
                qa = qa + ex_ref[0, 0, j // tps, rows, :]
            return jnp.dot(qa, operands(j)[0], preferred_element_type=F32)

        def accumulate(rows, alpha, p, vv):
            acc_sc[rows, :] = alpha * acc_sc[rows, :] + jnp.dot(p, vv, preferred_element_type=F32)

        pending = [logits(it) for it in items[:FLASH_AHEAD]]
        held = None
        for n, (j, r, mask) in enumerate(items):
            rows = pl.ds(r * rs, rs)
            s = pending.pop(0)
            if n + FLASH_AHEAD < len(items):
                pending.append(logits(items[n + FLASH_AHEAD]))
            if held is not None:
                accumulate(*held)
            if mask is not None:
                s = jnp.where(mask, s, MASKED)
            m_prev = m_sc[rows, :]
            m_new = jnp.maximum(m_prev, jnp.max(s, axis=-1, keepdims=True))
            p = jnp.exp(s - jnp.tile(m_new, (1, tk // LANES)))
            m_sc[rows, :] = m_new
            held = (rows, jnp.exp(m_prev - m_new), p.astype(BF16), operands(j)[1])
        accumulate(*held)

    def edge_items(dj_list, j_of):
        items = []
        for dj in dj_list:
            j = j_of(dj)
            for r in range(nst):
                off, ko = r * rs, dj * tk
                lo = off - WINDOW + 1 if band else None
                if ko > off + rs - 1 or (band and ko + tk - 1 < lo):
                    continue
                full = ko + tk - 1 <= off and (not band or ko > off + rs - 1 - WINDOW)
                mask = None
                if not full:
                    mask = col + ko <= row + off
                    if band:
                        mask = mask & (col + ko > row + (off - WINDOW))
                items.append((j, r, mask))
        return items

    def body(jjs):
        tiles = [jj * kpq + dj for jj in jjs for dj in range(kpq)]
        run([(j, r, None) for j in tiles for r in range(tq // rs_main)], rs_main)

    def sweep(n, jj_of):
        def group(t, carry):
            body([jj_of(FLASH_GROUP * t + u) for u in range(FLASH_GROUP)])
            return carry
        lax.fori_loop(0, n // FLASH_GROUP, group, 0)

        def single(t, carry):
            body([jj_of(n - n % FLASH_GROUP + t)])
            return carry
        lax.fori_loop(0, n % FLASH_GROUP, single, 0)

    def run_diag():
        diag = [i * kpq + dj for dj in range(kpq)]
        run(edge_items(list(range(kpq)), lambda dj: diag[dj]))

    if decay:
        s_len = kt_ref.shape[-1]

        @pl.when(i == 0)
        def _():
            def chunk(c, best):
                kk = kt_ref[0, :, pl.ds(pl.multiple_of(c * tq, tq), tq)].astype(F32)
                return jnp.maximum(best, jnp.sum(kk * kk, axis=0, keepdims=True))
            ksq = lax.fori_loop(0, s_len // tq, chunk, jnp.zeros((1, tq), F32))
            kmax_sc[...] = jnp.broadcast_to(jnp.sqrt(jnp.max(ksq, axis=1, keepdims=True)), kmax_sc.shape)

        run_diag()
        q = qa_ref[0].astype(F32)
        qn = jnp.sqrt(jnp.sum(q * q, axis=1, keepdims=True))
        slack = jnp.max(qn * kmax_sc[0:1, 0:1] - m_sc[:, 0:1], axis=0, keepdims=True)
        cvec = -jnp.sum(kx_ref[0, 0, HEAD_DIM:HEAD_DIM + 3, :].astype(F32), axis=0, keepdims=True)
        pos = lax.broadcasted_iota(jnp.int32, (1, s_len), 1)
        q0 = i * tq
        c_q0 = jnp.sum(jnp.where(pos == q0, cvec, 0.0), axis=1, keepdims=True)
        dead = (pos < q0) & (slack + c_q0 - cvec <= -DECAY_CUT)
        n_dead = jnp.sum(jnp.where(dead, 1.0, 0.0)).astype(jnp.int32) // tq
        sweep(i - n_dead, lambda t: i - 1 - t)
    elif band:
        pl.when(i == 0)(run_diag)

        @pl.when(i > 0)
        def _():
            near = {dj: i * kpq + dj for dj in range(-(WINDOW // tk), kpq)}
            run(edge_items(sorted(near), near.get))
    else:
        sweep(i, lambda t: t)
        run_diag()
    acc = acc_sc[...]
    o_ref[0] = (acc / acc[:, HEAD_DIM:HEAD_DIM + 1]).astype(o_ref.dtype)


def flash(name, q_all, q0, kt_all, k0, v_all, v0, nh, rep, extra=None, kx=None, band=False,
          decay=False):
    assert not decay or (kx is not None and not band)
    b, s, _ = q_all.shape
    tk = min(WINDOW if band else FLASH_TK, s)
    tq = min(FLASH_TQ, s)
    rs = min(FLASH_STRIP, tq)
    assert s % tq == 0 and tq % tk == 0 and tq % rs == 0 and (not band or WINDOW % tk == 0)
    in_specs = [pl.BlockSpec((1, tq, LANES), lambda bi, hi, i: (bi, i, q0 + hi))]
    args = [q_all]
    tps = 1
    if extra is not None:
        he, nsup = extra.shape[1], extra.shape[2]
        rep_e = nh // he
        assert (s // nsup) % tk == 0
        tps = (s // nsup) // tk
        in_specs.append(pl.BlockSpec((1, 1, nsup, tq, LANES), lambda bi, hi, i: (bi, hi // rep_e, 0, i, 0)))
        args.append(extra)
    in_specs.append(pl.BlockSpec((1, LANES, s), lambda bi, hi, i: (bi, k0 + hi // rep, 0)))
    args.append(kt_all)
    if kx is not None:
        bx, hx = kx.shape[0], kx.shape[1]
        in_specs.append(pl.BlockSpec(
            (1, 1, LANES, s), lambda bi, hi, i: (bi if bx > 1 else 0, hi if hx > 1 else 0, 0, 0)))
        args.append(kx)
    in_specs.append(pl.BlockSpec((1, s, LANES), lambda bi, hi, i: (bi, 0, v0 + hi // rep)))
    args.append(v_all)
    return pl.pallas_call(
        functools.partial(_flash_kernel, tq=tq, tk=tk, rs=rs, rs_main=min(FLASH_STRIP_MAIN, tq),
                          tps=tps, band=band, decay=decay,
                          has_extra=extra is not None, has_kx=kx is not None),
        grid=(b, nh, s // tq),
        in_specs=in_specs,
        out_specs=pl.BlockSpec((1, tq, LANES), lambda bi, hi, i: (bi, i, hi)),
        out_shape=jax.ShapeDtypeStruct((b, s, nh * LANES), BF16),
        scratch_shapes=[pltpu.VMEM((tq, LANES), F32), pltpu.VMEM((tq, LANES), F32)]
        + ([pltpu.VMEM((8, LANES), F32)] if decay else []),
        compiler_params=_cparams("parallel", "parallel", "arbitrary" if decay else "parallel"),
        name=name,
    )(*args)


def _proj_norm_res_kernel(a_ref, w_ref, g_ref, x_ref, o_ref):
    y = jnp.dot(a_ref[...], w_ref[...], preferred_element_type=F32)
    o_ref[...] = x_ref[...] + _rms(y, g_ref[...])


def proj_norm_res(a, w, g, x, tm=512):
    n, k = a.shape
    d = w.shape[1]
    return pl.pallas_call(
        _proj_norm_res_kernel,
        grid=(n // tm,),
        in_specs=[pl.BlockSpec((tm, k), lambda i: (i, 0)),
                  pl.BlockSpec((k, d), lambda i: (0, 0)),
                  pl.BlockSpec((1, d), lambda i: (0, 0)),
                  pl.BlockSpec((tm, d), lambda i: (i, 0))],
        out_specs=pl.BlockSpec((tm, d), lambda i: (i, 0)),
        out_shape=jax.ShapeDtypeStruct((n, d), F32),
        compiler_params=_cparams("parallel"),
        name="proj_norm_res",
    )(a, w, g.reshape(1, d), x)


def _even_out_kernel(of_ref, oc_ref, os_ref, ow_ref, gl_ref, wf_ref, wn_ref, g_ref, x_ref, o_ref,
                     *, nh, g0):
    gate = jax.nn.sigmoid(gl_ref[...])
    parts = []
    for h in range(nh):
        sl = slice(h * LANES, (h + 1) * LANES)
        c = g0 + 3 * h
        parts.append(gate[:, c:c + 1] * oc_ref[:, sl].astype(F32)
                     + gate[:, c + 1:c + 2] * os_ref[:, sl].astype(F32)
                     + gate[:, c + 2:c + 3] * ow_ref[:, sl].astype(F32))
    a = jnp.concatenate(parts, axis=-1).astype(BF16)
    y = (jnp.dot(of_ref[...], wf_ref[...], preferred_element_type=F32)
         + jnp.dot(a, wn_ref[...], preferred_element_type=F32))
    o_ref[...] = x_ref[...] + _rms(y, g_ref[...])


def even_out(o_fox, o_cmp, o_slc, o_win, small, wf, wn, g, x, nh, g0, tm=512):
    n, k = o_fox.shape
    d = wf.shape[1]
    act = pl.BlockSpec((tm, k), lambda i: (i, 0))
    wspec = pl.BlockSpec((k, d), lambda i: (0, 0))
    return pl.pallas_call(
        functools.partial(_even_out_kernel, nh=nh, g0=g0),
        grid=(n // tm,),
        in_specs=[act, act, act, act,
                  pl.BlockSpec((tm, LANES), lambda i: (i, 0)),
                  wspec, wspec,
                  pl.BlockSpec((1, d), lambda i: (0, 0)),
                  pl.BlockSpec((tm, d), lambda i: (i, 0))],
        out_specs=pl.BlockSpec((tm, d), lambda i: (i, 0)),
        out_shape=jax.ShapeDtypeStruct((n, d), F32),
        compiler_params=_cparams("parallel"),
        name="even_out",
    )(o_fox, o_cmp, o_slc, o_win, small, wf, wn, g.reshape(1, d), x)


def _ffn_kernel(x_ref, gpre_ref, wg_ref, wu_ref, wd_ref, gpost_ref, o_ref, h_sc, acc_sc):
    f = pl.program_id(1)

    @pl.when(f == 0)
    def _():
        h_sc[...] = _rms(x_ref[...], gpre_ref[...]).astype(BF16)
        acc_sc[...] = jnp.zeros(acc_sc.shape, F32)

    h = h_sc[...]
    a = jnp.dot(h, wg_ref[...], preferred_element_type=F32)
    u = jnp.dot(h, wu_ref[...], preferred_element_type=F32)
    act = (jax.nn.silu(a) * u).astype(BF16)
    acc_sc[...] += jnp.dot(act, wd_ref[...], preferred_element_type=F32)

    @pl.when(f == pl.num_programs(1) - 1)
    def _():
        o_ref[...] = x_ref[...] + _rms(acc_sc[...], gpost_ref[...])


def ffn(x, gpre, wg, wu, wd, gpost, tm=1024, tf=1408):
    n, d = x.shape
    dff = wg.shape[1]
    assert n % tm == 0 and dff % tf == 0
    return pl.pallas_call(
        _ffn_kernel,
        grid=(n // tm, dff // tf),
        in_specs=[pl.BlockSpec((tm, d), lambda i, f: (i, 0)),
                  pl.BlockSpec((1, d), lambda i, f: (0, 0)),
                  pl.BlockSpec((d, tf), lambda i, f: (0, f)),
                  pl.BlockSpec((d, tf), lambda i, f: (0, f)),
                  pl.BlockSpec((tf, d), lambda i, f: (f, 0)),
                  pl.BlockSpec((1, d), lambda i, f: (0, 0))],
        out_specs=pl.BlockSpec((tm, d), lambda i, f: (i, 0)),
        out_shape=jax.ShapeDtypeStruct((n, d), F32),
        scratch_shapes=[pltpu.VMEM((tm, d), BF16), pltpu.VMEM((tm, d), F32)],
        compiler_params=_cparams("parallel", "arbitrary"),
        name="ffn",
    )(x, gpre.reshape(1, d), wg, wu, wd, gpost.reshape(1, d))


def _rope_tables(s):
    inv = ROPE_THETA ** (-jnp.arange(0, HEAD_DIM, 2, dtype=F32) / HEAD_DIM)
    ang = jnp.arange(s, dtype=F32)[:, None] * inv[None, :]
    cos, sin = jnp.cos(ang), jnp.sin(ang)
    reps = LANES // HEAD_DIM
    cos2 = jnp.tile(jnp.concatenate([cos, cos], -1), (1, reps))
    sin2 = jnp.tile(jnp.concatenate([-sin, sin], -1), (1, reps))
    return (cos2, sin2), (cos2.T[:HEAD_DIM], sin2.T[:HEAD_DIM])


def _slab_cols(w, scale=1.0):
    d, c = w.shape
    w3 = (w * scale).reshape(d, c // HEAD_DIM, HEAD_DIM)
    return jnp.concatenate([w3, jnp.zeros_like(w3)], axis=-1).reshape(d, 2 * c)


def _slab_rows(w):
    return _slab_cols(w.T).T


def _ones_lane(n_slabs):
    one = np.zeros((n_slabs, LANES), np.float32)
    one[:, HEAD_DIM] = 1.0
    return jnp.asarray(one.reshape(-1))


def _block_indicator_rows(s, block):
    blk = (np.arange(s) // block) % HEAD_DIM
    ind = np.zeros((LANES, s), np.float32)
    ind[HEAD_DIM + blk, np.arange(s)] = 1.0
    return jnp.asarray(ind, BF16)[None, None]


def _overlap_matrix(ncp, ns):
    ratio = SLC_BLOCK // CMP_STRIDE
    m = np.arange(ncp)[:, None]
    j = np.arange(ns)[None, :]
    ok = (m >= ratio * j - 1) & (m <= ratio * j + ratio - 1) & (m < ncp - 1)
    return jnp.asarray(ok, BF16)


def _compress_weights(pe, w1, w2, g):
    hid = w1.shape[-1]
    w1r = w1.reshape(2, CMP_STRIDE, HEAD_DIM, hid)
    w1g = jnp.zeros((g, 2, CMP_STRIDE, g, LANES, hid), F32)
    for gi in range(g):
        w1g = w1g.at[gi, :, :, gi, :HEAD_DIM].set(w1r)
    w1g = w1g.reshape(g, 2, CMP_STRIDE * g * LANES, hid).astype(BF16)
    pe2 = jnp.zeros((2, CMP_STRIDE, g, LANES), F32).at[..., :HEAD_DIM].set(
        pe.reshape(2, CMP_STRIDE, 1, HEAD_DIM))
    w2p = jnp.concatenate([w2, jnp.zeros_like(w2)], axis=-1).astype(BF16)
    return pe2.reshape(2, CMP_STRIDE * g * LANES), w1g, w2p


def _out_rows(w):
    return _slab_rows(w).astype(BF16)


def _even_layer(x2, b, s, g_pre, g_post, w_in, b_f, pe_k, w1_k, w2_k, pe_v, w1_v, w2_v, w_out, tabs, tabs_t):
    hf, hn, g = H_FOX, H_NSA, NSA_GROUPS
    hd = HEAD_DIM
    rep = hn // g
    (w_fq, w_fk, w_fv, w_fl, w_nq, w_kc, w_vc, w_ks, w_vs, w_kw, w_vw, w_gl) = jnp.split(
        w_in, list(np.cumsum([hf * hd] * 3 + [hf] + [hn * hd] + [g * hd] * 6)), axis=1)
    scale = hd ** -0.5

    w_plain = jnp.concatenate([w_fq * scale, w_fv, w_vs, w_vw, w_vc], axis=1).astype(BF16)
    bias_plain = jnp.concatenate([jnp.zeros((hf * LANES,), F32), _ones_lane(hf + 2 * g),
                                  jnp.zeros((g * LANES,), F32)])
    fq0, fv0, vs0, vw0 = 0, hf, 2 * hf, 2 * hf + g
    vc_col = (2 * hf + 2 * g) * LANES
    w_rope = jnp.concatenate([w_nq * scale, w_kc], axis=1).astype(BF16)
    n_small = hf + 3 * hn
    w_small = jnp.pad(jnp.concatenate([w_fl, w_gl], axis=1), ((0, 0), (0, LANES - n_small))).astype(BF16)
    p_plain, p_rope, p_small, kt_fox, kt_nsa = project(x2, g_pre, tabs, tabs_t, b, s, [
        ("slab", w_plain, bias_plain, False),
        ("slab", w_rope, jnp.zeros((2 * w_rope.shape[1],), F32), True),
        ("plain", w_small, F32),
        ("keys", w_fk.T.astype(BF16), False),
        ("keys", jnp.concatenate([w_ks, w_kw], axis=1).T.astype(BF16), True),
    ])
    p_plain = p_plain.reshape(b, s, -1)
    p_rope = p_rope.reshape(b, s, -1)

    qx, kx = fox_decay(p_small.reshape(b, s, LANES), jnp.pad(b_f, (0, LANES - hf)).reshape(1, LANES), hf)
    o_fox = flash("flash_fox", p_plain, fq0, kt_fox, 0, p_plain, fv0, hf, 1, extra=qx, kx=kx, decay=True)

    ncp = s // CMP_STRIDE
    ns = s // SLC_BLOCK
    kc = p_rope[:, :, hn * LANES:].reshape(b, ncp, CMP_STRIDE * g * LANES)
    vc = p_plain[:, :, vc_col:].reshape(b, ncp, CMP_STRIDE * g * LANES)
    kcmp = nsa_compress(kc, *_compress_weights(pe_k, w1_k, w2_k, g))
    vcmp = nsa_compress(vc, *_compress_weights(pe_v, w1_v, w2_v, g))
    o_cmp, selb = nsa_select(p_rope, kcmp.transpose(0, 1, 3, 2), vcmp, _overlap_matrix(ncp, ns), rep)
    o_slc = flash("flash_sel", p_rope, 0, kt_nsa, 0, p_plain, vs0, hn, rep, extra=selb,
                  kx=_block_indicator_rows(s, SLC_BLOCK))
    o_win = flash("flash_band", p_rope, 0, kt_nsa, g, p_plain, vw0, hn, rep, band=True)

    n = b * s
    return even_out(o_fox.reshape(n, -1), o_cmp.reshape(n, -1), o_slc.reshape(n, -1),
                    o_win.reshape(n, -1), p_small, _out_rows(w_out[:hf * hd]), _out_rows(w_out[hf * hd:]),
                    g_post, x2, hn, hf)


def _odd_layer(x2, b, s, g_pre, g_post, w_in, w_out, tabs, tabs_t):
    h = H_MOBA
    hd = HEAD_DIM
    d = h * hd
    scale = hd ** -0.5
    w_q, w_k, w_v = w_in[:, :d], w_in[:, d:2 * d], w_in[:, 2 * d:]
    q_all, v_all, kt_all = project(x2, g_pre, tabs, tabs_t, b, s, [
        ("slab", (w_q * scale).astype(BF16), jnp.zeros((h * LANES,), F32), True),
        ("slab", w_v.astype(BF16), _ones_lane(h), False),
        ("keys", w_k.T.astype(BF16), True),
    ])
    q_all = q_all.reshape(b, s, -1)
    v_all = v_all.reshape(b, s, -1)
    ind = _block_indicator_rows(s, MOBA_BLOCK)
    selb = moba_select(q_all, kt_all, ind[0, 0].T, h)
    o = flash("flash_moba", q_all, 0, kt_all, 0, v_all, 0, h, 1, extra=selb, kx=ind)
    return proj_norm_res(o.reshape(b * s, -1), _out_rows(w_out), g_post, x2)


def kernel(x, ev_w_in, ev_b_f, ev_cmp_pe_k, ev_cmp_w1_k, ev_cmp_w2_k, ev_cmp_pe_v, ev_cmp_w1_v,
           ev_cmp_w2_v, ev_w_out, od_w_in, od_w_out, g_mix_pre, g_mix_post, g_ffn_pre, g_ffn_post,
           ffn_w_gate, ffn_w_up, ffn_w_down):
    b, s, d = x.shape
    depth = g_mix_pre.shape[0]
    tabs, tabs_t = _rope_tables(s)
    x2 = x.reshape(b * s, d)
    for layer in range(depth):
        if layer % 2 == 0:
            e = layer // 2
            x2 = _even_layer(x2, b, s, g_mix_pre[layer], g_mix_post[layer], ev_w_in[e], ev_b_f[e],
                             ev_cmp_pe_k[e], ev_cmp_w1_k[e], ev_cmp_w2_k[e], ev_cmp_pe_v[e],
                             ev_cmp_w1_v[e], ev_cmp_w2_v[e], ev_w_out[e], tabs, tabs_t)
        else:
            o = layer // 2
            x2 = _odd_layer(x2, b, s, g_mix_pre[layer], g_mix_post[layer], od_w_in[o], od_w_out[o],
                            tabs, tabs_t)
        x2 = ffn(x2, g_ffn_pre[layer], ffn_w_gate[layer].astype(BF16), ffn_w_up[layer].astype(BF16),
                 ffn_w_down[layer].astype(BF16), g_ffn_post[layer])
    return x2.reshape(b, s, d)
```

```python
import functools

import jax
import jax.numpy as jnp
import numpy as np
from jax import lax
from jax.experimental import pallas as pl
from jax.experimental.pallas import tpu as pltpu

F32 = jnp.float32
BF16 = jnp.bfloat16

HEAD_DIM = 64
LANES = 128
ROPE_THETA = 10000.0
RMS_EPS = 1e-6
CMP_STRIDE = 16
CMP_LEN = 32
SLC_BLOCK = 64
SLC_TOPN = 16
WINDOW = 512
MOBA_BLOCK = 256
MOBA_TOPK = 3
H_FOX, H_NSA, NSA_GROUPS, H_MOBA = 8, 8, 2, 16
SLC_SHIFT = SLC_BLOCK.bit_length() - 1
MOBA_SHIFT = MOBA_BLOCK.bit_length() - 1
MASKED = -1e30
M_INIT = -1e29
LOG2E = 1.4426950408889634
QK_SCALE = HEAD_DIM ** -0.5 * LOG2E
DECAY_CUT = 100.0 * LOG2E
FLASH_TQ = 1024
FLASH_TK = 512
FLASH_STRIP = 512
FLASH_STRIP_MAIN = 1024
FLASH_GROUP = 2
FLASH_AHEAD = 2
VMEM_LIMIT = 48 * 1024 * 1024
NT_DIMS = (((1,), (1,)), ((), ()))


def _cparams(*sem):
    return pltpu.CompilerParams(dimension_semantics=sem, vmem_limit_bytes=VMEM_LIMIT)


def _rms(x, g):
    return x * lax.rsqrt(jnp.mean(x * x, axis=-1, keepdims=True) + RMS_EPS) * g


def _split3(x):
    hi = x.astype(BF16).astype(F32)
    r = x - hi
    mid = r.astype(BF16).astype(F32)
    lo = (r - mid).astype(BF16).astype(F32)
    return hi, mid, lo


def _slab_project(h, w_ref, b_ref, cos_ref, sin_ref):
    t = jnp.dot(h, w_ref[...], preferred_element_type=F32)
    tm, tn = t.shape
    if cos_ref is not None:
        reps = tn // LANES
        cos = jnp.tile(cos_ref[...], (1, reps))
        sin = jnp.tile(sin_ref[...], (1, reps))
        lane = lax.broadcasted_iota(jnp.int32, t.shape, 1)
        first = (lane & (HEAD_DIM - 1)) < HEAD_DIM // 2
        rot = jnp.where(first, pltpu.roll(t, tn - HEAD_DIM // 2, 1), pltpu.roll(t, HEAD_DIM // 2, 1))
        t = t * cos + rot * sin
    low = lax.broadcasted_iota(jnp.int32, (tm, LANES), 1) < HEAD_DIM
    parts = []
    for pair in range(tn // LANES):
        v = t[:, pair * LANES:(pair + 1) * LANES]
        parts.append(jnp.where(low, v, 0.0))
        parts.append(jnp.where(low, pltpu.roll(v, HEAD_DIM, 1), 0.0))
    return jnp.concatenate(parts, axis=1) + b_ref[...]


def _key_project(h, wt_ref, cos_ref, sin_ref, o_ref):
    t = lax.dot_general(wt_ref[...], h, NT_DIMS, preferred_element_type=F32)
    tn, tm = t.shape
    nheads = tn // HEAD_DIM
    if cos_ref is not None:
        cos = jnp.tile(cos_ref[...], (nheads, 1))
        sin = jnp.tile(sin_ref[...], (nheads, 1))
        r = lax.broadcasted_iota(jnp.int32, t.shape, 0)
        first = (r & (HEAD_DIM - 1)) < HEAD_DIM // 2
        rot = jnp.where(first, pltpu.roll(t, tn - HEAD_DIM // 2, 0), pltpu.roll(t, HEAD_DIM // 2, 0))
        t = t * cos + rot * sin
    spare = jnp.zeros((LANES - HEAD_DIM, tm), o_ref.dtype)
    for hh in range(nheads):
        o_ref[0, hh * LANES:hh * LANES + HEAD_DIM, :] = t[hh * HEAD_DIM:(hh + 1) * HEAD_DIM].astype(o_ref.dtype)
        o_ref[0, hh * LANES + HEAD_DIM:(hh + 1) * LANES, :] = spare


def _project_kernel(x_ref, g_ref, cos_ref, sin_ref, cos_t_ref, sin_t_ref, *refs, kinds):
    n_out = len(kinds)
    ins, outs = list(refs[:-n_out]), refs[-n_out:]
    h = _rms(x_ref[...], g_ref[...]).astype(BF16)
    for (kind, rope), o_ref in zip(kinds, outs):
        if kind == "slab":
            w_ref, b_ref = ins.pop(0), ins.pop(0)
            tabs = (cos_ref, sin_ref) if rope else (None, None)
            o_ref[...] = _slab_project(h, w_ref, b_ref, *tabs).astype(o_ref.dtype)
        elif kind == "plain":
            o_ref[...] = jnp.dot(h, ins.pop(0)[...], preferred_element_type=F32).astype(o_ref.dtype)
        else:
            tabs = (cos_t_ref, sin_t_ref) if rope else (None, None)
            _key_project(h, ins.pop(0), *tabs, o_ref)


def project(x, g, tabs, tabs_t, b, seq, outputs, tm=512):
    n, d = x.shape
    assert n % tm == 0 and seq % tm == 0
    spt = seq // tm
    in_specs = [pl.BlockSpec((tm, d), lambda i: (i, 0)),
                pl.BlockSpec((1, d), lambda i: (0, 0)),
                pl.BlockSpec((tm, LANES), lambda i: (i % spt, 0)),
                pl.BlockSpec((tm, LANES), lambda i: (i % spt, 0)),
                pl.BlockSpec((HEAD_DIM, tm), lambda i: (0, i % spt)),
                pl.BlockSpec((HEAD_DIM, tm), lambda i: (0, i % spt))]
    args = [x, g.reshape(1, d), *tabs, *tabs_t]
    kinds, out_specs, out_shapes = [], [], []
    for out in outputs:
        w = out[1]
        in_specs.append(pl.BlockSpec(w.shape, lambda i: (0, 0)))
        args.append(w)
        if out[0] == "slab":
            nc = 2 * w.shape[1]
            in_specs.append(pl.BlockSpec((1, nc), lambda i: (0, 0)))
            args.append(out[2].reshape(1, nc))
            kinds.append(("slab", out[3]))
            out_specs.append(pl.BlockSpec((tm, nc), lambda i: (i, 0)))
            out_shapes.append(jax.ShapeDtypeStruct((n, nc), BF16))
        elif out[0] == "plain":
            kinds.append(("plain", False))
            out_specs.append(pl.BlockSpec((tm, w.shape[1]), lambda i: (i, 0)))
            out_shapes.append(jax.ShapeDtypeStruct((n, w.shape[1]), out[2]))
        else:
            rows = 2 * w.shape[0]
            kinds.append(("keys", out[2]))
            out_specs.append(pl.BlockSpec((1, rows, tm), lambda i: (i // spt, 0, i % spt)))
            out_shapes.append(jax.ShapeDtypeStruct((b, rows, seq), BF16))
    return pl.pallas_call(
        functools.partial(_project_kernel, kinds=tuple(kinds)),
        grid=(n // tm,),
        in_specs=in_specs,
        out_specs=out_specs,
        out_shape=out_shapes,
        compiler_params=_cparams("parallel"),
        name="project",
    )(*args)


def _fox_decay_kernel(fl_ref, b_ref, pq_ref, pk_ref, oq_ref, ok_ref, carry_sc, *, nh):
    @pl.when(pl.program_id(1) == 0)
    def _():
        carry_sc[...] = jnp.zeros(carry_sc.shape, F32)

    c = jax.nn.log_sigmoid(fl_ref[0] + b_ref[...])
    ts = c.shape[0]
    row = lax.broadcasted_iota(jnp.int32, c.shape, 0)
    lane = lax.broadcasted_iota(jnp.int32, c.shape, 1)
    sh = 1
    while sh < ts:
        c = c + jnp.where(row >= sh, pltpu.roll(c, sh, 0), 0.0)
        sh *= 2
    c = c + carry_sc[0:1, :]
    carry_sc[0:1, :] = c[ts - 1:ts, :]
    hi, mid, lo = _split3(c * LOG2E)
    c3 = jnp.where(lane < nh, hi,
                   jnp.where(lane < 2 * nh, pltpu.roll(mid, nh, 1),
                             jnp.where(lane < 3 * nh, pltpu.roll(lo, 2 * nh, 1),
                                       jnp.where(lane == 3 * nh, 1.0, 0.0)))).astype(BF16)
    for h in range(nh):
        oq_ref[0, h, 0] = jnp.dot(c3, pq_ref[h], preferred_element_type=F32).astype(oq_ref.dtype)
        ok_ref[0, h] = lax.dot_general(pk_ref[h], c3, NT_DIMS,
                                       preferred_element_type=F32).astype(ok_ref.dtype)


def _fox_placement(nh):
    pq = np.zeros((nh, LANES, LANES), np.float32)
    pk = np.zeros((nh, LANES, LANES), np.float32)
    one = 3 * nh
    for h in range(nh):
        for t in range(3):
            pq[h, one, HEAD_DIM + t] = 1.0
            pq[h, t * nh + h, HEAD_DIM + 3 + t] = 1.0
            pk[h, HEAD_DIM + t, t * nh + h] = -1.0
            pk[h, HEAD_DIM + 3 + t, one] = 1.0
    return jnp.asarray(pq, BF16), jnp.asarray(pk, BF16)


def fox_decay(small, b_row, nh, ts=2048):
    b, s, _ = small.shape
    ts = min(ts, s)
    pq, pk = _fox_placement(nh)
    return pl.pallas_call(
        functools.partial(_fox_decay_kernel, nh=nh),
        grid=(b, s // ts),
        in_specs=[pl.BlockSpec((1, ts, LANES), lambda i, j: (i, j, 0)),
                  pl.BlockSpec((1, LANES), lambda i, j: (0, 0)),
                  pl.BlockSpec((nh, LANES, LANES), lambda i, j: (0, 0, 0)),
                  pl.BlockSpec((nh, LANES, LANES), lambda i, j: (0, 0, 0))],
        out_specs=[pl.BlockSpec((1, nh, 1, ts, LANES), lambda i, j: (i, 0, 0, j, 0)),
                   pl.BlockSpec((1, nh, LANES, ts), lambda i, j: (i, 0, 0, j))],
        out_shape=[jax.ShapeDtypeStruct((b, nh, 1, s, LANES), BF16),
                   jax.ShapeDtypeStruct((b, nh, LANES, s), BF16)],
        scratch_shapes=[pltpu.VMEM((8, LANES), F32)],
        compiler_params=_cparams("parallel", "arbitrary"),
        name="fox_decay",
    )(small, b_row, pq, pk)


def _nsa_compress_kernel(t_ref, pe_ref, w1_ref, w2_ref, o_ref):
    t = t_ref[0].astype(F32)
    ncp = t.shape[0]
    a = jnp.dot((t + pe_ref[0:1, :]).astype(BF16), w1_ref[0, 0], preferred_element_type=F32)
    bm = jnp.dot((t + pe_ref[1:2, :]).astype(BF16), w1_ref[0, 1], preferred_element_type=F32)
    pre = a + pltpu.roll(bm, ncp - 1, 0)
    hid = jax.nn.gelu(pre)
    o_ref[0, 0] = jnp.dot(hid.astype(BF16), w2_ref[...], preferred_element_type=F32).astype(o_ref.dtype)


def nsa_compress(t, pe, w1, w2):
    b, ncp, cw = t.shape
    g, _, _, hid = w1.shape
    return pl.pallas_call(
        _nsa_compress_kernel,
        grid=(b, g),
        in_specs=[pl.BlockSpec((1, ncp, cw), lambda i, j: (i, 0, 0)),
                  pl.BlockSpec((2, cw), lambda i, j: (0, 0)),
                  pl.BlockSpec((1, 2, cw, hid), lambda i, j: (j, 0, 0, 0)),
                  pl.BlockSpec((hid, LANES), lambda i, j: (0, 0))],
        out_specs=pl.BlockSpec((1, 1, ncp, LANES), lambda i, j: (i, j, 0, 0)),
        out_shape=jax.ShapeDtypeStruct((b, g, ncp, LANES), BF16),
        compiler_params=_cparams("parallel", "parallel"),
        name="nsa_compress",
    )(t, pe, w1, w2)


def _topk_mask(work, col, k, axis=-1):
    sel = jnp.zeros(work.shape, jnp.bool_)
    col = col.astype(F32)
    for _ in range(k):
        mx = jnp.max(work, axis=axis, keepdims=True)
        first = jnp.min(jnp.where(work == mx, col, jnp.inf), axis=axis, keepdims=True)
        hit = col == first
        sel = jnp.logical_or(sel, hit)
        work = jnp.where(hit, -jnp.inf, work)
    return sel


def _nsa_select_kernel(q_ref, kct_ref, vc_ref, m_ref, oc_ref, sb_ref, *, tq, rep, nsup, sps):
    i = pl.program_id(2)
    ncp = kct_ref.shape[-1]
    ns = m_ref.shape[-1]
    qpos = i * tq + lax.broadcasted_iota(jnp.int32, (tq, 1), 0)
    has_block = jnp.where(qpos >= CMP_LEN - 1, 1.0, 0.0)
    qblk = qpos >> SLC_SHIFT
    zeros = jnp.zeros((tq, HEAD_DIM), sb_ref.dtype)
    hidden = jnp.concatenate([zeros, jnp.full((tq, HEAD_DIM), MASKED, sb_ref.dtype)], axis=-1)

    def prefix(nv):
        wc, ws = ncp * nv // nsup, ns * nv // nsup
        cend = lax.broadcasted_iota(jnp.int32, (1, wc), 1) * CMP_STRIDE + (CMP_LEN - 1)
        cmask = cend <= qpos
        kct = kct_ref[0, 0, :, :wc]
        vc = vc_ref[0, 0, :wc, :]
        pcs = jnp.zeros((tq, wc), F32)
        for r in range(rep):
            s = jnp.dot(q_ref[0, :, r * LANES:(r + 1) * LANES], kct, preferred_element_type=F32)
            s = jnp.where(cmask, s, MASKED)
            e = jnp.exp2(s - jnp.max(s, axis=-1, keepdims=True))
            p = e * (has_block / jnp.maximum(jnp.sum(e, axis=-1, keepdims=True), 1e-30))
            oc_ref[0, :, r * LANES:(r + 1) * LANES] = jnp.dot(
                p.astype(BF16), vc, preferred_element_type=F32).astype(oc_ref.dtype)
            pcs = pcs + p
        mm = m_ref[:wc, :ws]
        imp = sum(jnp.dot(part.astype(BF16), mm, preferred_element_type=F32) for part in _split3(pcs))
        sblk = lax.broadcasted_iota(jnp.int32, (1, ws), 1)
        forced = (sblk == 0) | (sblk == qblk) | (sblk == qblk - 1)
        free = jnp.where(forced | (sblk > qblk), -jnp.inf, imp)
        sel = _topk_mask(free, sblk, min(SLC_TOPN, ns) - 3)
        bias = jnp.where((sel | forced) & (sblk <= qblk), 0.0, MASKED).astype(sb_ref.dtype)
        for j in range(nsup):
            sb_ref[0, 0, j] = hidden if j >= nv else jnp.concatenate(
                [zeros, bias[:, j * HEAD_DIM:(j + 1) * HEAD_DIM]], axis=-1)

    for nv in range(1, nsup + 1):
        pl.when(i // sps == nv - 1)(functools.partial(prefix, nv))


def nsa_select(q_all, kct, vc, m, rep, tq=512):
    b, s, _ = q_all.shape
    g = kct.shape[1]
    ncp = kct.shape[-1]
    ns = m.shape[-1]
    nsup = ns // HEAD_DIM
    tq = min(tq, s)
    assert SLC_BLOCK == 64 and ns % HEAD_DIM == 0 and (s // tq) % nsup == 0
    return pl.pallas_call(
        functools.partial(_nsa_select_kernel, tq=tq, rep=rep, nsup=nsup, sps=(s // tq) // nsup),
        grid=(b, g, s // tq),
        in_specs=[pl.BlockSpec((1, tq, rep * LANES), lambda bi, gi, i: (bi, i, gi)),
                  pl.BlockSpec((1, 1, LANES, ncp), lambda bi, gi, i: (bi, gi, 0, 0)),
                  pl.BlockSpec((1, 1, ncp, LANES), lambda bi, gi, i: (bi, gi, 0, 0)),
                  pl.BlockSpec((ncp, ns), lambda bi, gi, i: (0, 0))],
        out_specs=[pl.BlockSpec((1, tq, rep * LANES), lambda bi, gi, i: (bi, i, gi)),
                   pl.BlockSpec((1, 1, nsup, tq, LANES), lambda bi, gi, i: (bi, gi, 0, i, 0))],
        out_shape=[jax.ShapeDtypeStruct((b, s, g * rep * LANES), BF16),
                   jax.ShapeDtypeStruct((b, g, nsup, s, LANES), BF16)],
        compiler_params=_cparams("parallel", "parallel", "parallel"),
        name="nsa_select",
    )(q_all, kct, vc, m)


def _moba_select_kernel(q_ref, kt_ref, ind_ref, sb_ref, kbar_sc, *, tq):
    i = pl.program_id(2)

    @pl.when(i == 0)
    def _():
        kbar = jnp.dot(kt_ref[0], ind_ref[...], preferred_element_type=F32) * (1.0 / MOBA_BLOCK)
        kbar_sc[...] = kbar.T

    q = q_ref[0]
    gate = sum(lax.dot_general(part.astype(BF16), q, NT_DIMS, preferred_element_type=F32)
               for part in _split3(kbar_sc[...]))
    qpos = i * tq + lax.broadcasted_iota(jnp.int32, (1, tq), 1)
    cur = qpos >> MOBA_SHIFT
    row = lax.broadcasted_iota(jnp.int32, (LANES, 1), 0)
    blk = row - HEAD_DIM
    past = (blk >= 0) & (blk < cur)
    sel = _topk_mask(jnp.where(past, gate, -jnp.inf), row, MOBA_TOPK, axis=0)
    keep = (blk < 0) | (sel & past) | (blk == cur)
    sb_ref[0, 0, 0] = jnp.where(keep, 0.0, MASKED).T.astype(sb_ref.dtype)


def moba_select(q_all, kt_all, ind, nh, tq=2048):
    b, s, _ = q_all.shape
    assert MOBA_BLOCK == 256 and s // MOBA_BLOCK <= HEAD_DIM
    tq = min(tq, s)
    return pl.pallas_call(
        functools.partial(_moba_select_kernel, tq=tq),
        grid=(b, nh, s // tq),
        in_specs=[pl.BlockSpec((1, tq, LANES), lambda bi, hi, i: (bi, i, hi)),
                  pl.BlockSpec((1, LANES, s), lambda bi, hi, i: (bi, hi, 0)),
                  pl.BlockSpec((s, LANES), lambda bi, hi, i: (0, 0))],
        out_specs=pl.BlockSpec((1, 1, 1, tq, LANES), lambda bi, hi, i: (bi, hi, 0, i, 0)),
        out_shape=jax.ShapeDtypeStruct((b, nh, 1, s, LANES), BF16),
        scratch_shapes=[pltpu.VMEM((LANES, LANES), F32)],
        compiler_params=_cparams("parallel", "parallel", "arbitrary"),
        name="moba_select",
    )(q_all, kt_all, ind)


def _flash_kernel(*refs, tq, tk, rs, rs_main, tps, band, decay, has_extra, has_kx):
    refs = list(refs)
    qa_ref = refs.pop(0)
    ex_ref = refs.pop(0) if has_extra else None
    kt_ref = refs.pop(0)
    kx_ref = refs.pop(0) if has_kx else None
    kmax_sc = refs.pop() if decay else None
    v_ref, o_ref, m_sc, acc_sc = refs
    i = pl.program_id(2)
    m_sc[...] = jnp.full(m_sc.shape, M_INIT, F32)
    acc_sc[...] = jnp.zeros(acc_sc.shape, F32)
    row = lax.broadcasted_iota(jnp.int32, (rs, tk), 0)
    col = lax.broadcasted_iota(jnp.int32, (rs, tk), 1)
    nst = tq // rs
    kpq = tq // tk

    def run(items, rs=rs):
        tiles = {}

        def operands(j):
            if id(j) not in tiles:
                start = pl.multiple_of(j * tk, tk)
                kt = kt_ref[0, :, pl.ds(start, tk)]
                if kx_ref is not None:
                    kt = kt + kx_ref[0, 0, :, pl.ds(start, tk)]
                tiles[id(j)] = (kt, v_ref[0, pl.ds(start, tk), :])
            return tiles[id(j)]

        def logits(item):
            j, r, _ = item
            rows = pl.ds(r * rs, rs)
            qa = qa_ref[0, rows, :]
            if ex_ref is not None:
                qa = qa + ex_ref[0, 0, j // tps, rows, :]
            return jnp.dot(qa, operands(j)[0], preferred_element_type=F32)

        def accumulate(rows, alpha, p, vv):
            acc_sc[rows, :] = alpha * acc_sc[rows, :] + jnp.dot(p, vv, preferred_element_type=F32)

        pending = [logits(it) for it in items[:FLASH_AHEAD]]
        held = None
        for n, (j, r, mask) in enumerate(items):
            rows = pl.ds(r * rs, rs)
            s = pending.pop(0)
            if n + FLASH_AHEAD < len(items):
                pending.append(logits(items[n + FLASH_AHEAD]))
            if held is not None:
                accumulate(*held)
            if mask is not None:
                s = jnp.where(mask, s, MASKED)
            m_prev = m_sc[rows, :]
            m_new = jnp.maximum(m_prev, jnp.max(s, axis=-1, keepdims=True))
            p = jnp.exp2(s - jnp.tile(m_new, (1, tk // LANES)))
            m_sc[rows, :] = m_new
            held = (rows, jnp.exp2(m_prev - m_new), p.astype(BF16), operands(j)[1])
        accumulate(*held)

    def edge_items(dj_list, j_of):
        items = []
        for dj in dj_list:
            j = j_of(dj)
            for r in range(nst):
                off, ko = r * rs, dj * tk
                lo = off - WINDOW + 1 if band else None
                if ko > off + rs - 1 or (band and ko + tk - 1 < lo):
                    continue
                full = ko + tk - 1 <= off and (not band or ko > off + rs - 1 - WINDOW)
                mask = None
                if not full:
                    mask = col + ko <= row + off
                    if band:
                        mask = mask & (col + ko > row + (off - WINDOW))
                items.append((j, r, mask))
        return items

    def body(jjs):
        tiles = [jj * kpq + dj for jj in jjs for dj in range(kpq)]
        run([(j, r, None) for j in tiles for r in range(tq // rs_main)], rs_main)

    def sweep(n, jj_of):
        def group(t, carry):
            body([jj_of(FLASH_GROUP * t + u) for u in range(FLASH_GROUP)])
            return carry
        lax.fori_loop(0, n // FLASH_GROUP, group, 0)

        def single(t, carry):
            body([jj_of(n - n % FLASH_GROUP + t)])
            return carry
        lax.fori_loop(0, n % FLASH_GROUP, single, 0)

    def run_diag():
        diag = [i * kpq + dj for dj in range(kpq)]
        run(edge_items(list(range(kpq)), lambda dj: diag[dj]))

    if decay:
        s_len = kt_ref.shape[-1]

        @pl.when(i == 0)
        def _():
            def chunk(c, best):
                kk = kt_ref[0, :, pl.ds(pl.multiple_of(c * tq, tq), tq)].astype(F32)
                return jnp.maximum(best, jnp.sum(kk * kk, axis=0, keepdims=True))
            ksq = lax.fori_loop(0, s_len // tq, chunk, jnp.zeros((1, tq), F32))
            kmax_sc[...] = jnp.broadcast_to(jnp.sqrt(jnp.max(ksq, axis=1, keepdims=True)), kmax_sc.shape)

        run_diag()
        q = qa_ref[0].astype(F32)
        qn = jnp.sqrt(jnp.sum(q * q, axis=1, keepdims=True))
        slack = jnp.max(qn * kmax_sc[0:1, 0:1] - m_sc[:, 0:1], axis=0, keepdims=True)
        cvec = -jnp.sum(kx_ref[0, 0, HEAD_DIM:HEAD_DIM + 3, :].astype(F32), axis=0, keepdims=True)
        pos = lax.broadcasted_iota(jnp.int32, (1, s_len), 1)
        q0 = i * tq
        c_q0 = jnp.sum(jnp.where(pos == q0, cvec, 0.0), axis=1, keepdims=True)
        dead = (pos < q0) & (slack + c_q0 - cvec <= -DECAY_CUT)
        n_dead = jnp.sum(jnp.where(dead, 1.0, 0.0)).astype(jnp.int32) // tq
        sweep(i - n_dead, lambda t: i - 1 - t)
    elif band:
        pl.when(i == 0)(run_diag)

        @pl.when(i > 0)
        def _():
            near = {dj: i * kpq + dj for dj in range(-(WINDOW // tk), kpq)}
            run(edge_items(sorted(near), near.get))
    else:
        sweep(i, lambda t: t)
        run_diag()
    acc = acc_sc[...]
    o_ref[0] = (acc / acc[:, HEAD_DIM:HEAD_DIM + 1]).astype(o_ref.dtype)


def flash(name, q_all, q0, kt_all, k0, v_all, v0, nh, rep, extra=None, kx=None, band=False,
          decay=False):
    assert not decay or (kx is not None and not band)
    b, s, _ = q_all.shape
    tk = min(WINDOW if band else FLASH_TK, s)
    tq = min(FLASH_TQ, s)
    rs = min(FLASH_STRIP, tq)
    assert s % tq == 0 and tq % tk == 0 and tq % rs == 0 and (not band or WINDOW % tk == 0)
    in_specs = [pl.BlockSpec((1, tq, LANES), lambda bi, hi, i: (bi, i, q0 + hi))]
    args = [q_all]
    tps = 1
    if extra is not None:
        he, nsup = extra.shape[1], extra.shape[2]
        rep_e = nh // he
        assert (s // nsup) % tk == 0
        tps = (s // nsup) // tk
        in_specs.append(pl.BlockSpec((1, 1, nsup, tq, LANES), lambda bi, hi, i: (bi, hi // rep_e, 0, i, 0)))
        args.append(extra)
    in_specs.append(pl.BlockSpec((1, LANES, s), lambda bi, hi, i: (bi, k0 + hi // rep, 0)))
    args.append(kt_all)
    if kx is not None:
        bx, hx = kx.shape[0], kx.shape[1]
        in_specs.append(pl.BlockSpec(
            (1, 1, LANES, s), lambda bi, hi, i: (bi if bx > 1 else 0, hi if hx > 1 else 0, 0, 0)))
        args.append(kx)
    in_specs.append(pl.BlockSpec((1, s, LANES), lambda bi, hi, i: (bi, 0, v0 + hi // rep)))
    args.append(v_all)
    return pl.pallas_call(
        functools.partial(_flash_kernel, tq=tq, tk=tk, rs=rs, rs_main=min(FLASH_STRIP_MAIN, tq),
                          tps=tps, band=band, decay=decay,
                          has_extra=extra is not None, has_kx=kx is not None),
        grid=(b, nh, s // tq),
        in_specs=in_specs,
        out_specs=pl.BlockSpec((1, tq, LANES), lambda bi, hi, i: (bi, i, hi)),
        out_shape=jax.ShapeDtypeStruct((b, s, nh * LANES), BF16),
        scratch_shapes=[pltpu.VMEM((tq, LANES), F32), pltpu.VMEM((tq, LANES), F32)]
        + ([pltpu.VMEM((8, LANES), F32)] if decay else []),
        compiler_params=_cparams("parallel", "parallel", "arbitrary" if decay else "parallel"),
        name=name,
    )(*args)


def _proj_norm_res_kernel(a_ref, w_ref, g_ref, x_ref, o_ref):
    y = jnp.dot(a_ref[...], w_ref[...], preferred_element_type=F32)
    o_ref[...] = x_ref[...] + _rms(y, g_ref[...])


def proj_norm_res(a, w, g, x, tm=512):
    n, k = a.shape
    d = w.shape[1]
    return pl.pallas_call(
        _proj_norm_res_kernel,
        grid=(n // tm,),
        in_specs=[pl.BlockSpec((tm, k), lambda i: (i, 0)),
                  pl.BlockSpec((k, d), lambda i: (0, 0)),
                  pl.BlockSpec((1, d), lambda i: (0, 0)),
                  pl.BlockSpec((tm, d), lambda i: (i, 0))],
        out_specs=pl.BlockSpec((tm, d), lambda i: (i, 0)),
        out_shape=jax.ShapeDtypeStruct((n, d), F32),
        compiler_params=_cparams("parallel"),
        name="proj_norm_res",
    )(a, w, g.reshape(1, d), x)


def _even_out_kernel(of_ref, oc_ref, os_ref, ow_ref, gl_ref, wf_ref, wn_ref, g_ref, x_ref, o_ref,
                     *, nh, g0):
    gate = jax.nn.sigmoid(gl_ref[...])
    parts = []
    for h in range(nh):
        sl = slice(h * LANES, (h + 1) * LANES)
        c = g0 + 3 * h
        parts.append(gate[:, c:c + 1] * oc_ref[:, sl].astype(F32)
                     + gate[:, c + 1:c + 2] * os_ref[:, sl].astype(F32)
                     + gate[:, c + 2:c + 3] * ow_ref[:, sl].astype(F32))
    a = jnp.concatenate(parts, axis=-1).astype(BF16)
    y = (jnp.dot(of_ref[...], wf_ref[...], preferred_element_type=F32)
         + jnp.dot(a, wn_ref[...], preferred_element_type=F32))
    o_ref[...] = x_ref[...] + _rms(y, g_ref[...])


def even_out(o_fox, o_cmp, o_slc, o_win, small, wf, wn, g, x, nh, g0, tm=512):
    n, k = o_fox.shape
    d = wf.shape[1]
    act = pl.BlockSpec((tm, k), lambda i: (i, 0))
    wspec = pl.BlockSpec((k, d), lambda i: (0, 0))
    return pl.pallas_call(
        functools.partial(_even_out_kernel, nh=nh, g0=g0),
        grid=(n // tm,),
        in_specs=[act, act, act, act,
                  pl.BlockSpec((tm, LANES), lambda i: (i, 0)),
                  wspec, wspec,
                  pl.BlockSpec((1, d), lambda i: (0, 0)),
                  pl.BlockSpec((tm, d), lambda i: (i, 0))],
        out_specs=pl.BlockSpec((tm, d), lambda i: (i, 0)),
        out_shape=jax.ShapeDtypeStruct((n, d), F32),
        compiler_params=_cparams("parallel"),
        name="even_out",
    )(o_fox, o_cmp, o_slc, o_win, small, wf, wn, g.reshape(1, d), x)


def _ffn_kernel(x_ref, gpre_ref, wg_ref, wu_ref, wd_ref, gpost_ref, o_ref, h_sc, acc_sc):
    f = pl.program_id(1)

    @pl.when(f == 0)
    def _():
        h_sc[...] = _rms(x_ref[...], gpre_ref[...]).astype(BF16)
        acc_sc[...] = jnp.zeros(acc_sc.shape, F32)

    h = h_sc[...]
    a = jnp.dot(h, wg_ref[...], preferred_element_type=F32)
    u = jnp.dot(h, wu_ref[...], preferred_element_type=F32)
    act = (jax.nn.silu(a) * u).astype(BF16)
    acc_sc[...] += jnp.dot(act, wd_ref[...], preferred_element_type=F32)

    @pl.when(f == pl.num_programs(1) - 1)
    def _():
        o_ref[...] = x_ref[...] + _rms(acc_sc[...], gpost_ref[...])


def ffn(x, gpre, wg, wu, wd, gpost, tm=1024, tf=1408):
    n, d = x.shape
    dff = wg.shape[1]
    assert n % tm == 0 and dff % tf == 0
    return pl.pallas_call(
        _ffn_kernel,
        grid=(n // tm, dff // tf),
        in_specs=[pl.BlockSpec((tm, d), lambda i, f: (i, 0)),
                  pl.BlockSpec((1, d), lambda i, f: (0, 0)),
                  pl.BlockSpec((d, tf), lambda i, f: (0, f)),
                  pl.BlockSpec((d, tf), lambda i, f: (0, f)),
                  pl.BlockSpec((tf, d), lambda i, f: (f, 0)),
                  pl.BlockSpec((1, d), lambda i, f: (0, 0))],
        out_specs=pl.BlockSpec((tm, d), lambda i, f: (i, 0)),
        out_shape=jax.ShapeDtypeStruct((n, d), F32),
        scratch_shapes=[pltpu.VMEM((tm, d), BF16), pltpu.VMEM((tm, d), F32)],
        compiler_params=_cparams("parallel", "arbitrary"),
        name="ffn",
    )(x, gpre.reshape(1, d), wg, wu, wd, gpost.reshape(1, d))


def _rope_tables(s):
    inv = ROPE_THETA ** (-jnp.arange(0, HEAD_DIM, 2, dtype=F32) / HEAD_DIM)
    ang = jnp.arange(s, dtype=F32)[:, None] * inv[None, :]
    cos, sin = jnp.cos(ang), jnp.sin(ang)
    reps = LANES // HEAD_DIM
    cos2 = jnp.tile(jnp.concatenate([cos, cos], -1), (1, reps))
    sin2 = jnp.tile(jnp.concatenate([-sin, sin], -1), (1, reps))
    return (cos2, sin2), (cos2.T[:HEAD_DIM], sin2.T[:HEAD_DIM])


def _slab_cols(w, scale=1.0):
    d, c = w.shape
    w3 = (w * scale).reshape(d, c // HEAD_DIM, HEAD_DIM)
    return jnp.concatenate([w3, jnp.zeros_like(w3)], axis=-1).reshape(d, 2 * c)


def _slab_rows(w):
    return _slab_cols(w.T).T


def _ones_lane(n_slabs):
    one = np.zeros((n_slabs, LANES), np.float32)
    one[:, HEAD_DIM] = 1.0
    return jnp.asarray(one.reshape(-1))


def _block_indicator_rows(s, block):
    blk = (np.arange(s) // block) % HEAD_DIM
    ind = np.zeros((LANES, s), np.float32)
    ind[HEAD_DIM + blk, np.arange(s)] = 1.0
    return jnp.asarray(ind, BF16)[None, None]


def _overlap_matrix(ncp, ns):
    ratio = SLC_BLOCK // CMP_STRIDE
    m = np.arange(ncp)[:, None]
    j = np.arange(ns)[None, :]
    ok = (m >= ratio * j - 1) & (m <= ratio * j + ratio - 1) & (m < ncp - 1)
    return jnp.asarray(ok, BF16)


def _compress_weights(pe, w1, w2, g):
    hid = w1.shape[-1]
    w1r = w1.reshape(2, CMP_STRIDE, HEAD_DIM, hid)
    w1g = jnp.zeros((g, 2, CMP_STRIDE, g, LANES, hid), F32)
    for gi in range(g):
        w1g = w1g.at[gi, :, :, gi, :HEAD_DIM].set(w1r)
    w1g = w1g.reshape(g, 2, CMP_STRIDE * g * LANES, hid).astype(BF16)
    pe2 = jnp.zeros((2, CMP_STRIDE, g, LANES), F32).at[..., :HEAD_DIM].set(
        pe.reshape(2, CMP_STRIDE, 1, HEAD_DIM))
    w2p = jnp.concatenate([w2, jnp.zeros_like(w2)], axis=-1).astype(BF16)
    return pe2.reshape(2, CMP_STRIDE * g * LANES), w1g, w2p


def _out_rows(w):
    return _slab_rows(w).astype(BF16)


def _even_layer(x2, b, s, g_pre, g_post, w_in, b_f, pe_k, w1_k, w2_k, pe_v, w1_v, w2_v, w_out, tabs, tabs_t):
    hf, hn, g = H_FOX, H_NSA, NSA_GROUPS
    hd = HEAD_DIM
    rep = hn // g
    (w_fq, w_fk, w_fv, w_fl, w_nq, w_kc, w_vc, w_ks, w_vs, w_kw, w_vw, w_gl) = jnp.split(
        w_in, list(np.cumsum([hf * hd] * 3 + [hf] + [hn * hd] + [g * hd] * 6)), axis=1)

    w_plain = jnp.concatenate([w_fq * QK_SCALE, w_fv, w_vs, w_vw, w_vc], axis=1).astype(BF16)
    bias_plain = jnp.concatenate([jnp.zeros((hf * LANES,), F32), _ones_lane(hf + 2 * g),
                                  jnp.zeros((g * LANES,), F32)])
    fq0, fv0, vs0, vw0 = 0, hf, 2 * hf, 2 * hf + g
    vc_col = (2 * hf + 2 * g) * LANES
    w_rope = jnp.concatenate([w_nq * QK_SCALE, w_kc], axis=1).astype(BF16)
    n_small = hf + 3 * hn
    w_small = jnp.pad(jnp.concatenate([w_fl, w_gl], axis=1), ((0, 0), (0, LANES - n_small))).astype(BF16)
    p_plain, p_rope, p_small, kt_fox, kt_nsa = project(x2, g_pre, tabs, tabs_t, b, s, [
        ("slab", w_plain, bias_plain, False),
        ("slab", w_rope, jnp.zeros((2 * w_rope.shape[1],), F32), True),
        ("plain", w_small, F32),
        ("keys", w_fk.T.astype(BF16), False),
        ("keys", jnp.concatenate([w_ks, w_kw], axis=1).T.astype(BF16), True),
    ])
    p_plain = p_plain.reshape(b, s, -1)
    p_rope = p_rope.reshape(b, s, -1)

    qx, kx = fox_decay(p_small.reshape(b, s, LANES), jnp.pad(b_f, (0, LANES - hf)).reshape(1, LANES), hf)
    o_fox = flash("flash_fox", p_plain, fq0, kt_fox, 0, p_plain, fv0, hf, 1, extra=qx, kx=kx, decay=True)

    ncp = s // CMP_STRIDE
    ns = s // SLC_BLOCK
    kc = p_rope[:, :, hn * LANES:].reshape(b, ncp, CMP_STRIDE * g * LANES)
    vc = p_plain[:, :, vc_col:].reshape(b, ncp, CMP_STRIDE * g * LANES)
    kcmp = nsa_compress(kc, *_compress_weights(pe_k, w1_k, w2_k, g))
    vcmp = nsa_compress(vc, *_compress_weights(pe_v, w1_v, w2_v, g))
    o_cmp, selb = nsa_select(p_rope, kcmp.transpose(0, 1, 3, 2), vcmp, _overlap_matrix(ncp, ns), rep)
    o_slc = flash("flash_sel", p_rope, 0, kt_nsa, 0, p_plain, vs0, hn, rep, extra=selb,
                  kx=_block_indicator_rows(s, SLC_BLOCK))
    o_win = flash("flash_band", p_rope, 0, kt_nsa, g, p_plain, vw0, hn, rep, band=True)

    n = b * s
    return even_out(o_fox.reshape(n, -1), o_cmp.reshape(n, -1), o_slc.reshape(n, -1),
                    o_win.reshape(n, -1), p_small, _out_rows(w_out[:hf * hd]), _out_rows(w_out[hf * hd:]),
                    g_post, x2, hn, hf)


def _odd_layer(x2, b, s, g_pre, g_post, w_in, w_out, tabs, tabs_t):
    h = H_MOBA
    hd = HEAD_DIM
    d = h * hd
    w_q, w_k, w_v = w_in[:, :d], w_in[:, d:2 * d], w_in[:, 2 * d:]
    q_all, v_all, kt_all = project(x2, g_pre, tabs, tabs_t, b, s, [
        ("slab", (w_q * QK_SCALE).astype(BF16), jnp.zeros((h * LANES,), F32), True),
        ("slab", w_v.astype(BF16), _ones_lane(h), False),
        ("keys", w_k.T.astype(BF16), True),
    ])
    q_all = q_all.reshape(b, s, -1)
    v_all = v_all.reshape(b, s, -1)
    ind = _block_indicator_rows(s, MOBA_BLOCK)
    selb = moba_select(q_all, kt_all, ind[0, 0].T, h)
    o = flash("flash_moba", q_all, 0, kt_all, 0, v_all, 0, h, 1, extra=selb, kx=ind)
    return proj_norm_res(o.reshape(b * s, -1), _out_rows(w_out), g_post, x2)


def kernel(x, ev_w_in, ev_b_f, ev_cmp_pe_k, ev_cmp_w1_k, ev_cmp_w2_k, ev_cmp_pe_v, ev_cmp_w1_v,
           ev_cmp_w2_v, ev_w_out, od_w_in, od_w_out, g_mix_pre, g_mix_post, g_ffn_pre, g_ffn_post,
           ffn_w_gate, ffn_w_up, ffn_w_down):
    b, s, d = x.shape
    depth = g_mix_pre.shape[0]
    tabs, tabs_t = _rope_tables(s)
    x2 = x.reshape(b * s, d)
    for layer in range(depth):
        if layer % 2 == 0:
            e = layer // 2
            x2 = _even_layer(x2, b, s, g_mix_pre[layer], g_mix_post[layer], ev_w_in[e], ev_b_f[e],
                             ev_cmp_pe_k[e], ev_cmp_w1_k[e], ev_cmp_w2_k[e], ev_cmp_pe_v[e],
                             ev_cmp_w1_v[e], ev_cmp_w2_v[e], ev_w_out[e], tabs, tabs_t)
        else:
            o = layer // 2
            x2 = _odd_layer(x2, b, s, g_mix_pre[layer], g_mix_post[layer], od_w_in[o], od_w_out[o],
                            tabs, tabs_t)
        x2 = ffn(x2, g_ffn_pre[layer], ffn_w_gate[layer].astype(BF16), ffn_w_up[layer].astype(BF16),
                 ffn_w_down[layer].astype(BF16), g_ffn_post[layer])
    return x2.reshape(b, s, d)
```

```python
import functools

import jax
import jax.numpy as jnp
import numpy as np
from jax import lax
from jax.experimental import pallas as pl
from jax.experimental.pallas import tpu as pltpu

F32 = jnp.float32
BF16 = jnp.bfloat16

HEAD_DIM = 64
LANES = 128
ROPE_THETA = 10000.0
RMS_EPS = 1e-6
CMP_STRIDE = 16
CMP_LEN = 32
SLC_BLOCK = 64
SLC_TOPN = 16
WINDOW = 512
MOBA_BLOCK = 256
MOBA_TOPK = 3
H_FOX, H_NSA, NSA_GROUPS, H_MOBA = 8, 8, 2, 16
SLC_SHIFT = SLC_BLOCK.bit_length() - 1
MOBA_SHIFT = MOBA_BLOCK.bit_length() - 1
MASKED = -1e30
M_INIT = -1e29
LOG2E = 1.4426950408889634
QK_SCALE = HEAD_DIM ** -0.5 * LOG2E
DECAY_CUT = 100.0 * LOG2E
FLASH_TQ = 2048
FLASH_TQ_EDGE = 1024
FLASH_TK = 512
FLASH_STRIP = 512
FLASH_STRIP_MAIN = 1024
FLASH_GROUP = 2
FLASH_AHEAD = 2
VMEM_LIMIT = 48 * 1024 * 1024
NT_DIMS = (((1,), (1,)), ((), ()))


def _cparams(*sem):
    return pltpu.CompilerParams(dimension_semantics=sem, vmem_limit_bytes=VMEM_LIMIT)


def _rms(x, g):
    return x * lax.rsqrt(jnp.mean(x * x, axis=-1, keepdims=True) + RMS_EPS) * g


def _split3(x):
    hi = x.astype(BF16).astype(F32)
    r = x - hi
    mid = r.astype(BF16).astype(F32)
    lo = (r - mid).astype(BF16).astype(F32)
    return hi, mid, lo


def _slab_project(h, w_ref, b_ref, cos_ref, sin_ref):
    t = jnp.dot(h, w_ref[...], preferred_element_type=F32)
    tm, tn = t.shape
    if cos_ref is not None:
        reps = tn // LANES
        cos = jnp.tile(cos_ref[...], (1, reps))
        sin = jnp.tile(sin_ref[...], (1, reps))
        lane = lax.broadcasted_iota(jnp.int32, t.shape, 1)
        first = (lane & (HEAD_DIM - 1)) < HEAD_DIM // 2
        rot = jnp.where(first, pltpu.roll(t, tn - HEAD_DIM // 2, 1), pltpu.roll(t, HEAD_DIM // 2, 1))
        t = t * cos + rot * sin
    low = lax.broadcasted_iota(jnp.int32, (tm, LANES), 1) < HEAD_DIM
    parts = []
    for pair in range(tn // LANES):
        v = t[:, pair * LANES:(pair + 1) * LANES]
        parts.append(jnp.where(low, v, 0.0))
        parts.append(jnp.where(low, pltpu.roll(v, HEAD_DIM, 1), 0.0))
    return jnp.concatenate(parts, axis=1) + b_ref[...]


def _key_project(h, wt_ref, cos_ref, sin_ref, o_ref):
    t = lax.dot_general(wt_ref[...], h, NT_DIMS, preferred_element_type=F32)
    tn, tm = t.shape
    nheads = tn // HEAD_DIM
    if cos_ref is not None:
        cos = jnp.tile(cos_ref[...], (nheads, 1))
        sin = jnp.tile(sin_ref[...], (nheads, 1))
        r = lax.broadcasted_iota(jnp.int32, t.shape, 0)
        first = (r & (HEAD_DIM - 1)) < HEAD_DIM // 2
        rot = jnp.where(first, pltpu.roll(t, tn - HEAD_DIM // 2, 0), pltpu.roll(t, HEAD_DIM // 2, 0))
        t = t * cos + rot * sin
    spare = jnp.zeros((LANES - HEAD_DIM, tm), o_ref.dtype)
    for hh in range(nheads):
        o_ref[0, hh * LANES:hh * LANES + HEAD_DIM, :] = t[hh * HEAD_DIM:(hh + 1) * HEAD_DIM].astype(o_ref.dtype)
        o_ref[0, hh * LANES + HEAD_DIM:(hh + 1) * LANES, :] = spare


def _project_kernel(x_ref, g_ref, cos_ref, sin_ref, cos_t_ref, sin_t_ref, *refs, kinds):
    n_out = len(kinds)
    ins, outs = list(refs[:-n_out]), refs[-n_out:]
    h = _rms(x_ref[...], g_ref[...]).astype(BF16)
    for (kind, rope), o_ref in zip(kinds, outs):
        if kind == "slab":
            w_ref, b_ref = ins.pop(0), ins.pop(0)
            tabs = (cos_ref, sin_ref) if rope else (None, None)
            o_ref[...] = _slab_project(h, w_ref, b_ref, *tabs).astype(o_ref.dtype)
        elif kind == "plain":
            o_ref[...] = jnp.dot(h, ins.pop(0)[...], preferred_element_type=F32).astype(o_ref.dtype)
        else:
            tabs = (cos_t_ref, sin_t_ref) if rope else (None, None)
            _key_project(h, ins.pop(0), *tabs, o_ref)


def project(x, g, tabs, tabs_t, b, seq, outputs, tm=512):
    n, d = x.shape
    assert n % tm == 0 and seq % tm == 0
    spt = seq // tm
    in_specs = [pl.BlockSpec((tm, d), lambda i: (i, 0)),
                pl.BlockSpec((1, d), lambda i: (0, 0)),
                pl.BlockSpec((tm, LANES), lambda i: (i % spt, 0)),
                pl.BlockSpec((tm, LANES), lambda i: (i % spt, 0)),
                pl.BlockSpec((HEAD_DIM, tm), lambda i: (0, i % spt)),
                pl.BlockSpec((HEAD_DIM, tm), lambda i: (0, i % spt))]
    args = [x, g.reshape(1, d), *tabs, *tabs_t]
    kinds, out_specs, out_shapes = [], [], []
    for out in outputs:
        w = out[1]
        in_specs.append(pl.BlockSpec(w.shape, lambda i: (0, 0)))
        args.append(w)
        if out[0] == "slab":
            nc = 2 * w.shape[1]
            in_specs.append(pl.BlockSpec((1, nc), lambda i: (0, 0)))
            args.append(out[2].reshape(1, nc))
            kinds.append(("slab", out[3]))
            out_specs.append(pl.BlockSpec((tm, nc), lambda i: (i, 0)))
            out_shapes.append(jax.ShapeDtypeStruct((n, nc), BF16))
        elif out[0] == "plain":
            kinds.append(("plain", False))
            out_specs.append(pl.BlockSpec((tm, w.shape[1]), lambda i: (i, 0)))
            out_shapes.append(jax.ShapeDtypeStruct((n, w.shape[1]), out[2]))
        else:
            rows = 2 * w.shape[0]
            kinds.append(("keys", out[2]))
            out_specs.append(pl.BlockSpec((1, rows, tm), lambda i: (i // spt, 0, i % spt)))
            out_shapes.append(jax.ShapeDtypeStruct((b, rows, seq), BF16))
    return pl.pallas_call(
        functools.partial(_project_kernel, kinds=tuple(kinds)),
        grid=(n // tm,),
        in_specs=in_specs,
        out_specs=out_specs,
        out_shape=out_shapes,
        compiler_params=_cparams("parallel"),
        name="project",
    )(*args)


def _fox_decay_kernel(fl_ref, b_ref, pq_ref, pk_ref, oq_ref, ok_ref, carry_sc, *, nh):
    @pl.when(pl.program_id(1) == 0)
    def _():
        carry_sc[...] = jnp.zeros(carry_sc.shape, F32)

    c = jax.nn.log_sigmoid(fl_ref[0] + b_ref[...])
    ts = c.shape[0]
    row = lax.broadcasted_iota(jnp.int32, c.shape, 0)
    lane = lax.broadcasted_iota(jnp.int32, c.shape, 1)
    sh = 1
    while sh < ts:
        c = c + jnp.where(row >= sh, pltpu.roll(c, sh, 0), 0.0)
        sh *= 2
    c = c + carry_sc[0:1, :]
    carry_sc[0:1, :] = c[ts - 1:ts, :]
    hi, mid, lo = _split3(c * LOG2E)
    c3 = jnp.where(lane < nh, hi,
                   jnp.where(lane < 2 * nh, pltpu.roll(mid, nh, 1),
                             jnp.where(lane < 3 * nh, pltpu.roll(lo, 2 * nh, 1),
                                       jnp.where(lane == 3 * nh, 1.0, 0.0)))).astype(BF16)
    for h in range(nh):
        oq_ref[0, h, 0] = jnp.dot(c3, pq_ref[h], preferred_element_type=F32).astype(oq_ref.dtype)
        ok_ref[0, h] = lax.dot_general(pk_ref[h], c3, NT_DIMS,
                                       preferred_element_type=F32).astype(ok_ref.dtype)


def _fox_placement(nh):
    pq = np.zeros((nh, LANES, LANES), np.float32)
    pk = np.zeros((nh, LANES, LANES), np.float32)
    one = 3 * nh
    for h in range(nh):
        for t in range(3):
            pq[h, one, HEAD_DIM + t] = 1.0
            pq[h, t * nh + h, HEAD_DIM + 3 + t] = 1.0
            pk[h, HEAD_DIM + t, t * nh + h] = -1.0
            pk[h, HEAD_DIM + 3 + t, one] = 1.0
    return jnp.asarray(pq, BF16), jnp.asarray(pk, BF16)


def fox_decay(small, b_row, nh, ts=2048):
    b, s, _ = small.shape
    ts = min(ts, s)
    pq, pk = _fox_placement(nh)
    return pl.pallas_call(
        functools.partial(_fox_decay_kernel, nh=nh),
        grid=(b, s // ts),
        in_specs=[pl.BlockSpec((1, ts, LANES), lambda i, j: (i, j, 0)),
                  pl.BlockSpec((1, LANES), lambda i, j: (0, 0)),
                  pl.BlockSpec((nh, LANES, LANES), lambda i, j: (0, 0, 0)),
                  pl.BlockSpec((nh, LANES, LANES), lambda i, j: (0, 0, 0))],
        out_specs=[pl.BlockSpec((1, nh, 1, ts, LANES), lambda i, j: (i, 0, 0, j, 0)),
                   pl.BlockSpec((1, nh, LANES, ts), lambda i, j: (i, 0, 0, j))],
        out_shape=[jax.ShapeDtypeStruct((b, nh, 1, s, LANES), BF16),
                   jax.ShapeDtypeStruct((b, nh, LANES, s), BF16)],
        scratch_shapes=[pltpu.VMEM((8, LANES), F32)],
        compiler_params=_cparams("parallel", "arbitrary"),
        name="fox_decay",
    )(small, b_row, pq, pk)


def _nsa_compress_kernel(t_ref, pe_ref, w1_ref, w2_ref, o_ref):
    t = t_ref[0].astype(F32)
    ncp = t.shape[0]
    a = jnp.dot((t + pe_ref[0:1, :]).astype(BF16), w1_ref[0, 0], preferred_element_type=F32)
    bm = jnp.dot((t + pe_ref[1:2, :]).astype(BF16), w1_ref[0, 1], preferred_element_type=F32)
    pre = a + pltpu.roll(bm, ncp - 1, 0)
    hid = jax.nn.gelu(pre)
    o_ref[0, 0] = jnp.dot(hid.astype(BF16), w2_ref[...], preferred_element_type=F32).astype(o_ref.dtype)


def nsa_compress(t, pe, w1, w2):
    b, ncp, cw = t.shape
    g, _, _, hid = w1.shape
    return pl.pallas_call(
        _nsa_compress_kernel,
        grid=(b, g),
        in_specs=[pl.BlockSpec((1, ncp, cw), lambda i, j: (i, 0, 0)),
                  pl.BlockSpec((2, cw), lambda i, j: (0, 0)),
                  pl.BlockSpec((1, 2, cw, hid), lambda i, j: (j, 0, 0, 0)),
                  pl.BlockSpec((hid, LANES), lambda i, j: (0, 0))],
        out_specs=pl.BlockSpec((1, 1, ncp, LANES), lambda i, j: (i, j, 0, 0)),
        out_shape=jax.ShapeDtypeStruct((b, g, ncp, LANES), BF16),
        compiler_params=_cparams("parallel", "parallel"),
        name="nsa_compress",
    )(t, pe, w1, w2)


def _topk_mask(work, col, k, axis=-1):
    sel = jnp.zeros(work.shape, jnp.bool_)
    col = col.astype(F32)
    for _ in range(k):
        mx = jnp.max(work, axis=axis, keepdims=True)
        first = jnp.min(jnp.where(work == mx, col, jnp.inf), axis=axis, keepdims=True)
        hit = col == first
        sel = jnp.logical_or(sel, hit)
        work = jnp.where(hit, -jnp.inf, work)
    return sel


def _nsa_select_kernel(q_ref, kct_ref, vc_ref, m_ref, oc_ref, sb_ref, *, tq, rep, nsup, sps):
    i = pl.program_id(2)
    ncp = kct_ref.shape[-1]
    ns = m_ref.shape[-1]
    qpos = i * tq + lax.broadcasted_iota(jnp.int32, (tq, 1), 0)
    has_block = jnp.where(qpos >= CMP_LEN - 1, 1.0, 0.0)
    qblk = qpos >> SLC_SHIFT
    zeros = jnp.zeros((tq, HEAD_DIM), sb_ref.dtype)
    hidden = jnp.concatenate([zeros, jnp.full((tq, HEAD_DIM), MASKED, sb_ref.dtype)], axis=-1)

    def prefix(nv):
        wc, ws = ncp * nv // nsup, ns * nv // nsup
        cend = lax.broadcasted_iota(jnp.int32, (1, wc), 1) * CMP_STRIDE + (CMP_LEN - 1)
        cmask = cend <= qpos
        kct = kct_ref[0, 0, :, :wc]
        vc = vc_ref[0, 0, :wc, :]
        pcs = jnp.zeros((tq, wc), F32)
        for r in range(rep):
            s = jnp.dot(q_ref[0, :, r * LANES:(r + 1) * LANES], kct, preferred_element_type=F32)
            s = jnp.where(cmask, s, MASKED)
            e = jnp.exp2(s - jnp.max(s, axis=-1, keepdims=True))
            p = e * (has_block / jnp.maximum(jnp.sum(e, axis=-1, keepdims=True), 1e-30))
            oc_ref[0, :, r * LANES:(r + 1) * LANES] = jnp.dot(
                p.astype(BF16), vc, preferred_element_type=F32).astype(oc_ref.dtype)
            pcs = pcs + p
        mm = m_ref[:wc, :ws]
        imp = sum(jnp.dot(part.astype(BF16), mm, preferred_element_type=F32) for part in _split3(pcs))
        sblk = lax.broadcasted_iota(jnp.int32, (1, ws), 1)
        forced = (sblk == 0) | (sblk == qblk) | (sblk == qblk - 1)
        free = jnp.where(forced | (sblk > qblk), -jnp.inf, imp)
        sel = _topk_mask(free, sblk, min(SLC_TOPN, ns) - 3)
        bias = jnp.where((sel | forced) & (sblk <= qblk), 0.0, MASKED).astype(sb_ref.dtype)
        for j in range(nsup):
            sb_ref[0, 0, j] = hidden if j >= nv else jnp.concatenate(
                [zeros, bias[:, j * HEAD_DIM:(j + 1) * HEAD_DIM]], axis=-1)

    for nv in range(1, nsup + 1):
        pl.when(i // sps == nv - 1)(functools.partial(prefix, nv))


def nsa_select(q_all, kct, vc, m, rep, tq=512):
    b, s, _ = q_all.shape
    g = kct.shape[1]
    ncp = kct.shape[-1]
    ns = m.shape[-1]
    nsup = ns // HEAD_DIM
    tq = min(tq, s)
    assert SLC_BLOCK == 64 and ns % HEAD_DIM == 0 and (s // tq) % nsup == 0
    return pl.pallas_call(
        functools.partial(_nsa_select_kernel, tq=tq, rep=rep, nsup=nsup, sps=(s // tq) // nsup),
        grid=(b, g, s // tq),
        in_specs=[pl.BlockSpec((1, tq, rep * LANES), lambda bi, gi, i: (bi, i, gi)),
                  pl.BlockSpec((1, 1, LANES, ncp), lambda bi, gi, i: (bi, gi, 0, 0)),
                  pl.BlockSpec((1, 1, ncp, LANES), lambda bi, gi, i: (bi, gi, 0, 0)),
                  pl.BlockSpec((ncp, ns), lambda bi, gi, i: (0, 0))],
        out_specs=[pl.BlockSpec((1, tq, rep * LANES), lambda bi, gi, i: (bi, i, gi)),
                   pl.BlockSpec((1, 1, nsup, tq, LANES), lambda bi, gi, i: (bi, gi, 0, i, 0))],
        out_shape=[jax.ShapeDtypeStruct((b, s, g * rep * LANES), BF16),
                   jax.ShapeDtypeStruct((b, g, nsup, s, LANES), BF16)],
        compiler_params=_cparams("parallel", "parallel", "parallel"),
        name="nsa_select",
    )(q_all, kct, vc, m)


def _moba_select_kernel(q_ref, kt_ref, ind_ref, sb_ref, kbar_sc, *, tq):
    i = pl.program_id(2)

    @pl.when(i == 0)
    def _():
        kbar = jnp.dot(kt_ref[0], ind_ref[...], preferred_element_type=F32) * (1.0 / MOBA_BLOCK)
        kbar_sc[...] = kbar.T

    q = q_ref[0]
    gate = sum(lax.dot_general(part.astype(BF16), q, NT_DIMS, preferred_element_type=F32)
               for part in _split3(kbar_sc[...]))
    qpos = i * tq + lax.broadcasted_iota(jnp.int32, (1, tq), 1)
    cur = qpos >> MOBA_SHIFT
    row = lax.broadcasted_iota(jnp.int32, (LANES, 1), 0)
    blk = row - HEAD_DIM
    past = (blk >= 0) & (blk < cur)
    sel = _topk_mask(jnp.where(past, gate, -jnp.inf), row, MOBA_TOPK, axis=0)
    keep = (blk < 0) | (sel & past) | (blk == cur)
    sb_ref[0, 0, 0] = jnp.where(keep, 0.0, MASKED).T.astype(sb_ref.dtype)


def moba_select(q_all, kt_all, ind, nh, tq=2048):
    b, s, _ = q_all.shape
    assert MOBA_BLOCK == 256 and s // MOBA_BLOCK <= HEAD_DIM
    tq = min(tq, s)
    return pl.pallas_call(
        functools.partial(_moba_select_kernel, tq=tq),
        grid=(b, nh, s // tq),
        in_specs=[pl.BlockSpec((1, tq, LANES), lambda bi, hi, i: (bi, i, hi)),
                  pl.BlockSpec((1, LANES, s), lambda bi, hi, i: (bi, hi, 0)),
                  pl.BlockSpec((s, LANES), lambda bi, hi, i: (0, 0))],
        out_specs=pl.BlockSpec((1, 1, 1, tq, LANES), lambda bi, hi, i: (bi, hi, 0, i, 0)),
        out_shape=jax.ShapeDtypeStruct((b, nh, 1, s, LANES), BF16),
        scratch_shapes=[pltpu.VMEM((LANES, LANES), F32)],
        compiler_params=_cparams("parallel", "parallel", "arbitrary"),
        name="moba_select",
    )(q_all, kt_all, ind)


def _flash_kernel(*refs, tq, tk, rs, rs_main, tps, band, decay, has_extra, has_kx):
    refs = list(refs)
    qa_ref = refs.pop(0)
    ex_ref = refs.pop(0) if has_extra else None
    kt_ref = refs.pop(0)
    kx_ref = refs.pop(0) if has_kx else None
    kmax_sc = refs.pop() if decay else None
    v_ref, o_ref, m_sc, acc_sc = refs
    i = pl.program_id(2)
    m_sc[...] = jnp.full(m_sc.shape, M_INIT, F32)
    acc_sc[...] = jnp.zeros(acc_sc.shape, F32)
    row = lax.broadcasted_iota(jnp.int32, (rs, tk), 0)
    col = lax.broadcasted_iota(jnp.int32, (rs, tk), 1)
    nst = tq // rs
    kpq = tq // tk

    def run(items, rs=rs):
        tiles = {}

        def operands(j):
            if id(j) not in tiles:
                start = pl.multiple_of(j * tk, tk)
                kt = kt_ref[0, :, pl.ds(start, tk)]
                if kx_ref is not None:
                    kt = kt + kx_ref[0, 0, :, pl.ds(start, tk)]
                tiles[id(j)] = (kt, v_ref[0, pl.ds(start, tk), :])
            return tiles[id(j)]

        def logits(item):
            j, r, _ = item
            rows = pl.ds(r * rs, rs)
            qa = qa_ref[0, rows, :]
            if ex_ref is not None:
                qa = qa + ex_ref[0, 0, j // tps, rows, :]
            return jnp.dot(qa, operands(j)[0], preferred_element_type=F32)

        def accumulate(rows, alpha, p, vv):
            acc_sc[rows, :] = alpha * acc_sc[rows, :] + jnp.dot(p, vv, preferred_element_type=F32)

        pending = [logits(it) for it in items[:FLASH_AHEAD]]
        held = None
        for n, (j, r, mask) in enumerate(items):
            rows = pl.ds(r * rs, rs)
            s = pending.pop(0)
            if n + FLASH_AHEAD < len(items):
                pending.append(logits(items[n + FLASH_AHEAD]))
            if held is not None:
                accumulate(*held)
            if mask is not None:
                s = jnp.where(mask, s, MASKED)
            m_prev = m_sc[rows, :]
            m_new = jnp.maximum(m_prev, jnp.max(s, axis=-1, keepdims=True))
            p = jnp.exp2(s - jnp.tile(m_new, (1, tk // LANES)))
            m_sc[rows, :] = m_new
            held = (rows, jnp.exp2(m_prev - m_new), p.astype(BF16), operands(j)[1])
        accumulate(*held)

    def edge_items(dj_list, j_of):
        items = []
        for dj in dj_list:
            j = j_of(dj)
            for r in range(nst):
                off, ko = r * rs, dj * tk
                lo = off - WINDOW + 1 if band else None
                if ko > off + rs - 1 or (band and ko + tk - 1 < lo):
                    continue
                full = ko + tk - 1 <= off and (not band or ko > off + rs - 1 - WINDOW)
                mask = None
                if not full:
                    mask = col + ko <= row + off
                    if band:
                        mask = mask & (col + ko > row + (off - WINDOW))
                items.append((j, r, mask))
        return items

    def body(jjs):
        tiles = [jj * kpq + dj for jj in jjs for dj in range(kpq)]
        run([(j, r, None) for j in tiles for r in range(tq // rs_main)], rs_main)

    def sweep(n, jj_of):
        def group(t, carry):
            body([jj_of(FLASH_GROUP * t + u) for u in range(FLASH_GROUP)])
            return carry
        lax.fori_loop(0, n // FLASH_GROUP, group, 0)

        def single(t, carry):
            body([jj_of(n - n % FLASH_GROUP + t)])
            return carry
        lax.fori_loop(0, n % FLASH_GROUP, single, 0)

    def run_diag():
        diag = [i * kpq + dj for dj in range(kpq)]
        run(edge_items(list(range(kpq)), lambda dj: diag[dj]))

    if decay:
        s_len = kt_ref.shape[-1]

        @pl.when(i == 0)
        def _():
            def chunk(c, best):
                kk = kt_ref[0, :, pl.ds(pl.multiple_of(c * tq, tq), tq)].astype(F32)
                return jnp.maximum(best, jnp.sum(kk * kk, axis=0, keepdims=True))
            ksq = lax.fori_loop(0, s_len // tq, chunk, jnp.zeros((1, tq), F32))
            kmax_sc[...] = jnp.broadcast_to(jnp.sqrt(jnp.max(ksq, axis=1, keepdims=True)), kmax_sc.shape)

        run_diag()
        q = qa_ref[0].astype(F32)
        qn = jnp.sqrt(jnp.sum(q * q, axis=1, keepdims=True))
        slack = jnp.max(qn * kmax_sc[0:1, 0:1] - m_sc[:, 0:1], axis=0, keepdims=True)
        cvec = -jnp.sum(kx_ref[0, 0, HEAD_DIM:HEAD_DIM + 3, :].astype(F32), axis=0, keepdims=True)
        pos = lax.broadcasted_iota(jnp.int32, (1, s_len), 1)
        q0 = i * tq
        c_q0 = jnp.sum(jnp.where(pos == q0, cvec, 0.0), axis=1, keepdims=True)
        dead = (pos < q0) & (slack + c_q0 - cvec <= -DECAY_CUT)
        n_dead = jnp.sum(jnp.where(dead, 1.0, 0.0)).astype(jnp.int32) // tq
        sweep(i - n_dead, lambda t: i - 1 - t)
    elif band:
        pl.when(i == 0)(run_diag)

        @pl.when(i > 0)
        def _():
            near = {dj: i * kpq + dj for dj in range(-(WINDOW // tk), kpq)}
            run(edge_items(sorted(near), near.get))
    else:
        sweep(i, lambda t: t)
        run_diag()
    acc = acc_sc[...]
    o_ref[0] = (acc / acc[:, HEAD_DIM:HEAD_DIM + 1]).astype(o_ref.dtype)


def flash(name, q_all, q0, kt_all, k0, v_all, v0, nh, rep, extra=None, kx=None, band=False,
          decay=False):
    assert not decay or (kx is not None and not band)
    b, s, _ = q_all.shape
    tk = min(WINDOW if band else FLASH_TK, s)
    tq = min(FLASH_TQ_EDGE if (band or decay) else FLASH_TQ, s)
    rs = min(FLASH_STRIP, tq)
    assert s % tq == 0 and tq % tk == 0 and tq % rs == 0 and (not band or WINDOW % tk == 0)
    in_specs = [pl.BlockSpec((1, tq, LANES), lambda bi, hi, i: (bi, i, q0 + hi))]
    args = [q_all]
    tps = 1
    if extra is not None:
        he, nsup = extra.shape[1], extra.shape[2]
        rep_e = nh // he
        assert (s // nsup) % tk == 0
        tps = (s // nsup) // tk
        in_specs.append(pl.BlockSpec((1, 1, nsup, tq, LANES), lambda bi, hi, i: (bi, hi // rep_e, 0, i, 0)))
        args.append(extra)
    in_specs.append(pl.BlockSpec((1, LANES, s), lambda bi, hi, i: (bi, k0 + hi // rep, 0)))
    args.append(kt_all)
    if kx is not None:
        bx, hx = kx.shape[0], kx.shape[1]
        in_specs.append(pl.BlockSpec(
            (1, 1, LANES, s), lambda bi, hi, i: (bi if bx > 1 else 0, hi if hx > 1 else 0, 0, 0)))
        args.append(kx)
    in_specs.append(pl.BlockSpec((1, s, LANES), lambda bi, hi, i: (bi, 0, v0 + hi // rep)))
    args.append(v_all)
    return pl.pallas_call(
        functools.partial(_flash_kernel, tq=tq, tk=tk, rs=rs, rs_main=min(FLASH_STRIP_MAIN, tq),
                          tps=tps, band=band, decay=decay,
                          has_extra=extra is not None, has_kx=kx is not None),
        grid=(b, nh, s // tq),
        in_specs=in_specs,
        out_specs=pl.BlockSpec((1, tq, LANES), lambda bi, hi, i: (bi, i, hi)),
        out_shape=jax.ShapeDtypeStruct((b, s, nh * LANES), BF16),
        scratch_shapes=[pltpu.VMEM((tq, LANES), F32), pltpu.VMEM((tq, LANES), F32)]
        + ([pltpu.VMEM((8, LANES), F32)] if decay else []),
        compiler_params=_cparams("parallel", "parallel", "arbitrary" if decay else "parallel"),
        name=name,
    )(*args)


def _proj_norm_res_kernel(a_ref, w_ref, g_ref, x_ref, o_ref):
    y = jnp.dot(a_ref[...], w_ref[...], preferred_element_type=F32)
    o_ref[...] = x_ref[...] + _rms(y, g_ref[...])


def proj_norm_res(a, w, g, x, tm=512):
    n, k = a.shape
    d = w.shape[1]
    return pl.pallas_call(
        _proj_norm_res_kernel,
        grid=(n // tm,),
        in_specs=[pl.BlockSpec((tm, k), lambda i: (i, 0)),
                  pl.BlockSpec((k, d), lambda i: (0, 0)),
                  pl.BlockSpec((1, d), lambda i: (0, 0)),
                  pl.BlockSpec((tm, d), lambda i: (i, 0))],
        out_specs=pl.BlockSpec((tm, d), lambda i: (i, 0)),
        out_shape=jax.ShapeDtypeStruct((n, d), F32),
        compiler_params=_cparams("parallel"),
        name="proj_norm_res",
    )(a, w, g.reshape(1, d), x)


def _even_out_kernel(of_ref, oc_ref, os_ref, ow_ref, gl_ref, wf_ref, wn_ref, g_ref, x_ref, o_ref,
                     *, nh, g0):
    gate = jax.nn.sigmoid(gl_ref[...])
    parts = []
    for h in range(nh):
        sl = slice(h * LANES, (h + 1) * LANES)
        c = g0 + 3 * h
        parts.append(gate[:, c:c + 1] * oc_ref[:, sl].astype(F32)
                     + gate[:, c + 1:c + 2] * os_ref[:, sl].astype(F32)
                     + gate[:, c + 2:c + 3] * ow_ref[:, sl].astype(F32))
    a = jnp.concatenate(parts, axis=-1).astype(BF16)
    y = (jnp.dot(of_ref[...], wf_ref[...], preferred_element_type=F32)
         + jnp.dot(a, wn_ref[...], preferred_element_type=F32))
    o_ref[...] = x_ref[...] + _rms(y, g_ref[...])


def even_out(o_fox, o_cmp, o_slc, o_win, small, wf, wn, g, x, nh, g0, tm=512):
    n, k = o_fox.shape
    d = wf.shape[1]
    act = pl.BlockSpec((tm, k), lambda i: (i, 0))
    wspec = pl.BlockSpec((k, d), lambda i: (0, 0))
    return pl.pallas_call(
        functools.partial(_even_out_kernel, nh=nh, g0=g0),
        grid=(n // tm,),
        in_specs=[act, act, act, act,
                  pl.BlockSpec((tm, LANES), lambda i: (i, 0)),
                  wspec, wspec,
                  pl.BlockSpec((1, d), lambda i: (0, 0)),
                  pl.BlockSpec((tm, d), lambda i: (i, 0))],
        out_specs=pl.BlockSpec((tm, d), lambda i: (i, 0)),
        out_shape=jax.ShapeDtypeStruct((n, d), F32),
        compiler_params=_cparams("parallel"),
        name="even_out",
    )(o_fox, o_cmp, o_slc, o_win, small, wf, wn, g.reshape(1, d), x)


def _ffn_kernel(x_ref, gpre_ref, wg_ref, wu_ref, wd_ref, gpost_ref, o_ref, h_sc, acc_sc):
    f = pl.program_id(1)

    @pl.when(f == 0)
    def _():
        h_sc[...] = _rms(x_ref[...], gpre_ref[...]).astype(BF16)
        acc_sc[...] = jnp.zeros(acc_sc.shape, F32)

    h = h_sc[...]
    a = jnp.dot(h, wg_ref[...], preferred_element_type=F32)
    u = jnp.dot(h, wu_ref[...], preferred_element_type=F32)
    act = (jax.nn.silu(a) * u).astype(BF16)
    acc_sc[...] += jnp.dot(act, wd_ref[...], preferred_element_type=F32)

    @pl.when(f == pl.num_programs(1) - 1)
    def _():
        o_ref[...] = x_ref[...] + _rms(acc_sc[...], gpost_ref[...])


def ffn(x, gpre, wg, wu, wd, gpost, tm=1024, tf=1408):
    n, d = x.shape
    dff = wg.shape[1]
    assert n % tm == 0 and dff % tf == 0
    return pl.pallas_call(
        _ffn_kernel,
        grid=(n // tm, dff // tf),
        in_specs=[pl.BlockSpec((tm, d), lambda i, f: (i, 0)),
                  pl.BlockSpec((1, d), lambda i, f: (0, 0)),
                  pl.BlockSpec((d, tf), lambda i, f: (0, f)),
                  pl.BlockSpec((d, tf), lambda i, f: (0, f)),
                  pl.BlockSpec((tf, d), lambda i, f: (f, 0)),
                  pl.BlockSpec((1, d), lambda i, f: (0, 0))],
        out_specs=pl.BlockSpec((tm, d), lambda i, f: (i, 0)),
        out_shape=jax.ShapeDtypeStruct((n, d), F32),
        scratch_shapes=[pltpu.VMEM((tm, d), BF16), pltpu.VMEM((tm, d), F32)],
        compiler_params=_cparams("parallel", "arbitrary"),
        name="ffn",
    )(x, gpre.reshape(1, d), wg, wu, wd, gpost.reshape(1, d))


def _rope_tables(s):
    inv = ROPE_THETA ** (-jnp.arange(0, HEAD_DIM, 2, dtype=F32) / HEAD_DIM)
    ang = jnp.arange(s, dtype=F32)[:, None] * inv[None, :]
    cos, sin = jnp.cos(ang), jnp.sin(ang)
    reps = LANES // HEAD_DIM
    cos2 = jnp.tile(jnp.concatenate([cos, cos], -1), (1, reps))
    sin2 = jnp.tile(jnp.concatenate([-sin, sin], -1), (1, reps))
    return (cos2, sin2), (cos2.T[:HEAD_DIM], sin2.T[:HEAD_DIM])


def _slab_cols(w, scale=1.0):
    d, c = w.shape
    w3 = (w * scale).reshape(d, c // HEAD_DIM, HEAD_DIM)
    return jnp.concatenate([w3, jnp.zeros_like(w3)], axis=-1).reshape(d, 2 * c)


def _slab_rows(w):
    return _slab_cols(w.T).T


def _ones_lane(n_slabs):
    one = np.zeros((n_slabs, LANES), np.float32)
    one[:, HEAD_DIM] = 1.0
    return jnp.asarray(one.reshape(-1))


def _block_indicator_rows(s, block):
    blk = (np.arange(s) // block) % HEAD_DIM
    ind = np.zeros((LANES, s), np.float32)
    ind[HEAD_DIM + blk, np.arange(s)] = 1.0
    return jnp.asarray(ind, BF16)[None, None]


def _overlap_matrix(ncp, ns):
    ratio = SLC_BLOCK // CMP_STRIDE
    m = np.arange(ncp)[:, None]
    j = np.arange(ns)[None, :]
    ok = (m >= ratio * j - 1) & (m <= ratio * j + ratio - 1) & (m < ncp - 1)
    return jnp.asarray(ok, BF16)


def _compress_weights(pe, w1, w2, g):
    hid = w1.shape[-1]
    w1r = w1.reshape(2, CMP_STRIDE, HEAD_DIM, hid)
    w1g = jnp.zeros((g, 2, CMP_STRIDE, g, LANES, hid), F32)
    for gi in range(g):
        w1g = w1g.at[gi, :, :, gi, :HEAD_DIM].set(w1r)
    w1g = w1g.reshape(g, 2, CMP_STRIDE * g * LANES, hid).astype(BF16)
    pe2 = jnp.zeros((2, CMP_STRIDE, g, LANES), F32).at[..., :HEAD_DIM].set(
        pe.reshape(2, CMP_STRIDE, 1, HEAD_DIM))
    w2p = jnp.concatenate([w2, jnp.zeros_like(w2)], axis=-1).astype(BF16)
    return pe2.reshape(2, CMP_STRIDE * g * LANES), w1g, w2p


def _out_rows(w):
    return _slab_rows(w).astype(BF16)


def _even_layer(x2, b, s, g_pre, g_post, w_in, b_f, pe_k, w1_k, w2_k, pe_v, w1_v, w2_v, w_out, tabs, tabs_t):
    hf, hn, g = H_FOX, H_NSA, NSA_GROUPS
    hd = HEAD_DIM
    rep = hn // g
    (w_fq, w_fk, w_fv, w_fl, w_nq, w_kc, w_vc, w_ks, w_vs, w_kw, w_vw, w_gl) = jnp.split(
        w_in, list(np.cumsum([hf * hd] * 3 + [hf] + [hn * hd] + [g * hd] * 6)), axis=1)

    w_plain = jnp.concatenate([w_fq * QK_SCALE, w_fv, w_vs, w_vw, w_vc], axis=1).astype(BF16)
    bias_plain = jnp.concatenate([jnp.zeros((hf * LANES,), F32), _ones_lane(hf + 2 * g),
                                  jnp.zeros((g * LANES,), F32)])
    fq0, fv0, vs0, vw0 = 0, hf, 2 * hf, 2 * hf + g
    vc_col = (2 * hf + 2 * g) * LANES
    w_rope = jnp.concatenate([w_nq * QK_SCALE, w_kc], axis=1).astype(BF16)
    n_small = hf + 3 * hn
    w_small = jnp.pad(jnp.concatenate([w_fl, w_gl], axis=1), ((0, 0), (0, LANES - n_small))).astype(BF16)
    p_plain, p_rope, p_small, kt_fox, kt_nsa = project(x2, g_pre, tabs, tabs_t, b, s, [
        ("slab", w_plain, bias_plain, False),
        ("slab", w_rope, jnp.zeros((2 * w_rope.shape[1],), F32), True),
        ("plain", w_small, F32),
        ("keys", w_fk.T.astype(BF16), False),
        ("keys", jnp.concatenate([w_ks, w_kw], axis=1).T.astype(BF16), True),
    ])
    p_plain = p_plain.reshape(b, s, -1)
    p_rope = p_rope.reshape(b, s, -1)

    qx, kx = fox_decay(p_small.reshape(b, s, LANES), jnp.pad(b_f, (0, LANES - hf)).reshape(1, LANES), hf)
    o_fox = flash("flash_fox", p_plain, fq0, kt_fox, 0, p_plain, fv0, hf, 1, extra=qx, kx=kx, decay=True)

    ncp = s // CMP_STRIDE
    ns = s // SLC_BLOCK
    kc = p_rope[:, :, hn * LANES:].reshape(b, ncp, CMP_STRIDE * g * LANES)
    vc = p_plain[:, :, vc_col:].reshape(b, ncp, CMP_STRIDE * g * LANES)
    kcmp = nsa_compress(kc, *_compress_weights(pe_k, w1_k, w2_k, g))
    vcmp = nsa_compress(vc, *_compress_weights(pe_v, w1_v, w2_v, g))
    o_cmp, selb = nsa_select(p_rope, kcmp.transpose(0, 1, 3, 2), vcmp, _overlap_matrix(ncp, ns), rep)
    o_slc = flash("flash_sel", p_rope, 0, kt_nsa, 0, p_plain, vs0, hn, rep, extra=selb,
                  kx=_block_indicator_rows(s, SLC_BLOCK))
    o_win = flash("flash_band", p_rope, 0, kt_nsa, g, p_plain, vw0, hn, rep, band=True)

    n = b * s
    return even_out(o_fox.reshape(n, -1), o_cmp.reshape(n, -1), o_slc.reshape(n, -1),
                    o_win.reshape(n, -1), p_small, _out_rows(w_out[:hf * hd]), _out_rows(w_out[hf * hd:]),
                    g_post, x2, hn, hf)


def _odd_layer(x2, b, s, g_pre, g_post, w_in, w_out, tabs, tabs_t):
    h = H_MOBA
    hd = HEAD_DIM
    d = h * hd
    w_q, w_k, w_v = w_in[:, :d], w_in[:, d:2 * d], w_in[:, 2 * d:]
    q_all, v_all, kt_all = project(x2, g_pre, tabs, tabs_t, b, s, [
        ("slab", (w_q * QK_SCALE).astype(BF16), jnp.zeros((h * LANES,), F32), True),
        ("slab", w_v.astype(BF16), _ones_lane(h), False),
        ("keys", w_k.T.astype(BF16), True),
    ])
    q_all = q_all.reshape(b, s, -1)
    v_all = v_all.reshape(b, s, -1)
    ind = _block_indicator_rows(s, MOBA_BLOCK)
    selb = moba_select(q_all, kt_all, ind[0, 0].T, h)
    o = flash("flash_moba", q_all, 0, kt_all, 0, v_all, 0, h, 1, extra=selb, kx=ind)
    return proj_norm_res(o.reshape(b * s, -1), _out_rows(w_out), g_post, x2)


def kernel(x, ev_w_in, ev_b_f, ev_cmp_pe_k, ev_cmp_w1_k, ev_cmp_w2_k, ev_cmp_pe_v, ev_cmp_w1_v,
           ev_cmp_w2_v, ev_w_out, od_w_in, od_w_out, g_mix_pre, g_mix_post, g_ffn_pre, g_ffn_post,
           ffn_w_gate, ffn_w_up, ffn_w_down):
    b, s, d = x.shape
    depth = g_mix_pre.shape[0]
    tabs, tabs_t = _rope_tables(s)
    x2 = x.reshape(b * s, d)
    for layer in range(depth):
        if layer % 2 == 0:
            e = layer // 2
            x2 = _even_layer(x2, b, s, g_mix_pre[layer], g_mix_post[layer], ev_w_in[e], ev_b_f[e],
                             ev_cmp_pe_k[e], ev_cmp_w1_k[e], ev_cmp_w2_k[e], ev_cmp_pe_v[e],
                             ev_cmp_w1_v[e], ev_cmp_w2_v[e], ev_w_out[e], tabs, tabs_t)
        else:
            o = layer // 2
            x2 = _odd_layer(x2, b, s, g_mix_pre[layer], g_mix_post[layer], od_w_in[o], od_w_out[o],
                            tabs, tabs_t)
        x2 = ffn(x2, g_ffn_pre[layer], ffn_w_gate[layer].astype(BF16), ffn_w_up[layer].astype(BF16),
                 ffn_w_down[layer].astype(BF16), g_ffn_post[layer])
    return x2.reshape(b, s, d)
```

```python
import functools

import jax
import jax.numpy as jnp
import numpy as np
from jax import lax
from jax.experimental import pallas as pl
from jax.experimental.pallas import tpu as pltpu

F32 = jnp.float32
BF16 = jnp.bfloat16

HEAD_DIM = 64
LANES = 128
ROPE_THETA = 10000.0
RMS_EPS = 1e-6
CMP_STRIDE = 16
CMP_LEN = 32
SLC_BLOCK = 64
SLC_TOPN = 16
WINDOW = 512
MOBA_BLOCK = 256
MOBA_TOPK = 3
H_FOX, H_NSA, NSA_GROUPS, H_MOBA = 8, 8, 2, 16
SLC_SHIFT = SLC_BLOCK.bit_length() - 1
MOBA_SHIFT = MOBA_BLOCK.bit_length() - 1
MASKED = -1e30
M_INIT = -1e29
LOG2E = 1.4426950408889634
QK_SCALE = HEAD_DIM ** -0.5 * LOG2E
DECAY_CUT = 100.0 * LOG2E
FLASH_TQ = 2048
FLASH_TQ_EDGE = 1024
FLASH_TK = 512
FLASH_STRIP = 512
FLASH_STRIP_MAIN = 1024
FLASH_GROUP = 2
FLASH_AHEAD = 2
VMEM_LIMIT = 48 * 1024 * 1024
NT_DIMS = (((1,), (1,)), ((), ()))


def _cparams(*sem):
    return pltpu.CompilerParams(dimension_semantics=sem, vmem_limit_bytes=VMEM_LIMIT)


def _rms(x, g):
    return x * lax.rsqrt(jnp.mean(x * x, axis=-1, keepdims=True) + RMS_EPS) * g


def _split3(x):
    hi = x.astype(BF16).astype(F32)
    r = x - hi
    mid = r.astype(BF16).astype(F32)
    lo = (r - mid).astype(BF16).astype(F32)
    return hi, mid, lo


def _slab_project(h, w_ref, b_ref, cos_ref, sin_ref):
    t = jnp.dot(h, w_ref[...], preferred_element_type=F32)
    tm, tn = t.shape
    if cos_ref is not None:
        reps = tn // LANES
        cos = jnp.tile(cos_ref[...], (1, reps))
        sin = jnp.tile(sin_ref[...], (1, reps))
        lane = lax.broadcasted_iota(jnp.int32, t.shape, 1)
        first = (lane & (HEAD_DIM - 1)) < HEAD_DIM // 2
        rot = jnp.where(first, pltpu.roll(t, tn - HEAD_DIM // 2, 1), pltpu.roll(t, HEAD_DIM // 2, 1))
        t = t * cos + rot * sin
    low = lax.broadcasted_iota(jnp.int32, (tm, LANES), 1) < HEAD_DIM
    parts = []
    for pair in range(tn // LANES):
        v = t[:, pair * LANES:(pair + 1) * LANES]
        parts.append(jnp.where(low, v, 0.0))
        parts.append(jnp.where(low, pltpu.roll(v, HEAD_DIM, 1), 0.0))
    return jnp.concatenate(parts, axis=1) + b_ref[...]


def _key_project(h, wt_ref, cos_ref, sin_ref, o_ref):
    t = lax.dot_general(wt_ref[...], h, NT_DIMS, preferred_element_type=F32)
    tn, tm = t.shape
    nheads = tn // HEAD_DIM
    if cos_ref is not None:
        cos = jnp.tile(cos_ref[...], (nheads, 1))
        sin = jnp.tile(sin_ref[...], (nheads, 1))
        r = lax.broadcasted_iota(jnp.int32, t.shape, 0)
        first = (r & (HEAD_DIM - 1)) < HEAD_DIM // 2
        rot = jnp.where(first, pltpu.roll(t, tn - HEAD_DIM // 2, 0), pltpu.roll(t, HEAD_DIM // 2, 0))
        t = t * cos + rot * sin
    spare = jnp.zeros((LANES - HEAD_DIM, tm), o_ref.dtype)
    for hh in range(nheads):
        o_ref[0, hh * LANES:hh * LANES + HEAD_DIM, :] = t[hh * HEAD_DIM:(hh + 1) * HEAD_DIM].astype(o_ref.dtype)
        o_ref[0, hh * LANES + HEAD_DIM:(hh + 1) * LANES, :] = spare


def _project_kernel(x_ref, g_ref, cos_ref, sin_ref, cos_t_ref, sin_t_ref, *refs, kinds):
    n_out = len(kinds)
    ins, outs = list(refs[:-n_out]), refs[-n_out:]
    h = _rms(x_ref[...], g_ref[...]).astype(BF16)
    for (kind, rope), o_ref in zip(kinds, outs):
        if kind == "slab":
            w_ref, b_ref = ins.pop(0), ins.pop(0)
            tabs = (cos_ref, sin_ref) if rope else (None, None)
            o_ref[...] = _slab_project(h, w_ref, b_ref, *tabs).astype(o_ref.dtype)
        elif kind == "plain":
            o_ref[...] = jnp.dot(h, ins.pop(0)[...], preferred_element_type=F32).astype(o_ref.dtype)
        else:
            tabs = (cos_t_ref, sin_t_ref) if rope else (None, None)
            _key_project(h, ins.pop(0), *tabs, o_ref)


def project(x, g, tabs, tabs_t, b, seq, outputs, tm=512):
    n, d = x.shape
    assert n % tm == 0 and seq % tm == 0
    spt = seq // tm
    in_specs = [pl.BlockSpec((tm, d), lambda i: (i, 0)),
                pl.BlockSpec((1, d), lambda i: (0, 0)),
                pl.BlockSpec((tm, LANES), lambda i: (i % spt, 0)),
                pl.BlockSpec((tm, LANES), lambda i: (i % spt, 0)),
                pl.BlockSpec((HEAD_DIM, tm), lambda i: (0, i % spt)),
                pl.BlockSpec((HEAD_DIM, tm), lambda i: (0, i % spt))]
    args = [x, g.reshape(1, d), *tabs, *tabs_t]
    kinds, out_specs, out_shapes = [], [], []
    for out in outputs:
        w = out[1]
        in_specs.append(pl.BlockSpec(w.shape, lambda i: (0, 0)))
        args.append(w)
        if out[0] == "slab":
            nc = 2 * w.shape[1]
            in_specs.append(pl.BlockSpec((1, nc), lambda i: (0, 0)))
            args.append(out[2].reshape(1, nc))
            kinds.append(("slab", out[3]))
            out_specs.append(pl.BlockSpec((tm, nc), lambda i: (i, 0)))
            out_shapes.append(jax.ShapeDtypeStruct((n, nc), BF16))
        elif out[0] == "plain":
            kinds.append(("plain", False))
            out_specs.append(pl.BlockSpec((tm, w.shape[1]), lambda i: (i, 0)))
            out_shapes.append(jax.ShapeDtypeStruct((n, w.shape[1]), out[2]))
        else:
            rows = 2 * w.shape[0]
            kinds.append(("keys", out[2]))
            out_specs.append(pl.BlockSpec((1, rows, tm), lambda i: (i // spt, 0, i % spt)))
            out_shapes.append(jax.ShapeDtypeStruct((b, rows, seq), BF16))
    return pl.pallas_call(
        functools.partial(_project_kernel, kinds=tuple(kinds)),
        grid=(n // tm,),
        in_specs=in_specs,
        out_specs=out_specs,
        out_shape=out_shapes,
        compiler_params=_cparams("parallel"),
        name="project",
    )(*args)


def _fox_decay_kernel(fl_ref, b_ref, pq_ref, pk_ref, oq_ref, ok_ref, carry_sc, *, nh):
    @pl.when(pl.program_id(1) == 0)
    def _():
        carry_sc[...] = jnp.zeros(carry_sc.shape, F32)

    c = jax.nn.log_sigmoid(fl_ref[0] + b_ref[...])
    ts = c.shape[0]
    row = lax.broadcasted_iota(jnp.int32, c.shape, 0)
    lane = lax.broadcasted_iota(jnp.int32, c.shape, 1)
    sh = 1
    while sh < ts:
        c = c + jnp.where(row >= sh, pltpu.roll(c, sh, 0), 0.0)
        sh *= 2
    c = c + carry_sc[0:1, :]
    carry_sc[0:1, :] = c[ts - 1:ts, :]
    hi, mid, lo = _split3(c * LOG2E)
    c3 = jnp.where(lane < nh, hi,
                   jnp.where(lane < 2 * nh, pltpu.roll(mid, nh, 1),
                             jnp.where(lane < 3 * nh, pltpu.roll(lo, 2 * nh, 1),
                                       jnp.where(lane == 3 * nh, 1.0, 0.0)))).astype(BF16)
    for h in range(nh):
        oq_ref[0, h, 0] = jnp.dot(c3, pq_ref[h], preferred_element_type=F32).astype(oq_ref.dtype)
        ok_ref[0, h] = lax.dot_general(pk_ref[h], c3, NT_DIMS,
                                       preferred_element_type=F32).astype(ok_ref.dtype)


def _fox_placement(nh):
    pq = np.zeros((nh, LANES, LANES), np.float32)
    pk = np.zeros((nh, LANES, LANES), np.float32)
    one = 3 * nh
    for h in range(nh):
        for t in range(3):
            pq[h, one, HEAD_DIM + t] = 1.0
            pq[h, t * nh + h, HEAD_DIM + 3 + t] = 1.0
            pk[h, HEAD_DIM + t, t * nh + h] = -1.0
            pk[h, HEAD_DIM + 3 + t, one] = 1.0
    return jnp.asarray(pq, BF16), jnp.asarray(pk, BF16)


def fox_decay(small, b_row, nh, ts=2048):
    b, s, _ = small.shape
    ts = min(ts, s)
    pq, pk = _fox_placement(nh)
    return pl.pallas_call(
        functools.partial(_fox_decay_kernel, nh=nh),
        grid=(b, s // ts),
        in_specs=[pl.BlockSpec((1, ts, LANES), lambda i, j: (i, j, 0)),
                  pl.BlockSpec((1, LANES), lambda i, j: (0, 0)),
                  pl.BlockSpec((nh, LANES, LANES), lambda i, j: (0, 0, 0)),
                  pl.BlockSpec((nh, LANES, LANES), lambda i, j: (0, 0, 0))],
        out_specs=[pl.BlockSpec((1, nh, 1, ts, LANES), lambda i, j: (i, 0, 0, j, 0)),
                   pl.BlockSpec((1, nh, LANES, ts), lambda i, j: (i, 0, 0, j))],
        out_shape=[jax.ShapeDtypeStruct((b, nh, 1, s, LANES), BF16),
                   jax.ShapeDtypeStruct((b, nh, LANES, s), BF16)],
        scratch_shapes=[pltpu.VMEM((8, LANES), F32)],
        compiler_params=_cparams("parallel", "arbitrary"),
        name="fox_decay",
    )(small, b_row, pq, pk)


def _nsa_compress_kernel(t_ref, pe_ref, w1_ref, w2_ref, o_ref):
    t = t_ref[0].astype(F32)
    ncp = t.shape[0]
    a = jnp.dot((t + pe_ref[0:1, :]).astype(BF16), w1_ref[0, 0], preferred_element_type=F32)
    bm = jnp.dot((t + pe_ref[1:2, :]).astype(BF16), w1_ref[0, 1], preferred_element_type=F32)
    pre = a + pltpu.roll(bm, ncp - 1, 0)
    hid = jax.nn.gelu(pre)
    o_ref[0, 0] = jnp.dot(hid.astype(BF16), w2_ref[...], preferred_element_type=F32).astype(o_ref.dtype)


def nsa_compress(t, pe, w1, w2):
    b, ncp, cw = t.shape
    g, _, _, hid = w1.shape
    return pl.pallas_call(
        _nsa_compress_kernel,
        grid=(b, g),
        in_specs=[pl.BlockSpec((1, ncp, cw), lambda i, j: (i, 0, 0)),
                  pl.BlockSpec((2, cw), lambda i, j: (0, 0)),
                  pl.BlockSpec((1, 2, cw, hid), lambda i, j: (j, 0, 0, 0)),
                  pl.BlockSpec((hid, LANES), lambda i, j: (0, 0))],
        out_specs=pl.BlockSpec((1, 1, ncp, LANES), lambda i, j: (i, j, 0, 0)),
        out_shape=jax.ShapeDtypeStruct((b, g, ncp, LANES), BF16),
        compiler_params=_cparams("parallel", "parallel"),
        name="nsa_compress",
    )(t, pe, w1, w2)


def _topk_mask(work, col, k, axis=-1):
    sel = jnp.zeros(work.shape, jnp.bool_)
    col = col.astype(F32)
    for _ in range(k):
        mx = jnp.max(work, axis=axis, keepdims=True)
        first = jnp.min(jnp.where(work == mx, col, jnp.inf), axis=axis, keepdims=True)
        hit = col == first
        sel = jnp.logical_or(sel, hit)
        work = jnp.where(hit, -jnp.inf, work)
    return sel


def _nsa_select_kernel(q_ref, kct_ref, vc_ref, m_ref, oc_ref, sb_ref, *, tq, rep, nsup, sps):
    i = pl.program_id(2)
    ncp = kct_ref.shape[-1]
    ns = m_ref.shape[-1]
    qpos = i * tq + lax.broadcasted_iota(jnp.int32, (tq, 1), 0)
    has_block = jnp.where(qpos >= CMP_LEN - 1, 1.0, 0.0)
    qblk = qpos >> SLC_SHIFT
    zeros = jnp.zeros((tq, HEAD_DIM), sb_ref.dtype)
    hidden = jnp.concatenate([zeros, jnp.full((tq, HEAD_DIM), MASKED, sb_ref.dtype)], axis=-1)

    def prefix(nv):
        wc, ws = ncp * nv // nsup, ns * nv // nsup
        cend = lax.broadcasted_iota(jnp.int32, (1, wc), 1) * CMP_STRIDE + (CMP_LEN - 1)
        cmask = cend <= qpos
        kct = kct_ref[0, 0, :, :wc]
        vc = vc_ref[0, 0, :wc, :]
        pcs = jnp.zeros((tq, wc), F32)
        for r in range(rep):
            s = jnp.dot(q_ref[0, :, r * LANES:(r + 1) * LANES], kct, preferred_element_type=F32)
            s = jnp.where(cmask, s, MASKED)
            e = jnp.exp2(s - jnp.max(s, axis=-1, keepdims=True))
            p = e * (has_block / jnp.maximum(jnp.sum(e, axis=-1, keepdims=True), 1e-30))
            oc_ref[0, :, r * LANES:(r + 1) * LANES] = jnp.dot(
                p.astype(BF16), vc, preferred_element_type=F32).astype(oc_ref.dtype)
            pcs = pcs + p
        mm = m_ref[:wc, :ws]
        imp = sum(jnp.dot(part.astype(BF16), mm, preferred_element_type=F32) for part in _split3(pcs))
        sblk = lax.broadcasted_iota(jnp.int32, (1, ws), 1)
        forced = (sblk == 0) | (sblk == qblk) | (sblk == qblk - 1)
        free = jnp.where(forced | (sblk > qblk), -jnp.inf, imp)
        sel = _topk_mask(free, sblk, min(SLC_TOPN, ns) - 3)
        bias = jnp.where((sel | forced) & (sblk <= qblk), 0.0, MASKED).astype(sb_ref.dtype)
        for j in range(nsup):
            sb_ref[0, 0, j] = hidden if j >= nv else jnp.concatenate(
                [zeros, bias[:, j * HEAD_DIM:(j + 1) * HEAD_DIM]], axis=-1)

    for nv in range(1, nsup + 1):
        pl.when(i // sps == nv - 1)(functools.partial(prefix, nv))


def nsa_select(q_all, kct, vc, m, rep, tq=512):
    b, s, _ = q_all.shape
    g = kct.shape[1]
    ncp = kct.shape[-1]
    ns = m.shape[-1]
    nsup = ns // HEAD_DIM
    tq = min(tq, s)
    assert SLC_BLOCK == 64 and ns % HEAD_DIM == 0 and (s // tq) % nsup == 0
    return pl.pallas_call(
        functools.partial(_nsa_select_kernel, tq=tq, rep=rep, nsup=nsup, sps=(s // tq) // nsup),
        grid=(b, g, s // tq),
        in_specs=[pl.BlockSpec((1, tq, rep * LANES), lambda bi, gi, i: (bi, i, gi)),
                  pl.BlockSpec((1, 1, LANES, ncp), lambda bi, gi, i: (bi, gi, 0, 0)),
                  pl.BlockSpec((1, 1, ncp, LANES), lambda bi, gi, i: (bi, gi, 0, 0)),
                  pl.BlockSpec((ncp, ns), lambda bi, gi, i: (0, 0))],
        out_specs=[pl.BlockSpec((1, tq, rep * LANES), lambda bi, gi, i: (bi, i, gi)),
                   pl.BlockSpec((1, 1, nsup, tq, LANES), lambda bi, gi, i: (bi, gi, 0, i, 0))],
        out_shape=[jax.ShapeDtypeStruct((b, s, g * rep * LANES), BF16),
                   jax.ShapeDtypeStruct((b, g, nsup, s, LANES), BF16)],
        compiler_params=_cparams("parallel", "parallel", "parallel"),
        name="nsa_select",
    )(q_all, kct, vc, m)


def _moba_select_kernel(q_ref, kt_ref, ind_ref, sb_ref, kbar_sc, *, tq):
    i = pl.program_id(2)

    @pl.when(i == 0)
    def _():
        kbar = jnp.dot(kt_ref[0], ind_ref[...], preferred_element_type=F32) * (1.0 / MOBA_BLOCK)
        kbar_sc[...] = kbar.T

    q = q_ref[0]
    gate = sum(lax.dot_general(part.astype(BF16), q, NT_DIMS, preferred_element_type=F32)
               for part in _split3(kbar_sc[...]))
    qpos = i * tq + lax.broadcasted_iota(jnp.int32, (1, tq), 1)
    cur = qpos >> MOBA_SHIFT
    row = lax.broadcasted_iota(jnp.int32, (LANES, 1), 0)
    blk = row - HEAD_DIM
    past = (blk >= 0) & (blk < cur)
    sel = _topk_mask(jnp.where(past, gate, -jnp.inf), row, MOBA_TOPK, axis=0)
    keep = (blk < 0) | (sel & past) | (blk == cur)
    sb_ref[0, 0, 0] = jnp.where(keep, 0.0, MASKED).T.astype(sb_ref.dtype)


def moba_select(q_all, kt_all, ind, nh, tq=2048):
    b, s, _ = q_all.shape
    assert MOBA_BLOCK == 256 and s // MOBA_BLOCK <= HEAD_DIM
    tq = min(tq, s)
    return pl.pallas_call(
        functools.partial(_moba_select_kernel, tq=tq),
        grid=(b, nh, s // tq),
        in_specs=[pl.BlockSpec((1, tq, LANES), lambda bi, hi, i: (bi, i, hi)),
                  pl.BlockSpec((1, LANES, s), lambda bi, hi, i: (bi, hi, 0)),
                  pl.BlockSpec((s, LANES), lambda bi, hi, i: (0, 0))],
        out_specs=pl.BlockSpec((1, 1, 1, tq, LANES), lambda bi, hi, i: (bi, hi, 0, i, 0)),
        out_shape=jax.ShapeDtypeStruct((b, nh, 1, s, LANES), BF16),
        scratch_shapes=[pltpu.VMEM((LANES, LANES), F32)],
        compiler_params=_cparams("parallel", "parallel", "arbitrary"),
        name="moba_select",
    )(q_all, kt_all, ind)


def _flash_kernel(*refs, tq, tk, rs, rs_main, tps, band, decay, has_extra, has_kx):
    refs = list(refs)
    qa_ref = refs.pop(0)
    ex_ref = refs.pop(0) if has_extra else None
    kt_ref = refs.pop(0)
    kx_ref = refs.pop(0) if has_kx else None
    kmax_sc = refs.pop() if decay else None
    v_ref, o_ref, m_sc, acc_sc = refs
    i = pl.program_id(2)
    m_sc[...] = jnp.full(m_sc.shape, M_INIT, F32)
    acc_sc[...] = jnp.zeros(acc_sc.shape, F32)
    row = lax.broadcasted_iota(jnp.int32, (rs, tk), 0)
    col = lax.broadcasted_iota(jnp.int32, (rs, tk), 1)
    nst = tq // rs
    kpq = tq // tk

    def run(items, rs=rs):
        tiles = {}

        def operands(j):
            if id(j) not in tiles:
                start = pl.multiple_of(j * tk, tk)
                kt = kt_ref[0, :, pl.ds(start, tk)]
                if kx_ref is not None:
                    kt = kt + kx_ref[0, 0, :, pl.ds(start, tk)]
                tiles[id(j)] = (kt, v_ref[0, pl.ds(start, tk), :])
            return tiles[id(j)]

        def logits(item):
            j, r, _ = item
            rows = pl.ds(r * rs, rs)
            qa = qa_ref[0, rows, :]
            if ex_ref is not None:
                qa = qa + ex_ref[0, 0, j // tps, rows, :]
            return jnp.dot(qa, operands(j)[0], preferred_element_type=F32)

        def accumulate(rows, alpha, p, vv):
            acc_sc[rows, :] = alpha * acc_sc[rows, :] + jnp.dot(p, vv, preferred_element_type=F32)

        pending = [logits(it) for it in items[:FLASH_AHEAD]]
        held = None
        for n, (j, r, mask) in enumerate(items):
            rows = pl.ds(r * rs, rs)
            s = pending.pop(0)
            if n + FLASH_AHEAD < len(items):
                pending.append(logits(items[n + FLASH_AHEAD]))
            if held is not None:
                accumulate(*held)
            if mask is not None:
                s = jnp.where(mask, s, MASKED)
            m_prev = m_sc[rows, :]
            m_new = jnp.maximum(m_prev, jnp.max(s, axis=-1, keepdims=True))
            p = jnp.exp2(s - jnp.tile(m_new, (1, tk // LANES)))
            m_sc[rows, :] = m_new
            held = (rows, jnp.exp2(m_prev - m_new), p.astype(BF16), operands(j)[1])
        accumulate(*held)

    def edge_items(dj_list, j_of):
        items = []
        for dj in dj_list:
            j = j_of(dj)
            for r in range(nst):
                off, ko = r * rs, dj * tk
                lo = off - WINDOW + 1 if band else None
                if ko > off + rs - 1 or (band and ko + tk - 1 < lo):
                    continue
                full = ko + tk - 1 <= off and (not band or ko > off + rs - 1 - WINDOW)
                mask = None
                if not full:
                    mask = col + ko <= row + off
                    if band:
                        mask = mask & (col + ko > row + (off - WINDOW))
                items.append((j, r, mask))
        return items

    def body(jjs):
        tiles = [jj * kpq + dj for jj in jjs for dj in range(kpq)]
        run([(j, r, None) for j in tiles for r in range(tq // rs_main)], rs_main)

    def sweep(n, jj_of):
        def group(t, carry):
            body([jj_of(FLASH_GROUP * t + u) for u in range(FLASH_GROUP)])
            return carry
        lax.fori_loop(0, n // FLASH_GROUP, group, 0)

        def single(t, carry):
            body([jj_of(n - n % FLASH_GROUP + t)])
            return carry
        lax.fori_loop(0, n % FLASH_GROUP, single, 0)

    def run_diag():
        diag = [i * kpq + dj for dj in range(kpq)]
        run(edge_items(list(range(kpq)), lambda dj: diag[dj]))

    if decay:
        s_len = kt_ref.shape[-1]

        @pl.when(i == 0)
        def _():
            def chunk(c, best):
                kk = kt_ref[0, :, pl.ds(pl.multiple_of(c * tq, tq), tq)].astype(F32)
                return jnp.maximum(best, jnp.sum(kk * kk, axis=0, keepdims=True))
            ksq = lax.fori_loop(0, s_len // tq, chunk, jnp.zeros((1, tq), F32))
            kmax_sc[...] = jnp.broadcast_to(jnp.sqrt(jnp.max(ksq, axis=1, keepdims=True)), kmax_sc.shape)

        run_diag()
        q = qa_ref[0].astype(F32)
        qn = jnp.sqrt(jnp.sum(q * q, axis=1, keepdims=True))
        slack = jnp.max(qn * kmax_sc[0:1, 0:1] - m_sc[:, 0:1], axis=0, keepdims=True)
        cvec = -jnp.sum(kx_ref[0, 0, HEAD_DIM:HEAD_DIM + 3, :].astype(F32), axis=0, keepdims=True)
        pos = lax.broadcasted_iota(jnp.int32, (1, s_len), 1)
        q0 = i * tq
        c_q0 = jnp.sum(jnp.where(pos == q0, cvec, 0.0), axis=1, keepdims=True)
        dead = (pos < q0) & (slack + c_q0 - cvec <= -DECAY_CUT)
        n_dead = jnp.sum(jnp.where(dead, 1.0, 0.0)).astype(jnp.int32) // tq
        sweep(i - n_dead, lambda t: i - 1 - t)
    elif band:
        pl.when(i == 0)(run_diag)

        @pl.when(i > 0)
        def _():
            near = {dj: i * kpq + dj for dj in range(-(WINDOW // tk), kpq)}
            run(edge_items(sorted(near), near.get))
    else:
        sweep(i, lambda t: t)
        run_diag()
    acc = acc_sc[...]
    o_ref[0] = (acc / acc[:, HEAD_DIM:HEAD_DIM + 1]).astype(o_ref.dtype)


def flash(name, q_all, q0, kt_all, k0, v_all, v0, nh, rep, extra=None, kx=None, band=False,
          decay=False):
    assert not decay or (kx is not None and not band)
    b, s, _ = q_all.shape
    tk = min(WINDOW if band else FLASH_TK, s)
    tq = min(FLASH_TQ_EDGE if (band or decay) else FLASH_TQ, s)
    rs = min(FLASH_STRIP, tq)
    assert s % tq == 0 and tq % tk == 0 and tq % rs == 0 and (not band or WINDOW % tk == 0)
    in_specs = [pl.BlockSpec((1, tq, LANES), lambda bi, hi, i: (bi, i, q0 + hi))]
    args = [q_all]
    tps = 1
    if extra is not None:
        he, nsup = extra.shape[1], extra.shape[2]
        rep_e = nh // he
        assert (s // nsup) % tk == 0
        tps = (s // nsup) // tk
        in_specs.append(pl.BlockSpec((1, 1, nsup, tq, LANES), lambda bi, hi, i: (bi, hi // rep_e, 0, i, 0)))
        args.append(extra)
    in_specs.append(pl.BlockSpec((1, LANES, s), lambda bi, hi, i: (bi, k0 + hi // rep, 0)))
    args.append(kt_all)
    if kx is not None:
        bx, hx = kx.shape[0], kx.shape[1]
        in_specs.append(pl.BlockSpec(
            (1, 1, LANES, s), lambda bi, hi, i: (bi if bx > 1 else 0, hi if hx > 1 else 0, 0, 0)))
        args.append(kx)
    in_specs.append(pl.BlockSpec((1, s, LANES), lambda bi, hi, i: (bi, 0, v0 + hi // rep)))
    args.append(v_all)
    return pl.pallas_call(
        functools.partial(_flash_kernel, tq=tq, tk=tk, rs=rs, rs_main=min(FLASH_STRIP_MAIN, tq),
                          tps=tps, band=band, decay=decay,
                          has_extra=extra is not None, has_kx=kx is not None),
        grid=(b, nh, s // tq),
        in_specs=in_specs,
        out_specs=pl.BlockSpec((1, tq, LANES), lambda bi, hi, i: (bi, i, hi)),
        out_shape=jax.ShapeDtypeStruct((b, s, nh * LANES), BF16),
        scratch_shapes=[pltpu.VMEM((tq, LANES), F32), pltpu.VMEM((tq, LANES), F32)]
        + ([pltpu.VMEM((8, LANES), F32)] if decay else []),
        compiler_params=_cparams("parallel", "parallel", "arbitrary" if decay else "parallel"),
        name=name,
    )(*args)


def _proj_norm_res_kernel(a_ref, w_ref, g_ref, x_ref, o_ref):
    y = jnp.dot(a_ref[...], w_ref[...], preferred_element_type=F32)
    o_ref[...] = x_ref[...] + _rms(y, g_ref[...])


def proj_norm_res(a, w, g, x, tm=512):
    n, k = a.shape
    d = w.shape[1]
    return pl.pallas_call(
        _proj_norm_res_kernel,
        grid=(n // tm,),
        in_specs=[pl.BlockSpec((tm, k), lambda i: (i, 0)),
                  pl.BlockSpec((k, d), lambda i: (0, 0)),
                  pl.BlockSpec((1, d), lambda i: (0, 0)),
                  pl.BlockSpec((tm, d), lambda i: (i, 0))],
        out_specs=pl.BlockSpec((tm, d), lambda i: (i, 0)),
        out_shape=jax.ShapeDtypeStruct((n, d), F32),
        compiler_params=_cparams("parallel"),
        name="proj_norm_res",
    )(a, w, g.reshape(1, d), x)


def _even_out_kernel(of_ref, oc_ref, os_ref, ow_ref, gl_ref, wf_ref, wn_ref, g_ref, x_ref, o_ref,
                     *, nh, g0):
    gate = jax.nn.sigmoid(gl_ref[...])
    parts = []
    for h in range(nh):
        sl = slice(h * LANES, (h + 1) * LANES)
        c = g0 + 3 * h
        parts.append(gate[:, c:c + 1] * oc_ref[:, sl].astype(F32)
                     + gate[:, c + 1:c + 2] * os_ref[:, sl].astype(F32)
                     + gate[:, c + 2:c + 3] * ow_ref[:, sl].astype(F32))
    a = jnp.concatenate(parts, axis=-1).astype(BF16)
    y = (jnp.dot(of_ref[...], wf_ref[...], preferred_element_type=F32)
         + jnp.dot(a, wn_ref[...], preferred_element_type=F32))
    o_ref[...] = x_ref[...] + _rms(y, g_ref[...])


def even_out(o_fox, o_cmp, o_slc, o_win, small, wf, wn, g, x, nh, g0, tm=512):
    n, k = o_fox.shape
    d = wf.shape[1]
    act = pl.BlockSpec((tm, k), lambda i: (i, 0))
    wspec = pl.BlockSpec((k, d), lambda i: (0, 0))
    return pl.pallas_call(
        functools.partial(_even_out_kernel, nh=nh, g0=g0),
        grid=(n // tm,),
        in_specs=[act, act, act, act,
                  pl.BlockSpec((tm, LANES), lambda i: (i, 0)),
                  wspec, wspec,
                  pl.BlockSpec((1, d), lambda i: (0, 0)),
                  pl.BlockSpec((tm, d), lambda i: (i, 0))],
        out_specs=pl.BlockSpec((tm, d), lambda i: (i, 0)),
        out_shape=jax.ShapeDtypeStruct((n, d), F32),
        compiler_params=_cparams("parallel"),
        name="even_out",
    )(o_fox, o_cmp, o_slc, o_win, small, wf, wn, g.reshape(1, d), x)


def _ffn_kernel(x_ref, gpre_ref, wg_ref, wu_ref, wd_ref, gpost_ref, o_ref, *, tf):
    x = x_ref[...]
    h = _rms(x, gpre_ref[...]).astype(BF16)
    y = None
    for c in range(wg_ref.shape[1] // tf):
        cols = slice(c * tf, (c + 1) * tf)
        a = jnp.dot(h, wg_ref[:, cols], preferred_element_type=F32)
        u = jnp.dot(h, wu_ref[:, cols], preferred_element_type=F32)
        act = (jax.nn.silu(a) * u).astype(BF16)
        part = jnp.dot(act, wd_ref[cols, :], preferred_element_type=F32)
        y = part if y is None else y + part
    o_ref[...] = x + _rms(y, gpost_ref[...])


def ffn(x, gpre, wg, wu, wd, gpost, tm=512, tf=1408):
    n, d = x.shape
    dff = wg.shape[1]
    assert n % tm == 0 and dff % tf == 0
    once = pl.Buffered(1)
    return pl.pallas_call(
        functools.partial(_ffn_kernel, tf=tf),
        grid=(n // tm,),
        in_specs=[pl.BlockSpec((tm, d), lambda i: (i, 0)),
                  pl.BlockSpec((1, d), lambda i: (0, 0)),
                  pl.BlockSpec((d, dff), lambda i: (0, 0), pipeline_mode=once),
                  pl.BlockSpec((d, dff), lambda i: (0, 0), pipeline_mode=once),
                  pl.BlockSpec((dff, d), lambda i: (0, 0), pipeline_mode=once),
                  pl.BlockSpec((1, d), lambda i: (0, 0))],
        out_specs=pl.BlockSpec((tm, d), lambda i: (i, 0)),
        out_shape=jax.ShapeDtypeStruct((n, d), F32),
        compiler_params=_cparams("parallel"),
        name="ffn",
    )(x, gpre.reshape(1, d), wg, wu, wd, gpost.reshape(1, d))


def _rope_tables(s):
    inv = ROPE_THETA ** (-jnp.arange(0, HEAD_DIM, 2, dtype=F32) / HEAD_DIM)
    ang = jnp.arange(s, dtype=F32)[:, None] * inv[None, :]
    cos, sin = jnp.cos(ang), jnp.sin(ang)
    reps = LANES // HEAD_DIM
    cos2 = jnp.tile(jnp.concatenate([cos, cos], -1), (1, reps))
    sin2 = jnp.tile(jnp.concatenate([-sin, sin], -1), (1, reps))
    return (cos2, sin2), (cos2.T[:HEAD_DIM], sin2.T[:HEAD_DIM])


def _slab_cols(w, scale=1.0):
    d, c = w.shape
    w3 = (w * scale).reshape(d, c // HEAD_DIM, HEAD_DIM)
    return jnp.concatenate([w3, jnp.zeros_like(w3)], axis=-1).reshape(d, 2 * c)


def _slab_rows(w):
    return _slab_cols(w.T).T


def _ones_lane(n_slabs):
    one = np.zeros((n_slabs, LANES), np.float32)
    one[:, HEAD_DIM] = 1.0
    return jnp.asarray(one.reshape(-1))


def _block_indicator_rows(s, block):
    blk = (np.arange(s) // block) % HEAD_DIM
    ind = np.zeros((LANES, s), np.float32)
    ind[HEAD_DIM + blk, np.arange(s)] = 1.0
    return jnp.asarray(ind, BF16)[None, None]


def _overlap_matrix(ncp, ns):
    ratio = SLC_BLOCK // CMP_STRIDE
    m = np.arange(ncp)[:, None]
    j = np.arange(ns)[None, :]
    ok = (m >= ratio * j - 1) & (m <= ratio * j + ratio - 1) & (m < ncp - 1)
    return jnp.asarray(ok, BF16)


def _compress_weights(pe, w1, w2, g):
    hid = w1.shape[-1]
    w1r = w1.reshape(2, CMP_STRIDE, HEAD_DIM, hid)
    w1g = jnp.zeros((g, 2, CMP_STRIDE, g, LANES, hid), F32)
    for gi in range(g):
        w1g = w1g.at[gi, :, :, gi, :HEAD_DIM].set(w1r)
    w1g = w1g.reshape(g, 2, CMP_STRIDE * g * LANES, hid).astype(BF16)
    pe2 = jnp.zeros((2, CMP_STRIDE, g, LANES), F32).at[..., :HEAD_DIM].set(
        pe.reshape(2, CMP_STRIDE, 1, HEAD_DIM))
    w2p = jnp.concatenate([w2, jnp.zeros_like(w2)], axis=-1).astype(BF16)
    return pe2.reshape(2, CMP_STRIDE * g * LANES), w1g, w2p


def _out_rows(w):
    return _slab_rows(w).astype(BF16)


def _even_layer(x2, b, s, g_pre, g_post, w_in, b_f, pe_k, w1_k, w2_k, pe_v, w1_v, w2_v, w_out, tabs, tabs_t):
    hf, hn, g = H_FOX, H_NSA, NSA_GROUPS
    hd = HEAD_DIM
    rep = hn // g
    (w_fq, w_fk, w_fv, w_fl, w_nq, w_kc, w_vc, w_ks, w_vs, w_kw, w_vw, w_gl) = jnp.split(
        w_in, list(np.cumsum([hf * hd] * 3 + [hf] + [hn * hd] + [g * hd] * 6)), axis=1)

    w_plain = jnp.concatenate([w_fq * QK_SCALE, w_fv, w_vs, w_vw, w_vc], axis=1).astype(BF16)
    bias_plain = jnp.concatenate([jnp.zeros((hf * LANES,), F32), _ones_lane(hf + 2 * g),
                                  jnp.zeros((g * LANES,), F32)])
    fq0, fv0, vs0, vw0 = 0, hf, 2 * hf, 2 * hf + g
    vc_col = (2 * hf + 2 * g) * LANES
    w_rope = jnp.concatenate([w_nq * QK_SCALE, w_kc], axis=1).astype(BF16)
    n_small = hf + 3 * hn
    w_small = jnp.pad(jnp.concatenate([w_fl, w_gl], axis=1), ((0, 0), (0, LANES - n_small))).astype(BF16)
    p_plain, p_rope, p_small, kt_fox, kt_nsa = project(x2, g_pre, tabs, tabs_t, b, s, [
        ("slab", w_plain, bias_plain, False),
        ("slab", w_rope, jnp.zeros((2 * w_rope.shape[1],), F32), True),
        ("plain", w_small, F32),
        ("keys", w_fk.T.astype(BF16), False),
        ("keys", jnp.concatenate([w_ks, w_kw], axis=1).T.astype(BF16), True),
    ])
    p_plain = p_plain.reshape(b, s, -1)
    p_rope = p_rope.reshape(b, s, -1)

    qx, kx = fox_decay(p_small.reshape(b, s, LANES), jnp.pad(b_f, (0, LANES - hf)).reshape(1, LANES), hf)
    o_fox = flash("flash_fox", p_plain, fq0, kt_fox, 0, p_plain, fv0, hf, 1, extra=qx, kx=kx, decay=True)

    ncp = s // CMP_STRIDE
    ns = s // SLC_BLOCK
    kc = p_rope[:, :, hn * LANES:].reshape(b, ncp, CMP_STRIDE * g * LANES)
    vc = p_plain[:, :, vc_col:].reshape(b, ncp, CMP_STRIDE * g * LANES)
    kcmp = nsa_compress(kc, *_compress_weights(pe_k, w1_k, w2_k, g))
    vcmp = nsa_compress(vc, *_compress_weights(pe_v, w1_v, w2_v, g))
    o_cmp, selb = nsa_select(p_rope, kcmp.transpose(0, 1, 3, 2), vcmp, _overlap_matrix(ncp, ns), rep)
    o_slc = flash("flash_sel", p_rope, 0, kt_nsa, 0, p_plain, vs0, hn, rep, extra=selb,
                  kx=_block_indicator_rows(s, SLC_BLOCK))
    o_win = flash("flash_band", p_rope, 0, kt_nsa, g, p_plain, vw0, hn, rep, band=True)

    n = b * s
    return even_out(o_fox.reshape(n, -1), o_cmp.reshape(n, -1), o_slc.reshape(n, -1),
                    o_win.reshape(n, -1), p_small, _out_rows(w_out[:hf * hd]), _out_rows(w_out[hf * hd:]),
                    g_post, x2, hn, hf)


def _odd_layer(x2, b, s, g_pre, g_post, w_in, w_out, tabs, tabs_t):
    h = H_MOBA
    hd = HEAD_DIM
    d = h * hd
    w_q, w_k, w_v = w_in[:, :d], w_in[:, d:2 * d], w_in[:, 2 * d:]
    q_all, v_all, kt_all = project(x2, g_pre, tabs, tabs_t, b, s, [
        ("slab", (w_q * QK_SCALE).astype(BF16), jnp.zeros((h * LANES,), F32), True),
        ("slab", w_v.astype(BF16), _ones_lane(h), False),
        ("keys", w_k.T.astype(BF16), True),
    ])
    q_all = q_all.reshape(b, s, -1)
    v_all = v_all.reshape(b, s, -1)
    ind = _block_indicator_rows(s, MOBA_BLOCK)
    selb = moba_select(q_all, kt_all, ind[0, 0].T, h)
    o = flash("flash_moba", q_all, 0, kt_all, 0, v_all, 0, h, 1, extra=selb, kx=ind)
    return proj_norm_res(o.reshape(b * s, -1), _out_rows(w_out), g_post, x2)


def kernel(x, ev_w_in, ev_b_f, ev_cmp_pe_k, ev_cmp_w1_k, ev_cmp_w2_k, ev_cmp_pe_v, ev_cmp_w1_v,
           ev_cmp_w2_v, ev_w_out, od_w_in, od_w_out, g_mix_pre, g_mix_post, g_ffn_pre, g_ffn_post,
           ffn_w_gate, ffn_w_up, ffn_w_down):
    b, s, d = x.shape
    depth = g_mix_pre.shape[0]
    tabs, tabs_t = _rope_tables(s)
    x2 = x.reshape(b * s, d)
    for layer in range(depth):
        if layer % 2 == 0:
            e = layer // 2
            x2 = _even_layer(x2, b, s, g_mix_pre[layer], g_mix_post[layer], ev_w_in[e], ev_b_f[e],
                             ev_cmp_pe_k[e], ev_cmp_w1_k[e], ev_cmp_w2_k[e], ev_cmp_pe_v[e],
                             ev_cmp_w1_v[e], ev_cmp_w2_v[e], ev_w_out[e], tabs, tabs_t)
        else:
            o = layer // 2
            x2 = _odd_layer(x2, b, s, g_mix_pre[layer], g_mix_post[layer], od_w_in[o], od_w_out[o],
                            tabs, tabs_t)
        x2 = ffn(x2, g_ffn_pre[layer], ffn_w_gate[layer].astype(BF16), ffn_w_up[layer].astype(BF16),
                 ffn_w_down[layer].astype(BF16), g_ffn_post[layer])
    return x2.reshape(b, s, d)
```

```python
import functools

import jax
import jax.numpy as jnp
import numpy as np
from jax import lax
from jax.experimental import pallas as pl
from jax.experimental.pallas import tpu as pltpu

F32 = jnp.float32
BF16 = jnp.bfloat16

HEAD_DIM = 64
LANES = 128
ROPE_THETA = 10000.0
RMS_EPS = 1e-6
CMP_STRIDE = 16
CMP_LEN = 32
SLC_BLOCK = 64
SLC_TOPN = 16
WINDOW = 512
MOBA_BLOCK = 256
MOBA_TOPK = 3
H_FOX, H_NSA, NSA_GROUPS, H_MOBA = 8, 8, 2, 16
SLC_SHIFT = SLC_BLOCK.bit_length() - 1
MOBA_SHIFT = MOBA_BLOCK.bit_length() - 1
MASKED = -1e30
M_INIT = -1e29
LOG2E = 1.4426950408889634
QK_SCALE = HEAD_DIM ** -0.5 * LOG2E
DECAY_CUT = 100.0 * LOG2E
FLASH_TQ = 2048
FLASH_TQ_EDGE = 1024
FLASH_TK = 512
FLASH_STRIP = 512
FLASH_STRIP_MAIN = 1024
FLASH_GROUP = 2
FLASH_AHEAD = 2
VMEM_LIMIT = 48 * 1024 * 1024
NT_DIMS = (((1,), (1,)), ((), ()))


def _cparams(*sem):
    return pltpu.CompilerParams(dimension_semantics=sem, vmem_limit_bytes=VMEM_LIMIT)


def _rms(x, g):
    return x * lax.rsqrt(jnp.mean(x * x, axis=-1, keepdims=True) + RMS_EPS) * g


def _split3(x):
    hi = x.astype(BF16).astype(F32)
    r = x - hi
    mid = r.astype(BF16).astype(F32)
    lo = (r - mid).astype(BF16).astype(F32)
    return hi, mid, lo


def _slab_project(h, w_ref, b_ref, cos_ref, sin_ref):
    t = jnp.dot(h, w_ref[...], preferred_element_type=F32)
    tm, tn = t.shape
    if cos_ref is not None:
        reps = tn // LANES
        cos = jnp.tile(cos_ref[...], (1, reps))
        sin = jnp.tile(sin_ref[...], (1, reps))
        lane = lax.broadcasted_iota(jnp.int32, t.shape, 1)
        first = (lane & (HEAD_DIM - 1)) < HEAD_DIM // 2
        rot = jnp.where(first, pltpu.roll(t, tn - HEAD_DIM // 2, 1), pltpu.roll(t, HEAD_DIM // 2, 1))
        t = t * cos + rot * sin
    low = lax.broadcasted_iota(jnp.int32, (tm, LANES), 1) < HEAD_DIM
    parts = []
    for pair in range(tn // LANES):
        v = t[:, pair * LANES:(pair + 1) * LANES]
        parts.append(jnp.where(low, v, 0.0))
        parts.append(jnp.where(low, pltpu.roll(v, HEAD_DIM, 1), 0.0))
    return jnp.concatenate(parts, axis=1) + b_ref[...]


def _key_project(h, wt_ref, cos_ref, sin_ref, o_ref):
    t = lax.dot_general(wt_ref[...], h, NT_DIMS, preferred_element_type=F32)
    tn, tm = t.shape
    nheads = tn // HEAD_DIM
    if cos_ref is not None:
        cos = jnp.tile(cos_ref[...], (nheads, 1))
        sin = jnp.tile(sin_ref[...], (nheads, 1))
        r = lax.broadcasted_iota(jnp.int32, t.shape, 0)
        first = (r & (HEAD_DIM - 1)) < HEAD_DIM // 2
        rot = jnp.where(first, pltpu.roll(t, tn - HEAD_DIM // 2, 0), pltpu.roll(t, HEAD_DIM // 2, 0))
        t = t * cos + rot * sin
    spare = jnp.zeros((LANES - HEAD_DIM, tm), o_ref.dtype)
    for hh in range(nheads):
        o_ref[0, hh * LANES:hh * LANES + HEAD_DIM, :] = t[hh * HEAD_DIM:(hh + 1) * HEAD_DIM].astype(o_ref.dtype)
        o_ref[0, hh * LANES + HEAD_DIM:(hh + 1) * LANES, :] = spare


def _project_kernel(x_ref, g_ref, cos_ref, sin_ref, cos_t_ref, sin_t_ref, *refs, kinds):
    n_out = len(kinds)
    ins, outs = list(refs[:-n_out]), refs[-n_out:]
    h = _rms(x_ref[...], g_ref[...]).astype(BF16)
    for (kind, rope), o_ref in zip(kinds, outs):
        if kind == "slab":
            w_ref, b_ref = ins.pop(0), ins.pop(0)
            tabs = (cos_ref, sin_ref) if rope else (None, None)
            o_ref[...] = _slab_project(h, w_ref, b_ref, *tabs).astype(o_ref.dtype)
        elif kind == "plain":
            o_ref[...] = jnp.dot(h, ins.pop(0)[...], preferred_element_type=F32).astype(o_ref.dtype)
        else:
            tabs = (cos_t_ref, sin_t_ref) if rope else (None, None)
            _key_project(h, ins.pop(0), *tabs, o_ref)


def project(x, g, tabs, tabs_t, b, seq, outputs, tm=512):
    n, d = x.shape
    assert n % tm == 0 and seq % tm == 0
    spt = seq // tm
    in_specs = [pl.BlockSpec((tm, d), lambda i: (i, 0)),
                pl.BlockSpec((1, d), lambda i: (0, 0)),
                pl.BlockSpec((tm, LANES), lambda i: (i % spt, 0)),
                pl.BlockSpec((tm, LANES), lambda i: (i % spt, 0)),
                pl.BlockSpec((HEAD_DIM, tm), lambda i: (0, i % spt)),
                pl.BlockSpec((HEAD_DIM, tm), lambda i: (0, i % spt))]
    args = [x, g.reshape(1, d), *tabs, *tabs_t]
    kinds, out_specs, out_shapes = [], [], []
    for out in outputs:
        w = out[1]
        in_specs.append(pl.BlockSpec(w.shape, lambda i: (0, 0)))
        args.append(w)
        if out[0] == "slab":
            nc = 2 * w.shape[1]
            in_specs.append(pl.BlockSpec((1, nc), lambda i: (0, 0)))
            args.append(out[2].reshape(1, nc))
            kinds.append(("slab", out[3]))
            out_specs.append(pl.BlockSpec((tm, nc), lambda i: (i, 0)))
            out_shapes.append(jax.ShapeDtypeStruct((n, nc), BF16))
        elif out[0] == "plain":
            kinds.append(("plain", False))
            out_specs.append(pl.BlockSpec((tm, w.shape[1]), lambda i: (i, 0)))
            out_shapes.append(jax.ShapeDtypeStruct((n, w.shape[1]), out[2]))
        else:
            rows = 2 * w.shape[0]
            kinds.append(("keys", out[2]))
            out_specs.append(pl.BlockSpec((1, rows, tm), lambda i: (i // spt, 0, i % spt)))
            out_shapes.append(jax.ShapeDtypeStruct((b, rows, seq), BF16))
    return pl.pallas_call(
        functools.partial(_project_kernel, kinds=tuple(kinds)),
        grid=(n // tm,),
        in_specs=in_specs,
        out_specs=out_specs,
        out_shape=out_shapes,
        compiler_params=_cparams("parallel"),
        name="project",
    )(*args)


def _fox_decay_kernel(fl_ref, b_ref, pq_ref, pk_ref, oq_ref, ok_ref, carry_sc, *, nh):
    @pl.when(pl.program_id(1) == 0)
    def _():
        carry_sc[...] = jnp.zeros(carry_sc.shape, F32)

    c = jax.nn.log_sigmoid(fl_ref[0] + b_ref[...])
    ts = c.shape[0]
    row = lax.broadcasted_iota(jnp.int32, c.shape, 0)
    lane = lax.broadcasted_iota(jnp.int32, c.shape, 1)
    sh = 1
    while sh < ts:
        c = c + jnp.where(row >= sh, pltpu.roll(c, sh, 0), 0.0)
        sh *= 2
    c = c + carry_sc[0:1, :]
    carry_sc[0:1, :] = c[ts - 1:ts, :]
    hi, mid, lo = _split3(c * LOG2E)
    c3 = jnp.where(lane < nh, hi,
                   jnp.where(lane < 2 * nh, pltpu.roll(mid, nh, 1),
                             jnp.where(lane < 3 * nh, pltpu.roll(lo, 2 * nh, 1),
                                       jnp.where(lane == 3 * nh, 1.0, 0.0)))).astype(BF16)
    for h in range(nh):
        oq_ref[0, h, 0] = jnp.dot(c3, pq_ref[h], preferred_element_type=F32).astype(oq_ref.dtype)
        ok_ref[0, h] = lax.dot_general(pk_ref[h], c3, NT_DIMS,
                                       preferred_element_type=F32).astype(ok_ref.dtype)


def _fox_placement(nh):
    pq = np.zeros((nh, LANES, LANES), np.float32)
    pk = np.zeros((nh, LANES, LANES), np.float32)
    one = 3 * nh
    for h in range(nh):
        for t in range(3):
            pq[h, one, HEAD_DIM + t] = 1.0
            pq[h, t * nh + h, HEAD_DIM + 3 + t] = 1.0
            pk[h, HEAD_DIM + t, t * nh + h] = -1.0
            pk[h, HEAD_DIM + 3 + t, one] = 1.0
    return jnp.asarray(pq, BF16), jnp.asarray(pk, BF16)


def fox_decay(small, b_row, nh, ts=2048):
    b, s, _ = small.shape
    ts = min(ts, s)
    pq, pk = _fox_placement(nh)
    return pl.pallas_call(
        functools.partial(_fox_decay_kernel, nh=nh),
        grid=(b, s // ts),
        in_specs=[pl.BlockSpec((1, ts, LANES), lambda i, j: (i, j, 0)),
                  pl.BlockSpec((1, LANES), lambda i, j: (0, 0)),
                  pl.BlockSpec((nh, LANES, LANES), lambda i, j: (0, 0, 0)),
                  pl.BlockSpec((nh, LANES, LANES), lambda i, j: (0, 0, 0))],
        out_specs=[pl.BlockSpec((1, nh, 1, ts, LANES), lambda i, j: (i, 0, 0, j, 0)),
                   pl.BlockSpec((1, nh, LANES, ts), lambda i, j: (i, 0, 0, j))],
        out_shape=[jax.ShapeDtypeStruct((b, nh, 1, s, LANES), BF16),
                   jax.ShapeDtypeStruct((b, nh, LANES, s), BF16)],
        scratch_shapes=[pltpu.VMEM((8, LANES), F32)],
        compiler_params=_cparams("parallel", "arbitrary"),
        name="fox_decay",
    )(small, b_row, pq, pk)


def _nsa_compress_kernel(t_ref, pe_ref, w1_ref, w2_ref, o_ref):
    t = t_ref[0].astype(F32)
    ncp = t.shape[0]
    a = jnp.dot((t + pe_ref[0:1, :]).astype(BF16), w1_ref[0, 0], preferred_element_type=F32)
    bm = jnp.dot((t + pe_ref[1:2, :]).astype(BF16), w1_ref[0, 1], preferred_element_type=F32)
    pre = a + pltpu.roll(bm, ncp - 1, 0)
    hid = jax.nn.gelu(pre)
    o_ref[0, 0] = jnp.dot(hid.astype(BF16), w2_ref[...], preferred_element_type=F32).astype(o_ref.dtype)


def nsa_compress(t, pe, w1, w2):
    b, ncp, cw = t.shape
    g, _, _, hid = w1.shape
    return pl.pallas_call(
        _nsa_compress_kernel,
        grid=(b, g),
        in_specs=[pl.BlockSpec((1, ncp, cw), lambda i, j: (i, 0, 0)),
                  pl.BlockSpec((2, cw), lambda i, j: (0, 0)),
                  pl.BlockSpec((1, 2, cw, hid), lambda i, j: (j, 0, 0, 0)),
                  pl.BlockSpec((hid, LANES), lambda i, j: (0, 0))],
        out_specs=pl.BlockSpec((1, 1, ncp, LANES), lambda i, j: (i, j, 0, 0)),
        out_shape=jax.ShapeDtypeStruct((b, g, ncp, LANES), BF16),
        compiler_params=_cparams("parallel", "parallel"),
        name="nsa_compress",
    )(t, pe, w1, w2)


def _topk_mask(work, col, k, axis=-1):
    sel = jnp.zeros(work.shape, jnp.bool_)
    col = col.astype(F32)
    for _ in range(k):
        mx = jnp.max(work, axis=axis, keepdims=True)
        first = jnp.min(jnp.where(work == mx, col, jnp.inf), axis=axis, keepdims=True)
        hit = col == first
        sel = jnp.logical_or(sel, hit)
        work = jnp.where(hit, -jnp.inf, work)
    return sel


def _nsa_select_kernel(q_ref, kct_ref, vc_ref, m_ref, oc_ref, sb_ref, *, tq, rep, nsup, sps):
    i = pl.program_id(2)
    ncp = kct_ref.shape[-1]
    ns = m_ref.shape[-1]
    qpos = i * tq + lax.broadcasted_iota(jnp.int32, (tq, 1), 0)
    has_block = jnp.where(qpos >= CMP_LEN - 1, 1.0, 0.0)
    qblk = qpos >> SLC_SHIFT
    zeros = jnp.zeros((tq, HEAD_DIM), sb_ref.dtype)
    hidden = jnp.concatenate([zeros, jnp.full((tq, HEAD_DIM), MASKED, sb_ref.dtype)], axis=-1)

    def prefix(nv):
        wc, ws = ncp * nv // nsup, ns * nv // nsup
        cend = lax.broadcasted_iota(jnp.int32, (1, wc), 1) * CMP_STRIDE + (CMP_LEN - 1)
        cmask = cend <= qpos
        kct = kct_ref[0, 0, :, :wc]
        vc = vc_ref[0, 0, :wc, :]
        pcs = jnp.zeros((tq, wc), F32)
        for r in range(rep):
            s = jnp.dot(q_ref[0, :, r * LANES:(r + 1) * LANES], kct, preferred_element_type=F32)
            s = jnp.where(cmask, s, MASKED)
            e = jnp.exp2(s - jnp.max(s, axis=-1, keepdims=True))
            p = e * (has_block / jnp.maximum(jnp.sum(e, axis=-1, keepdims=True), 1e-30))
            oc_ref[0, :, r * LANES:(r + 1) * LANES] = jnp.dot(
                p.astype(BF16), vc, preferred_element_type=F32).astype(oc_ref.dtype)
            pcs = pcs + p
        mm = m_ref[:wc, :ws]
        imp = sum(jnp.dot(part.astype(BF16), mm, preferred_element_type=F32) for part in _split3(pcs))
        sblk = lax.broadcasted_iota(jnp.int32, (1, ws), 1)
        forced = (sblk == 0) | (sblk == qblk) | (sblk == qblk - 1)
        free = jnp.where(forced | (sblk > qblk), -jnp.inf, imp)
        sel = _topk_mask(free, sblk, min(SLC_TOPN, ns) - 3)
        bias = jnp.where((sel | forced) & (sblk <= qblk), 0.0, MASKED).astype(sb_ref.dtype)
        for j in range(nsup):
            sb_ref[0, 0, j] = hidden if j >= nv else jnp.concatenate(
                [zeros, bias[:, j * HEAD_DIM:(j + 1) * HEAD_DIM]], axis=-1)

    for nv in range(1, nsup + 1):
        pl.when(i // sps == nv - 1)(functools.partial(prefix, nv))


def nsa_select(q_all, kct, vc, m, rep, tq=512):
    b, s, _ = q_all.shape
    g = kct.shape[1]
    ncp = kct.shape[-1]
    ns = m.shape[-1]
    nsup = ns // HEAD_DIM
    tq = min(tq, s)
    assert SLC_BLOCK == 64 and ns % HEAD_DIM == 0 and (s // tq) % nsup == 0
    return pl.pallas_call(
        functools.partial(_nsa_select_kernel, tq=tq, rep=rep, nsup=nsup, sps=(s // tq) // nsup),
        grid=(b, g, s // tq),
        in_specs=[pl.BlockSpec((1, tq, rep * LANES), lambda bi, gi, i: (bi, i, gi)),
                  pl.BlockSpec((1, 1, LANES, ncp), lambda bi, gi, i: (bi, gi, 0, 0)),
                  pl.BlockSpec((1, 1, ncp, LANES), lambda bi, gi, i: (bi, gi, 0, 0)),
                  pl.BlockSpec((ncp, ns), lambda bi, gi, i: (0, 0))],
        out_specs=[pl.BlockSpec((1, tq, rep * LANES), lambda bi, gi, i: (bi, i, gi)),
                   pl.BlockSpec((1, 1, nsup, tq, LANES), lambda bi, gi, i: (bi, gi, 0, i, 0))],
        out_shape=[jax.ShapeDtypeStruct((b, s, g * rep * LANES), BF16),
                   jax.ShapeDtypeStruct((b, g, nsup, s, LANES), BF16)],
        compiler_params=_cparams("parallel", "parallel", "parallel"),
        name="nsa_select",
    )(q_all, kct, vc, m)


def _moba_select_kernel(q_ref, kt_ref, ind_ref, sb_ref, kbar_sc, *, tq):
    i = pl.program_id(2)

    @pl.when(i == 0)
    def _():
        kbar = jnp.dot(kt_ref[0], ind_ref[...], preferred_element_type=F32) * (1.0 / MOBA_BLOCK)
        kbar_sc[...] = kbar.T

    q = q_ref[0]
    gate = sum(lax.dot_general(part.astype(BF16), q, NT_DIMS, preferred_element_type=F32)
               for part in _split3(kbar_sc[...]))
    qpos = i * tq + lax.broadcasted_iota(jnp.int32, (1, tq), 1)
    cur = qpos >> MOBA_SHIFT
    row = lax.broadcasted_iota(jnp.int32, (LANES, 1), 0)
    blk = row - HEAD_DIM
    past = (blk >= 0) & (blk < cur)
    sel = _topk_mask(jnp.where(past, gate, -jnp.inf), row, MOBA_TOPK, axis=0)
    keep = (blk < 0) | (sel & past) | (blk == cur)
    sb_ref[0, 0, 0] = jnp.where(keep, 0.0, MASKED).T.astype(sb_ref.dtype)


def moba_select(q_all, kt_all, ind, nh, tq=2048):
    b, s, _ = q_all.shape
    assert MOBA_BLOCK == 256 and s // MOBA_BLOCK <= HEAD_DIM
    tq = min(tq, s)
    return pl.pallas_call(
        functools.partial(_moba_select_kernel, tq=tq),
        grid=(b, nh, s // tq),
        in_specs=[pl.BlockSpec((1, tq, LANES), lambda bi, hi, i: (bi, i, hi)),
                  pl.BlockSpec((1, LANES, s), lambda bi, hi, i: (bi, hi, 0)),
                  pl.BlockSpec((s, LANES), lambda bi, hi, i: (0, 0))],
        out_specs=pl.BlockSpec((1, 1, 1, tq, LANES), lambda bi, hi, i: (bi, hi, 0, i, 0)),
        out_shape=jax.ShapeDtypeStruct((b, nh, 1, s, LANES), BF16),
        scratch_shapes=[pltpu.VMEM((LANES, LANES), F32)],
        compiler_params=_cparams("parallel", "parallel", "arbitrary"),
        name="moba_select",
    )(q_all, kt_all, ind)


def _flash_kernel(*refs, tq, tk, rs, rs_main, tps, band, decay, has_extra, has_kx):
    refs = list(refs)
    qa_ref = refs.pop(0)
    ex_ref = refs.pop(0) if has_extra else None
    kt_ref = refs.pop(0)
    kx_ref = refs.pop(0) if has_kx else None
    kmax_sc = refs.pop() if decay else None
    v_ref, o_ref, m_sc, acc_sc = refs
    i = pl.program_id(2)
    m_sc[...] = jnp.full(m_sc.shape, M_INIT, F32)
    acc_sc[...] = jnp.zeros(acc_sc.shape, F32)
    row = lax.broadcasted_iota(jnp.int32, (rs, tk), 0)
    col = lax.broadcasted_iota(jnp.int32, (rs, tk), 1)
    nst = tq // rs
    kpq = tq // tk

    def run(items, rs=rs):
        tiles = {}

        def operands(j):
            if id(j) not in tiles:
                start = pl.multiple_of(j * tk, tk)
                kt = kt_ref[0, :, pl.ds(start, tk)]
                if kx_ref is not None:
                    kt = kt + kx_ref[0, 0, :, pl.ds(start, tk)]
                tiles[id(j)] = (kt, v_ref[0, pl.ds(start, tk), :])
            return tiles[id(j)]

        def logits(item):
            j, r, _ = item
            rows = pl.ds(r * rs, rs)
            qa = qa_ref[0, rows, :]
            if ex_ref is not None:
                qa = qa + ex_ref[0, 0, j // tps, rows, :]
            return jnp.dot(qa, operands(j)[0], preferred_element_type=F32)

        def accumulate(rows, alpha, p, vv):
            acc_sc[rows, :] = alpha * acc_sc[rows, :] + jnp.dot(p, vv, preferred_element_type=F32)

        pending = [logits(it) for it in items[:FLASH_AHEAD]]
        held = None
        for n, (j, r, mask) in enumerate(items):
            rows = pl.ds(r * rs, rs)
            s = pending.pop(0)
            if n + FLASH_AHEAD < len(items):
                pending.append(logits(items[n + FLASH_AHEAD]))
            if held is not None:
                accumulate(*held)
            if mask is not None:
                s = jnp.where(mask, s, MASKED)
            m_prev = m_sc[rows, :]
            m_new = jnp.maximum(m_prev, jnp.max(s, axis=-1, keepdims=True))
            p = jnp.exp2(s - jnp.tile(m_new, (1, tk // LANES)))
            m_sc[rows, :] = m_new
            held = (rows, jnp.exp2(m_prev - m_new), p.astype(BF16), operands(j)[1])
        accumulate(*held)

    def edge_items(dj_list, j_of):
        items = []
        for dj in dj_list:
            j = j_of(dj)
            for r in range(nst):
                off, ko = r * rs, dj * tk
                lo = off - WINDOW + 1 if band else None
                if ko > off + rs - 1 or (band and ko + tk - 1 < lo):
                    continue
                full = ko + tk - 1 <= off and (not band or ko > off + rs - 1 - WINDOW)
                mask = None
                if not full:
                    mask = col + ko <= row + off
                    if band:
                        mask = mask & (col + ko > row + (off - WINDOW))
                items.append((j, r, mask))
        return items

    def body(jjs):
        tiles = [jj * kpq + dj for jj in jjs for dj in range(kpq)]
        run([(j, r, None) for j in tiles for r in range(tq // rs_main)], rs_main)

    def sweep(n, jj_of):
        def group(t, carry):
            body([jj_of(FLASH_GROUP * t + u) for u in range(FLASH_GROUP)])
            return carry
        lax.fori_loop(0, n // FLASH_GROUP, group, 0)

        def single(t, carry):
            body([jj_of(n - n % FLASH_GROUP + t)])
            return carry
        lax.fori_loop(0, n % FLASH_GROUP, single, 0)

    def run_diag():
        diag = [i * kpq + dj for dj in range(kpq)]
        run(edge_items(list(range(kpq)), lambda dj: diag[dj]))

    if decay:
        s_len = kt_ref.shape[-1]

        @pl.when(i == 0)
        def _():
            def chunk(c, best):
                kk = kt_ref[0, :, pl.ds(pl.multiple_of(c * tq, tq), tq)].astype(F32)
                return jnp.maximum(best, jnp.sum(kk * kk, axis=0, keepdims=True))
            ksq = lax.fori_loop(0, s_len // tq, chunk, jnp.zeros((1, tq), F32))
            kmax_sc[...] = jnp.broadcast_to(jnp.sqrt(jnp.max(ksq, axis=1, keepdims=True)), kmax_sc.shape)

        run_diag()
        q = qa_ref[0].astype(F32)
        qn = jnp.sqrt(jnp.sum(q * q, axis=1, keepdims=True))
        slack = jnp.max(qn * kmax_sc[0:1, 0:1] - m_sc[:, 0:1], axis=0, keepdims=True)
        cvec = -jnp.sum(kx_ref[0, 0, HEAD_DIM:HEAD_DIM + 3, :].astype(F32), axis=0, keepdims=True)
        pos = lax.broadcasted_iota(jnp.int32, (1, s_len), 1)
        q0 = i * tq
        c_q0 = jnp.sum(jnp.where(pos == q0, cvec, 0.0), axis=1, keepdims=True)
        dead = (pos < q0) & (slack + c_q0 - cvec <= -DECAY_CUT)
        n_dead = jnp.sum(jnp.where(dead, 1.0, 0.0)).astype(jnp.int32) // tq
        sweep(i - n_dead, lambda t: i - 1 - t)
    elif band:
        pl.when(i == 0)(run_diag)

        @pl.when(i > 0)
        def _():
            near = {dj: i * kpq + dj for dj in range(-(WINDOW // tk), kpq)}
            run(edge_items(sorted(near), near.get))
    else:
        sweep(i, lambda t: t)
        run_diag()
    acc = acc_sc[...]
    o_ref[0] = (acc / acc[:, HEAD_DIM:HEAD_DIM + 1]).astype(o_ref.dtype)


def flash(name, q_all, q0, kt_all, k0, v_all, v0, nh, rep, extra=None, kx=None, band=False,
          decay=False):
    assert not decay or (kx is not None and not band)
    b, s, _ = q_all.shape
    tk = min(WINDOW if band else FLASH_TK, s)
    tq = min(FLASH_TQ_EDGE if (band or decay) else FLASH_TQ, s)
    rs = min(FLASH_STRIP, tq)
    assert s % tq == 0 and tq % tk == 0 and tq % rs == 0 and (not band or WINDOW % tk == 0)
    in_specs = [pl.BlockSpec((1, tq, LANES), lambda bi, hi, i: (bi, i, q0 + hi))]
    args = [q_all]
    tps = 1
    if extra is not None:
        he, nsup = extra.shape[1], extra.shape[2]
        rep_e = nh // he
        assert (s // nsup) % tk == 0
        tps = (s // nsup) // tk
        in_specs.append(pl.BlockSpec((1, 1, nsup, tq, LANES), lambda bi, hi, i: (bi, hi // rep_e, 0, i, 0)))
        args.append(extra)
    in_specs.append(pl.BlockSpec((1, LANES, s), lambda bi, hi, i: (bi, k0 + hi // rep, 0)))
    args.append(kt_all)
    if kx is not None:
        bx, hx = kx.shape[0], kx.shape[1]
        in_specs.append(pl.BlockSpec(
            (1, 1, LANES, s), lambda bi, hi, i: (bi if bx > 1 else 0, hi if hx > 1 else 0, 0, 0)))
        args.append(kx)
    in_specs.append(pl.BlockSpec((1, s, LANES), lambda bi, hi, i: (bi, 0, v0 + hi // rep)))
    args.append(v_all)
    return pl.pallas_call(
        functools.partial(_flash_kernel, tq=tq, tk=tk, rs=rs, rs_main=min(FLASH_STRIP_MAIN, tq),
                          tps=tps, band=band, decay=decay,
                          has_extra=extra is not None, has_kx=kx is not None),
        grid=(b, nh, s // tq),
        in_specs=in_specs,
        out_specs=pl.BlockSpec((1, tq, LANES), lambda bi, hi, i: (bi, i, hi)),
        out_shape=jax.ShapeDtypeStruct((b, s, nh * LANES), BF16),
        scratch_shapes=[pltpu.VMEM((tq, LANES), F32), pltpu.VMEM((tq, LANES), F32)]
        + ([pltpu.VMEM((8, LANES), F32)] if decay else []),
        compiler_params=_cparams("parallel", "parallel", "arbitrary" if decay else "parallel"),
        name=name,
    )(*args)


def _proj_norm_res_kernel(a_ref, w_ref, g_ref, x_ref, o_ref):
    y = jnp.dot(a_ref[...], w_ref[...], preferred_element_type=F32)
    o_ref[...] = x_ref[...] + _rms(y, g_ref[...])


def proj_norm_res(a, w, g, x, tm=512):
    n, k = a.shape
    d = w.shape[1]
    return pl.pallas_call(
        _proj_norm_res_kernel,
        grid=(n // tm,),
        in_specs=[pl.BlockSpec((tm, k), lambda i: (i, 0)),
                  pl.BlockSpec((k, d), lambda i: (0, 0)),
                  pl.BlockSpec((1, d), lambda i: (0, 0)),
                  pl.BlockSpec((tm, d), lambda i: (i, 0))],
        out_specs=pl.BlockSpec((tm, d), lambda i: (i, 0)),
        out_shape=jax.ShapeDtypeStruct((n, d), F32),
        compiler_params=_cparams("parallel"),
        name="proj_norm_res",
    )(a, w, g.reshape(1, d), x)


def _even_out_kernel(of_ref, oc_ref, os_ref, ow_ref, gl_ref, wf_ref, wn_ref, g_ref, x_ref, o_ref,
                     *, nh, g0):
    gate = jax.nn.sigmoid(gl_ref[...])
    parts = []
    for h in range(nh):
        sl = slice(h * LANES, (h + 1) * LANES)
        c = g0 + 3 * h
        parts.append(gate[:, c:c + 1] * oc_ref[:, sl].astype(F32)
                     + gate[:, c + 1:c + 2] * os_ref[:, sl].astype(F32)
                     + gate[:, c + 2:c + 3] * ow_ref[:, sl].astype(F32))
    a = jnp.concatenate(parts, axis=-1).astype(BF16)
    y = (jnp.dot(of_ref[...], wf_ref[...], preferred_element_type=F32)
         + jnp.dot(a, wn_ref[...], preferred_element_type=F32))
    o_ref[...] = x_ref[...] + _rms(y, g_ref[...])


def even_out(o_fox, o_cmp, o_slc, o_win, small, wf, wn, g, x, nh, g0, tm=512):
    n, k = o_fox.shape
    d = wf.shape[1]
    act = pl.BlockSpec((tm, k), lambda i: (i, 0))
    wspec = pl.BlockSpec((k, d), lambda i: (0, 0))
    return pl.pallas_call(
        functools.partial(_even_out_kernel, nh=nh, g0=g0),
        grid=(n // tm,),
        in_specs=[act, act, act, act,
                  pl.BlockSpec((tm, LANES), lambda i: (i, 0)),
                  wspec, wspec,
                  pl.BlockSpec((1, d), lambda i: (0, 0)),
                  pl.BlockSpec((tm, d), lambda i: (i, 0))],
        out_specs=pl.BlockSpec((tm, d), lambda i: (i, 0)),
        out_shape=jax.ShapeDtypeStruct((n, d), F32),
        compiler_params=_cparams("parallel"),
        name="even_out",
    )(o_fox, o_cmp, o_slc, o_win, small, wf, wn, g.reshape(1, d), x)


def _ffn_kernel(x_ref, gpre_ref, wg_ref, wu_ref, wd_ref, gpost_ref, o_ref, *, tf):
    x = x_ref[...]
    h = _rms(x, gpre_ref[...]).astype(BF16)
    y = None
    for c in range(wg_ref.shape[1] // tf):
        cols = slice(c * tf, (c + 1) * tf)
        a = jnp.dot(h, wg_ref[:, cols], preferred_element_type=F32)
        u = jnp.dot(h, wu_ref[:, cols], preferred_element_type=F32)
        act = (jax.nn.silu(a) * u).astype(BF16)
        part = jnp.dot(act, wd_ref[cols, :], preferred_element_type=F32)
        y = part if y is None else y + part
    o_ref[...] = x + _rms(y, gpost_ref[...])


def ffn(x, gpre, wg, wu, wd, gpost, tm=1024, tf=256):
    n, d = x.shape
    dff = wg.shape[1]
    assert n % tm == 0 and dff % tf == 0
    once = pl.Buffered(1)
    return pl.pallas_call(
        functools.partial(_ffn_kernel, tf=tf),
        grid=(n // tm,),
        in_specs=[pl.BlockSpec((tm, d), lambda i: (i, 0)),
                  pl.BlockSpec((1, d), lambda i: (0, 0)),
                  pl.BlockSpec((d, dff), lambda i: (0, 0), pipeline_mode=once),
                  pl.BlockSpec((d, dff), lambda i: (0, 0), pipeline_mode=once),
                  pl.BlockSpec((dff, d), lambda i: (0, 0), pipeline_mode=once),
                  pl.BlockSpec((1, d), lambda i: (0, 0))],
        out_specs=pl.BlockSpec((tm, d), lambda i: (i, 0)),
        out_shape=jax.ShapeDtypeStruct((n, d), F32),
        compiler_params=_cparams("parallel"),
        name="ffn",
    )(x, gpre.reshape(1, d), wg, wu, wd, gpost.reshape(1, d))


def _rope_tables(s):
    inv = ROPE_THETA ** (-jnp.arange(0, HEAD_DIM, 2, dtype=F32) / HEAD_DIM)
    ang = jnp.arange(s, dtype=F32)[:, None] * inv[None, :]
    cos, sin = jnp.cos(ang), jnp.sin(ang)
    reps = LANES // HEAD_DIM
    cos2 = jnp.tile(jnp.concatenate([cos, cos], -1), (1, reps))
    sin2 = jnp.tile(jnp.concatenate([-sin, sin], -1), (1, reps))
    return (cos2, sin2), (cos2.T[:HEAD_DIM], sin2.T[:HEAD_DIM])


def _slab_cols(w, scale=1.0):
    d, c = w.shape
    w3 = (w * scale).reshape(d, c // HEAD_DIM, HEAD_DIM)
    return jnp.concatenate([w3, jnp.zeros_like(w3)], axis=-1).reshape(d, 2 * c)


def _slab_rows(w):
    return _slab_cols(w.T).T


def _ones_lane(n_slabs):
    one = np.zeros((n_slabs, LANES), np.float32)
    one[:, HEAD_DIM] = 1.0
    return jnp.asarray(one.reshape(-1))


def _block_indicator_rows(s, block):
    blk = (np.arange(s) // block) % HEAD_DIM
    ind = np.zeros((LANES, s), np.float32)
    ind[HEAD_DIM + blk, np.arange(s)] = 1.0
    return jnp.asarray(ind, BF16)[None, None]


def _overlap_matrix(ncp, ns):
    ratio = SLC_BLOCK // CMP_STRIDE
    m = np.arange(ncp)[:, None]
    j = np.arange(ns)[None, :]
    ok = (m >= ratio * j - 1) & (m <= ratio * j + ratio - 1) & (m < ncp - 1)
    return jnp.asarray(ok, BF16)


def _compress_weights(pe, w1, w2, g):
    hid = w1.shape[-1]
    w1r = w1.reshape(2, CMP_STRIDE, HEAD_DIM, hid)
    w1g = jnp.zeros((g, 2, CMP_STRIDE, g, LANES, hid), F32)
    for gi in range(g):
        w1g = w1g.at[gi, :, :, gi, :HEAD_DIM].set(w1r)
    w1g = w1g.reshape(g, 2, CMP_STRIDE * g * LANES, hid).astype(BF16)
    pe2 = jnp.zeros((2, CMP_STRIDE, g, LANES), F32).at[..., :HEAD_DIM].set(
        pe.reshape(2, CMP_STRIDE, 1, HEAD_DIM))
    w2p = jnp.concatenate([w2, jnp.zeros_like(w2)], axis=-1).astype(BF16)
    return pe2.reshape(2, CMP_STRIDE * g * LANES), w1g, w2p


def _out_rows(w):
    return _slab_rows(w).astype(BF16)


def _even_layer(x2, b, s, g_pre, g_post, w_in, b_f, pe_k, w1_k, w2_k, pe_v, w1_v, w2_v, w_out, tabs, tabs_t):
    hf, hn, g = H_FOX, H_NSA, NSA_GROUPS
    hd = HEAD_DIM
    rep = hn // g
    (w_fq, w_fk, w_fv, w_fl, w_nq, w_kc, w_vc, w_ks, w_vs, w_kw, w_vw, w_gl) = jnp.split(
        w_in, list(np.cumsum([hf * hd] * 3 + [hf] + [hn * hd] + [g * hd] * 6)), axis=1)

    w_plain = jnp.concatenate([w_fq * QK_SCALE, w_fv, w_vs, w_vw, w_vc], axis=1).astype(BF16)
    bias_plain = jnp.concatenate([jnp.zeros((hf * LANES,), F32), _ones_lane(hf + 2 * g),
                                  jnp.zeros((g * LANES,), F32)])
    fq0, fv0, vs0, vw0 = 0, hf, 2 * hf, 2 * hf + g
    vc_col = (2 * hf + 2 * g) * LANES
    w_rope = jnp.concatenate([w_nq * QK_SCALE, w_kc], axis=1).astype(BF16)
    n_small = hf + 3 * hn
    w_small = jnp.pad(jnp.concatenate([w_fl, w_gl], axis=1), ((0, 0), (0, LANES - n_small))).astype(BF16)
    p_plain, p_rope, p_small, kt_fox, kt_nsa = project(x2, g_pre, tabs, tabs_t, b, s, [
        ("slab", w_plain, bias_plain, False),
        ("slab", w_rope, jnp.zeros((2 * w_rope.shape[1],), F32), True),
        ("plain", w_small, F32),
        ("keys", w_fk.T.astype(BF16), False),
        ("keys", jnp.concatenate([w_ks, w_kw], axis=1).T.astype(BF16), True),
    ])
    p_plain = p_plain.reshape(b, s, -1)
    p_rope = p_rope.reshape(b, s, -1)

    qx, kx = fox_decay(p_small.reshape(b, s, LANES), jnp.pad(b_f, (0, LANES - hf)).reshape(1, LANES), hf)
    o_fox = flash("flash_fox", p_plain, fq0, kt_fox, 0, p_plain, fv0, hf, 1, extra=qx, kx=kx, decay=True)

    ncp = s // CMP_STRIDE
    ns = s // SLC_BLOCK
    kc = p_rope[:, :, hn * LANES:].reshape(b, ncp, CMP_STRIDE * g * LANES)
    vc = p_plain[:, :, vc_col:].reshape(b, ncp, CMP_STRIDE * g * LANES)
    kcmp = nsa_compress(kc, *_compress_weights(pe_k, w1_k, w2_k, g))
    vcmp = nsa_compress(vc, *_compress_weights(pe_v, w1_v, w2_v, g))
    o_cmp, selb = nsa_select(p_rope, kcmp.transpose(0, 1, 3, 2), vcmp, _overlap_matrix(ncp, ns), rep)
    o_slc = flash("flash_sel", p_rope, 0, kt_nsa, 0, p_plain, vs0, hn, rep, extra=selb,
                  kx=_block_indicator_rows(s, SLC_BLOCK))
    o_win = flash("flash_band", p_rope, 0, kt_nsa, g, p_plain, vw0, hn, rep, band=True)

    n = b * s
    return even_out(o_fox.reshape(n, -1), o_cmp.reshape(n, -1), o_slc.reshape(n, -1),
                    o_win.reshape(n, -1), p_small, _out_rows(w_out[:hf * hd]), _out_rows(w_out[hf * hd:]),
                    g_post, x2, hn, hf)


def _odd_layer(x2, b, s, g_pre, g_post, w_in, w_out, tabs, tabs_t):
    h = H_MOBA
    hd = HEAD_DIM
    d = h * hd
    w_q, w_k, w_v = w_in[:, :d], w_in[:, d:2 * d], w_in[:, 2 * d:]
    q_all, v_all, kt_all = project(x2, g_pre, tabs, tabs_t, b, s, [
        ("slab", (w_q * QK_SCALE).astype(BF16), jnp.zeros((h * LANES,), F32), True),
        ("slab", w_v.astype(BF16), _ones_lane(h), False),
        ("keys", w_k.T.astype(BF16), True),
    ])
    q_all = q_all.reshape(b, s, -1)
    v_all = v_all.reshape(b, s, -1)
    ind = _block_indicator_rows(s, MOBA_BLOCK)
    selb = moba_select(q_all, kt_all, ind[0, 0].T, h)
    o = flash("flash_moba", q_all, 0, kt_all, 0, v_all, 0, h, 1, extra=selb, kx=ind)
    return proj_norm_res(o.reshape(b * s, -1), _out_rows(w_out), g_post, x2)


def kernel(x, ev_w_in, ev_b_f, ev_cmp_pe_k, ev_cmp_w1_k, ev_cmp_w2_k, ev_cmp_pe_v, ev_cmp_w1_v,
           ev_cmp_w2_v, ev_w_out, od_w_in, od_w_out, g_mix_pre, g_mix_post, g_ffn_pre, g_ffn_post,
           ffn_w_gate, ffn_w_up, ffn_w_down):
    b, s, d = x.shape
    depth = g_mix_pre.shape[0]
    tabs, tabs_t = _rope_tables(s)
    x2 = x.reshape(b * s, d)
    for layer in range(depth):
        if layer % 2 == 0:
            e = layer // 2
            x2 = _even_layer(x2, b, s, g_mix_pre[layer], g_mix_post[layer], ev_w_in[e], ev_b_f[e],
                             ev_cmp_pe_k[e], ev_cmp_w1_k[e], ev_cmp_w2_k[e], ev_cmp_pe_v[e],
                             ev_cmp_w1_v[e], ev_cmp_w2_v[e], ev_w_out[e], tabs, tabs_t)
        else:
            o = layer // 2
            x2 = _odd_layer(x2, b, s, g_mix_pre[layer], g_mix_post[layer], od_w_in[o], od_w_out[o],
                            tabs, tabs_t)
        x2 = ffn(x2, g_ffn_pre[layer], ffn_w_gate[layer].astype(BF16), ffn_w_up[layer].astype(BF16),
                 ffn_w_down[layer].astype(BF16), g_ffn_post[layer])
    return x2.reshape(b, s, d)
```
